```python
import jax, jax.numpy as jnp
from jax import lax
import numpy as np

D_MODEL = 2048
BATCH = 16
SEQ = 2048
DEPTH = 1

CTX_LEN = 256
GRID_W = 64
MIX_WIDTH = D_MODEL
MLSTM_WIDTH = MIX_WIDTH // 2
N_HEADS = 4
DV = MLSTM_WIDTH // N_HEADS
DQK = DV // 2
QK_WIDTH = N_HEADS * DQK
N_DIRS = 2
GATE_COLS = N_DIRS * 2 * N_HEADS
CHUNK = 64
QK_CONV = 5
FORGET_BIAS = 3.0
POOL_WIDTH = MIX_WIDTH - MLSTM_WIDTH
POOL_WINDOWS = (2, 4, 8, 16)
POOL_GC = POOL_WIDTH // len(POOL_WINDOWS)
N_EXPERTS = 16
EC_FACTOR = 2
D_EXPERT = D_MODEL // 2
N_MOD = 6
EPS = 1e-6

K_OFF = 0
V_OFF = K_OFF + QK_WIDTH
G_OFF = V_OFF + MLSTM_WIDTH
Q_OFF = G_OFF + GATE_COLS
O_OFF = Q_OFF + QK_WIDTH
P_OFF = O_OFF + MLSTM_WIDTH
IN_COLS = P_OFF + POOL_WIDTH

kernel_name = "hybrid_mlstm_pool_ecmoe_dit_layer"


def rmsnorm(x, g):
    xf = x.astype(jnp.float32)
    y = xf * lax.rsqrt(jnp.mean(xf * xf, -1, keepdims=True) + EPS)
    return (y * g.astype(jnp.float32)).astype(x.dtype)


def short_conv(x, w):
    pad = QK_CONV // 2
    T = x.shape[1]
    xp = jnp.pad(x, ((0, 0), (pad, pad), (0, 0)))
    y = xp[:, 0:T] * w[0]
    for j in range(1, QK_CONV):
        y = y + xp[:, j:j + T] * w[j]
    return jax.nn.silu(y)


def split_heads(t, dh):
    B_, T, _ = t.shape
    return t.reshape(B_, T, N_HEADS, dh).transpose(0, 2, 1, 3)


def _rev(t):
    return jnp.flip(t, axis=2)


def zero_state(b):
    return (jnp.zeros((b, N_HEADS, DQK, DV), jnp.float32),
            jnp.zeros((b, N_HEADS, DQK), jnp.float32),
            jnp.zeros((b, N_HEADS), jnp.float32))


def mlstm_kvg(p, conv_k_w, b_gates):
    B_, T, _ = p.shape
    k = short_conv(p[..., K_OFF:V_OFF], conv_k_w) * (DQK ** -0.5)
    v = p[..., V_OFF:G_OFF]
    g = (p[..., G_OFF:Q_OFF] + b_gates).astype(jnp.float32)
    g = g.reshape(B_, T, N_DIRS, 2, N_HEADS).transpose(2, 3, 0, 4, 1)
    li = g[:, 0]
    lf = jax.nn.log_sigmoid(g[:, 1])
    return split_heads(k, DQK), split_heads(v, DV), li, lf


def mlstm_q(p, conv_q_w):
    return split_heads(short_conv(p[..., Q_OFF:O_OFF], conv_q_w), DQK)


def mlstm_chunkwise(k, v, li, lf, state0, q=None):
    B_, H, T, _ = k.shape
    nc = T // CHUNK
    kc = k.astype(jnp.float32).reshape(B_, H, nc, CHUNK, DQK)
    vc = v.astype(jnp.float32).reshape(B_, H, nc, CHUNK, DV)
    lic = li.reshape(B_, H, nc, CHUNK)
    b = jnp.cumsum(lf.reshape(B_, H, nc, CHUNK), -1)
    b_end = b[..., -1]
    a = b_end[..., None] - b + lic
    m_loc = jnp.max(a, -1)
    w = jnp.exp(a - m_loc[..., None])
    C_loc = jnp.einsum('bhnl,bhnlk,bhnlv->bhnkv', w, kc, vc)
    n_loc = jnp.einsum('bhnl,bhnlk->bhnk', w, kc)

    def step(state, inp):
        C, n, m = state
        Cl, nl, ml, be = inp
        m_new = jnp.maximum(be + m, ml)
        sp = jnp.exp(be + m - m_new)
        sl = jnp.exp(ml - m_new)
        C_new = sp[..., None, None] * C + sl[..., None, None] * Cl
        n_new = sp[..., None] * n + sl[..., None] * nl
        return (C_new, n_new, m_new), (C, n, m)

    xs = (jnp.moveaxis(C_loc, 2, 0), jnp.moveaxis(n_loc, 2, 0),
          jnp.moveaxis(m_loc, 2, 0), jnp.moveaxis(b_end, 2, 0))
    final, prev = lax.scan(step, state0, xs)
    if q is None:
        return None, final
    Cp, npv, mp = [jnp.moveaxis(s, 0, 2) for s in prev]
    qc = q.astype(jnp.float32).reshape(B_, H, nc, CHUNK, DQK)
    lower = jnp.tril(jnp.ones((CHUNK, CHUNK), bool))
    d = b[..., :, None] - b[..., None, :] + lic[..., None, :]
    d = jnp.where(lower, d, -jnp.inf)
    g = b + mp[..., None]
    m_out = jnp.maximum(g, jnp.max(d, -1))
    s = jnp.einsum('bhntk,bhnsk->bhnts', qc, kc) * jnp.exp(d - m_out[..., None])
    wi = jnp.exp(g - m_out)
    num = (jnp.einsum('bhnts,bhnsv->bhntv', s, vc)
           + wi[..., None] * jnp.einsum('bhntk,bhnkv->bhntv', qc, Cp))
    den = jnp.sum(s, -1) + wi * jnp.einsum('bhntk,bhnk->bhnt', qc, npv)
    h = num / jnp.maximum(jnp.abs(den), jnp.exp(-m_out))[..., None]
    return h.reshape(B_, H, T, DV), final


def mlstm_scans(k, v, li, lf, init, q=None):
    h_f, fin_f = mlstm_chunkwise(k, v, li[0], lf[0], init[0], q)
    h_b, fin_b = mlstm_chunkwise(_rev(k), _rev(v), _rev(li[1]), _rev(lf[1]), init[1],
                                 None if q is None else _rev(q))
    h = None if q is None else h_f + _rev(h_b)
    return h, (fin_f, fin_b)


def mlstm_out(p, h, head_g):
    B_, H, T, _ = h.shape
    h = h * lax.rsqrt(jnp.mean(h * h, -1, keepdims=True) + EPS)
    h = h.transpose(0, 2, 1, 3).reshape(B_, T, MLSTM_WIDTH).astype(p.dtype) * head_g
    return h * jax.nn.sigmoid(p[..., O_OFF:P_OFF])


def box_sum(x, w, axis):
    n = x.shape[axis]
    P = jnp.cumsum(x.astype(jnp.float32), axis)
    P = jnp.concatenate([jnp.zeros_like(lax.slice_in_dim(P, 0, 1, axis=axis)), P], axis)
    t = jnp.arange(n)
    lo = jnp.clip(t - w // 2, 0, n)
    hi = jnp.clip(t + w // 2, 0, n)
    s = jnp.take(P, hi, axis=axis) - jnp.take(P, lo, axis=axis)
    return s, (hi - lo).astype(jnp.float32)


def pool_grid(u, w):
    B_, T, C = u.shape
    rows = T // GRID_W
    g = u.reshape(B_, rows, GRID_W, C)
    s, cc = box_sum(g, w, 2)
    s, cr = box_sum(s, w, 1)
    mean = s / (cr[:, None, None] * cc[None, :, None])
    return (mean.reshape(B_, T, C) - u.astype(jnp.float32)).astype(u.dtype)


def pool_seq(u, w):
    s, cnt = box_sum(u, w, 1)
    return (s / cnt[:, None] - u.astype(jnp.float32)).astype(u.dtype)


def pool_mix(p, pool_w, pool_scale, grid):
    u = p[..., P_OFF:IN_COLS]
    outs = []
    for gi, w in enumerate(POOL_WINDOWS):
        ug = u[..., gi * POOL_GC:(gi + 1) * POOL_GC]
        d = pool_grid(ug, w) if grid else pool_seq(ug, w)
        outs.append(d @ pool_w[gi])
    return jnp.concatenate(outs, -1) * pool_scale


def expert_choice_ffn(h, w_router, w_gate, w_up, w_down):
    B_, T, D = h.shape
    cap = EC_FACTOR * T // N_EXPERTS
    aff = jax.nn.softmax(jnp.einsum('btd,de->bte', h, w_router).astype(jnp.float32), -1)
    gate, idx = lax.top_k(jnp.swapaxes(aff, 1, 2), cap)
    xs = jax.vmap(lambda hb, ib: hb[ib])(h, idx)
    a = (jax.nn.silu(jnp.einsum('becd,edf->becf', xs, w_gate))
         * jnp.einsum('becd,edf->becf', xs, w_up))
    y = jnp.einsum('becf,efd->becd', a, w_down) * gate[..., None].astype(h.dtype)
    return jax.vmap(lambda yb, ib: jnp.zeros((T, D), yb.dtype).at[ib.reshape(-1)].add(
        yb.reshape(-1, D)))(y, idx)


def setup_inputs(seed: int = 0) -> dict:
    key = jax.random.key(seed)
    ks = jax.random.split(key, 24)
    nrm = jax.random.normal
    L = DEPTH
    base_gate = jnp.asarray(np.tile(np.repeat(np.array([0.0, FORGET_BIAS], np.float32), N_HEADS), N_DIRS))
    return {
        "x": nrm(ks[0], (BATCH, SEQ, D_MODEL), jnp.float32),
        "c": nrm(ks[1], (BATCH, D_MODEL), jnp.float32),
        "ctx": nrm(ks[2], (BATCH, CTX_LEN, D_MODEL), jnp.float32),
        "c_ctx": nrm(ks[3], (D_MODEL,), jnp.float32),
        "w_mod": nrm(ks[4], (L, D_MODEL, N_MOD * D_MODEL), jnp.float32) * (0.5 * D_MODEL ** -0.5),
        "b_mod": nrm(ks[5], (L, N_MOD * D_MODEL), jnp.float32) * 0.02,
        "norm1": 1.0 + 0.1 * nrm(ks[6], (L, D_MODEL), jnp.float32),
        "w_in": nrm(ks[7], (L, D_MODEL, IN_COLS), jnp.float32) * D_MODEL ** -0.5,
        "conv_q_w": nrm(ks[8], (L, QK_CONV, QK_WIDTH), jnp.float32) * QK_CONV ** -0.5,
        "conv_k_w": nrm(ks[9], (L, QK_CONV, QK_WIDTH), jnp.float32) * QK_CONV ** -0.5,
        "b_gates": base_gate + 0.3 * nrm(ks[10], (L, GATE_COLS), jnp.float32),
        "head_g": 1.0 + 0.1 * nrm(ks[11], (L, MLSTM_WIDTH), jnp.float32),
        "pool_w": nrm(ks[12], (L, len(POOL_WINDOWS), POOL_GC, POOL_GC), jnp.float32) * POOL_GC ** -0.5,
        "pool_scale": 1.0 + 0.1 * nrm(ks[13], (L, POOL_WIDTH), jnp.float32),
        "w_out": nrm(ks[14], (L, MIX_WIDTH, D_MODEL), jnp.float32) * MIX_WIDTH ** -0.5,
        "norm2": 1.0 + 0.1 * nrm(ks[15], (L, D_MODEL), jnp.float32),
        "w_router": nrm(ks[16], (L, D_MODEL, N_EXPERTS), jnp.float32) * D_MODEL ** -0.5,
        "w_gate": nrm(ks[17], (L, N_EXPERTS, D_MODEL, D_EXPERT), jnp.float32) * D_MODEL ** -0.5,
        "w_up": nrm(ks[18], (L, N_EXPERTS, D_MODEL, D_EXPERT), jnp.float32) * D_MODEL ** -0.5,
        "w_down": nrm(ks[19], (L, N_EXPERTS, D_EXPERT, D_MODEL), jnp.float32) * D_EXPERT ** -0.5,
        "norm_f": 1.0 + 0.1 * nrm(ks[20], (D_MODEL,), jnp.float32),
    }


def reference(x, c, ctx, c_ctx, w_mod, b_mod, norm1, w_in, conv_q_w, conv_k_w, b_gates,
              head_g, pool_w, pool_scale, w_out, norm2, w_router, w_gate, w_up, w_down, norm_f):
    B_ = x.shape[0]
    xc = ctx
    for l in range(DEPTH):
        update_ctx = l < DEPTH - 1
        mod = (jax.nn.silu(c) @ w_mod[l] + b_mod[l]).reshape(B_, N_MOD, 1, D_MODEL)
        mod_c = (jax.nn.silu(c_ctx) @ w_mod[l] + b_mod[l]).reshape(N_MOD, D_MODEL)

        hc = rmsnorm(xc, norm1[l]) * (1.0 + mod_c[1]) + mod_c[0]
        pc = hc @ (w_in[l] if update_ctx else w_in[l][:, :Q_OFF])
        kc, vc, lic, lfc = mlstm_kvg(pc, conv_k_w[l], b_gates[l])
        qc = mlstm_q(pc, conv_q_w[l]) if update_ctx else None
        zs = zero_state(B_)
        hm_c, ctx_states = mlstm_scans(kc, vc, lic, lfc, (zs, zs), qc)

        h = rmsnorm(x, norm1[l]) * (1.0 + mod[:, 1]) + mod[:, 0]
        p = h @ w_in[l]
        k, v, li, lf = mlstm_kvg(p, conv_k_w[l], b_gates[l])
        hm, _ = mlstm_scans(k, v, li, lf, ctx_states, mlstm_q(p, conv_q_w[l]))
        mix = jnp.concatenate([mlstm_out(p, hm, head_g[l]),
                               pool_mix(p, pool_w[l], pool_scale[l], True)], -1) @ w_out[l]
        x = x + mod[:, 2] * mix
        if update_ctx:
            mix_c = jnp.concatenate([mlstm_out(pc, hm_c, head_g[l]),
                                     pool_mix(pc, pool_w[l], pool_scale[l], False)], -1) @ w_out[l]
            xc = xc + mod_c[2] * mix_c

        h2 = rmsnorm(x, norm2[l]) * (1.0 + mod[:, 4]) + mod[:, 3]
        x = x + mod[:, 5] * expert_choice_ffn(h2, w_router[l], w_gate[l], w_up[l], w_down[l])
        if update_ctx:
            h2c = rmsnorm(xc, norm2[l]) * (1.0 + mod_c[4]) + mod_c[3]
            xc = xc + mod_c[5] * expert_choice_ffn(h2c, w_router[l], w_gate[l], w_up[l], w_down[l])
    return rmsnorm(x, norm_f)
```

```python
import functools

import jax
import jax.numpy as jnp
from jax import lax
from jax.experimental import pallas as pl
from jax.experimental.pallas import tpu as pltpu

F32 = jnp.float32
BF16 = jnp.bfloat16

N_HEADS = 4
DQK = 128
DV = 256
QK_CONV = 5
POOL_WINDOWS = (2, 4, 8, 16)
POOL_GC = 256
GRID_W = 64
GRID_SHIFT = GRID_W.bit_length() - 1
N_EXPERTS = 16
EC_FACTOR = 2
N_MOD = 6
EPS = 1e-6

LANES = 128
SUBLANES = 8
MLSTM_CHUNK = 128
VMEM_LIMIT = 56 * 1024 * 1024


def _cparams(sem):
    return pltpu.CompilerParams(dimension_semantics=sem, vmem_limit_bytes=VMEM_LIMIT)


def _sigmoid(x):
    return 1.0 / (1.0 + jnp.exp(-x))


def _silu(x):
    return x * _sigmoid(x)


def _log_sigmoid(x):
    return jnp.minimum(x, 0.0) - jnp.log(1.0 + jnp.exp(-jnp.abs(x)))


def _mod_kernel(c_ref, w_ref, b_ref, o_ref):
    s = _silu(c_ref[...]).astype(BF16)
    o_ref[...] = jnp.dot(s, w_ref[...].astype(BF16), preferred_element_type=F32) + b_ref[...]


def _modulation(cs, w_mod, b_mod):
    rows, d = cs.shape
    n = w_mod.shape[1]
    tn = 1024
    return pl.pallas_call(
        _mod_kernel,
        grid=(n // tn,),
        in_specs=[pl.BlockSpec((rows, d), lambda j: (0, 0)),
                  pl.BlockSpec((d, tn), lambda j: (0, j)),
                  pl.BlockSpec((1, tn), lambda j: (0, j))],
        out_specs=pl.BlockSpec((rows, tn), lambda j: (0, j)),
        out_shape=jax.ShapeDtypeStruct((rows, n), F32),
        compiler_params=_cparams(("arbitrary",)),
    )(cs, w_mod, b_mod.reshape(1, n))


def _inproj_kernel(x_ref, mod_ref, n1_ref, w_ref, wg_ref, bg_ref, p_ref, g_ref, h_scr):
    @pl.when(pl.program_id(1) == 0)
    def _():
        x = x_ref[...]
        y = x * lax.rsqrt(jnp.mean(x * x, axis=-1, keepdims=True) + EPS)
        h = (y * n1_ref[...]) * (1.0 + mod_ref[0, 1:2, :]) + mod_ref[0, 0:1, :]
        hb = h.astype(BF16)
        h_scr[...] = hb
        g_ref[...] = jnp.dot(hb, wg_ref[...], preferred_element_type=F32) + bg_ref[...]

    p_ref[...] = jnp.dot(h_scr[...], w_ref[...], preferred_element_type=F32).astype(BF16)


def _inproj(x2, mod, rows_per_mod, n1, w, wg, bg, n_cols, tm, tn):
    m, d = x2.shape
    tiles_per_mod = rows_per_mod // tm
    return pl.pallas_call(
        _inproj_kernel,
        grid=(m // tm, n_cols // tn),
        in_specs=[pl.BlockSpec((tm, d), lambda i, j: (i, 0)),
                  pl.BlockSpec((1, N_MOD, d), lambda i, j: (i // tiles_per_mod, 0, 0)),
                  pl.BlockSpec((1, d), lambda i, j: (0, 0)),
                  pl.BlockSpec((d, tn), lambda i, j: (0, j)),
                  pl.BlockSpec((d, LANES), lambda i, j: (0, 0)),
                  pl.BlockSpec((1, LANES), lambda i, j: (0, 0))],
        out_specs=[pl.BlockSpec((tm, tn), lambda i, j: (i, j)),
                   pl.BlockSpec((tm, LANES), lambda i, j: (i, 0))],
        out_shape=[jax.ShapeDtypeStruct((m, n_cols), BF16),
                   jax.ShapeDtypeStruct((m, LANES), F32)],
        scratch_shapes=[pltpu.VMEM((tm, d), BF16)],
        compiler_params=_cparams(("arbitrary", "arbitrary")),
    )(x2, mod, n1, w, wg, bg)


def _prefix_sum_lanes(x, reverse):
    lane = lax.broadcasted_iota(jnp.int32, x.shape, 1)
    k = 1
    while k < LANES:
        if reverse:
            x = x + jnp.where(lane < LANES - k, pltpu.roll(x, LANES - k, 1), 0.0)
        else:
            x = x + jnp.where(lane >= k, pltpu.roll(x, k, 1), 0.0)
        k *= 2
    return x


def _mlstm_kernel(kc_ref, vc_ref, gc_ref, k_ref, v_ref, q_ref, o_ref, g_ref, wk_ref, wq_ref, hg_ref,
                  out_ref,
                  cpad, kt, qt, kct, vext, vcext, gs, gsc, cf, cb, hf, hb):
    L = MLSTM_CHUNK
    t_lat = k_ref.shape[1]
    t_ctx = kc_ref.shape[1]
    nc = t_lat // L
    ncc = t_ctx // L
    pad = SUBLANES
    half = QK_CONV // 2

    def conv_silu(src, w_ref, dst, t, scale):
        cpad[0:pad, :] = jnp.zeros((pad, LANES), F32)
        cpad[pad:pad + t, :] = src.astype(F32)
        cpad[pad + t:pad + t + pad, :] = jnp.zeros((pad, LANES), F32)
        for r in range(t // L):
            base = pad + r * L - half
            acc = w_ref[0:1, :] * cpad[base:base + L, :]
            for j in range(1, QK_CONV):
                acc = acc + w_ref[j:j + 1, :] * cpad[base + j:base + j + L, :]
            y = _silu(acc)
            if scale is not None:
                y = y * scale
            dst[r * L:(r + 1) * L, :] = y.astype(BF16)

    conv_silu(kc_ref[0], wk_ref, kct, t_ctx, DQK ** -0.5)
    conv_silu(k_ref[0], wk_ref, kt, t_lat, DQK ** -0.5)
    conv_silu(q_ref[0], wq_ref, qt, t_lat, None)

    def fill_vext(dst, src, t):
        lane = lax.broadcasted_iota(jnp.int32, (t, LANES), 1)
        dst[:, 0:DV] = src
        dst[:, DV:DV + LANES] = jnp.where(lane == 0, 1.0, 0.0).astype(BF16)

    fill_vext(vcext, vc_ref[0], t_ctx)
    fill_vext(vext, v_ref[0], t_lat)

    def gate_prep(garr, dst):
        for d in range(2):
            li = garr[2 * d]
            lf = _log_sigmoid(garr[2 * d + 1])
            b = _prefix_sum_lanes(lf, reverse=(d == 1))
            b_end = b[:, LANES - 1:LANES] if d == 0 else b[:, 0:1]
            a = b_end - b + li
            m_loc = jnp.max(a, axis=1, keepdims=True)
            w = jnp.exp(a - m_loc)
            dst[6 * d + 0] = b
            dst[6 * d + 1] = li - b
            dst[6 * d + 2] = w
            dst[6 * d + 3] = jnp.broadcast_to(b_end, b.shape)
            dst[6 * d + 4] = jnp.broadcast_to(m_loc, b.shape)
            dst[6 * d + 5] = jnp.zeros_like(b)

    gate_prep(gc_ref[0, 0], gsc)
    gate_prep(g_ref[0, 0], gs)

    cf[...] = jnp.zeros(cf.shape, F32)
    cb[...] = jnp.zeros(cb.shape, F32)

    row_i = lax.broadcasted_iota(jnp.int32, (L, L), 0)
    col_i = lax.broadcasted_iota(jnp.int32, (L, L), 1)
    tri_f = col_i <= row_i
    tri_b = col_i >= row_i

    sub_i = lax.broadcasted_iota(jnp.int32, (SUBLANES, L), 0)

    def columns(rows):
        stack = jnp.zeros((SUBLANES, L), F32)
        for i, r in enumerate(rows):
            stack = jnp.where(sub_i == i, r, stack)
        sq = jnp.concatenate([stack] * (L // SUBLANES), axis=0)
        tr = sq.T
        return [tr[:, i:i + 1] for i in range(len(rows))]

    def step(c, d, g_scr, k_scr, v_scr, q_scr, c_scr, m_prev, b_col, w_col, h_scr):
        off = c * L if isinstance(c, int) else pl.multiple_of(c * L, L)
        k_c = k_scr[pl.ds(off, L), :]
        v_c = v_scr[pl.ds(off, L), :]
        r_row = g_scr[6 * d + 1, pl.ds(c, 1), :]
        b_end = g_scr[6 * d + 3, pl.ds(c, 1), 0:1]
        m_loc = g_scr[6 * d + 4, pl.ds(c, 1), 0:1]
        c_prev = c_scr[...]
        if q_scr is not None:
            q_c = q_scr[pl.ds(off, L), :]
            qk = lax.dot_general(q_c, k_c, (((1,), (1,)), ((), ())), preferred_element_type=F32)
            dmat = jnp.where(tri_f if d == 0 else tri_b, b_col + r_row, -jnp.inf)
            g_col = b_col + m_prev
            m_out = jnp.maximum(g_col, jnp.max(dmat, axis=1, keepdims=True))
            s = (qk * jnp.exp(dmat - m_out)).astype(BF16)
            wi = jnp.exp(g_col - m_out)
            inter = jnp.dot(q_c, c_prev.astype(BF16), preferred_element_type=F32)
            res = jnp.dot(s, v_c, preferred_element_type=F32) + wi * inter
            den = res[:, DV:DV + 1]
            inv = 1.0 / jnp.maximum(jnp.abs(den), jnp.exp(-m_out))
            h_scr[pl.ds(off, L), :] = res[:, 0:DV] * inv
        kw = (k_c.astype(F32) * w_col).astype(BF16)
        c_loc = lax.dot_general(kw, v_c, (((0,), (0,)), ((), ())), preferred_element_type=F32)
        m_new = jnp.maximum(b_end + m_prev, m_loc)
        sp = jnp.exp(b_end + m_prev - m_new)
        sl = jnp.exp(m_loc - m_new)
        c_scr[...] = sp * c_prev + sl * c_loc
        return m_new

    def both_dirs(i, n_chunks, g_scr, k_scr, v_scr, q_scr, m_f, m_b):
        c_f = i
        c_b = n_chunks - 1 - i
        cols = columns([g_scr[0, pl.ds(c_f, 1), :], g_scr[2, pl.ds(c_f, 1), :],
                        g_scr[6, pl.ds(c_b, 1), :], g_scr[8, pl.ds(c_b, 1), :]])
        m_f = step(c_f, 0, g_scr, k_scr, v_scr, q_scr, cf, m_f, cols[0], cols[1], hf)
        m_b = step(c_b, 1, g_scr, k_scr, v_scr, q_scr, cb, m_b, cols[2], cols[3], hb)
        return m_f, m_b

    m_f = jnp.zeros((1, 1), F32)
    m_b = jnp.zeros((1, 1), F32)
    for i in range(ncc):
        m_f, m_b = both_dirs(i, ncc, gsc, kct, vcext, None, m_f, m_b)

    def body(i, carry):
        return both_dirs(i, nc, gs, kt, vext, qt, carry[0], carry[1])

    lax.fori_loop(0, nc, body, (m_f, m_b))

    def finish(r, carry):
        off = pl.multiple_of(r * L, L)
        h = hf[pl.ds(off, L), :] + hb[pl.ds(off, L), :]
        h = h * lax.rsqrt(jnp.mean(h * h, axis=-1, keepdims=True) + EPS)
        y = (h * hg_ref[...]) * _sigmoid(o_ref[0, pl.ds(off, L), :].astype(F32))
        out_ref[0, pl.ds(off, L), :] = y.astype(BF16)
        return carry

    lax.fori_loop(0, nc, finish, 0)


def _mlstm(p, pc, g4, gc4, wk, wq, head_g):
    b_, t, _ = p.shape
    tc = pc.shape[1]
    L = MLSTM_CHUNK
    nc, ncc = t // L, tc // L
    ext = DV + LANES
    kb = DQK // LANES
    return pl.pallas_call(
        _mlstm_kernel,
        grid=(b_, N_HEADS),
        in_specs=[
            pl.BlockSpec((1, tc, DQK), lambda b, h: (b, 0, h)),
            pl.BlockSpec((1, tc, DV), lambda b, h: (b, 0, 2 + h)),
            pl.BlockSpec((1, 1, 4, ncc, L), lambda b, h: (b, h, 0, 0, 0)),
            pl.BlockSpec((1, t, DQK), lambda b, h: (b, 0, h)),
            pl.BlockSpec((1, t, DV), lambda b, h: (b, 0, 2 + h)),
            pl.BlockSpec((1, t, DQK), lambda b, h: (b, 0, 12 + h)),
            pl.BlockSpec((1, t, DV), lambda b, h: (b, 0, 8 + h)),
            pl.BlockSpec((1, 1, 4, nc, L), lambda b, h: (b, h, 0, 0, 0)),
            pl.BlockSpec((SUBLANES, DQK), lambda b, h: (0, h)),
            pl.BlockSpec((SUBLANES, DQK), lambda b, h: (0, h)),
            pl.BlockSpec((1, DV), lambda b, h: (0, h)),
        ],
        out_specs=pl.BlockSpec((1, t, DV), lambda b, h: (b, 0, h)),
        out_shape=jax.ShapeDtypeStruct((b_, t, N_HEADS * DV), BF16),
        scratch_shapes=[
            pltpu.VMEM((t + 2 * SUBLANES, LANES), F32),
            pltpu.VMEM((t, DQK), BF16),
            pltpu.VMEM((t, DQK), BF16),
            pltpu.VMEM((tc, DQK), BF16),
            pltpu.VMEM((t, ext), BF16),
            pltpu.VMEM((tc, ext), BF16),
            pltpu.VMEM((12, nc, L), F32),
            pltpu.VMEM((12, ncc, L), F32),
            pltpu.VMEM((DQK, ext), F32),
            pltpu.VMEM((DQK, ext), F32),
            pltpu.VMEM((t, DV), F32),
            pltpu.VMEM((t, DV), F32),
        ],
        compiler_params=_cparams(("arbitrary", "arbitrary")),
    )(pc, pc, gc4, p, p, p, p, g4, wk, wq, head_g)


def _pool_kernel(u_ref, pw_ref, ps_ref, out_ref, spad):
    t = u_ref.shape[1]
    rows = t // GRID_W
    blk = 256
    halo = (max(POOL_WINDOWS) // 2) * GRID_W
    spad[0:halo, :] = jnp.zeros((halo, POOL_GC), F32)
    spad[halo + t:halo + t + halo, :] = jnp.zeros((halo, POOL_GC), F32)
    ti = lax.broadcasted_iota(jnp.int32, (blk, blk), 0)
    tj = lax.broadcasted_iota(jnp.int32, (blk, blk), 1)
    same_row = (ti >> GRID_SHIFT) == (tj >> GRID_SHIFT)
    diff = tj - ti
    tok = lax.broadcasted_iota(jnp.int32, (blk, 1), 0)
    for gi, w in enumerate(POOL_WINDOWS):
        hw = w // 2
        c0, c1 = gi * POOL_GC, (gi + 1) * POOL_GC
        band = jnp.where(same_row & (diff >= -hw) & (diff <= hw - 1), 1.0, 0.0).astype(BF16)
        for r in range(t // blk):
            spad[halo + r * blk:halo + (r + 1) * blk, :] = jnp.dot(
                band, u_ref[0, r * blk:(r + 1) * blk, c0:c1], preferred_element_type=F32)
        for r in range(t // blk):
            base = halo + r * blk
            acc = spad[base - hw * GRID_W:base - hw * GRID_W + blk, :]
            for j in range(-hw + 1, hw):
                acc = acc + spad[base + j * GRID_W:base + j * GRID_W + blk, :]
            tt = tok + r * blk
            gr = tt >> GRID_SHIFT
            gc = tt & (GRID_W - 1)
            cr = jnp.minimum(gr + hw, rows) - jnp.maximum(gr - hw, 0)
            cc = jnp.minimum(gc + hw, GRID_W) - jnp.maximum(gc - hw, 0)
            cnt = (cr * cc).astype(F32)
            d = acc / cnt - u_ref[0, r * blk:(r + 1) * blk, c0:c1].astype(F32)
            y = jnp.dot(d.astype(BF16), pw_ref[gi], preferred_element_type=F32) * ps_ref[:, c0:c1]
            out_ref[0, r * blk:(r + 1) * blk, c0:c1] = y.astype(BF16)


def _pool(p, pool_w, pool_scale):
    b_, t, _ = p.shape
    pw = len(POOL_WINDOWS) * POOL_GC
    halo = (max(POOL_WINDOWS) // 2) * GRID_W
    return pl.pallas_call(
        _pool_kernel,
        grid=(b_,),
        in_specs=[pl.BlockSpec((1, t, pw), lambda b: (b, 0, 3)),
                  pl.BlockSpec((len(POOL_WINDOWS), POOL_GC, POOL_GC), lambda b: (0, 0, 0)),
                  pl.BlockSpec((1, pw), lambda b: (0, 0))],
        out_specs=pl.BlockSpec((1, t, pw), lambda b: (b, 0, 0)),
        out_shape=jax.ShapeDtypeStruct((b_, t, pw), BF16),
        scratch_shapes=[pltpu.VMEM((t + 2 * halo, POOL_GC), F32)],
        compiler_params=_cparams(("arbitrary",)),
    )(p, pool_w, pool_scale)


def _outproj_kernel(ym_ref, yp_ref, wm_ref, wp_ref, x_ref, mod_ref, n2_ref, wr_ref,
                    x1_ref, h2_ref, lg_ref):
    mix = (jnp.dot(ym_ref[...], wm_ref[...], preferred_element_type=F32)
           + jnp.dot(yp_ref[...], wp_ref[...], preferred_element_type=F32))
    x1 = x_ref[...] + mod_ref[0, 2:3, :] * mix
    x1_ref[...] = x1
    y = x1 * lax.rsqrt(jnp.mean(x1 * x1, axis=-1, keepdims=True) + EPS)
    h2 = ((y * n2_ref[...]) * (1.0 + mod_ref[0, 4:5, :]) + mod_ref[0, 3:4, :]).astype(BF16)
    h2_ref[...] = h2
    lg_ref[...] = jnp.dot(h2, wr_ref[...], preferred_element_type=F32)


def _outproj(ym, yp, wm, wp, x2, mod, rows_per_mod, n2, wr, tm):
    m, d = x2.shape
    km = ym.shape[1]
    kp = yp.shape[1]
    tiles_per_mod = rows_per_mod // tm
    return pl.pallas_call(
        _outproj_kernel,
        grid=(m // tm,),
        in_specs=[pl.BlockSpec((tm, km), lambda i: (i, 0)),
                  pl.BlockSpec((tm, kp), lambda i: (i, 0)),
                  pl.BlockSpec((km, d), lambda i: (0, 0)),
                  pl.BlockSpec((kp, d), lambda i: (0, 0)),
                  pl.BlockSpec((tm, d), lambda i: (i, 0)),
                  pl.BlockSpec((1, N_MOD, d), lambda i: (i // tiles_per_mod, 0, 0)),
                  pl.BlockSpec((1, d), lambda i: (0, 0)),
                  pl.BlockSpec((d, LANES), lambda i: (0, 0))],
        out_specs=[pl.BlockSpec((tm, d), lambda i: (i, 0)),
                   pl.BlockSpec((tm, d), lambda i: (i, 0)),
                   pl.BlockSpec((tm, LANES), lambda i: (i, 0))],
        out_shape=[jax.ShapeDtypeStruct((m, d), F32),
                   jax.ShapeDtypeStruct((m, d), BF16),
                   jax.ShapeDtypeStruct((m, LANES), F32)],
        compiler_params=_cparams(("arbitrary",)),
    )(ym, yp, wm, wp, x2, mod, n2, wr)


def _scan_lanes_i32(x, exclusive_of=None):
    n = x.shape[1]
    lane = lax.broadcasted_iota(jnp.int32, x.shape, 1)
    k = 1
    while k < n:
        x = x + jnp.where(lane >= k, pltpu.roll(x, k, 1), 0)
        k *= 2
    return x


def _route_kernel(lg_ref, idx_ref, gate_ref, afft):
    t = lg_ref.shape[1]
    cap = idx_ref.shape[2]
    lane = lax.broadcasted_iota(jnp.int32, (t, LANES), 1)
    lg = jnp.where(lane < N_EXPERTS, lg_ref[0], -jnp.inf)
    ex = jnp.exp(lg - jnp.max(lg, axis=1, keepdims=True))
    aff = ex / jnp.sum(ex, axis=1, keepdims=True)
    for r in range(t // LANES):
        afft[:, r * LANES:(r + 1) * LANES] = aff[r * LANES:(r + 1) * LANES, :].T
    bits = lax.bitcast_convert_type(afft[0:N_EXPERTS, :], jnp.int32)

    def count(mask):
        return jnp.sum(jnp.where(mask, 1.0, 0.0), axis=1, keepdims=True).astype(jnp.int32)

    def bit_step(i, thr):
        cand = thr | (jnp.int32(1) << (30 - i))
        return jnp.where(count(bits >= cand) >= cap, cand, thr)

    thr = lax.fori_loop(0, 31, bit_step, jnp.zeros((N_EXPERTS, 1), jnp.int32))
    gt = bits > thr
    eq = bits == thr
    need = cap - count(gt)
    eq_i = jnp.where(eq, 1, 0)
    eq_rank = _scan_lanes_i32(eq_i) - eq_i
    sel = gt | (eq & (eq_rank < need))
    sel_i = jnp.where(sel, 1, 0)
    pos = _scan_lanes_i32(sel_i) - sel_i
    key = jnp.where(sel, pos, -1)

    a_hi = aff.astype(BF16).astype(F32)
    a_mid = (aff - a_hi).astype(BF16).astype(F32)
    a_lo = (aff - a_hi - a_mid).astype(BF16).astype(F32)
    tok = lax.broadcasted_iota(jnp.int32, (t, LANES), 0)
    vals = (a_hi + pltpu.roll(a_mid, N_EXPERTS, 1) + pltpu.roll(a_lo, 2 * N_EXPERTS, 1)
            + jnp.where(lane == 3 * N_EXPERTS, (tok >> GRID_SHIFT).astype(F32), 0.0)
            + jnp.where(lane == 3 * N_EXPERTS + 1, (tok & (GRID_W - 1)).astype(F32), 0.0)).astype(BF16)
    slot = lax.broadcasted_iota(jnp.int32, (cap, 1), 0)
    for e in range(N_EXPERTS):
        onehot = jnp.where(slot == key[e:e + 1, :], 1.0, 0.0).astype(BF16)
        res = jnp.dot(onehot, vals, preferred_element_type=F32)
        rt = jnp.concatenate([res[r * LANES:(r + 1) * LANES, :].T for r in range(cap // LANES)], axis=1)
        gate_ref[0, e:e + 1, :] = (rt[e:e + 1, :] + rt[N_EXPERTS + e:N_EXPERTS + e + 1, :]) \
            + rt[2 * N_EXPERTS + e:2 * N_EXPERTS + e + 1, :]
        tokf = rt[3 * N_EXPERTS:3 * N_EXPERTS + 1, :] * float(GRID_W) + rt[3 * N_EXPERTS + 1:3 * N_EXPERTS + 2, :]
        idx_ref[0, e:e + 1, :] = tokf.astype(jnp.int32)


def _route(lg3, cap):
    b_, t, _ = lg3.shape
    return pl.pallas_call(
        _route_kernel,
        grid=(b_,),
        in_specs=[pl.BlockSpec((1, t, LANES), lambda b: (b, 0, 0))],
        out_specs=[pl.BlockSpec((1, N_EXPERTS, cap), lambda b: (b, 0, 0)),
                   pl.BlockSpec((1, N_EXPERTS, cap), lambda b: (b, 0, 0))],
        out_shape=[jax.ShapeDtypeStruct((b_, N_EXPERTS, cap), jnp.int32),
                   jax.ShapeDtypeStruct((b_, N_EXPERTS, cap), F32)],
        scratch_shapes=[pltpu.VMEM((LANES, t), F32)],
        compiler_params=_cparams(("arbitrary",)),
    )(lg3)


def _ffn_kernel(xs_ref, gate_ref, wg_ref, wu_ref, wd_ref, y_ref):
    xs = xs_ref[0, 0]
    a = _silu(jnp.dot(xs, wg_ref[0], preferred_element_type=F32)) * jnp.dot(xs, wu_ref[0], preferred_element_type=F32)
    y = jnp.dot(a.astype(BF16), wd_ref[0], preferred_element_type=F32)
    y_ref[0, 0] = y * gate_ref[0, 0]


def _ffn(xs, gate4, wg, wu, wd):
    b_, e_, cap, d = xs.shape
    f = wg.shape[2]
    return pl.pallas_call(
        _ffn_kernel,
        grid=(e_, b_),
        in_specs=[pl.BlockSpec((1, 1, cap, d), lambda e, b: (b, e, 0, 0)),
                  pl.BlockSpec((1, 1, cap, 1), lambda e, b: (b, e, 0, 0)),
                  pl.BlockSpec((1, d, f), lambda e, b: (e, 0, 0)),
                  pl.BlockSpec((1, d, f), lambda e, b: (e, 0, 0)),
                  pl.BlockSpec((1, f, d), lambda e, b: (e, 0, 0))],
        out_specs=pl.BlockSpec((1, 1, cap, d), lambda e, b: (b, e, 0, 0)),
        out_shape=jax.ShapeDtypeStruct((b_, e_, cap, d), F32),
        compiler_params=_cparams(("arbitrary", "arbitrary")),
    )(xs, gate4, wg, wu, wd)


def _final_kernel(x1_ref, moe_ref, mod_ref, nf_ref, o_ref):
    x = x1_ref[...] + mod_ref[0, 5:6, :] * moe_ref[...]
    o_ref[...] = (x * lax.rsqrt(jnp.mean(x * x, axis=-1, keepdims=True) + EPS)) * nf_ref[...]


def _final(x1, moe, mod, rows_per_mod, nf, tm):
    m, d = x1.shape
    tiles_per_mod = rows_per_mod // tm
    return pl.pallas_call(
        _final_kernel,
        grid=(m // tm,),
        in_specs=[pl.BlockSpec((tm, d), lambda i: (i, 0)),
                  pl.BlockSpec((tm, d), lambda i: (i, 0)),
                  pl.BlockSpec((1, N_MOD, d), lambda i: (i // tiles_per_mod, 0, 0)),
                  pl.BlockSpec((1, d), lambda i: (0, 0))],
        out_specs=pl.BlockSpec((tm, d), lambda i: (i, 0)),
        out_shape=jax.ShapeDtypeStruct((m, d), F32),
        compiler_params=_cparams(("arbitrary",)),
    )(x1, moe, mod, nf)


def _gate_layout(g, b_, t):
    g = g[:, :2 * 2 * N_HEADS].reshape(b_, t, 2, 2, N_HEADS)
    g = g.transpose(0, 4, 2, 3, 1)
    return g.reshape(b_, N_HEADS, 4, t // MLSTM_CHUNK, MLSTM_CHUNK)


def kernel(x, c, ctx, c_ctx, w_mod, b_mod, norm1, w_in, conv_q_w, conv_k_w, b_gates, head_g, pool_w,
           pool_scale, w_out, norm2, w_router, w_gate, w_up, w_down, norm_f):
    b_, t, d = x.shape
    tc = ctx.shape[1]
    depth = w_mod.shape[0]
    assert depth == 1
    l = 0
    qk_w = N_HEADS * DQK
    mw = N_HEADS * DV
    pool_wd = len(POOL_WINDOWS) * POOL_GC
    n_gate = 2 * 2 * N_HEADS
    k_off, v_off = 0, qk_w
    g_off = v_off + mw
    q_off = g_off + n_gate
    o_off = q_off + qk_w
    p_off = o_off + mw
    cap = EC_FACTOR * t // N_EXPERTS

    mod_rows = -(-(b_ + 1) // SUBLANES) * SUBLANES
    cs = jnp.zeros((mod_rows, d), F32).at[:b_].set(c).at[b_].set(c_ctx)
    mod_all = _modulation(cs, w_mod[l], b_mod[l])
    mod = mod_all[:b_].reshape(b_, N_MOD, d)
    mod_c = mod_all[b_:b_ + 1].reshape(1, N_MOD, d)

    wl = w_in[l]
    w_main = jnp.concatenate([wl[:, k_off:v_off], wl[:, v_off:g_off], wl[:, q_off:o_off],
                              wl[:, o_off:p_off], wl[:, p_off:]], axis=1).astype(BF16)
    w_g = jnp.pad(wl[:, g_off:q_off], ((0, 0), (0, LANES - n_gate))).astype(BF16)
    b_g = jnp.pad(b_gates[l], (0, LANES - n_gate)).reshape(1, LANES)
    n1 = norm1[l].reshape(1, d)

    p, g = _inproj(x.reshape(b_ * t, d), mod, t, n1, w_main, w_g, b_g, w_main.shape[1], 1024, 512)
    pc, gc = _inproj(ctx.reshape(b_ * tc, d), mod_c, b_ * tc, n1, w_main, w_g, b_g, qk_w + mw,
                     min(1024, b_ * tc), 512)
    p = p.reshape(b_, t, -1)
    pc = pc.reshape(b_, tc, -1)

    wk = jnp.pad(conv_k_w[l], ((0, SUBLANES - QK_CONV), (0, 0)))
    wq = jnp.pad(conv_q_w[l], ((0, SUBLANES - QK_CONV), (0, 0)))
    ym = _mlstm(p, pc, _gate_layout(g, b_, t), _gate_layout(gc, b_, tc), wk, wq, head_g[l].reshape(1, mw))
    yp = _pool(p, pool_w[l].astype(BF16), pool_scale[l].reshape(1, pool_wd))

    wo = w_out[l].astype(BF16)
    wr = jnp.pad(w_router[l], ((0, 0), (0, LANES - N_EXPERTS))).astype(BF16)
    x1, h2, lg = _outproj(ym.reshape(b_ * t, mw), yp.reshape(b_ * t, pool_wd), wo[:mw], wo[mw:],
                          x.reshape(b_ * t, d), mod, t, norm2[l].reshape(1, d), wr, 512)

    idx, gate = _route(lg.reshape(b_, t, LANES), cap)

    h2 = h2.reshape(b_, t, d)
    xs = jax.vmap(lambda hb, ib: hb[ib])(h2, idx)
    y = _ffn(xs, gate.reshape(b_, N_EXPERTS, cap, 1), w_gate[l].astype(BF16), w_up[l].astype(BF16),
             w_down[l].astype(BF16))
    moe = jax.vmap(lambda yb, ib: jnp.zeros((t, d), F32).at[ib.reshape(-1)].add(yb.reshape(-1, d)))(y, idx)

    out = _final(x1, moe.reshape(b_ * t, d), mod, t, norm_f.reshape(1, d), 512)
    return out.reshape(b_, t, d)
```

```python
import functools

import jax
import jax.numpy as jnp
from jax import lax
from jax.experimental import pallas as pl
from jax.experimental.pallas import tpu as pltpu

F32 = jnp.float32
BF16 = jnp.bfloat16

N_HEADS = 4
DQK = 128
DV = 256
QK_CONV = 5
POOL_WINDOWS = (2, 4, 8, 16)
POOL_GC = 256
GRID_W = 64
GRID_SHIFT = GRID_W.bit_length() - 1
N_EXPERTS = 16
EC_FACTOR = 2
N_MOD = 6
EPS = 1e-6

LANES = 128
SUBLANES = 8
MLSTM_CHUNK = 128
VMEM_LIMIT = 56 * 1024 * 1024


def _cparams(sem):
    return pltpu.CompilerParams(dimension_semantics=sem, vmem_limit_bytes=VMEM_LIMIT)


def _sigmoid(x):
    return 1.0 / (1.0 + jnp.exp(-x))


def _silu(x):
    return x * _sigmoid(x)


def _log_sigmoid(x):
    return jnp.minimum(x, 0.0) - jnp.log(1.0 + jnp.exp(-jnp.abs(x)))


def _mod_kernel(c_ref, w_ref, b_ref, o_ref):
    s = _silu(c_ref[...]).astype(BF16)
    o_ref[...] = jnp.dot(s, w_ref[...].astype(BF16), preferred_element_type=F32) + b_ref[...]


def _modulation(cs, w_mod, b_mod):
    rows, d = cs.shape
    n = w_mod.shape[1]
    tn = 1024
    return pl.pallas_call(
        _mod_kernel,
        grid=(n // tn,),
        in_specs=[pl.BlockSpec((rows, d), lambda j: (0, 0)),
                  pl.BlockSpec((d, tn), lambda j: (0, j)),
                  pl.BlockSpec((1, tn), lambda j: (0, j))],
        out_specs=pl.BlockSpec((rows, tn), lambda j: (0, j)),
        out_shape=jax.ShapeDtypeStruct((rows, n), F32),
        compiler_params=_cparams(("arbitrary",)),
    )(cs, w_mod, b_mod.reshape(1, n))


def _inproj_kernel(x_ref, mod_ref, n1_ref, w_ref, wg_ref, bg_ref, p_ref, g_ref, h_scr):
    @pl.when(pl.program_id(1) == 0)
    def _():
        x = x_ref[...]
        y = x * lax.rsqrt(jnp.mean(x * x, axis=-1, keepdims=True) + EPS)
        h = (y * n1_ref[...]) * (1.0 + mod_ref[0, 1:2, :]) + mod_ref[0, 0:1, :]
        hb = h.astype(BF16)
        h_scr[...] = hb
        g_ref[...] = jnp.dot(hb, wg_ref[...], preferred_element_type=F32) + bg_ref[...]

    p_ref[...] = jnp.dot(h_scr[...], w_ref[...], preferred_element_type=F32).astype(BF16)


def _inproj(x2, mod, rows_per_mod, n1, w, wg, bg, n_cols, tm, tn):
    m, d = x2.shape
    tiles_per_mod = rows_per_mod // tm
    return pl.pallas_call(
        _inproj_kernel,
        grid=(m // tm, n_cols // tn),
        in_specs=[pl.BlockSpec((tm, d), lambda i, j: (i, 0)),
                  pl.BlockSpec((1, N_MOD, d), lambda i, j: (i // tiles_per_mod, 0, 0)),
                  pl.BlockSpec((1, d), lambda i, j: (0, 0)),
                  pl.BlockSpec((d, tn), lambda i, j: (0, j)),
                  pl.BlockSpec((d, LANES), lambda i, j: (0, 0)),
                  pl.BlockSpec((1, LANES), lambda i, j: (0, 0))],
        out_specs=[pl.BlockSpec((tm, tn), lambda i, j: (i, j)),
                   pl.BlockSpec((tm, LANES), lambda i, j: (i, 0))],
        out_shape=[jax.ShapeDtypeStruct((m, n_cols), BF16),
                   jax.ShapeDtypeStruct((m, LANES), F32)],
        scratch_shapes=[pltpu.VMEM((tm, d), BF16)],
        compiler_params=_cparams(("arbitrary", "arbitrary")),
    )(x2, mod, n1, w, wg, bg)


def _prefix_sum_lanes(x, reverse):
    lane = lax.broadcasted_iota(jnp.int32, x.shape, 1)
    k = 1
    while k < LANES:
        if reverse:
            x = x + jnp.where(lane < LANES - k, pltpu.roll(x, LANES - k, 1), 0.0)
        else:
            x = x + jnp.where(lane >= k, pltpu.roll(x, k, 1), 0.0)
        k *= 2
    return x


def _mlstm_kernel(kc_ref, vc_ref, gc_ref, k_ref, v_ref, q_ref, o_ref, g_ref, wk_ref, wq_ref, hg_ref,
                  out_ref,
                  cpad, kt, qt, kct, vext, vcext, gs, gsc, cf, cb, hf, hb):
    L = MLSTM_CHUNK
    t_lat = k_ref.shape[1]
    t_ctx = kc_ref.shape[1]
    nc = t_lat // L
    ncc = t_ctx // L
    pad = SUBLANES
    half = QK_CONV // 2

    def conv_silu(src, w_ref, dst, t, scale):
        cpad[0:pad, :] = jnp.zeros((pad, LANES), F32)
        cpad[pad:pad + t, :] = src.astype(F32)
        cpad[pad + t:pad + t + pad, :] = jnp.zeros((pad, LANES), F32)
        for r in range(t // L):
            base = pad + r * L - half
            acc = w_ref[0:1, :] * cpad[base:base + L, :]
            for j in range(1, QK_CONV):
                acc = acc + w_ref[j:j + 1, :] * cpad[base + j:base + j + L, :]
            y = _silu(acc)
            if scale is not None:
                y = y * scale
            dst[r * L:(r + 1) * L, :] = y.astype(BF16)

    conv_silu(kc_ref[0], wk_ref, kct, t_ctx, DQK ** -0.5)
    conv_silu(k_ref[0], wk_ref, kt, t_lat, DQK ** -0.5)
    conv_silu(q_ref[0], wq_ref, qt, t_lat, None)

    def fill_vext(dst, src, t):
        lane = lax.broadcasted_iota(jnp.int32, (t, LANES), 1)
        dst[:, 0:DV] = src
        dst[:, DV:DV + LANES] = jnp.where(lane == 0, 1.0, 0.0).astype(BF16)

    fill_vext(vcext, vc_ref[0], t_ctx)
    fill_vext(vext, v_ref[0], t_lat)

    def gate_prep(garr, dst):
        for d in range(2):
            li = garr[2 * d]
            lf = _log_sigmoid(garr[2 * d + 1])
            b = _prefix_sum_lanes(lf, reverse=(d == 1))
            b_end = b[:, LANES - 1:LANES] if d == 0 else b[:, 0:1]
            a = b_end - b + li
            m_loc = jnp.max(a, axis=1, keepdims=True)
            w = jnp.exp(a - m_loc)
            dst[6 * d + 0] = b
            dst[6 * d + 1] = li - b
            dst[6 * d + 2] = w
            dst[6 * d + 3] = jnp.broadcast_to(b_end, b.shape)
            dst[6 * d + 4] = jnp.broadcast_to(m_loc, b.shape)
            dst[6 * d + 5] = jnp.zeros_like(b)

    gate_prep(gc_ref[0, 0], gsc)
    gate_prep(g_ref[0, 0], gs)

    cf[...] = jnp.zeros(cf.shape, F32)
    cb[...] = jnp.zeros(cb.shape, F32)

    row_i = lax.broadcasted_iota(jnp.int32, (L, L), 0)
    col_i = lax.broadcasted_iota(jnp.int32, (L, L), 1)
    tri_f = col_i <= row_i
    tri_b = col_i >= row_i

    sub_i = lax.broadcasted_iota(jnp.int32, (SUBLANES, L), 0)

    def columns(rows):
        stack = jnp.zeros((SUBLANES, L), F32)
        for i, r in enumerate(rows):
            stack = jnp.where(sub_i == i, r, stack)
        sq = jnp.concatenate([stack] * (L // SUBLANES), axis=0)
        tr = sq.T
        return [tr[:, i:i + 1] for i in range(len(rows))]

    def step(c, d, g_scr, k_scr, v_scr, q_scr, c_scr, m_prev, b_col, w_col, h_scr):
        off = c * L if isinstance(c, int) else pl.multiple_of(c * L, L)
        k_c = k_scr[pl.ds(off, L), :]
        v_c = v_scr[pl.ds(off, L), :]
        r_row = g_scr[6 * d + 1, pl.ds(c, 1), :]
        b_end = g_scr[6 * d + 3, pl.ds(c, 1), 0:1]
        m_loc = g_scr[6 * d + 4, pl.ds(c, 1), 0:1]
        c_prev = c_scr[...]
        if q_scr is not None:
            q_c = q_scr[pl.ds(off, L), :]
            qk = lax.dot_general(q_c, k_c, (((1,), (1,)), ((), ())), preferred_element_type=F32)
            dmat = jnp.where(tri_f if d == 0 else tri_b, b_col + r_row, -jnp.inf)
            g_col = b_col + m_prev
            m_out = jnp.maximum(g_col, jnp.max(dmat, axis=1, keepdims=True))
            s = (qk * jnp.exp(dmat - m_out)).astype(BF16)
            wi = jnp.exp(g_col - m_out)
            inter = jnp.dot(q_c, c_prev.astype(BF16), preferred_element_type=F32)
            res = jnp.dot(s, v_c, preferred_element_type=F32) + wi * inter
            den = res[:, DV:DV + 1]
            inv = 1.0 / jnp.maximum(jnp.abs(den), jnp.exp(-m_out))
            h_scr[pl.ds(off, L), :] = res[:, 0:DV] * inv
        kw = (k_c.astype(F32) * w_col).astype(BF16)
        c_loc = lax.dot_general(kw, v_c, (((0,), (0,)), ((), ())), preferred_element_type=F32)
        m_new = jnp.maximum(b_end + m_prev, m_loc)
        sp = jnp.exp(b_end + m_prev - m_new)
        sl = jnp.exp(m_loc - m_new)
        c_scr[...] = sp * c_prev + sl * c_loc
        return m_new

    def both_dirs(i, n_chunks, g_scr, k_scr, v_scr, q_scr, m_f, m_b):
        c_f = i
        c_b = n_chunks - 1 - i
        cols = columns([g_scr[0, pl.ds(c_f, 1), :], g_scr[2, pl.ds(c_f, 1), :],
                        g_scr[6, pl.ds(c_b, 1), :], g_scr[8, pl.ds(c_b, 1), :]])
        m_f = step(c_f, 0, g_scr, k_scr, v_scr, q_scr, cf, m_f, cols[0], cols[1], hf)
        m_b = step(c_b, 1, g_scr, k_scr, v_scr, q_scr, cb, m_b, cols[2], cols[3], hb)
        return m_f, m_b

    m_f = jnp.zeros((1, 1), F32)
    m_b = jnp.zeros((1, 1), F32)
    for i in range(ncc):
        m_f, m_b = both_dirs(i, ncc, gsc, kct, vcext, None, m_f, m_b)

    def body(i, carry):
        return both_dirs(i, nc, gs, kt, vext, qt, carry[0], carry[1])

    lax.fori_loop(0, nc, body, (m_f, m_b))

    def finish(r, carry):
        off = pl.multiple_of(r * L, L)
        h = hf[pl.ds(off, L), :] + hb[pl.ds(off, L), :]
        h = h * lax.rsqrt(jnp.mean(h * h, axis=-1, keepdims=True) + EPS)
        y = (h * hg_ref[...]) * _sigmoid(o_ref[0, pl.ds(off, L), :].astype(F32))
        out_ref[0, pl.ds(off, L), :] = y.astype(BF16)
        return carry

    lax.fori_loop(0, nc, finish, 0)


def _mlstm(p, pc, g4, gc4, wk, wq, head_g):
    b_, t, _ = p.shape
    tc = pc.shape[1]
    L = MLSTM_CHUNK
    nc, ncc = t // L, tc // L
    ext = DV + LANES
    kb = DQK // LANES
    return pl.pallas_call(
        _mlstm_kernel,
        grid=(b_, N_HEADS),
        in_specs=[
            pl.BlockSpec((1, tc, DQK), lambda b, h: (b, 0, h)),
            pl.BlockSpec((1, tc, DV), lambda b, h: (b, 0, 2 + h)),
            pl.BlockSpec((1, 1, 4, ncc, L), lambda b, h: (b, h, 0, 0, 0)),
            pl.BlockSpec((1, t, DQK), lambda b, h: (b, 0, h)),
            pl.BlockSpec((1, t, DV), lambda b, h: (b, 0, 2 + h)),
            pl.BlockSpec((1, t, DQK), lambda b, h: (b, 0, 12 + h)),
            pl.BlockSpec((1, t, DV), lambda b, h: (b, 0, 8 + h)),
            pl.BlockSpec((1, 1, 4, nc, L), lambda b, h: (b, h, 0, 0, 0)),
            pl.BlockSpec((SUBLANES, DQK), lambda b, h: (0, h)),
            pl.BlockSpec((SUBLANES, DQK), lambda b, h: (0, h)),
            pl.BlockSpec((1, DV), lambda b, h: (0, h)),
        ],
        out_specs=pl.BlockSpec((1, t, DV), lambda b, h: (b, 0, h)),
        out_shape=jax.ShapeDtypeStruct((b_, t, N_HEADS * DV), BF16),
        scratch_shapes=[
            pltpu.VMEM((t + 2 * SUBLANES, LANES), F32),
            pltpu.VMEM((t, DQK), BF16),
            pltpu.VMEM((t, DQK), BF16),
            pltpu.VMEM((tc, DQK), BF16),
            pltpu.VMEM((t, ext), BF16),
            pltpu.VMEM((tc, ext), BF16),
            pltpu.VMEM((12, nc, L), F32),
            pltpu.VMEM((12, ncc, L), F32),
            pltpu.VMEM((DQK, ext), F32),
            pltpu.VMEM((DQK, ext), F32),
            pltpu.VMEM((t, DV), F32),
            pltpu.VMEM((t, DV), F32),
        ],
        compiler_params=_cparams(("arbitrary", "arbitrary")),
    )(pc, pc, gc4, p, p, p, p, g4, wk, wq, head_g)


def _pool_kernel(u_ref, pw_ref, ps_ref, out_ref, spad):
    t = u_ref.shape[1]
    rows = t // GRID_W
    blk = 256
    halo = (max(POOL_WINDOWS) // 2) * GRID_W
    spad[0:halo, :] = jnp.zeros((halo, POOL_GC), F32)
    spad[halo + t:halo + t + halo, :] = jnp.zeros((halo, POOL_GC), F32)
    ti = lax.broadcasted_iota(jnp.int32, (blk, blk), 0)
    tj = lax.broadcasted_iota(jnp.int32, (blk, blk), 1)
    same_row = (ti >> GRID_SHIFT) == (tj >> GRID_SHIFT)
    diff = tj - ti
    tok = lax.broadcasted_iota(jnp.int32, (blk, 1), 0)
    for gi, w in enumerate(POOL_WINDOWS):
        hw = w // 2
        c0, c1 = gi * POOL_GC, (gi + 1) * POOL_GC
        band = jnp.where(same_row & (diff >= -hw) & (diff <= hw - 1), 1.0, 0.0).astype(BF16)
        for r in range(t // blk):
            spad[halo + r * blk:halo + (r + 1) * blk, :] = jnp.dot(
                band, u_ref[0, r * blk:(r + 1) * blk, c0:c1], preferred_element_type=F32)
        for r in range(t // blk):
            base = halo + r * blk
            acc = spad[base - hw * GRID_W:base - hw * GRID_W + blk, :]
            for j in range(-hw + 1, hw):
                acc = acc + spad[base + j * GRID_W:base + j * GRID_W + blk, :]
            tt = tok + r * blk
            gr = tt >> GRID_SHIFT
            gc = tt & (GRID_W - 1)
            cr = jnp.minimum(gr + hw, rows) - jnp.maximum(gr - hw, 0)
            cc = jnp.minimum(gc + hw, GRID_W) - jnp.maximum(gc - hw, 0)
            cnt = (cr * cc).astype(F32)
            d = acc / cnt - u_ref[0, r * blk:(r + 1) * blk, c0:c1].astype(F32)
            y = jnp.dot(d.astype(BF16), pw_ref[gi], preferred_element_type=F32) * ps_ref[:, c0:c1]
            out_ref[0, r * blk:(r + 1) * blk, c0:c1] = y.astype(BF16)


def _pool(p, pool_w, pool_scale):
    b_, t, _ = p.shape
    pw = len(POOL_WINDOWS) * POOL_GC
    halo = (max(POOL_WINDOWS) // 2) * GRID_W
    return pl.pallas_call(
        _pool_kernel,
        grid=(b_,),
        in_specs=[pl.BlockSpec((1, t, pw), lambda b: (b, 0, 3)),
                  pl.BlockSpec((len(POOL_WINDOWS), POOL_GC, POOL_GC), lambda b: (0, 0, 0)),
                  pl.BlockSpec((1, pw), lambda b: (0, 0))],
        out_specs=pl.BlockSpec((1, t, pw), lambda b: (b, 0, 0)),
        out_shape=jax.ShapeDtypeStruct((b_, t, pw), BF16),
        scratch_shapes=[pltpu.VMEM((t + 2 * halo, POOL_GC), F32)],
        compiler_params=_cparams(("arbitrary",)),
    )(p, pool_w, pool_scale)


def _outproj_kernel(ym_ref, yp_ref, wm_ref, wp_ref, x_ref, mod_ref, n2_ref, wr_ref,
                    x1_ref, h2_ref, lg_ref):
    mix = (jnp.dot(ym_ref[...], wm_ref[...], preferred_element_type=F32)
           + jnp.dot(yp_ref[...], wp_ref[...], preferred_element_type=F32))
    x1 = x_ref[...] + mod_ref[0, 2:3, :] * mix
    x1_ref[...] = x1
    y = x1 * lax.rsqrt(jnp.mean(x1 * x1, axis=-1, keepdims=True) + EPS)
    h2 = ((y * n2_ref[...]) * (1.0 + mod_ref[0, 4:5, :]) + mod_ref[0, 3:4, :]).astype(BF16)
    h2_ref[...] = h2
    lg_ref[...] = jnp.dot(h2, wr_ref[...], preferred_element_type=F32)


def _outproj(ym, yp, wm, wp, x2, mod, rows_per_mod, n2, wr, tm):
    m, d = x2.shape
    km = ym.shape[1]
    kp = yp.shape[1]
    tiles_per_mod = rows_per_mod // tm
    return pl.pallas_call(
        _outproj_kernel,
        grid=(m // tm,),
        in_specs=[pl.BlockSpec((tm, km), lambda i: (i, 0)),
                  pl.BlockSpec((tm, kp), lambda i: (i, 0)),
                  pl.BlockSpec((km, d), lambda i: (0, 0)),
                  pl.BlockSpec((kp, d), lambda i: (0, 0)),
                  pl.BlockSpec((tm, d), lambda i: (i, 0)),
                  pl.BlockSpec((1, N_MOD, d), lambda i: (i // tiles_per_mod, 0, 0)),
                  pl.BlockSpec((1, d), lambda i: (0, 0)),
                  pl.BlockSpec((d, LANES), lambda i: (0, 0))],
        out_specs=[pl.BlockSpec((tm, d), lambda i: (i, 0)),
                   pl.BlockSpec((tm, d), lambda i: (i, 0)),
                   pl.BlockSpec((tm, LANES), lambda i: (i, 0))],
        out_shape=[jax.ShapeDtypeStruct((m, d), F32),
                   jax.ShapeDtypeStruct((m, d), BF16),
                   jax.ShapeDtypeStruct((m, LANES), F32)],
        compiler_params=_cparams(("arbitrary",)),
    )(ym, yp, wm, wp, x2, mod, n2, wr)


def _scan_lanes_i32(x, exclusive_of=None):
    n = x.shape[1]
    lane = lax.broadcasted_iota(jnp.int32, x.shape, 1)
    k = 1
    while k < n:
        x = x + jnp.where(lane >= k, pltpu.roll(x, k, 1), 0)
        k *= 2
    return x


def _route_kernel(lg_ref, idx_ref, gate_ref, afft):
    t = lg_ref.shape[1]
    cap = idx_ref.shape[2]
    lane = lax.broadcasted_iota(jnp.int32, (t, LANES), 1)
    lg = jnp.where(lane < N_EXPERTS, lg_ref[0], -jnp.inf)
    ex = jnp.exp(lg - jnp.max(lg, axis=1, keepdims=True))
    aff = ex / jnp.sum(ex, axis=1, keepdims=True)
    for r in range(t // LANES):
        afft[:, r * LANES:(r + 1) * LANES] = aff[r * LANES:(r + 1) * LANES, :].T
    aff_t = afft[0:N_EXPERTS, :]

    def count(mask):
        return jnp.sum(jnp.where(mask, 1.0, 0.0), axis=1, keepdims=True).astype(jnp.int32)

    def bit_step(i, thr_bits):
        cand = thr_bits | (jnp.int32(1) << (30 - i))
        cand_f = lax.bitcast_convert_type(cand, F32)
        return jnp.where(count(aff_t >= cand_f) >= cap, cand, thr_bits)

    thr_bits = lax.fori_loop(0, 31, bit_step, jnp.zeros((N_EXPERTS, 1), jnp.int32))
    thr = lax.bitcast_convert_type(thr_bits, F32)
    gt = aff_t > thr
    eq = aff_t == thr
    need = cap - count(gt)
    eq_i = jnp.where(eq, 1, 0)
    eq_rank = _scan_lanes_i32(eq_i) - eq_i
    sel = gt | (eq & (eq_rank < need))
    sel_i = jnp.where(sel, 1, 0)
    pos = _scan_lanes_i32(sel_i) - sel_i
    key = jnp.where(sel, pos, -1)

    a_hi = aff.astype(BF16).astype(F32)
    a_mid = (aff - a_hi).astype(BF16).astype(F32)
    a_lo = (aff - a_hi - a_mid).astype(BF16).astype(F32)
    tok = lax.broadcasted_iota(jnp.int32, (t, LANES), 0)
    vals = (a_hi + pltpu.roll(a_mid, N_EXPERTS, 1) + pltpu.roll(a_lo, 2 * N_EXPERTS, 1)
            + jnp.where(lane == 3 * N_EXPERTS, (tok >> GRID_SHIFT).astype(F32), 0.0)
            + jnp.where(lane == 3 * N_EXPERTS + 1, (tok & (GRID_W - 1)).astype(F32), 0.0)).astype(BF16)
    slot = lax.broadcasted_iota(jnp.int32, (cap, 1), 0)
    for e in range(N_EXPERTS):
        onehot = jnp.where(slot == key[e:e + 1, :], 1.0, 0.0).astype(BF16)
        res = jnp.dot(onehot, vals, preferred_element_type=F32)
        rt = jnp.concatenate([res[r * LANES:(r + 1) * LANES, :].T for r in range(cap // LANES)], axis=1)
        gate_ref[0, e:e + 1, :] = (rt[e:e + 1, :] + rt[N_EXPERTS + e:N_EXPERTS + e + 1, :]) \
            + rt[2 * N_EXPERTS + e:2 * N_EXPERTS + e + 1, :]
        tokf = rt[3 * N_EXPERTS:3 * N_EXPERTS + 1, :] * float(GRID_W) + rt[3 * N_EXPERTS + 1:3 * N_EXPERTS + 2, :]
        idx_ref[0, e:e + 1, :] = tokf.astype(jnp.int32)


def _route(lg3, cap):
    b_, t, _ = lg3.shape
    return pl.pallas_call(
        _route_kernel,
        grid=(b_,),
        in_specs=[pl.BlockSpec((1, t, LANES), lambda b: (b, 0, 0))],
        out_specs=[pl.BlockSpec((1, N_EXPERTS, cap), lambda b: (b, 0, 0)),
                   pl.BlockSpec((1, N_EXPERTS, cap), lambda b: (b, 0, 0))],
        out_shape=[jax.ShapeDtypeStruct((b_, N_EXPERTS, cap), jnp.int32),
                   jax.ShapeDtypeStruct((b_, N_EXPERTS, cap), F32)],
        scratch_shapes=[pltpu.VMEM((LANES, t), F32)],
        compiler_params=_cparams(("arbitrary",)),
    )(lg3)


def _ffn_kernel(xs_ref, gate_ref, wg_ref, wu_ref, wd_ref, y_ref):
    xs = xs_ref[0, 0]
    a = _silu(jnp.dot(xs, wg_ref[0], preferred_element_type=F32)) * jnp.dot(xs, wu_ref[0], preferred_element_type=F32)
    y = jnp.dot(a.astype(BF16), wd_ref[0], preferred_element_type=F32)
    y_ref[0, 0] = (y * gate_ref[0, 0]).astype(BF16)


def _ffn(xs, gate4, wg, wu, wd):
    b_, e_, cap, d = xs.shape
    f = wg.shape[2]
    return pl.pallas_call(
        _ffn_kernel,
        grid=(e_, b_),
        in_specs=[pl.BlockSpec((1, 1, cap, d), lambda e, b: (b, e, 0, 0)),
                  pl.BlockSpec((1, 1, cap, 1), lambda e, b: (b, e, 0, 0)),
                  pl.BlockSpec((1, d, f), lambda e, b: (e, 0, 0)),
                  pl.BlockSpec((1, d, f), lambda e, b: (e, 0, 0)),
                  pl.BlockSpec((1, f, d), lambda e, b: (e, 0, 0))],
        out_specs=pl.BlockSpec((1, 1, cap, d), lambda e, b: (b, e, 0, 0)),
        out_shape=jax.ShapeDtypeStruct((b_, e_, cap, d), BF16),
        compiler_params=_cparams(("arbitrary", "arbitrary")),
    )(xs, gate4, wg, wu, wd)


COMBINE_SUB = 256


def _combine_kernel(y_ref, idx_ref, x1_ref, mod_ref, nf_ref, o_ref):
    i = pl.program_id(1)
    e = pl.program_id(2)
    tm = o_ref.shape[1]

    @pl.when(e == 0)
    def _():
        o_ref[...] = jnp.zeros(o_ref.shape, F32)

    y = y_ref[0, 0]
    tok_row = idx_ref[0, 0]
    tok_col = lax.broadcasted_iota(jnp.int32, (COMBINE_SUB, 1), 0) + i * tm
    for j in range(tm // COMBINE_SUB):
        onehot = jnp.where(tok_col + j * COMBINE_SUB == tok_row, 1.0, 0.0).astype(BF16)
        o_ref[0, j * COMBINE_SUB:(j + 1) * COMBINE_SUB, :] += jnp.dot(onehot, y, preferred_element_type=F32)

    @pl.when(e == pl.num_programs(2) - 1)
    def _():
        x = x1_ref[0] + mod_ref[0, 5:6, :] * o_ref[0]
        o_ref[0] = (x * lax.rsqrt(jnp.mean(x * x, axis=-1, keepdims=True) + EPS)) * nf_ref[...]


def _combine(y, idx4, x1, mod, nf, tm):
    b_, e_, cap, d = y.shape
    t = x1.shape[1]
    return pl.pallas_call(
        _combine_kernel,
        grid=(b_, t // tm, e_),
        in_specs=[pl.BlockSpec((1, 1, cap, d), lambda b, i, e: (b, e, 0, 0)),
                  pl.BlockSpec((1, 1, 1, cap), lambda b, i, e: (b, e, 0, 0)),
                  pl.BlockSpec((1, tm, d), lambda b, i, e: (b, i, 0)),
                  pl.BlockSpec((1, N_MOD, d), lambda b, i, e: (b, 0, 0)),
                  pl.BlockSpec((1, d), lambda b, i, e: (0, 0))],
        out_specs=pl.BlockSpec((1, tm, d), lambda b, i, e: (b, i, 0)),
        out_shape=jax.ShapeDtypeStruct((b_, t, d), F32),
        compiler_params=_cparams(("arbitrary", "arbitrary", "arbitrary")),
    )(y, idx4, x1, mod, nf)


def _gate_layout(g, b_, t):
    g = g[:, :2 * 2 * N_HEADS].reshape(b_, t, 2, 2, N_HEADS)
    g = g.transpose(0, 4, 2, 3, 1)
    return g.reshape(b_, N_HEADS, 4, t // MLSTM_CHUNK, MLSTM_CHUNK)


def kernel(x, c, ctx, c_ctx, w_mod, b_mod, norm1, w_in, conv_q_w, conv_k_w, b_gates, head_g, pool_w,
           pool_scale, w_out, norm2, w_router, w_gate, w_up, w_down, norm_f):
    b_, t, d = x.shape
    tc = ctx.shape[1]
    depth = w_mod.shape[0]
    assert depth == 1
    l = 0
    qk_w = N_HEADS * DQK
    mw = N_HEADS * DV
    pool_wd = len(POOL_WINDOWS) * POOL_GC
    n_gate = 2 * 2 * N_HEADS
    k_off, v_off = 0, qk_w
    g_off = v_off + mw
    q_off = g_off + n_gate
    o_off = q_off + qk_w
    p_off = o_off + mw
    cap = EC_FACTOR * t // N_EXPERTS

    mod_rows = -(-(b_ + 1) // SUBLANES) * SUBLANES
    cs = jnp.zeros((mod_rows, d), F32).at[:b_].set(c).at[b_].set(c_ctx)
    mod_all = _modulation(cs, w_mod[l], b_mod[l])
    mod = mod_all[:b_].reshape(b_, N_MOD, d)
    mod_c = mod_all[b_:b_ + 1].reshape(1, N_MOD, d)

    wl = w_in[l]
    w_main = jnp.concatenate([wl[:, k_off:v_off], wl[:, v_off:g_off], wl[:, q_off:o_off],
                              wl[:, o_off:p_off], wl[:, p_off:]], axis=1).astype(BF16)
    w_g = jnp.pad(wl[:, g_off:q_off], ((0, 0), (0, LANES - n_gate))).astype(BF16)
    b_g = jnp.pad(b_gates[l], (0, LANES - n_gate)).reshape(1, LANES)
    n1 = norm1[l].reshape(1, d)

    p, g = _inproj(x.reshape(b_ * t, d), mod, t, n1, w_main, w_g, b_g, w_main.shape[1], 1024, 512)
    pc, gc = _inproj(ctx.reshape(b_ * tc, d), mod_c, b_ * tc, n1, w_main, w_g, b_g, qk_w + mw,
                     min(1024, b_ * tc), 512)
    p = p.reshape(b_, t, -1)
    pc = pc.reshape(b_, tc, -1)

    wk = jnp.pad(conv_k_w[l], ((0, SUBLANES - QK_CONV), (0, 0)))
    wq = jnp.pad(conv_q_w[l], ((0, SUBLANES - QK_CONV), (0, 0)))
    ym = _mlstm(p, pc, _gate_layout(g, b_, t), _gate_layout(gc, b_, tc), wk, wq, head_g[l].reshape(1, mw))
    yp = _pool(p, pool_w[l].astype(BF16), pool_scale[l].reshape(1, pool_wd))

    wo = w_out[l].astype(BF16)
    wr = jnp.pad(w_router[l], ((0, 0), (0, LANES - N_EXPERTS))).astype(BF16)
    x1, h2, lg = _outproj(ym.reshape(b_ * t, mw), yp.reshape(b_ * t, pool_wd), wo[:mw], wo[mw:],
                          x.reshape(b_ * t, d), mod, t, norm2[l].reshape(1, d), wr, 512)

    idx, gate = _route(lg.reshape(b_, t, LANES), cap)

    h2 = h2.reshape(b_, t, d)
    xs = jax.vmap(lambda hb, ib: hb[ib])(h2, idx)
    y = _ffn(xs, gate.reshape(b_, N_EXPERTS, cap, 1), w_gate[l].astype(BF16), w_up[l].astype(BF16),
             w_down[l].astype(BF16))
    return _combine(y, idx.reshape(b_, N_EXPERTS, 1, cap), x1.reshape(b_, t, d), mod, norm_f.reshape(1, d), 1024)
```

```python
import functools

import jax
import jax.numpy as jnp
from jax import lax
from jax.experimental import pallas as pl
from jax.experimental.pallas import tpu as pltpu

F32 = jnp.float32
BF16 = jnp.bfloat16

N_HEADS = 4
DQK = 128
DV = 256
QK_CONV = 5
POOL_WINDOWS = (2, 4, 8, 16)
POOL_GC = 256
GRID_W = 64
GRID_SHIFT = GRID_W.bit_length() - 1
N_EXPERTS = 16
EC_FACTOR = 2
N_MOD = 6
EPS = 1e-6

LANES = 128
SUBLANES = 8
MLSTM_CHUNK = 128
VMEM_LIMIT = 56 * 1024 * 1024


def _cparams(sem):
    return pltpu.CompilerParams(dimension_semantics=sem, vmem_limit_bytes=VMEM_LIMIT)


def _sigmoid(x):
    return 1.0 / (1.0 + jnp.exp(-x))


def _silu(x):
    return x * _sigmoid(x)


def _log_sigmoid(x):
    return jnp.minimum(x, 0.0) - jnp.log(1.0 + jnp.exp(-jnp.abs(x)))


def _mod_kernel(c_ref, w_ref, b_ref, o_ref):
    s = _silu(c_ref[...]).astype(BF16)
    o_ref[...] = jnp.dot(s, w_ref[...].astype(BF16), preferred_element_type=F32) + b_ref[...]


def _modulation(cs, w_mod, b_mod):
    rows, d = cs.shape
    n = w_mod.shape[1]
    tn = 1024
    return pl.pallas_call(
        _mod_kernel,
        grid=(n // tn,),
        in_specs=[pl.BlockSpec((rows, d), lambda j: (0, 0)),
                  pl.BlockSpec((d, tn), lambda j: (0, j)),
                  pl.BlockSpec((1, tn), lambda j: (0, j))],
        out_specs=pl.BlockSpec((rows, tn), lambda j: (0, j)),
        out_shape=jax.ShapeDtypeStruct((rows, n), F32),
        compiler_params=_cparams(("arbitrary",)),
    )(cs, w_mod, b_mod.reshape(1, n))


def _inproj_kernel(x_ref, mod_ref, n1_ref, w_ref, wg_ref, bg_ref, p_ref, g_ref, h_scr):
    @pl.when(pl.program_id(1) == 0)
    def _():
        x = x_ref[...]
        y = x * lax.rsqrt(jnp.mean(x * x, axis=-1, keepdims=True) + EPS)
        h = (y * n1_ref[...]) * (1.0 + mod_ref[0, 1:2, :]) + mod_ref[0, 0:1, :]
        hb = h.astype(BF16)
        h_scr[...] = hb
        g_ref[...] = jnp.dot(hb, wg_ref[...], preferred_element_type=F32) + bg_ref[...]

    p_ref[...] = jnp.dot(h_scr[...], w_ref[...], preferred_element_type=F32).astype(BF16)


def _inproj(x2, mod, rows_per_mod, n1, w, wg, bg, n_cols, tm, tn):
    m, d = x2.shape
    tiles_per_mod = rows_per_mod // tm
    return pl.pallas_call(
        _inproj_kernel,
        grid=(m // tm, n_cols // tn),
        in_specs=[pl.BlockSpec((tm, d), lambda i, j: (i, 0)),
                  pl.BlockSpec((1, N_MOD, d), lambda i, j: (i // tiles_per_mod, 0, 0)),
                  pl.BlockSpec((1, d), lambda i, j: (0, 0)),
                  pl.BlockSpec((d, tn), lambda i, j: (0, j)),
                  pl.BlockSpec((d, LANES), lambda i, j: (0, 0)),
                  pl.BlockSpec((1, LANES), lambda i, j: (0, 0))],
        out_specs=[pl.BlockSpec((tm, tn), lambda i, j: (i, j)),
                   pl.BlockSpec((tm, LANES), lambda i, j: (i, 0))],
        out_shape=[jax.ShapeDtypeStruct((m, n_cols), BF16),
                   jax.ShapeDtypeStruct((m, LANES), F32)],
        scratch_shapes=[pltpu.VMEM((tm, d), BF16)],
        compiler_params=_cparams(("arbitrary", "arbitrary")),
    )(x2, mod, n1, w, wg, bg)


def _scan_lanes(x, op, fill, reverse):
    lane = lax.broadcasted_iota(jnp.int32, x.shape, 1)
    k = 1
    while k < LANES:
        if reverse:
            x = op(x, jnp.where(lane < LANES - k, pltpu.roll(x, LANES - k, 1), fill))
        else:
            x = op(x, jnp.where(lane >= k, pltpu.roll(x, k, 1), fill))
        k *= 2
    return x


G_B, G_R, G_W, G_BEND, G_MLOC, G_RMAX, G_ROWS = 0, 1, 2, 3, 4, 5, 6


def _mlstm_kernel(kc_ref, vc_ref, gc_ref, k_ref, v_ref, q_ref, o_ref, g_ref, wk_ref, wq_ref, hg_ref,
                  out_ref,
                  cpad, ktt, qt, kctt, vext, vcext, gs, gsc, cf, cb, hf, hb):
    L = MLSTM_CHUNK
    t_lat = k_ref.shape[1]
    t_ctx = kc_ref.shape[1]
    nc = t_lat // L
    ncc = t_ctx // L
    pad = SUBLANES
    half = QK_CONV // 2

    def conv_silu(src, w_ref, dst, t, scale, transposed):
        cpad[0:pad, :] = jnp.zeros((pad, LANES), F32)
        cpad[pad:pad + t, :] = src.astype(F32)
        cpad[pad + t:pad + t + pad, :] = jnp.zeros((pad, LANES), F32)
        for r in range(t // L):
            base = pad + r * L - half
            acc = w_ref[0:1, :] * cpad[base:base + L, :]
            for j in range(1, QK_CONV):
                acc = acc + w_ref[j:j + 1, :] * cpad[base + j:base + j + L, :]
            y = _silu(acc)
            if scale is not None:
                y = y * scale
            dst[r] = (y.T if transposed else y).astype(BF16)

    conv_silu(kc_ref[0], wk_ref, kctt, t_ctx, DQK ** -0.5, True)
    conv_silu(k_ref[0], wk_ref, ktt, t_lat, DQK ** -0.5, True)
    conv_silu(q_ref[0], wq_ref, qt, t_lat, None, False)

    def fill_vext(dst, src, t):
        dst[:, 0:DV] = src
        dst[:, DV:DV + LANES] = jnp.ones((t, LANES), BF16)

    fill_vext(vcext, vc_ref[0], t_ctx)
    fill_vext(vext, v_ref[0], t_lat)

    def gate_prep(garr, dst):
        for d in range(2):
            li = garr[2 * d]
            lf = _log_sigmoid(garr[2 * d + 1])
            b = _scan_lanes(lf, jnp.add, 0.0, reverse=(d == 1))
            b_end = b[:, LANES - 1:LANES] if d == 0 else b[:, 0:1]
            r = li - b
            a = b_end + r
            m_loc = jnp.max(a, axis=1, keepdims=True)
            dst[G_ROWS * d + G_B] = b
            dst[G_ROWS * d + G_R] = r
            dst[G_ROWS * d + G_W] = jnp.exp(a - m_loc)
            dst[G_ROWS * d + G_BEND] = jnp.broadcast_to(b_end, b.shape)
            dst[G_ROWS * d + G_MLOC] = jnp.broadcast_to(m_loc, b.shape)
            dst[G_ROWS * d + G_RMAX] = _scan_lanes(r, jnp.maximum, -jnp.inf, reverse=(d == 1))

    gate_prep(gc_ref[0, 0], gsc)
    gate_prep(g_ref[0, 0], gs)

    cf[...] = jnp.zeros(cf.shape, F32)
    cb[...] = jnp.zeros(cb.shape, F32)

    row_i = lax.broadcasted_iota(jnp.int32, (L, L), 0)
    col_i = lax.broadcasted_iota(jnp.int32, (L, L), 1)
    visible = (col_i <= row_i, col_i >= row_i)

    def per_token(row):
        return jnp.broadcast_to(row, (L, L)).T

    def step(c, d, g_scr, kt_scr, v_scr, q_scr, c_scr, m_prev):
        off = c * L if isinstance(c, int) else pl.multiple_of(c * L, L)
        g0 = G_ROWS * d
        kt_c = kt_scr[c]
        v_c = v_scr[pl.ds(off, L), :]
        b_end = g_scr[g0 + G_BEND, pl.ds(c, 1), 0:1]
        m_loc = g_scr[g0 + G_MLOC, pl.ds(c, 1), 0:1]
        c_prev = c_scr[...]
        h = None
        if q_scr is not None:
            q_c = q_scr[c]
            r_row = g_scr[g0 + G_R, pl.ds(c, 1), :]
            u = jnp.maximum(per_token(g_scr[g0 + G_RMAX, pl.ds(c, 1), :]), m_prev)
            b_t = per_token(g_scr[g0 + G_B, pl.ds(c, 1), :])
            qk = jnp.dot(q_c, kt_c, preferred_element_type=F32)
            s = (qk * jnp.exp(jnp.where(visible[d], r_row - u, -jnp.inf))).astype(BF16)
            wi = jnp.exp(m_prev - u)
            inter = jnp.dot(q_c, c_prev.astype(BF16), preferred_element_type=F32)
            res = jnp.dot(s, v_c, preferred_element_type=F32) \
                + jnp.concatenate([wi] * (c_prev.shape[1] // L), axis=1) * inter
            inv = 1.0 / jnp.maximum(jnp.abs(res[:, DV:DV + L]), jnp.exp(-(b_t + u)))
            h = res[:, 0:DV] * jnp.concatenate([inv] * (DV // L), axis=1)
        kwt = (kt_c.astype(F32) * g_scr[g0 + G_W, pl.ds(c, 1), :]).astype(BF16)
        c_loc = jnp.dot(kwt, v_c, preferred_element_type=F32)
        m_new = jnp.maximum(b_end + m_prev, m_loc)
        sp = jnp.exp(b_end + m_prev - m_new)
        sl = jnp.exp(m_loc - m_new)
        c_scr[...] = sp * c_prev + sl * c_loc
        return m_new, h

    def finish(c, h):
        off = pl.multiple_of(c * L, L)
        h = h * lax.rsqrt(jnp.mean(h * h, axis=-1, keepdims=True) + EPS)
        y = (h * hg_ref[...]) * _sigmoid(o_ref[0, pl.ds(off, L), :].astype(F32))
        out_ref[0, pl.ds(off, L), :] = y.astype(BF16)

    m_f = jnp.zeros((1, 1), F32)
    m_b = jnp.zeros((1, 1), F32)
    for i in range(ncc):
        m_f, _ = step(i, 0, gsc, kctt, vcext, None, cf, m_f)
        m_b, _ = step(ncc - 1 - i, 1, gsc, kctt, vcext, None, cb, m_b)

    assert nc % 2 == 0

    def first_half(i, carry):
        c_f, c_b = i, nc - 1 - i
        m_f, h_f = step(c_f, 0, gs, ktt, vext, qt, cf, carry[0])
        m_b, h_b = step(c_b, 1, gs, ktt, vext, qt, cb, carry[1])
        hf[pl.ds(pl.multiple_of(c_f * L, L), L), :] = h_f
        hb[pl.ds(pl.multiple_of(c_b * L, L), L), :] = h_b
        return m_f, m_b

    def second_half(i, carry):
        c_f, c_b = i, nc - 1 - i
        m_f, h_f = step(c_f, 0, gs, ktt, vext, qt, cf, carry[0])
        m_b, h_b = step(c_b, 1, gs, ktt, vext, qt, cb, carry[1])
        finish(c_f, h_f + hb[pl.ds(pl.multiple_of(c_f * L, L), L), :])
        finish(c_b, h_b + hf[pl.ds(pl.multiple_of(c_b * L, L), L), :])
        return m_f, m_b

    carry = lax.fori_loop(0, nc // 2, first_half, (m_f, m_b), unroll=2)
    lax.fori_loop(nc // 2, nc, second_half, carry, unroll=2)


def _mlstm(p, pc, g4, gc4, wk, wq, head_g):
    b_, t, _ = p.shape
    tc = pc.shape[1]
    L = MLSTM_CHUNK
    nc, ncc = t // L, tc // L
    ext = DV + LANES
    return pl.pallas_call(
        _mlstm_kernel,
        grid=(b_, N_HEADS),
        in_specs=[
            pl.BlockSpec((1, tc, DQK), lambda b, h: (b, 0, h)),
            pl.BlockSpec((1, tc, DV), lambda b, h: (b, 0, 2 + h)),
            pl.BlockSpec((1, 1, 4, ncc, L), lambda b, h: (b, h, 0, 0, 0)),
            pl.BlockSpec((1, t, DQK), lambda b, h: (b, 0, h)),
            pl.BlockSpec((1, t, DV), lambda b, h: (b, 0, 2 + h)),
            pl.BlockSpec((1, t, DQK), lambda b, h: (b, 0, 12 + h)),
            pl.BlockSpec((1, t, DV), lambda b, h: (b, 0, 8 + h)),
            pl.BlockSpec((1, 1, 4, nc, L), lambda b, h: (b, h, 0, 0, 0)),
            pl.BlockSpec((SUBLANES, DQK), lambda b, h: (0, h)),
            pl.BlockSpec((SUBLANES, DQK), lambda b, h: (0, h)),
            pl.BlockSpec((1, DV), lambda b, h: (0, h)),
        ],
        out_specs=pl.BlockSpec((1, t, DV), lambda b, h: (b, 0, h)),
        out_shape=jax.ShapeDtypeStruct((b_, t, N_HEADS * DV), BF16),
        scratch_shapes=[
            pltpu.VMEM((t + 2 * SUBLANES, LANES), F32),
            pltpu.VMEM((nc, DQK, L), BF16),
            pltpu.VMEM((nc, L, DQK), BF16),
            pltpu.VMEM((ncc, DQK, L), BF16),
            pltpu.VMEM((t, ext), BF16),
            pltpu.VMEM((tc, ext), BF16),
            pltpu.VMEM((2 * G_ROWS, nc, L), F32),
            pltpu.VMEM((2 * G_ROWS, ncc, L), F32),
            pltpu.VMEM((DQK, ext), F32),
            pltpu.VMEM((DQK, ext), F32),
            pltpu.VMEM((t, DV), F32),
            pltpu.VMEM((t, DV), F32),
        ],
        compiler_params=_cparams(("arbitrary", "arbitrary")),
    )(pc, pc, gc4, p, p, p, p, g4, wk, wq, head_g)


def _pool_kernel(u_ref, pw_ref, ps_ref, out_ref, spad):
    t = u_ref.shape[1]
    rows = t // GRID_W
    blk = 256
    halo = (max(POOL_WINDOWS) // 2) * GRID_W
    spad[0:halo, :] = jnp.zeros((halo, POOL_GC), F32)
    spad[halo + t:halo + t + halo, :] = jnp.zeros((halo, POOL_GC), F32)
    ti = lax.broadcasted_iota(jnp.int32, (blk, blk), 0)
    tj = lax.broadcasted_iota(jnp.int32, (blk, blk), 1)
    same_row = (ti >> GRID_SHIFT) == (tj >> GRID_SHIFT)
    diff = tj - ti
    tok = lax.broadcasted_iota(jnp.int32, (blk, 1), 0)
    for gi, w in enumerate(POOL_WINDOWS):
        hw = w // 2
        c0, c1 = gi * POOL_GC, (gi + 1) * POOL_GC
        band = jnp.where(same_row & (diff >= -hw) & (diff <= hw - 1), 1.0, 0.0).astype(BF16)
        for r in range(t // blk):
            spad[halo + r * blk:halo + (r + 1) * blk, :] = jnp.dot(
                band, u_ref[0, r * blk:(r + 1) * blk, c0:c1], preferred_element_type=F32)
        for r in range(t // blk):
            base = halo + r * blk
            acc = spad[base - hw * GRID_W:base - hw * GRID_W + blk, :]
            for j in range(-hw + 1, hw):
                acc = acc + spad[base + j * GRID_W:base + j * GRID_W + blk, :]
            tt = tok + r * blk
            gr = tt >> GRID_SHIFT
            gc = tt & (GRID_W - 1)
            cr = jnp.minimum(gr + hw, rows) - jnp.maximum(gr - hw, 0)
            cc = jnp.minimum(gc + hw, GRID_W) - jnp.maximum(gc - hw, 0)
            cnt = (cr * cc).astype(F32)
            d = acc / cnt - u_ref[0, r * blk:(r + 1) * blk, c0:c1].astype(F32)
            y = jnp.dot(d.astype(BF16), pw_ref[gi], preferred_element_type=F32) * ps_ref[:, c0:c1]
            out_ref[0, r * blk:(r + 1) * blk, c0:c1] = y.astype(BF16)


def _pool(p, pool_w, pool_scale):
    b_, t, _ = p.shape
    pw = len(POOL_WINDOWS) * POOL_GC
    halo = (max(POOL_WINDOWS) // 2) * GRID_W
    return pl.pallas_call(
        _pool_kernel,
        grid=(b_,),
        in_specs=[pl.BlockSpec((1, t, pw), lambda b: (b, 0, 3)),
                  pl.BlockSpec((len(POOL_WINDOWS), POOL_GC, POOL_GC), lambda b: (0, 0, 0)),
                  pl.BlockSpec((1, pw), lambda b: (0, 0))],
        out_specs=pl.BlockSpec((1, t, pw), lambda b: (b, 0, 0)),
        out_shape=jax.ShapeDtypeStruct((b_, t, pw), BF16),
        scratch_shapes=[pltpu.VMEM((t + 2 * halo, POOL_GC), F32)],
        compiler_params=_cparams(("arbitrary",)),
    )(p, pool_w, pool_scale)


def _outproj_kernel(ym_ref, yp_ref, wm_ref, wp_ref, x_ref, mod_ref, n2_ref, wr_ref,
                    x1_ref, h2_ref, lg_ref):
    mix = (jnp.dot(ym_ref[...], wm_ref[...], preferred_element_type=F32)
           + jnp.dot(yp_ref[...], wp_ref[...], preferred_element_type=F32))
    x1 = x_ref[...] + mod_ref[0, 2:3, :] * mix
    x1_ref[...] = x1
    y = x1 * lax.rsqrt(jnp.mean(x1 * x1, axis=-1, keepdims=True) + EPS)
    h2 = ((y * n2_ref[...]) * (1.0 + mod_ref[0, 4:5, :]) + mod_ref[0, 3:4, :]).astype(BF16)
    h2_ref[...] = h2
    lg_ref[...] = jnp.dot(h2, wr_ref[...], preferred_element_type=F32)


def _outproj(ym, yp, wm, wp, x2, mod, rows_per_mod, n2, wr, tm):
    m, d = x2.shape
    km = ym.shape[1]
    kp = yp.shape[1]
    tiles_per_mod = rows_per_mod // tm
    return pl.pallas_call(
        _outproj_kernel,
        grid=(m // tm,),
        in_specs=[pl.BlockSpec((tm, km), lambda i: (i, 0)),
                  pl.BlockSpec((tm, kp), lambda i: (i, 0)),
                  pl.BlockSpec((km, d), lambda i: (0, 0)),
                  pl.BlockSpec((kp, d), lambda i: (0, 0)),
                  pl.BlockSpec((tm, d), lambda i: (i, 0)),
                  pl.BlockSpec((1, N_MOD, d), lambda i: (i // tiles_per_mod, 0, 0)),
                  pl.BlockSpec((1, d), lambda i: (0, 0)),
                  pl.BlockSpec((d, LANES), lambda i: (0, 0))],
        out_specs=[pl.BlockSpec((tm, d), lambda i: (i, 0)),
                   pl.BlockSpec((tm, d), lambda i: (i, 0)),
                   pl.BlockSpec((tm, LANES), lambda i: (i, 0))],
        out_shape=[jax.ShapeDtypeStruct((m, d), F32),
                   jax.ShapeDtypeStruct((m, d), BF16),
                   jax.ShapeDtypeStruct((m, LANES), F32)],
        compiler_params=_cparams(("arbitrary",)),
    )(ym, yp, wm, wp, x2, mod, n2, wr)


def _scan_lanes_i32(x, exclusive_of=None):
    n = x.shape[1]
    lane = lax.broadcasted_iota(jnp.int32, x.shape, 1)
    k = 1
    while k < n:
        x = x + jnp.where(lane >= k, pltpu.roll(x, k, 1), 0)
        k *= 2
    return x


def _route_kernel(lg_ref, idx_ref, gate_ref, afft):
    t = lg_ref.shape[1]
    cap = idx_ref.shape[2]
    lane = lax.broadcasted_iota(jnp.int32, (t, LANES), 1)
    lg = jnp.where(lane < N_EXPERTS, lg_ref[0], -jnp.inf)
    ex = jnp.exp(lg - jnp.max(lg, axis=1, keepdims=True))
    aff = ex / jnp.sum(ex, axis=1, keepdims=True)
    for r in range(t // LANES):
        afft[:, r * LANES:(r + 1) * LANES] = aff[r * LANES:(r + 1) * LANES, :].T
    aff_t = afft[0:N_EXPERTS, :]

    def count(mask):
        return jnp.sum(jnp.where(mask, 1.0, 0.0), axis=1, keepdims=True).astype(jnp.int32)

    def bit_step(i, thr_bits):
        cand = thr_bits | (jnp.int32(1) << (30 - i))
        cand_f = lax.bitcast_convert_type(cand, F32)
        return jnp.where(count(aff_t >= cand_f) >= cap, cand, thr_bits)

    thr_bits = lax.fori_loop(0, 31, bit_step, jnp.zeros((N_EXPERTS, 1), jnp.int32))
    thr = lax.bitcast_convert_type(thr_bits, F32)
    gt = aff_t > thr
    eq = aff_t == thr
    need = cap - count(gt)
    eq_i = jnp.where(eq, 1, 0)
    eq_rank = _scan_lanes_i32(eq_i) - eq_i
    sel = gt | (eq & (eq_rank < need))
    sel_i = jnp.where(sel, 1, 0)
    pos = _scan_lanes_i32(sel_i) - sel_i
    key = jnp.where(sel, pos, -1)

    a_hi = aff.astype(BF16).astype(F32)
    a_mid = (aff - a_hi).astype(BF16).astype(F32)
    a_lo = (aff - a_hi - a_mid).astype(BF16).astype(F32)
    tok = lax.broadcasted_iota(jnp.int32, (t, LANES), 0)
    vals = (a_hi + pltpu.roll(a_mid, N_EXPERTS, 1) + pltpu.roll(a_lo, 2 * N_EXPERTS, 1)
            + jnp.where(lane == 3 * N_EXPERTS, (tok >> GRID_SHIFT).astype(F32), 0.0)
            + jnp.where(lane == 3 * N_EXPERTS + 1, (tok & (GRID_W - 1)).astype(F32), 0.0)).astype(BF16)
    slot = lax.broadcasted_iota(jnp.int32, (cap, 1), 0)
    for e in range(N_EXPERTS):
        onehot = jnp.where(slot == key[e:e + 1, :], 1.0, 0.0).astype(BF16)
        res = jnp.dot(onehot, vals, preferred_element_type=F32)
        rt = jnp.concatenate([res[r * LANES:(r + 1) * LANES, :].T for r in range(cap // LANES)], axis=1)
        gate_ref[0, e:e + 1, :] = (rt[e:e + 1, :] + rt[N_EXPERTS + e:N_EXPERTS + e + 1, :]) \
            + rt[2 * N_EXPERTS + e:2 * N_EXPERTS + e + 1, :]
        tokf = rt[3 * N_EXPERTS:3 * N_EXPERTS + 1, :] * float(GRID_W) + rt[3 * N_EXPERTS + 1:3 * N_EXPERTS + 2, :]
        idx_ref[0, e:e + 1, :] = tokf.astype(jnp.int32)


def _route(lg3, cap):
    b_, t, _ = lg3.shape
    return pl.pallas_call(
        _route_kernel,
        grid=(b_,),
        in_specs=[pl.BlockSpec((1, t, LANES), lambda b: (b, 0, 0))],
        out_specs=[pl.BlockSpec((1, N_EXPERTS, cap), lambda b: (b, 0, 0)),
                   pl.BlockSpec((1, N_EXPERTS, cap), lambda b: (b, 0, 0))],
        out_shape=[jax.ShapeDtypeStruct((b_, N_EXPERTS, cap), jnp.int32),
                   jax.ShapeDtypeStruct((b_, N_EXPERTS, cap), F32)],
        scratch_shapes=[pltpu.VMEM((LANES, t), F32)],
        compiler_params=_cparams(("arbitrary",)),
    )(lg3)


def _ffn_kernel(xs_ref, gate_ref, wg_ref, wu_ref, wd_ref, y_ref):
    xs = xs_ref[0, 0]
    a = _silu(jnp.dot(xs, wg_ref[0], preferred_element_type=F32)) * jnp.dot(xs, wu_ref[0], preferred_element_type=F32)
    y = jnp.dot(a.astype(BF16), wd_ref[0], preferred_element_type=F32)
    y_ref[0, 0] = (y * gate_ref[0, 0]).astype(BF16)


def _ffn(xs, gate4, wg, wu, wd):
    b_, e_, cap, d = xs.shape
    f = wg.shape[2]
    return pl.pallas_call(
        _ffn_kernel,
        grid=(e_, b_),
        in_specs=[pl.BlockSpec((1, 1, cap, d), lambda e, b: (b, e, 0, 0)),
                  pl.BlockSpec((1, 1, cap, 1), lambda e, b: (b, e, 0, 0)),
                  pl.BlockSpec((1, d, f), lambda e, b: (e, 0, 0)),
                  pl.BlockSpec((1, d, f), lambda e, b: (e, 0, 0)),
                  pl.BlockSpec((1, f, d), lambda e, b: (e, 0, 0))],
        out_specs=pl.BlockSpec((1, 1, cap, d), lambda e, b: (b, e, 0, 0)),
        out_shape=jax.ShapeDtypeStruct((b_, e_, cap, d), BF16),
        compiler_params=_cparams(("arbitrary", "arbitrary")),
    )(xs, gate4, wg, wu, wd)


COMBINE_SUB = 256


def _combine_kernel(y_ref, idx_ref, x1_ref, mod_ref, nf_ref, o_ref):
    i = pl.program_id(1)
    e = pl.program_id(2)
    tm = o_ref.shape[1]

    @pl.when(e == 0)
    def _():
        o_ref[...] = jnp.zeros(o_ref.shape, F32)

    y = y_ref[0, 0]
    tok_row = idx_ref[0, 0]
    tok_col = lax.broadcasted_iota(jnp.int32, (COMBINE_SUB, 1), 0) + i * tm
    for j in range(tm // COMBINE_SUB):
        onehot = jnp.where(tok_col + j * COMBINE_SUB == tok_row, 1.0, 0.0).astype(BF16)
        o_ref[0, j * COMBINE_SUB:(j + 1) * COMBINE_SUB, :] += jnp.dot(onehot, y, preferred_element_type=F32)

    @pl.when(e == pl.num_programs(2) - 1)
    def _():
        x = x1_ref[0] + mod_ref[0, 5:6, :] * o_ref[0]
        o_ref[0] = (x * lax.rsqrt(jnp.mean(x * x, axis=-1, keepdims=True) + EPS)) * nf_ref[...]


def _combine(y, idx4, x1, mod, nf, tm):
    b_, e_, cap, d = y.shape
    t = x1.shape[1]
    return pl.pallas_call(
        _combine_kernel,
        grid=(b_, t // tm, e_),
        in_specs=[pl.BlockSpec((1, 1, cap, d), lambda b, i, e: (b, e, 0, 0)),
                  pl.BlockSpec((1, 1, 1, cap), lambda b, i, e: (b, e, 0, 0)),
                  pl.BlockSpec((1, tm, d), lambda b, i, e: (b, i, 0)),
                  pl.BlockSpec((1, N_MOD, d), lambda b, i, e: (b, 0, 0)),
                  pl.BlockSpec((1, d), lambda b, i, e: (0, 0))],
        out_specs=pl.BlockSpec((1, tm, d), lambda b, i, e: (b, i, 0)),
        out_shape=jax.ShapeDtypeStruct((b_, t, d), F32),
        compiler_params=_cparams(("arbitrary", "arbitrary", "arbitrary")),
    )(y, idx4, x1, mod, nf)


def _gate_layout(g, b_, t):
    g = g[:, :2 * 2 * N_HEADS].reshape(b_, t, 2, 2, N_HEADS)
    g = g.transpose(0, 4, 2, 3, 1)
    return g.reshape(b_, N_HEADS, 4, t // MLSTM_CHUNK, MLSTM_CHUNK)


def kernel(x, c, ctx, c_ctx, w_mod, b_mod, norm1, w_in, conv_q_w, conv_k_w, b_gates, head_g, pool_w,
           pool_scale, w_out, norm2, w_router, w_gate, w_up, w_down, norm_f):
    b_, t, d = x.shape
    tc = ctx.shape[1]
    depth = w_mod.shape[0]
    assert depth == 1
    l = 0
    qk_w = N_HEADS * DQK
    mw = N_HEADS * DV
    pool_wd = len(POOL_WINDOWS) * POOL_GC
    n_gate = 2 * 2 * N_HEADS
    k_off, v_off = 0, qk_w
    g_off = v_off + mw
    q_off = g_off + n_gate
    o_off = q_off + qk_w
    p_off = o_off + mw
    cap = EC_FACTOR * t // N_EXPERTS

    mod_rows = -(-(b_ + 1) // SUBLANES) * SUBLANES
    cs = jnp.zeros((mod_rows, d), F32).at[:b_].set(c).at[b_].set(c_ctx)
    mod_all = _modulation(cs, w_mod[l], b_mod[l])
    mod = mod_all[:b_].reshape(b_, N_MOD, d)
    mod_c = mod_all[b_:b_ + 1].reshape(1, N_MOD, d)

    wl = w_in[l]
    w_main = jnp.concatenate([wl[:, k_off:v_off], wl[:, v_off:g_off], wl[:, q_off:o_off],
                              wl[:, o_off:p_off], wl[:, p_off:]], axis=1).astype(BF16)
    w_g = jnp.pad(wl[:, g_off:q_off], ((0, 0), (0, LANES - n_gate))).astype(BF16)
    b_g = jnp.pad(b_gates[l], (0, LANES - n_gate)).reshape(1, LANES)
    n1 = norm1[l].reshape(1, d)

    p, g = _inproj(x.reshape(b_ * t, d), mod, t, n1, w_main, w_g, b_g, w_main.shape[1], 1024, 512)
    pc, gc = _inproj(ctx.reshape(b_ * tc, d), mod_c, b_ * tc, n1, w_main, w_g, b_g, qk_w + mw,
                     min(1024, b_ * tc), 512)
    p = p.reshape(b_, t, -1)
    pc = pc.reshape(b_, tc, -1)

    wk = jnp.pad(conv_k_w[l], ((0, SUBLANES - QK_CONV), (0, 0)))
    wq = jnp.pad(conv_q_w[l], ((0, SUBLANES - QK_CONV), (0, 0)))
    ym = _mlstm(p, pc, _gate_layout(g, b_, t), _gate_layout(gc, b_, tc), wk, wq, head_g[l].reshape(1, mw))
    yp = _pool(p, pool_w[l].astype(BF16), pool_scale[l].reshape(1, pool_wd))

    wo = w_out[l].astype(BF16)
    wr = jnp.pad(w_router[l], ((0, 0), (0, LANES - N_EXPERTS))).astype(BF16)
    x1, h2, lg = _outproj(ym.reshape(b_ * t, mw), yp.reshape(b_ * t, pool_wd), wo[:mw], wo[mw:],
                          x.reshape(b_ * t, d), mod, t, norm2[l].reshape(1, d), wr, 512)

    idx, gate = _route(lg.reshape(b_, t, LANES), cap)

    h2 = h2.reshape(b_, t, d)
    xs = jax.vmap(lambda hb, ib: hb[ib])(h2, idx)
    y = _ffn(xs, gate.reshape(b_, N_EXPERTS, cap, 1), w_gate[l].astype(BF16), w_up[l].astype(BF16),
             w_down[l].astype(BF16))
    return _combine(y, idx.reshape(b_, N_EXPERTS, 1, cap), x1.reshape(b_, t, d), mod, norm_f.reshape(1, d), 1024)
```

```python
import functools

import jax
import jax.numpy as jnp
from jax import lax
from jax.experimental import pallas as pl
from jax.experimental.pallas import tpu as pltpu

F32 = jnp.float32
BF16 = jnp.bfloat16

N_HEADS = 4
DQK = 128
DV = 256
QK_CONV = 5
POOL_WINDOWS = (2, 4, 8, 16)
POOL_GC = 256
GRID_W = 64
GRID_SHIFT = GRID_W.bit_length() - 1
N_EXPERTS = 16
EC_FACTOR = 2
N_MOD = 6
EPS = 1e-6

LANES = 128
SUBLANES = 8
MLSTM_CHUNK = 128
VMEM_LIMIT = 56 * 1024 * 1024


def _cparams(sem):
    return pltpu.CompilerParams(dimension_semantics=sem, vmem_limit_bytes=VMEM_LIMIT)


def _sigmoid(x):
    return 1.0 / (1.0 + jnp.exp(-x))


def _silu(x):
    return x * _sigmoid(x)


def _log_sigmoid(x):
    return jnp.minimum(x, 0.0) - jnp.log(1.0 + jnp.exp(-jnp.abs(x)))


def _mod_kernel(c_ref, w_ref, b_ref, o_ref):
    s = _silu(c_ref[...]).astype(BF16)
    o_ref[...] = jnp.dot(s, w_ref[...].astype(BF16), preferred_element_type=F32) + b_ref[...]


def _modulation(cs, w_mod, b_mod):
    rows, d = cs.shape
    n = w_mod.shape[1]
    tn = 1024
    return pl.pallas_call(
        _mod_kernel,
        grid=(n // tn,),
        in_specs=[pl.BlockSpec((rows, d), lambda j: (0, 0)),
                  pl.BlockSpec((d, tn), lambda j: (0, j)),
                  pl.BlockSpec((1, tn), lambda j: (0, j))],
        out_specs=pl.BlockSpec((rows, tn), lambda j: (0, j)),
        out_shape=jax.ShapeDtypeStruct((rows, n), F32),
        compiler_params=_cparams(("arbitrary",)),
    )(cs, w_mod, b_mod.reshape(1, n))


def _inproj_kernel(x_ref, mod_ref, n1_ref, w_ref, wg_ref, bg_ref, p_ref, g_ref, h_scr):
    @pl.when(pl.program_id(1) == 0)
    def _():
        x = x_ref[...]
        y = x * lax.rsqrt(jnp.mean(x * x, axis=-1, keepdims=True) + EPS)
        h = (y * n1_ref[...]) * (1.0 + mod_ref[0, 1:2, :]) + mod_ref[0, 0:1, :]
        hb = h.astype(BF16)
        h_scr[...] = hb
        g_ref[...] = jnp.dot(hb, wg_ref[...], preferred_element_type=F32) + bg_ref[...]

    p_ref[...] = jnp.dot(h_scr[...], w_ref[...], preferred_element_type=F32).astype(BF16)


def _inproj(x2, mod, rows_per_mod, n1, w, wg, bg, n_cols, tm, tn):
    m, d = x2.shape
    tiles_per_mod = rows_per_mod // tm
    return pl.pallas_call(
        _inproj_kernel,
        grid=(m // tm, n_cols // tn),
        in_specs=[pl.BlockSpec((tm, d), lambda i, j: (i, 0)),
                  pl.BlockSpec((1, N_MOD, d), lambda i, j: (i // tiles_per_mod, 0, 0)),
                  pl.BlockSpec((1, d), lambda i, j: (0, 0)),
                  pl.BlockSpec((d, tn), lambda i, j: (0, j)),
                  pl.BlockSpec((d, LANES), lambda i, j: (0, 0)),
                  pl.BlockSpec((1, LANES), lambda i, j: (0, 0))],
        out_specs=[pl.BlockSpec((tm, tn), lambda i, j: (i, j)),
                   pl.BlockSpec((tm, LANES), lambda i, j: (i, 0))],
        out_shape=[jax.ShapeDtypeStruct((m, n_cols), BF16),
                   jax.ShapeDtypeStruct((m, LANES), F32)],
        scratch_shapes=[pltpu.VMEM((tm, d), BF16)],
        compiler_params=_cparams(("arbitrary", "arbitrary")),
    )(x2, mod, n1, w, wg, bg)


def _scan_lanes(x, op, fill, reverse):
    lane = lax.broadcasted_iota(jnp.int32, x.shape, 1)
    k = 1
    while k < LANES:
        if reverse:
            x = op(x, jnp.where(lane < LANES - k, pltpu.roll(x, LANES - k, 1), fill))
        else:
            x = op(x, jnp.where(lane >= k, pltpu.roll(x, k, 1), fill))
        k *= 2
    return x


G_B, G_R, G_W, G_BEND, G_MLOC, G_RMAX, G_ROWS = 0, 1, 2, 3, 4, 5, 6


def _mlstm_kernel(kc_ref, vc_ref, gc_ref, k_ref, v_ref, q_ref, o_ref, g_ref, wk_ref, wq_ref, hg_ref,
                  out_ref,
                  cpad, ktt, qt, kctt, vext, vcext, gs, gsc, cf, cb, hf, hb):
    L = MLSTM_CHUNK
    t_lat = k_ref.shape[1]
    t_ctx = kc_ref.shape[1]
    nc = t_lat // L
    ncc = t_ctx // L
    pad = SUBLANES
    half = QK_CONV // 2

    def conv_silu(src, w_ref, dst, t, scale, transposed):
        cpad[0:pad, :] = jnp.zeros((pad, LANES), F32)
        cpad[pad:pad + t, :] = src.astype(F32)
        cpad[pad + t:pad + t + pad, :] = jnp.zeros((pad, LANES), F32)
        for r in range(t // L):
            base = pad + r * L - half
            acc = w_ref[0:1, :] * cpad[base:base + L, :]
            for j in range(1, QK_CONV):
                acc = acc + w_ref[j:j + 1, :] * cpad[base + j:base + j + L, :]
            y = _silu(acc)
            if scale is not None:
                y = y * scale
            dst[r] = (y.T if transposed else y).astype(BF16)

    conv_silu(kc_ref[0], wk_ref, kctt, t_ctx, DQK ** -0.5, True)
    conv_silu(k_ref[0], wk_ref, ktt, t_lat, DQK ** -0.5, True)
    conv_silu(q_ref[0], wq_ref, qt, t_lat, None, False)

    def fill_vext(dst, src, t):
        dst[:, 0:DV] = src
        dst[:, DV:DV + LANES] = jnp.ones((t, LANES), BF16)

    fill_vext(vcext, vc_ref[0], t_ctx)
    fill_vext(vext, v_ref[0], t_lat)

    def gate_prep(garr, dst):
        for d in range(2):
            li = garr[2 * d]
            lf = _log_sigmoid(garr[2 * d + 1])
            b = _scan_lanes(lf, jnp.add, 0.0, reverse=(d == 1))
            b_end = b[:, LANES - 1:LANES] if d == 0 else b[:, 0:1]
            r = li - b
            a = b_end + r
            m_loc = jnp.max(a, axis=1, keepdims=True)
            dst[G_ROWS * d + G_B] = b
            dst[G_ROWS * d + G_R] = r
            dst[G_ROWS * d + G_W] = jnp.exp(a - m_loc)
            dst[G_ROWS * d + G_BEND] = jnp.broadcast_to(b_end, b.shape)
            dst[G_ROWS * d + G_MLOC] = jnp.broadcast_to(m_loc, b.shape)
            dst[G_ROWS * d + G_RMAX] = _scan_lanes(r, jnp.maximum, -jnp.inf, reverse=(d == 1))

    gate_prep(gc_ref[0, 0], gsc)
    gate_prep(g_ref[0, 0], gs)

    cf[...] = jnp.zeros(cf.shape, F32)
    cb[...] = jnp.zeros(cb.shape, F32)

    row_i = lax.broadcasted_iota(jnp.int32, (L, L), 0)
    col_i = lax.broadcasted_iota(jnp.int32, (L, L), 1)
    visible = (col_i <= row_i, col_i >= row_i)

    def per_token(row):
        return jnp.broadcast_to(row, (L, L)).T

    def step(c, d, g_scr, kt_scr, v_scr, q_scr, c_scr, m_prev):
        off = c * L if isinstance(c, int) else pl.multiple_of(c * L, L)
        g0 = G_ROWS * d
        kt_c = kt_scr[c]
        v_c = v_scr[pl.ds(off, L), :]
        b_end = g_scr[g0 + G_BEND, pl.ds(c, 1), 0:1]
        m_loc = g_scr[g0 + G_MLOC, pl.ds(c, 1), 0:1]
        c_prev = c_scr[...]
        h = None
        if q_scr is not None:
            q_c = q_scr[c]
            r_row = g_scr[g0 + G_R, pl.ds(c, 1), :]
            u = jnp.maximum(per_token(g_scr[g0 + G_RMAX, pl.ds(c, 1), :]), m_prev)
            b_t = per_token(g_scr[g0 + G_B, pl.ds(c, 1), :])
            qk = jnp.dot(q_c, kt_c, preferred_element_type=F32)
            s = (qk * jnp.exp(jnp.where(visible[d], r_row - u, -jnp.inf))).astype(BF16)
            wi = jnp.exp(m_prev - u)
            inter = jnp.dot(q_c, c_prev.astype(BF16), preferred_element_type=F32)
            res = jnp.dot(s, v_c, preferred_element_type=F32) \
                + jnp.concatenate([wi] * (c_prev.shape[1] // L), axis=1) * inter
            inv = 1.0 / jnp.maximum(jnp.abs(res[:, DV:DV + L]), jnp.exp(-(b_t + u)))
            h = res[:, 0:DV] * jnp.concatenate([inv] * (DV // L), axis=1)
        kwt = (kt_c.astype(F32) * g_scr[g0 + G_W, pl.ds(c, 1), :]).astype(BF16)
        c_loc = jnp.dot(kwt, v_c, preferred_element_type=F32)
        m_new = jnp.maximum(b_end + m_prev, m_loc)
        sp = jnp.exp(b_end + m_prev - m_new)
        sl = jnp.exp(m_loc - m_new)
        c_scr[...] = sp * c_prev + sl * c_loc
        return m_new, h

    def finish(c, h):
        off = pl.multiple_of(c * L, L)
        h = h * lax.rsqrt(jnp.mean(h * h, axis=-1, keepdims=True) + EPS)
        y = (h * hg_ref[...]) * _sigmoid(o_ref[0, pl.ds(off, L), :].astype(F32))
        out_ref[0, pl.ds(off, L), :] = y.astype(BF16)

    m_f = jnp.zeros((1, 1), F32)
    m_b = jnp.zeros((1, 1), F32)
    for i in range(ncc):
        m_f, _ = step(i, 0, gsc, kctt, vcext, None, cf, m_f)
        m_b, _ = step(ncc - 1 - i, 1, gsc, kctt, vcext, None, cb, m_b)

    assert nc % 2 == 0

    def first_half(i, carry):
        c_f, c_b = i, nc - 1 - i
        m_f, h_f = step(c_f, 0, gs, ktt, vext, qt, cf, carry[0])
        m_b, h_b = step(c_b, 1, gs, ktt, vext, qt, cb, carry[1])
        hf[pl.ds(pl.multiple_of(c_f * L, L), L), :] = h_f
        hb[pl.ds(pl.multiple_of(c_b * L, L), L), :] = h_b
        return m_f, m_b

    def second_half(i, carry):
        c_f, c_b = i, nc - 1 - i
        m_f, h_f = step(c_f, 0, gs, ktt, vext, qt, cf, carry[0])
        m_b, h_b = step(c_b, 1, gs, ktt, vext, qt, cb, carry[1])
        finish(c_f, h_f + hb[pl.ds(pl.multiple_of(c_f * L, L), L), :])
        finish(c_b, h_b + hf[pl.ds(pl.multiple_of(c_b * L, L), L), :])
        return m_f, m_b

    carry = lax.fori_loop(0, nc // 2, first_half, (m_f, m_b), unroll=2)
    lax.fori_loop(nc // 2, nc, second_half, carry, unroll=2)


def _mlstm(p, pc, g4, gc4, wk, wq, head_g):
    b_, t, _ = p.shape
    tc = pc.shape[1]
    L = MLSTM_CHUNK
    nc, ncc = t // L, tc // L
    ext = DV + LANES
    return pl.pallas_call(
        _mlstm_kernel,
        grid=(b_, N_HEADS),
        in_specs=[
            pl.BlockSpec((1, tc, DQK), lambda b, h: (b, 0, h)),
            pl.BlockSpec((1, tc, DV), lambda b, h: (b, 0, 2 + h)),
            pl.BlockSpec((1, 1, 4, ncc, L), lambda b, h: (b, h, 0, 0, 0)),
            pl.BlockSpec((1, t, DQK), lambda b, h: (b, 0, h)),
            pl.BlockSpec((1, t, DV), lambda b, h: (b, 0, 2 + h)),
            pl.BlockSpec((1, t, DQK), lambda b, h: (b, 0, 12 + h)),
            pl.BlockSpec((1, t, DV), lambda b, h: (b, 0, 8 + h)),
            pl.BlockSpec((1, 1, 4, nc, L), lambda b, h: (b, h, 0, 0, 0)),
            pl.BlockSpec((SUBLANES, DQK), lambda b, h: (0, h)),
            pl.BlockSpec((SUBLANES, DQK), lambda b, h: (0, h)),
            pl.BlockSpec((1, DV), lambda b, h: (0, h)),
        ],
        out_specs=pl.BlockSpec((1, t, DV), lambda b, h: (b, 0, h)),
        out_shape=jax.ShapeDtypeStruct((b_, t, N_HEADS * DV), BF16),
        scratch_shapes=[
            pltpu.VMEM((t + 2 * SUBLANES, LANES), F32),
            pltpu.VMEM((nc, DQK, L), BF16),
            pltpu.VMEM((nc, L, DQK), BF16),
            pltpu.VMEM((ncc, DQK, L), BF16),
            pltpu.VMEM((t, ext), BF16),
            pltpu.VMEM((tc, ext), BF16),
            pltpu.VMEM((2 * G_ROWS, nc, L), F32),
            pltpu.VMEM((2 * G_ROWS, ncc, L), F32),
            pltpu.VMEM((DQK, ext), F32),
            pltpu.VMEM((DQK, ext), F32),
            pltpu.VMEM((t, DV), F32),
            pltpu.VMEM((t, DV), F32),
        ],
        compiler_params=_cparams(("arbitrary", "arbitrary")),
    )(pc, pc, gc4, p, p, p, p, g4, wk, wq, head_g)


def _pool_kernel(u_ref, pw_ref, ps_ref, out_ref, spad):
    t = u_ref.shape[1]
    rows = t // GRID_W
    blk = 256
    halo = (max(POOL_WINDOWS) // 2) * GRID_W
    spad[0:halo, :] = jnp.zeros((halo, POOL_GC), F32)
    spad[halo + t:halo + t + halo, :] = jnp.zeros((halo, POOL_GC), F32)
    ti = lax.broadcasted_iota(jnp.int32, (blk, blk), 0)
    tj = lax.broadcasted_iota(jnp.int32, (blk, blk), 1)
    same_row = (ti >> GRID_SHIFT) == (tj >> GRID_SHIFT)
    diff = tj - ti
    tok = lax.broadcasted_iota(jnp.int32, (blk, 1), 0)
    for gi, w in enumerate(POOL_WINDOWS):
        hw = w // 2
        c0, c1 = gi * POOL_GC, (gi + 1) * POOL_GC
        band = jnp.where(same_row & (diff >= -hw) & (diff <= hw - 1), 1.0, 0.0).astype(BF16)
        for r in range(t // blk):
            spad[halo + r * blk:halo + (r + 1) * blk, :] = jnp.dot(
                band, u_ref[0, r * blk:(r + 1) * blk, c0:c1], preferred_element_type=F32)
        for r in range(t // blk):
            base = halo + r * blk
            acc = spad[base - hw * GRID_W:base - hw * GRID_W + blk, :]
            for j in range(-hw + 1, hw):
                acc = acc + spad[base + j * GRID_W:base + j * GRID_W + blk, :]
            tt = tok + r * blk
            gr = tt >> GRID_SHIFT
            gc = tt & (GRID_W - 1)
            cr = jnp.minimum(gr + hw, rows) - jnp.maximum(gr - hw, 0)
            cc = jnp.minimum(gc + hw, GRID_W) - jnp.maximum(gc - hw, 0)
            cnt = (cr * cc).astype(F32)
            d = acc / cnt - u_ref[0, r * blk:(r + 1) * blk, c0:c1].astype(F32)
            y = jnp.dot(d.astype(BF16), pw_ref[gi], preferred_element_type=F32) * ps_ref[:, c0:c1]
            out_ref[0, r * blk:(r + 1) * blk, c0:c1] = y.astype(BF16)


def _pool(p, pool_w, pool_scale):
    b_, t, _ = p.shape
    pw = len(POOL_WINDOWS) * POOL_GC
    halo = (max(POOL_WINDOWS) // 2) * GRID_W
    return pl.pallas_call(
        _pool_kernel,
        grid=(b_,),
        in_specs=[pl.BlockSpec((1, t, pw), lambda b: (b, 0, 3)),
                  pl.BlockSpec((len(POOL_WINDOWS), POOL_GC, POOL_GC), lambda b: (0, 0, 0)),
                  pl.BlockSpec((1, pw), lambda b: (0, 0))],
        out_specs=pl.BlockSpec((1, t, pw), lambda b: (b, 0, 0)),
        out_shape=jax.ShapeDtypeStruct((b_, t, pw), BF16),
        scratch_shapes=[pltpu.VMEM((t + 2 * halo, POOL_GC), F32)],
        compiler_params=_cparams(("arbitrary",)),
    )(p, pool_w, pool_scale)


def _outproj_kernel(ym_ref, yp_ref, wm_ref, wp_ref, x_ref, mod_ref, n2_ref, wr_ref,
                    x1_ref, h2_ref, lg_ref):
    mix = (jnp.dot(ym_ref[...], wm_ref[...], preferred_element_type=F32)
           + jnp.dot(yp_ref[...], wp_ref[...], preferred_element_type=F32))
    x1 = x_ref[...] + mod_ref[0, 2:3, :] * mix
    x1_ref[...] = x1
    y = x1 * lax.rsqrt(jnp.mean(x1 * x1, axis=-1, keepdims=True) + EPS)
    h2 = (y * n2_ref[...]) * (1.0 + mod_ref[0, 4:5, :]) + mod_ref[0, 3:4, :]
    for j in range(h2_ref.shape[1]):
        h2_ref[:, j, :, :] = h2[:, j * LANES:(j + 1) * LANES].reshape(h2_ref.shape[0], SUBLANES, LANES)
    lg_ref[...] = jnp.dot(h2.astype(BF16), wr_ref[...], preferred_element_type=F32)


def _outproj(ym, yp, wm, wp, x2, mod, rows_per_mod, n2, wr, tm):
    m, d = x2.shape
    km = ym.shape[1]
    kp = yp.shape[1]
    tiles_per_mod = rows_per_mod // tm
    return pl.pallas_call(
        _outproj_kernel,
        grid=(m // tm,),
        in_specs=[pl.BlockSpec((tm, km), lambda i: (i, 0)),
                  pl.BlockSpec((tm, kp), lambda i: (i, 0)),
                  pl.BlockSpec((km, d), lambda i: (0, 0), pipeline_mode=pl.Buffered(1)),
                  pl.BlockSpec((kp, d), lambda i: (0, 0), pipeline_mode=pl.Buffered(1)),
                  pl.BlockSpec((tm, d), lambda i: (i, 0)),
                  pl.BlockSpec((1, N_MOD, d), lambda i: (i // tiles_per_mod, 0, 0)),
                  pl.BlockSpec((1, d), lambda i: (0, 0)),
                  pl.BlockSpec((d, LANES), lambda i: (0, 0))],
        out_specs=[pl.BlockSpec((tm, d), lambda i: (i, 0)),
                   pl.BlockSpec((tm // SUBLANES, d // LANES, SUBLANES, LANES), lambda i: (i, 0, 0, 0)),
                   pl.BlockSpec((tm, LANES), lambda i: (i, 0))],
        out_shape=[jax.ShapeDtypeStruct((m, d), F32),
                   jax.ShapeDtypeStruct((m // SUBLANES, d // LANES, SUBLANES, LANES), F32),
                   jax.ShapeDtypeStruct((m, LANES), F32)],
        compiler_params=_cparams(("arbitrary",)),
    )(ym, yp, wm, wp, x2, mod, n2, wr)


def _scan_lanes_i32(x, exclusive_of=None):
    n = x.shape[1]
    lane = lax.broadcasted_iota(jnp.int32, x.shape, 1)
    k = 1
    while k < n:
        x = x + jnp.where(lane >= k, pltpu.roll(x, k, 1), 0)
        k *= 2
    return x


def _route_kernel(lg_ref, idx_ref, gate_ref, afft):
    t = lg_ref.shape[1]
    cap = idx_ref.shape[2]
    lane = lax.broadcasted_iota(jnp.int32, (t, LANES), 1)
    lg = jnp.where(lane < N_EXPERTS, lg_ref[0], -jnp.inf)
    ex = jnp.exp(lg - jnp.max(lg, axis=1, keepdims=True))
    aff = ex / jnp.sum(ex, axis=1, keepdims=True)
    for r in range(t // LANES):
        afft[:, r * LANES:(r + 1) * LANES] = aff[r * LANES:(r + 1) * LANES, :].T
    aff_t = afft[0:N_EXPERTS, :]

    def count(mask):
        return jnp.sum(jnp.where(mask, 1.0, 0.0), axis=1, keepdims=True).astype(jnp.int32)

    def bit_step(i, thr_bits):
        cand = thr_bits | (jnp.int32(1) << (30 - i))
        cand_f = lax.bitcast_convert_type(cand, F32)
        return jnp.where(count(aff_t >= cand_f) >= cap, cand, thr_bits)

    thr_bits = lax.fori_loop(0, 31, bit_step, jnp.zeros((N_EXPERTS, 1), jnp.int32))
    thr = lax.bitcast_convert_type(thr_bits, F32)
    gt = aff_t > thr
    eq = aff_t == thr
    need = cap - count(gt)
    eq_i = jnp.where(eq, 1, 0)
    eq_rank = _scan_lanes_i32(eq_i) - eq_i
    sel = gt | (eq & (eq_rank < need))
    sel_i = jnp.where(sel, 1, 0)
    pos = _scan_lanes_i32(sel_i) - sel_i
    key = jnp.where(sel, pos, -1)

    a_hi = aff.astype(BF16).astype(F32)
    a_mid = (aff - a_hi).astype(BF16).astype(F32)
    a_lo = (aff - a_hi - a_mid).astype(BF16).astype(F32)
    tok = lax.broadcasted_iota(jnp.int32, (t, LANES), 0)
    vals = (a_hi + pltpu.roll(a_mid, N_EXPERTS, 1) + pltpu.roll(a_lo, 2 * N_EXPERTS, 1)
            + jnp.where(lane == 3 * N_EXPERTS, (tok >> GRID_SHIFT).astype(F32), 0.0)
            + jnp.where(lane == 3 * N_EXPERTS + 1, (tok & (GRID_W - 1)).astype(F32), 0.0)).astype(BF16)
    slot = lax.broadcasted_iota(jnp.int32, (cap, 1), 0)
    for e in range(N_EXPERTS):
        onehot = jnp.where(slot == key[e:e + 1, :], 1.0, 0.0).astype(BF16)
        res = jnp.dot(onehot, vals, preferred_element_type=F32)
        rt = jnp.concatenate([res[r * LANES:(r + 1) * LANES, :].T for r in range(cap // LANES)], axis=1)
        gate_ref[0, e:e + 1, :] = (rt[e:e + 1, :] + rt[N_EXPERTS + e:N_EXPERTS + e + 1, :]) \
            + rt[2 * N_EXPERTS + e:2 * N_EXPERTS + e + 1, :]
        tokf = rt[3 * N_EXPERTS:3 * N_EXPERTS + 1, :] * float(GRID_W) + rt[3 * N_EXPERTS + 1:3 * N_EXPERTS + 2, :]
        idx_ref[0, e:e + 1, :] = tokf.astype(jnp.int32)


def _route(lg3, cap):
    b_, t, _ = lg3.shape
    return pl.pallas_call(
        _route_kernel,
        grid=(b_,),
        in_specs=[pl.BlockSpec((1, t, LANES), lambda b: (b, 0, 0))],
        out_specs=[pl.BlockSpec((1, N_EXPERTS, cap), lambda b: (b, 0, 0)),
                   pl.BlockSpec((1, N_EXPERTS, cap), lambda b: (b, 0, 0))],
        out_shape=[jax.ShapeDtypeStruct((b_, N_EXPERTS, cap), jnp.int32),
                   jax.ShapeDtypeStruct((b_, N_EXPERTS, cap), F32)],
        scratch_shapes=[pltpu.VMEM((LANES, t), F32)],
        compiler_params=_cparams(("arbitrary",)),
    )(lg3)


def _ffn_kernel(idx_ref, gate_ref, h2_hbm, wg_ref, wu_ref, wd_ref, y_ref,
                xs_buf, wg_s, wu_s, wd_s, sem):
    p = pl.program_id(0)
    b = pl.program_id(1)
    n_e = pl.num_programs(0) - 1
    n_b = pl.num_programs(1)
    k_chunks, cap = xs_buf.shape[1], xs_buf.shape[2]
    groups = h2_hbm.shape[0] // n_b
    step = p * n_b + b
    cur = step % 2
    nxt = 1 - cur
    last_b = b == n_b - 1
    b_next = jnp.where(last_b, 0, b + 1)
    p_next = jnp.where(last_b, p + 1, p)

    @pl.when((p_next >= 1) & (p_next <= n_e))
    def _():
        def issue(r, carry):
            tok = idx_ref[0, 0, r]
            pltpu.make_async_copy(h2_hbm.at[b_next * groups + (tok >> 3), :, tok & (SUBLANES - 1), :],
                                  xs_buf.at[nxt, :, r, :], sem.at[nxt]).start()
            return carry

        lax.fori_loop(0, cap, issue, 0, unroll=8)

    @pl.when(p < n_e)
    def _():
        ws = p % 2
        rows_in = wg_ref.shape[1]
        rows_mid = wd_ref.shape[1]
        wg_s[ws, pl.ds(pl.multiple_of(b * rows_in, rows_in), rows_in), :] = wg_ref[0].astype(BF16)
        wu_s[ws, pl.ds(pl.multiple_of(b * rows_in, rows_in), rows_in), :] = wu_ref[0].astype(BF16)
        wd_s[ws, pl.ds(pl.multiple_of(b * rows_mid, rows_mid), rows_mid), :] = wd_ref[0].astype(BF16)

    @pl.when(p == 0)
    def _():
        y_ref[0, 0] = jnp.zeros(y_ref.shape[2:], BF16)

    @pl.when(p >= 1)
    def _():
        pltpu.make_async_copy(xs_buf.at[cur], xs_buf.at[cur], sem.at[cur]).wait()
        ws = (p - 1) % 2
        xs = jnp.concatenate([xs_buf[cur, j].astype(BF16) for j in range(k_chunks)], axis=1)
        a = _silu(jnp.dot(xs, wg_s[ws], preferred_element_type=F32)) \
            * jnp.dot(xs, wu_s[ws], preferred_element_type=F32)
        y = jnp.dot(a.astype(BF16), wd_s[ws], preferred_element_type=F32)
        y_ref[0, 0] = (y * gate_ref[0, 0]).astype(BF16)


def _ffn(idx, gate, h2, wg, wu, wd):
    e_, d, f = wg.shape
    b_, _, cap = idx.shape
    assert h2.shape[1] * h2.shape[3] == d and h2.shape[2] == SUBLANES and h2.shape[0] % b_ == 0
    assert d % b_ == 0 and f % b_ == 0
    rows_in, rows_mid = d // b_, f // b_
    assert rows_in % (2 * SUBLANES) == 0 and rows_mid % (2 * SUBLANES) == 0

    def next_tokens(p, b):
        last_b = b == b_ - 1
        b_next = jnp.where(last_b, 0, b + 1)
        e_next = jnp.clip(jnp.where(last_b, p, p - 1), 0, e_ - 1)
        return (b_next * e_ + e_next, 0, 0)

    def this_expert(p, b):
        return (b, jnp.maximum(p - 1, 0), 0, 0)

    def weight_piece(p, b):
        return (jnp.minimum(p, e_ - 1), b, 0)

    return pl.pallas_call(
        _ffn_kernel,
        grid=(e_ + 1, b_),
        in_specs=[pl.BlockSpec((1, 1, cap), next_tokens, memory_space=pltpu.SMEM),
                  pl.BlockSpec((1, 1, cap, 1), this_expert),
                  pl.BlockSpec(memory_space=pl.ANY),
                  pl.BlockSpec((1, rows_in, f), weight_piece),
                  pl.BlockSpec((1, rows_in, f), weight_piece),
                  pl.BlockSpec((1, rows_mid, d), weight_piece)],
        out_specs=pl.BlockSpec((1, 1, cap, d), lambda p, b: (b, p, 0, 0)),
        out_shape=jax.ShapeDtypeStruct((b_, e_ + 1, cap, d), BF16),
        scratch_shapes=[pltpu.VMEM((2, d // LANES, cap, LANES), F32),
                        pltpu.VMEM((2, d, f), BF16),
                        pltpu.VMEM((2, d, f), BF16),
                        pltpu.VMEM((2, f, d), BF16),
                        pltpu.SemaphoreType.DMA((2,))],
        compiler_params=_cparams(("arbitrary", "arbitrary")),
    )(idx.reshape(b_ * e_, 1, cap), gate.reshape(b_, e_, cap, 1), h2, wg, wu, wd)


COMBINE_SUB = 256


def _combine_kernel(y_ref, idx_ref, x1_ref, mod_ref, nf_ref, o_ref):
    i = pl.program_id(1)
    e = pl.program_id(2)
    tm = o_ref.shape[1]

    @pl.when(e == 0)
    def _():
        o_ref[...] = jnp.zeros(o_ref.shape, F32)

    y = y_ref[0, 0]
    tok_row = idx_ref[0, 0]
    tok_col = lax.broadcasted_iota(jnp.int32, (COMBINE_SUB, 1), 0) + i * tm
    for j in range(tm // COMBINE_SUB):
        onehot = jnp.where(tok_col + j * COMBINE_SUB == tok_row, 1.0, 0.0).astype(BF16)
        o_ref[0, j * COMBINE_SUB:(j + 1) * COMBINE_SUB, :] += jnp.dot(onehot, y, preferred_element_type=F32)

    @pl.when(e == pl.num_programs(2) - 1)
    def _():
        x = x1_ref[0] + mod_ref[0, 5:6, :] * o_ref[0]
        o_ref[0] = (x * lax.rsqrt(jnp.mean(x * x, axis=-1, keepdims=True) + EPS)) * nf_ref[...]


def _combine(y, idx4, x1, mod, nf, tm):
    b_, e_, cap, d = y.shape
    e_ -= 1
    t = x1.shape[1]
    return pl.pallas_call(
        _combine_kernel,
        grid=(b_, t // tm, e_),
        in_specs=[pl.BlockSpec((1, 1, cap, d), lambda b, i, e: (b, e + 1, 0, 0)),
                  pl.BlockSpec((1, 1, 1, cap), lambda b, i, e: (b, e, 0, 0)),
                  pl.BlockSpec((1, tm, d), lambda b, i, e: (b, i, 0)),
                  pl.BlockSpec((1, N_MOD, d), lambda b, i, e: (b, 0, 0)),
                  pl.BlockSpec((1, d), lambda b, i, e: (0, 0))],
        out_specs=pl.BlockSpec((1, tm, d), lambda b, i, e: (b, i, 0)),
        out_shape=jax.ShapeDtypeStruct((b_, t, d), F32),
        compiler_params=_cparams(("arbitrary", "arbitrary", "arbitrary")),
    )(y, idx4, x1, mod, nf)


def _gate_layout(g, b_, t):
    g = g[:, :2 * 2 * N_HEADS].reshape(b_, t, 2, 2, N_HEADS)
    g = g.transpose(0, 4, 2, 3, 1)
    return g.reshape(b_, N_HEADS, 4, t // MLSTM_CHUNK, MLSTM_CHUNK)


def kernel(x, c, ctx, c_ctx, w_mod, b_mod, norm1, w_in, conv_q_w, conv_k_w, b_gates, head_g, pool_w,
           pool_scale, w_out, norm2, w_router, w_gate, w_up, w_down, norm_f):
    b_, t, d = x.shape
    tc = ctx.shape[1]
    depth = w_mod.shape[0]
    assert depth == 1
    l = 0
    qk_w = N_HEADS * DQK
    mw = N_HEADS * DV
    pool_wd = len(POOL_WINDOWS) * POOL_GC
    n_gate = 2 * 2 * N_HEADS
    k_off, v_off = 0, qk_w
    g_off = v_off + mw
    q_off = g_off + n_gate
    o_off = q_off + qk_w
    p_off = o_off + mw
    cap = EC_FACTOR * t // N_EXPERTS

    mod_rows = -(-(b_ + 1) // SUBLANES) * SUBLANES
    cs = jnp.zeros((mod_rows, d), F32).at[:b_].set(c).at[b_].set(c_ctx)
    mod_all = _modulation(cs, w_mod[l], b_mod[l])
    mod = mod_all[:b_].reshape(b_, N_MOD, d)
    mod_c = mod_all[b_:b_ + 1].reshape(1, N_MOD, d)

    wl = w_in[l]
    w_main = jnp.concatenate([wl[:, k_off:v_off], wl[:, v_off:g_off], wl[:, q_off:o_off],
                              wl[:, o_off:p_off], wl[:, p_off:]], axis=1).astype(BF16)
    w_g = jnp.pad(wl[:, g_off:q_off], ((0, 0), (0, LANES - n_gate))).astype(BF16)
    b_g = jnp.pad(b_gates[l], (0, LANES - n_gate)).reshape(1, LANES)
    n1 = norm1[l].reshape(1, d)

    p, g = _inproj(x.reshape(b_ * t, d), mod, t, n1, w_main, w_g, b_g, w_main.shape[1], 1024, 512)
    pc, gc = _inproj(ctx.reshape(b_ * tc, d), mod_c, b_ * tc, n1, w_main, w_g, b_g, qk_w + mw,
                     min(1024, b_ * tc), 512)
    p = p.reshape(b_, t, -1)
    pc = pc.reshape(b_, tc, -1)

    wk = jnp.pad(conv_k_w[l], ((0, SUBLANES - QK_CONV), (0, 0)))
    wq = jnp.pad(conv_q_w[l], ((0, SUBLANES - QK_CONV), (0, 0)))
    ym = _mlstm(p, pc, _gate_layout(g, b_, t), _gate_layout(gc, b_, tc), wk, wq, head_g[l].reshape(1, mw))
    yp = _pool(p, pool_w[l].astype(BF16), pool_scale[l].reshape(1, pool_wd))

    wo = w_out[l].astype(BF16)
    wr = jnp.pad(w_router[l], ((0, 0), (0, LANES - N_EXPERTS))).astype(BF16)
    x1, h2, lg = _outproj(ym.reshape(b_ * t, mw), yp.reshape(b_ * t, pool_wd), wo[:mw], wo[mw:],
                          x.reshape(b_ * t, d), mod, t, norm2[l].reshape(1, d), wr, 512)

    idx, gate = _route(lg.reshape(b_, t, LANES), cap)

    y = _ffn(idx, gate, h2, w_gate[l], w_up[l], w_down[l])
    return _combine(y, idx.reshape(b_, N_EXPERTS, 1, cap), x1.reshape(b_, t, d), mod, norm_f.reshape(1, d), 1024)
```

```python
import functools

import jax
import jax.numpy as jnp
from jax import lax
from jax.experimental import pallas as pl
from jax.experimental.pallas import tpu as pltpu

F32 = jnp.float32
BF16 = jnp.bfloat16

N_HEADS = 4
DQK = 128
DV = 256
QK_CONV = 5
POOL_WINDOWS = (2, 4, 8, 16)
POOL_GC = 256
GRID_W = 64
GRID_SHIFT = GRID_W.bit_length() - 1
N_EXPERTS = 16
EC_FACTOR = 2
N_MOD = 6
EPS = 1e-6

LANES = 128
SUBLANES = 8
MLSTM_CHUNK = 128
VMEM_LIMIT = 56 * 1024 * 1024


def _cparams(sem):
    return pltpu.CompilerParams(dimension_semantics=sem, vmem_limit_bytes=VMEM_LIMIT)


def _sigmoid(x):
    return 1.0 / (1.0 + jnp.exp(-x))


def _silu(x):
    return x * _sigmoid(x)


def _log_sigmoid(x):
    return jnp.minimum(x, 0.0) - jnp.log(1.0 + jnp.exp(-jnp.abs(x)))


def _mod_kernel(c_ref, w_ref, b_ref, o_ref):
    s = _silu(c_ref[...]).astype(BF16)
    o_ref[...] = jnp.dot(s, w_ref[...].astype(BF16), preferred_element_type=F32) + b_ref[...]


def _modulation(cs, w_mod, b_mod):
    rows, d = cs.shape
    n = w_mod.shape[1]
    tn = 1024
    return pl.pallas_call(
        _mod_kernel,
        grid=(n // tn,),
        in_specs=[pl.BlockSpec((rows, d), lambda j: (0, 0)),
                  pl.BlockSpec((d, tn), lambda j: (0, j)),
                  pl.BlockSpec((1, tn), lambda j: (0, j))],
        out_specs=pl.BlockSpec((rows, tn), lambda j: (0, j)),
        out_shape=jax.ShapeDtypeStruct((rows, n), F32),
        compiler_params=_cparams(("arbitrary",)),
    )(cs, w_mod, b_mod.reshape(1, n))


def _inproj_kernel(x_ref, mod_ref, n1_ref, w_ref, wg_ref, bg_ref, p_ref, g_ref, h_scr):
    @pl.when(pl.program_id(1) == 0)
    def _():
        x = x_ref[...]
        y = x * lax.rsqrt(jnp.mean(x * x, axis=-1, keepdims=True) + EPS)
        h = (y * n1_ref[...]) * (1.0 + mod_ref[0, 1:2, :]) + mod_ref[0, 0:1, :]
        hb = h.astype(BF16)
        h_scr[...] = hb
        g_ref[...] = jnp.dot(hb, wg_ref[...], preferred_element_type=F32) + bg_ref[...]

    p_ref[...] = jnp.dot(h_scr[...], w_ref[...], preferred_element_type=F32).astype(BF16)


def _inproj(x2, mod, rows_per_mod, n1, w, wg, bg, n_cols, tm, tn):
    m, d = x2.shape
    tiles_per_mod = rows_per_mod // tm
    return pl.pallas_call(
        _inproj_kernel,
        grid=(m // tm, n_cols // tn),
        in_specs=[pl.BlockSpec((tm, d), lambda i, j: (i, 0)),
                  pl.BlockSpec((1, N_MOD, d), lambda i, j: (i // tiles_per_mod, 0, 0)),
                  pl.BlockSpec((1, d), lambda i, j: (0, 0)),
                  pl.BlockSpec((d, tn), lambda i, j: (0, j)),
                  pl.BlockSpec((d, LANES), lambda i, j: (0, 0)),
                  pl.BlockSpec((1, LANES), lambda i, j: (0, 0))],
        out_specs=[pl.BlockSpec((tm, tn), lambda i, j: (i, j)),
                   pl.BlockSpec((tm, LANES), lambda i, j: (i, 0))],
        out_shape=[jax.ShapeDtypeStruct((m, n_cols), BF16),
                   jax.ShapeDtypeStruct((m, LANES), F32)],
        scratch_shapes=[pltpu.VMEM((tm, d), BF16)],
        compiler_params=_cparams(("arbitrary", "arbitrary")),
    )(x2, mod, n1, w, wg, bg)


def _scan_lanes(x, op, fill, reverse):
    lane = lax.broadcasted_iota(jnp.int32, x.shape, 1)
    k = 1
    while k < LANES:
        if reverse:
            x = op(x, jnp.where(lane < LANES - k, pltpu.roll(x, LANES - k, 1), fill))
        else:
            x = op(x, jnp.where(lane >= k, pltpu.roll(x, k, 1), fill))
        k *= 2
    return x


G_B, G_R, G_W, G_BEND, G_MLOC, G_RMAX, G_ROWS = 0, 1, 2, 3, 4, 5, 6


def _mlstm_kernel(kc_ref, vc_ref, gc_ref, k_ref, v_ref, q_ref, o_ref, g_ref, wk_ref, wq_ref, hg_ref,
                  out_ref,
                  cpad, ktt, qt, kctt, vext, vcext, gs, gsc, cf, cb, hf, hb):
    L = MLSTM_CHUNK
    t_lat = k_ref.shape[1]
    t_ctx = kc_ref.shape[1]
    nc = t_lat // L
    ncc = t_ctx // L
    pad = SUBLANES
    half = QK_CONV // 2

    def conv_silu(src, w_ref, dst, t, scale, transposed):
        cpad[0:pad, :] = jnp.zeros((pad, LANES), F32)
        cpad[pad:pad + t, :] = src.astype(F32)
        cpad[pad + t:pad + t + pad, :] = jnp.zeros((pad, LANES), F32)
        for r in range(t // L):
            base = pad + r * L - half
            acc = w_ref[0:1, :] * cpad[base:base + L, :]
            for j in range(1, QK_CONV):
                acc = acc + w_ref[j:j + 1, :] * cpad[base + j:base + j + L, :]
            y = _silu(acc)
            if scale is not None:
                y = y * scale
            dst[r] = (y.T if transposed else y).astype(BF16)

    conv_silu(kc_ref[0], wk_ref, kctt, t_ctx, DQK ** -0.5, True)
    conv_silu(k_ref[0], wk_ref, ktt, t_lat, DQK ** -0.5, True)
    conv_silu(q_ref[0], wq_ref, qt, t_lat, None, False)

    def fill_vext(dst, src, t):
        dst[:, 0:DV] = src
        dst[:, DV:DV + LANES] = jnp.ones((t, LANES), BF16)

    fill_vext(vcext, vc_ref[0], t_ctx)
    fill_vext(vext, v_ref[0], t_lat)

    def gate_prep(garr, dst):
        for d in range(2):
            li = garr[2 * d]
            lf = _log_sigmoid(garr[2 * d + 1])
            b = _scan_lanes(lf, jnp.add, 0.0, reverse=(d == 1))
            b_end = b[:, LANES - 1:LANES] if d == 0 else b[:, 0:1]
            r = li - b
            a = b_end + r
            m_loc = jnp.max(a, axis=1, keepdims=True)
            dst[G_ROWS * d + G_B] = b
            dst[G_ROWS * d + G_R] = r
            dst[G_ROWS * d + G_W] = jnp.exp(a - m_loc)
            dst[G_ROWS * d + G_BEND] = jnp.broadcast_to(b_end, b.shape)
            dst[G_ROWS * d + G_MLOC] = jnp.broadcast_to(m_loc, b.shape)
            dst[G_ROWS * d + G_RMAX] = _scan_lanes(r, jnp.maximum, -jnp.inf, reverse=(d == 1))

    gate_prep(gc_ref[0, 0], gsc)
    gate_prep(g_ref[0, 0], gs)

    cf[...] = jnp.zeros(cf.shape, F32)
    cb[...] = jnp.zeros(cb.shape, F32)

    row_i = lax.broadcasted_iota(jnp.int32, (L, L), 0)
    col_i = lax.broadcasted_iota(jnp.int32, (L, L), 1)
    visible = (col_i <= row_i, col_i >= row_i)

    def per_token(row):
        return jnp.broadcast_to(row, (L, L)).T

    def step(c, d, g_scr, kt_scr, v_scr, q_scr, c_scr, m_prev):
        off = c * L if isinstance(c, int) else pl.multiple_of(c * L, L)
        g0 = G_ROWS * d
        kt_c = kt_scr[c]
        v_c = v_scr[pl.ds(off, L), :]
        b_end = g_scr[g0 + G_BEND, pl.ds(c, 1), 0:1]
        m_loc = g_scr[g0 + G_MLOC, pl.ds(c, 1), 0:1]
        c_prev = c_scr[...]
        h = None
        if q_scr is not None:
            q_c = q_scr[c]
            r_row = g_scr[g0 + G_R, pl.ds(c, 1), :]
            u = jnp.maximum(per_token(g_scr[g0 + G_RMAX, pl.ds(c, 1), :]), m_prev)
            b_t = per_token(g_scr[g0 + G_B, pl.ds(c, 1), :])
            qk = jnp.dot(q_c, kt_c, preferred_element_type=F32)
            s = (qk * jnp.exp(jnp.where(visible[d], r_row - u, -jnp.inf))).astype(BF16)
            wi = jnp.exp(m_prev - u)
            inter = jnp.dot(q_c, c_prev.astype(BF16), preferred_element_type=F32)
            res = jnp.dot(s, v_c, preferred_element_type=F32) \
                + jnp.concatenate([wi] * (c_prev.shape[1] // L), axis=1) * inter
            inv = 1.0 / jnp.maximum(jnp.abs(res[:, DV:DV + L]), jnp.exp(-(b_t + u)))
            h = res[:, 0:DV] * jnp.concatenate([inv] * (DV // L), axis=1)
        kwt = (kt_c.astype(F32) * g_scr[g0 + G_W, pl.ds(c, 1), :]).astype(BF16)
        c_loc = jnp.dot(kwt, v_c, preferred_element_type=F32)
        m_new = jnp.maximum(b_end + m_prev, m_loc)
        sp = jnp.exp(b_end + m_prev - m_new)
        sl = jnp.exp(m_loc - m_new)
        c_scr[...] = sp * c_prev + sl * c_loc
        return m_new, h

    def finish(c, h):
        off = pl.multiple_of(c * L, L)
        h = h * lax.rsqrt(jnp.mean(h * h, axis=-1, keepdims=True) + EPS)
        y = (h * hg_ref[...]) * _sigmoid(o_ref[0, pl.ds(off, L), :].astype(F32))
        out_ref[0, pl.ds(off, L), :] = y.astype(BF16)

    m_f = jnp.zeros((1, 1), F32)
    m_b = jnp.zeros((1, 1), F32)
    for i in range(ncc):
        m_f, _ = step(i, 0, gsc, kctt, vcext, None, cf, m_f)
        m_b, _ = step(ncc - 1 - i, 1, gsc, kctt, vcext, None, cb, m_b)

    assert nc % 2 == 0

    def first_half(i, carry):
        c_f, c_b = i, nc - 1 - i
        m_f, h_f = step(c_f, 0, gs, ktt, vext, qt, cf, carry[0])
        m_b, h_b = step(c_b, 1, gs, ktt, vext, qt, cb, carry[1])
        hf[pl.ds(pl.multiple_of(c_f * L, L), L), :] = h_f
        hb[pl.ds(pl.multiple_of(c_b * L, L), L), :] = h_b
        return m_f, m_b

    def second_half(i, carry):
        c_f, c_b = i, nc - 1 - i
        m_f, h_f = step(c_f, 0, gs, ktt, vext, qt, cf, carry[0])
        m_b, h_b = step(c_b, 1, gs, ktt, vext, qt, cb, carry[1])
        finish(c_f, h_f + hb[pl.ds(pl.multiple_of(c_f * L, L), L), :])
        finish(c_b, h_b + hf[pl.ds(pl.multiple_of(c_b * L, L), L), :])
        return m_f, m_b

    carry = lax.fori_loop(0, nc // 2, first_half, (m_f, m_b), unroll=2)
    lax.fori_loop(nc // 2, nc, second_half, carry, unroll=2)


def _mlstm(p, pc, g4, gc4, wk, wq, head_g):
    b_, t, _ = p.shape
    tc = pc.shape[1]
    L = MLSTM_CHUNK
    nc, ncc = t // L, tc // L
    ext = DV + LANES
    return pl.pallas_call(
        _mlstm_kernel,
        grid=(b_, N_HEADS),
        in_specs=[
            pl.BlockSpec((1, tc, DQK), lambda b, h: (b, 0, h)),
            pl.BlockSpec((1, tc, DV), lambda b, h: (b, 0, 2 + h)),
            pl.BlockSpec((1, 1, 4, ncc, L), lambda b, h: (b, h, 0, 0, 0)),
            pl.BlockSpec((1, t, DQK), lambda b, h: (b, 0, h)),
            pl.BlockSpec((1, t, DV), lambda b, h: (b, 0, 2 + h)),
            pl.BlockSpec((1, t, DQK), lambda b, h: (b, 0, 12 + h)),
            pl.BlockSpec((1, t, DV), lambda b, h: (b, 0, 8 + h)),
            pl.BlockSpec((1, 1, 4, nc, L), lambda b, h: (b, h, 0, 0, 0)),
            pl.BlockSpec((SUBLANES, DQK), lambda b, h: (0, h)),
            pl.BlockSpec((SUBLANES, DQK), lambda b, h: (0, h)),
            pl.BlockSpec((1, DV), lambda b, h: (0, h)),
        ],
        out_specs=pl.BlockSpec((1, t, DV), lambda b, h: (b, 0, h)),
        out_shape=jax.ShapeDtypeStruct((b_, t, N_HEADS * DV), BF16),
        scratch_shapes=[
            pltpu.VMEM((t + 2 * SUBLANES, LANES), F32),
            pltpu.VMEM((nc, DQK, L), BF16),
            pltpu.VMEM((nc, L, DQK), BF16),
            pltpu.VMEM((ncc, DQK, L), BF16),
            pltpu.VMEM((t, ext), BF16),
            pltpu.VMEM((tc, ext), BF16),
            pltpu.VMEM((2 * G_ROWS, nc, L), F32),
            pltpu.VMEM((2 * G_ROWS, ncc, L), F32),
            pltpu.VMEM((DQK, ext), F32),
            pltpu.VMEM((DQK, ext), F32),
            pltpu.VMEM((t, DV), F32),
            pltpu.VMEM((t, DV), F32),
        ],
        compiler_params=_cparams(("arbitrary", "arbitrary")),
    )(pc, pc, gc4, p, p, p, p, g4, wk, wq, head_g)


def _pool_kernel(u_ref, pw_ref, ps_ref, out_ref, spad):
    t = u_ref.shape[1]
    rows = t // GRID_W
    blk = 256
    halo = (max(POOL_WINDOWS) // 2) * GRID_W
    spad[0:halo, :] = jnp.zeros((halo, POOL_GC), F32)
    spad[halo + t:halo + t + halo, :] = jnp.zeros((halo, POOL_GC), F32)
    ti = lax.broadcasted_iota(jnp.int32, (blk, blk), 0)
    tj = lax.broadcasted_iota(jnp.int32, (blk, blk), 1)
    same_row = (ti >> GRID_SHIFT) == (tj >> GRID_SHIFT)
    diff = tj - ti
    tok = lax.broadcasted_iota(jnp.int32, (blk, 1), 0)
    for gi, w in enumerate(POOL_WINDOWS):
        hw = w // 2
        c0, c1 = gi * POOL_GC, (gi + 1) * POOL_GC
        band = jnp.where(same_row & (diff >= -hw) & (diff <= hw - 1), 1.0, 0.0).astype(BF16)
        for r in range(t // blk):
            spad[halo + r * blk:halo + (r + 1) * blk, :] = jnp.dot(
                band, u_ref[0, r * blk:(r + 1) * blk, c0:c1], preferred_element_type=F32)
        for r in range(t // blk):
            base = halo + r * blk
            acc = spad[base - hw * GRID_W:base - hw * GRID_W + blk, :]
            for j in range(-hw + 1, hw):
                acc = acc + spad[base + j * GRID_W:base + j * GRID_W + blk, :]
            tt = tok + r * blk
            gr = tt >> GRID_SHIFT
            gc = tt & (GRID_W - 1)
            cr = jnp.minimum(gr + hw, rows) - jnp.maximum(gr - hw, 0)
            cc = jnp.minimum(gc + hw, GRID_W) - jnp.maximum(gc - hw, 0)
            cnt = (cr * cc).astype(F32)
            d = acc / cnt - u_ref[0, r * blk:(r + 1) * blk, c0:c1].astype(F32)
            y = jnp.dot(d.astype(BF16), pw_ref[gi], preferred_element_type=F32) * ps_ref[:, c0:c1]
            out_ref[0, r * blk:(r + 1) * blk, c0:c1] = y.astype(BF16)


def _pool(p, pool_w, pool_scale):
    b_, t, _ = p.shape
    pw = len(POOL_WINDOWS) * POOL_GC
    halo = (max(POOL_WINDOWS) // 2) * GRID_W
    return pl.pallas_call(
        _pool_kernel,
        grid=(b_,),
        in_specs=[pl.BlockSpec((1, t, pw), lambda b: (b, 0, 3)),
                  pl.BlockSpec((len(POOL_WINDOWS), POOL_GC, POOL_GC), lambda b: (0, 0, 0)),
                  pl.BlockSpec((1, pw), lambda b: (0, 0))],
        out_specs=pl.BlockSpec((1, t, pw), lambda b: (b, 0, 0)),
        out_shape=jax.ShapeDtypeStruct((b_, t, pw), BF16),
        scratch_shapes=[pltpu.VMEM((t + 2 * halo, POOL_GC), F32)],
        compiler_params=_cparams(("arbitrary",)),
    )(p, pool_w, pool_scale)


def _outproj_kernel(ym_ref, yp_ref, wm_ref, wp_ref, x_ref, mod_ref, n2_ref, wr_ref,
                    x1_ref, h2_ref, lg_ref):
    mix = (jnp.dot(ym_ref[...], wm_ref[...], preferred_element_type=F32)
           + jnp.dot(yp_ref[...], wp_ref[...], preferred_element_type=F32))
    x1 = x_ref[...] + mod_ref[0, 2:3, :] * mix
    x1_ref[...] = x1
    y = x1 * lax.rsqrt(jnp.mean(x1 * x1, axis=-1, keepdims=True) + EPS)
    h2 = (y * n2_ref[...]) * (1.0 + mod_ref[0, 4:5, :]) + mod_ref[0, 3:4, :]
    for j in range(h2_ref.shape[1]):
        h2_ref[:, j, :, :] = h2[:, j * LANES:(j + 1) * LANES].reshape(h2_ref.shape[0], SUBLANES, LANES)
    lg_ref[...] = jnp.dot(h2.astype(BF16), wr_ref[...], preferred_element_type=F32)


def _outproj(ym, yp, wm, wp, x2, mod, rows_per_mod, n2, wr, tm):
    m, d = x2.shape
    km = ym.shape[1]
    kp = yp.shape[1]
    tiles_per_mod = rows_per_mod // tm
    return pl.pallas_call(
        _outproj_kernel,
        grid=(m // tm,),
        in_specs=[pl.BlockSpec((tm, km), lambda i: (i, 0)),
                  pl.BlockSpec((tm, kp), lambda i: (i, 0)),
                  pl.BlockSpec((km, d), lambda i: (0, 0), pipeline_mode=pl.Buffered(1)),
                  pl.BlockSpec((kp, d), lambda i: (0, 0), pipeline_mode=pl.Buffered(1)),
                  pl.BlockSpec((tm, d), lambda i: (i, 0)),
                  pl.BlockSpec((1, N_MOD, d), lambda i: (i // tiles_per_mod, 0, 0)),
                  pl.BlockSpec((1, d), lambda i: (0, 0)),
                  pl.BlockSpec((d, LANES), lambda i: (0, 0))],
        out_specs=[pl.BlockSpec((tm, d), lambda i: (i, 0)),
                   pl.BlockSpec((tm // SUBLANES, d // LANES, SUBLANES, LANES), lambda i: (i, 0, 0, 0)),
                   pl.BlockSpec((tm, LANES), lambda i: (i, 0))],
        out_shape=[jax.ShapeDtypeStruct((m, d), F32),
                   jax.ShapeDtypeStruct((m // SUBLANES, d // LANES, SUBLANES, LANES), F32),
                   jax.ShapeDtypeStruct((m, LANES), F32)],
        compiler_params=_cparams(("arbitrary",)),
    )(ym, yp, wm, wp, x2, mod, n2, wr)


def _scan_lanes_i32(x, exclusive_of=None):
    n = x.shape[1]
    lane = lax.broadcasted_iota(jnp.int32, x.shape, 1)
    k = 1
    while k < n:
        x = x + jnp.where(lane >= k, pltpu.roll(x, k, 1), 0)
        k *= 2
    return x


COMBINE_TILE = 256
GRANULE = 2 * SUBLANES


def _route_kernel(lg_ref, idx_ref, gate_ref, tokrep_ref, bounds_ref, afft):
    t = lg_ref.shape[1]
    cap = idx_ref.shape[2]
    lane = lax.broadcasted_iota(jnp.int32, (t, LANES), 1)
    lg = jnp.where(lane < N_EXPERTS, lg_ref[0], -jnp.inf)
    ex = jnp.exp(lg - jnp.max(lg, axis=1, keepdims=True))
    aff = ex / jnp.sum(ex, axis=1, keepdims=True)
    for r in range(t // LANES):
        afft[:, r * LANES:(r + 1) * LANES] = aff[r * LANES:(r + 1) * LANES, :].T
    aff_t = afft[0:N_EXPERTS, :]

    def count(mask):
        return jnp.sum(jnp.where(mask, 1.0, 0.0), axis=1, keepdims=True).astype(jnp.int32)

    def bit_step(i, thr_bits):
        cand = thr_bits | (jnp.int32(1) << (30 - i))
        cand_f = lax.bitcast_convert_type(cand, F32)
        return jnp.where(count(aff_t >= cand_f) >= cap, cand, thr_bits)

    thr_bits = lax.fori_loop(0, 31, bit_step, jnp.zeros((N_EXPERTS, 1), jnp.int32))
    thr = lax.bitcast_convert_type(thr_bits, F32)
    gt = aff_t > thr
    eq = aff_t == thr
    need = cap - count(gt)
    eq_i = jnp.where(eq, 1, 0)
    eq_rank = _scan_lanes_i32(eq_i) - eq_i
    sel = gt | (eq & (eq_rank < need))
    sel_i = jnp.where(sel, 1, 0)
    pos = _scan_lanes_i32(sel_i) - sel_i
    key = jnp.where(sel, pos, -1)

    a_hi = aff.astype(BF16).astype(F32)
    a_mid = (aff - a_hi).astype(BF16).astype(F32)
    a_lo = (aff - a_hi - a_mid).astype(BF16).astype(F32)
    tok = lax.broadcasted_iota(jnp.int32, (t, LANES), 0)
    vals = (a_hi + pltpu.roll(a_mid, N_EXPERTS, 1) + pltpu.roll(a_lo, 2 * N_EXPERTS, 1)
            + jnp.where(lane == 3 * N_EXPERTS, (tok >> GRID_SHIFT).astype(F32), 0.0)
            + jnp.where(lane == 3 * N_EXPERTS + 1, (tok & (GRID_W - 1)).astype(F32), 0.0)).astype(BF16)
    slot = lax.broadcasted_iota(jnp.int32, (cap, 1), 0)
    for e in range(N_EXPERTS):
        onehot = jnp.where(slot == key[e:e + 1, :], 1.0, 0.0).astype(BF16)
        res = jnp.dot(onehot, vals, preferred_element_type=F32)
        rt = jnp.concatenate([res[r * LANES:(r + 1) * LANES, :].T for r in range(cap // LANES)], axis=1)
        gate_ref[0, e:e + 1, :] = (rt[e:e + 1, :] + rt[N_EXPERTS + e:N_EXPERTS + e + 1, :]) \
            + rt[2 * N_EXPERTS + e:2 * N_EXPERTS + e + 1, :]
        tokf = rt[3 * N_EXPERTS:3 * N_EXPERTS + 1, :] * float(GRID_W) + rt[3 * N_EXPERTS + 1:3 * N_EXPERTS + 2, :]
        idx_ref[0, e:e + 1, :] = tokf.astype(jnp.int32)
        tok_col = res[:, 3 * N_EXPERTS:3 * N_EXPERTS + 1] * float(GRID_W) \
            + res[:, 3 * N_EXPERTS + 1:3 * N_EXPERTS + 2]
        tokrep_ref[0, e * cap:(e + 1) * cap, :] = jnp.broadcast_to(tok_col, (cap, LANES)).astype(jnp.int32)

    n_tiles = t // COMBINE_TILE
    lane_b = lax.broadcasted_iota(jnp.int32, (N_EXPERTS, LANES), 1)
    bounds = jnp.where(lane_b >= n_tiles, cap, 0)
    for i in range(n_tiles):
        bounds = jnp.where(lane_b == i, pos[:, i * COMBINE_TILE:i * COMBINE_TILE + 1], bounds)
    bounds_ref[0] = bounds


def _route(lg3, cap):
    b_, t, _ = lg3.shape
    return pl.pallas_call(
        _route_kernel,
        grid=(b_,),
        in_specs=[pl.BlockSpec((1, t, LANES), lambda b: (b, 0, 0))],
        out_specs=[pl.BlockSpec((1, N_EXPERTS, cap), lambda b: (b, 0, 0)),
                   pl.BlockSpec((1, N_EXPERTS, cap), lambda b: (b, 0, 0)),
                   pl.BlockSpec((1, N_EXPERTS * cap, LANES), lambda b: (b, 0, 0)),
                   pl.BlockSpec((1, N_EXPERTS, LANES), lambda b: (b, 0, 0))],
        out_shape=[jax.ShapeDtypeStruct((b_, N_EXPERTS, cap), jnp.int32),
                   jax.ShapeDtypeStruct((b_, N_EXPERTS, cap), F32),
                   jax.ShapeDtypeStruct((b_, N_EXPERTS * cap, LANES), jnp.int32),
                   jax.ShapeDtypeStruct((b_, N_EXPERTS, LANES), jnp.int32)],
        scratch_shapes=[pltpu.VMEM((LANES, t), F32)],
        compiler_params=_cparams(("arbitrary",)),
    )(lg3)


def _ffn_kernel(idx_ref, gate_ref, h2_hbm, wg_ref, wu_ref, wd_ref, y_ref,
                xs_buf, wg_s, wu_s, wd_s, sem):
    p = pl.program_id(0)
    b = pl.program_id(1)
    n_e = pl.num_programs(0) - 1
    n_b = pl.num_programs(1)
    k_chunks, cap = xs_buf.shape[1], xs_buf.shape[2]
    groups = h2_hbm.shape[0] // n_b
    step = p * n_b + b
    cur = step % 2
    nxt = 1 - cur
    last_b = b == n_b - 1
    b_next = jnp.where(last_b, 0, b + 1)
    p_next = jnp.where(last_b, p + 1, p)

    @pl.when((p_next >= 1) & (p_next <= n_e))
    def _():
        def issue(r, carry):
            tok = idx_ref[0, 0, r]
            pltpu.make_async_copy(h2_hbm.at[b_next * groups + (tok >> 3), :, tok & (SUBLANES - 1), :],
                                  xs_buf.at[nxt, :, r, :], sem.at[nxt]).start()
            return carry

        lax.fori_loop(0, cap, issue, 0, unroll=8)

    @pl.when(p < n_e)
    def _():
        ws = p % 2
        rows_in = wg_ref.shape[1]
        rows_mid = wd_ref.shape[1]
        wg_s[ws, pl.ds(pl.multiple_of(b * rows_in, rows_in), rows_in), :] = wg_ref[0].astype(BF16)
        wu_s[ws, pl.ds(pl.multiple_of(b * rows_in, rows_in), rows_in), :] = wu_ref[0].astype(BF16)
        wd_s[ws, pl.ds(pl.multiple_of(b * rows_mid, rows_mid), rows_mid), :] = wd_ref[0].astype(BF16)

    @pl.when(p == 0)
    def _():
        y_ref[0, 0] = jnp.zeros(y_ref.shape[2:], BF16)

    @pl.when(p >= 1)
    def _():
        pltpu.make_async_copy(xs_buf.at[cur], xs_buf.at[cur], sem.at[cur]).wait()
        ws = (p - 1) % 2
        xs = jnp.concatenate([xs_buf[cur, j].astype(BF16) for j in range(k_chunks)], axis=1)
        a = _silu(jnp.dot(xs, wg_s[ws], preferred_element_type=F32)) \
            * jnp.dot(xs, wu_s[ws], preferred_element_type=F32)
        y = jnp.dot(a.astype(BF16), wd_s[ws], preferred_element_type=F32)
        y_ref[0, 0] = (y * gate_ref[0, 0]).astype(BF16)


def _ffn(idx, gate, h2, wg, wu, wd):
    e_, d, f = wg.shape
    b_, _, cap = idx.shape
    assert h2.shape[1] * h2.shape[3] == d and h2.shape[2] == SUBLANES and h2.shape[0] % b_ == 0
    assert d % b_ == 0 and f % b_ == 0
    rows_in, rows_mid = d // b_, f // b_
    assert rows_in % (2 * SUBLANES) == 0 and rows_mid % (2 * SUBLANES) == 0

    def next_tokens(p, b):
        last_b = b == b_ - 1
        b_next = jnp.where(last_b, 0, b + 1)
        e_next = jnp.clip(jnp.where(last_b, p, p - 1), 0, e_ - 1)
        return (b_next * e_ + e_next, 0, 0)

    def this_expert(p, b):
        return (b, jnp.maximum(p - 1, 0), 0, 0)

    def weight_piece(p, b):
        return (jnp.minimum(p, e_ - 1), b, 0)

    return pl.pallas_call(
        _ffn_kernel,
        grid=(e_ + 1, b_),
        in_specs=[pl.BlockSpec((1, 1, cap), next_tokens, memory_space=pltpu.SMEM),
                  pl.BlockSpec((1, 1, cap, 1), this_expert),
                  pl.BlockSpec(memory_space=pl.ANY),
                  pl.BlockSpec((1, rows_in, f), weight_piece),
                  pl.BlockSpec((1, rows_in, f), weight_piece),
                  pl.BlockSpec((1, rows_mid, d), weight_piece)],
        out_specs=pl.BlockSpec((1, 1, cap, d), lambda p, b: (b, (p + e_) % (e_ + 1), 0, 0)),
        out_shape=jax.ShapeDtypeStruct((b_, e_ + 1, cap, d), BF16),
        scratch_shapes=[pltpu.VMEM((2, d // LANES, cap, LANES), F32),
                        pltpu.VMEM((2, d, f), BF16),
                        pltpu.VMEM((2, d, f), BF16),
                        pltpu.VMEM((2, f, d), BF16),
                        pltpu.SemaphoreType.DMA((2,))],
        compiler_params=_cparams(("arbitrary", "arbitrary")),
    )(idx.reshape(b_ * e_, 1, cap), gate.reshape(b_, e_, cap, 1), h2, wg, wu, wd)


COMBINE_CHUNK = 256


def _combine_kernel(bounds_ref, y_ref, tok_ref, x1_ref, mod_ref, nf_ref, o_ref, stage_y, stage_t):
    b = pl.program_id(0)
    i = pl.program_id(1)
    n_e, cap = y_ref.shape[1], y_ref.shape[2]
    tile = o_ref.shape[1]
    n_bounds = pl.num_programs(1) + 1
    per_chunk = COMBINE_CHUNK // GRANULE
    g_shift = GRANULE.bit_length() - 1
    assert per_chunk & (per_chunk - 1) == 0 and GRANULE == 1 << g_shift

    @pl.when((b == 0) & (i == 0))
    def _():
        stage_y[...] = jnp.zeros(stage_y.shape, BF16)
        stage_t[...] = jnp.full(stage_t.shape, -1, jnp.int32)

    o_ref[...] = jnp.zeros(o_ref.shape, F32)
    tok_lane = lax.broadcasted_iota(jnp.int32, (COMBINE_CHUNK, LANES), 1) + i * tile

    def flush():
        onehot_t = jnp.concatenate(
            [jnp.where(stage_t[...] == tok_lane + j * LANES, 1.0, 0.0).astype(BF16)
             for j in range(tile // LANES)], axis=1)
        o_ref[0] += lax.dot_general(onehot_t, stage_y[...], (((0,), (0,)), ((), ())),
                                    preferred_element_type=F32)
        stage_t[...] = jnp.full(stage_t.shape, -1, jnp.int32)

    def per_expert(e, k):
        base = (b * n_e + e) * n_bounds + i
        lo = bounds_ref[base]
        hi = bounds_ref[base + 1]
        g_lo = lax.shift_right_logical(lo, g_shift)
        g_hi = jnp.where(hi > lo, lax.shift_right_logical(hi + GRANULE - 1, g_shift), g_lo)

        def per_granule(g, k):
            src = pl.multiple_of(g * GRANULE, GRANULE)
            dst = pl.multiple_of((k & (per_chunk - 1)) * GRANULE, GRANULE)
            stage_y[pl.ds(dst, GRANULE), :] = y_ref[0, e, pl.ds(src, GRANULE), :]
            stage_t[pl.ds(dst, GRANULE), :] = tok_ref[0, pl.ds(pl.multiple_of(e * cap + src, GRANULE), GRANULE), :]
            k = k + 1

            @pl.when((k & (per_chunk - 1)) == 0)
            def _():
                flush()

            return k

        return lax.fori_loop(g_lo, g_hi, per_granule, k)

    k = lax.fori_loop(0, n_e, per_expert, jnp.int32(0))

    @pl.when((k & (per_chunk - 1)) != 0)
    def _():
        flush()

    x = x1_ref[0] + mod_ref[0, 5:6, :] * o_ref[0]
    o_ref[0] = (x * lax.rsqrt(jnp.mean(x * x, axis=-1, keepdims=True) + EPS)) * nf_ref[...]


def _combine(y, tokrep, bounds, x1, mod, nf):
    b_, e_, cap, d = y.shape
    e_ -= 1
    t = x1.shape[1]
    tile = COMBINE_TILE
    assert cap % GRANULE == 0 and tile % LANES == 0 and bounds.shape == (b_, e_, t // tile + 1)
    grid_spec = pltpu.PrefetchScalarGridSpec(
        num_scalar_prefetch=1,
        grid=(b_, t // tile),
        in_specs=[pl.BlockSpec((1, e_, cap, d), lambda b, i, bnd: (b, 0, 0, 0)),
                  pl.BlockSpec((1, e_ * cap, LANES), lambda b, i, bnd: (b, 0, 0)),
                  pl.BlockSpec((1, tile, d), lambda b, i, bnd: (b, i, 0)),
                  pl.BlockSpec((1, N_MOD, d), lambda b, i, bnd: (b, 0, 0)),
                  pl.BlockSpec((1, d), lambda b, i, bnd: (0, 0))],
        out_specs=pl.BlockSpec((1, tile, d), lambda b, i, bnd: (b, i, 0)),
        scratch_shapes=[pltpu.VMEM((COMBINE_CHUNK, d), BF16),
                        pltpu.VMEM((COMBINE_CHUNK, LANES), jnp.int32)])
    return pl.pallas_call(
        _combine_kernel,
        grid_spec=grid_spec,
        out_shape=jax.ShapeDtypeStruct((b_, t, d), F32),
        compiler_params=_cparams(("arbitrary", "arbitrary")),
    )(bounds.reshape(-1), y, tokrep, x1, mod, nf)


def _gate_layout(g, b_, t):
    g = g[:, :2 * 2 * N_HEADS].reshape(b_, t, 2, 2, N_HEADS)
    g = g.transpose(0, 4, 2, 3, 1)
    return g.reshape(b_, N_HEADS, 4, t // MLSTM_CHUNK, MLSTM_CHUNK)


def kernel(x, c, ctx, c_ctx, w_mod, b_mod, norm1, w_in, conv_q_w, conv_k_w, b_gates, head_g, pool_w,
           pool_scale, w_out, norm2, w_router, w_gate, w_up, w_down, norm_f):
    b_, t, d = x.shape
    tc = ctx.shape[1]
    depth = w_mod.shape[0]
    assert depth == 1
    l = 0
    qk_w = N_HEADS * DQK
    mw = N_HEADS * DV
    pool_wd = len(POOL_WINDOWS) * POOL_GC
    n_gate = 2 * 2 * N_HEADS
    k_off, v_off = 0, qk_w
    g_off = v_off + mw
    q_off = g_off + n_gate
    o_off = q_off + qk_w
    p_off = o_off + mw
    cap = EC_FACTOR * t // N_EXPERTS

    mod_rows = -(-(b_ + 1) // SUBLANES) * SUBLANES
    cs = jnp.zeros((mod_rows, d), F32).at[:b_].set(c).at[b_].set(c_ctx)
    mod_all = _modulation(cs, w_mod[l], b_mod[l])
    mod = mod_all[:b_].reshape(b_, N_MOD, d)
    mod_c = mod_all[b_:b_ + 1].reshape(1, N_MOD, d)

    wl = w_in[l]
    w_main = jnp.concatenate([wl[:, k_off:v_off], wl[:, v_off:g_off], wl[:, q_off:o_off],
                              wl[:, o_off:p_off], wl[:, p_off:]], axis=1).astype(BF16)
    w_g = jnp.pad(wl[:, g_off:q_off], ((0, 0), (0, LANES - n_gate))).astype(BF16)
    b_g = jnp.pad(b_gates[l], (0, LANES - n_gate)).reshape(1, LANES)
    n1 = norm1[l].reshape(1, d)

    p, g = _inproj(x.reshape(b_ * t, d), mod, t, n1, w_main, w_g, b_g, w_main.shape[1], 1024, 512)
    pc, gc = _inproj(ctx.reshape(b_ * tc, d), mod_c, b_ * tc, n1, w_main, w_g, b_g, qk_w + mw,
                     min(1024, b_ * tc), 512)
    p = p.reshape(b_, t, -1)
    pc = pc.reshape(b_, tc, -1)

    wk = jnp.pad(conv_k_w[l], ((0, SUBLANES - QK_CONV), (0, 0)))
    wq = jnp.pad(conv_q_w[l], ((0, SUBLANES - QK_CONV), (0, 0)))
    ym = _mlstm(p, pc, _gate_layout(g, b_, t), _gate_layout(gc, b_, tc), wk, wq, head_g[l].reshape(1, mw))
    yp = _pool(p, pool_w[l].astype(BF16), pool_scale[l].reshape(1, pool_wd))

    wo = w_out[l].astype(BF16)
    wr = jnp.pad(w_router[l], ((0, 0), (0, LANES - N_EXPERTS))).astype(BF16)
    x1, h2, lg = _outproj(ym.reshape(b_ * t, mw), yp.reshape(b_ * t, pool_wd), wo[:mw], wo[mw:],
                          x.reshape(b_ * t, d), mod, t, norm2[l].reshape(1, d), wr, 512)

    idx, gate, tokrep, bounds = _route(lg.reshape(b_, t, LANES), cap)

    y = _ffn(idx, gate, h2, w_gate[l], w_up[l], w_down[l])
    return _combine(y, tokrep, bounds[:, :, :t // COMBINE_TILE + 1], x1.reshape(b_, t, d), mod,
                    norm_f.reshape(1, d))
```

```python
import functools

import jax
import jax.numpy as jnp
from jax import lax
from jax.experimental import pallas as pl
from jax.experimental.pallas import tpu as pltpu

F32 = jnp.float32
BF16 = jnp.bfloat16

N_HEADS = 4
DQK = 128
DV = 256
QK_CONV = 5
POOL_WINDOWS = (2, 4, 8, 16)
POOL_GC = 256
GRID_W = 64
GRID_SHIFT = GRID_W.bit_length() - 1
N_EXPERTS = 16
EC_FACTOR = 2
N_MOD = 6
EPS = 1e-6

LANES = 128
SUBLANES = 8
MLSTM_CHUNK = 128
VMEM_LIMIT = 56 * 1024 * 1024


def _cparams(sem):
    return pltpu.CompilerParams(dimension_semantics=sem, vmem_limit_bytes=VMEM_LIMIT)


def _sigmoid(x):
    return 1.0 / (1.0 + jnp.exp(-x))


def _silu(x):
    return x * _sigmoid(x)


def _log_sigmoid(x):
    return jnp.minimum(x, 0.0) - jnp.log(1.0 + jnp.exp(-jnp.abs(x)))


def _mod_kernel(c_ref, w_ref, b_ref, o_ref):
    s = _silu(c_ref[...]).astype(BF16)
    o_ref[...] = jnp.dot(s, w_ref[...].astype(BF16), preferred_element_type=F32) + b_ref[...]


def _modulation(cs, w_mod, b_mod):
    rows, d = cs.shape
    n = w_mod.shape[1]
    tn = 1024
    return pl.pallas_call(
        _mod_kernel,
        grid=(n // tn,),
        in_specs=[pl.BlockSpec((rows, d), lambda j: (0, 0)),
                  pl.BlockSpec((d, tn), lambda j: (0, j)),
                  pl.BlockSpec((1, tn), lambda j: (0, j))],
        out_specs=pl.BlockSpec((rows, tn), lambda j: (0, j)),
        out_shape=jax.ShapeDtypeStruct((rows, n), F32),
        compiler_params=_cparams(("arbitrary",)),
    )(cs, w_mod, b_mod.reshape(1, n))


def _inproj_kernel(x_ref, mod_ref, n1_ref, w_ref, wg_ref, bg_ref, p_ref, g_ref, h_scr):
    @pl.when(pl.program_id(1) == 0)
    def _():
        x = x_ref[...]
        y = x * lax.rsqrt(jnp.mean(x * x, axis=-1, keepdims=True) + EPS)
        h = y * (n1_ref[...] * (1.0 + mod_ref[0, 1:2, :])) + mod_ref[0, 0:1, :]
        hb = h.astype(BF16)
        h_scr[...] = hb
        g_ref[...] = jnp.dot(hb, wg_ref[...], preferred_element_type=F32) + bg_ref[...]

    p_ref[...] = jnp.dot(h_scr[...], w_ref[...], preferred_element_type=F32).astype(BF16)


def _inproj(x2, mod, rows_per_mod, n1, w, wg, bg, n_cols, tm, tn):
    m, d = x2.shape
    tiles_per_mod = rows_per_mod // tm
    return pl.pallas_call(
        _inproj_kernel,
        grid=(m // tm, n_cols // tn),
        in_specs=[pl.BlockSpec((tm, d), lambda i, j: (i, 0)),
                  pl.BlockSpec((1, N_MOD, d), lambda i, j: (i // tiles_per_mod, 0, 0)),
                  pl.BlockSpec((1, d), lambda i, j: (0, 0)),
                  pl.BlockSpec((d, tn), lambda i, j: (0, j)),
                  pl.BlockSpec((d, LANES), lambda i, j: (0, 0)),
                  pl.BlockSpec((1, LANES), lambda i, j: (0, 0))],
        out_specs=[pl.BlockSpec((tm, tn), lambda i, j: (i, j)),
                   pl.BlockSpec((tm, LANES), lambda i, j: (i, 0))],
        out_shape=[jax.ShapeDtypeStruct((m, n_cols), BF16),
                   jax.ShapeDtypeStruct((m, LANES), F32)],
        scratch_shapes=[pltpu.VMEM((tm, d), BF16)],
        compiler_params=_cparams(("arbitrary", "arbitrary")),
    )(x2, mod, n1, w, wg, bg)


def _scan_lanes(x, op, fill, reverse):
    lane = lax.broadcasted_iota(jnp.int32, x.shape, 1)
    k = 1
    while k < LANES:
        if reverse:
            x = op(x, jnp.where(lane < LANES - k, pltpu.roll(x, LANES - k, 1), fill))
        else:
            x = op(x, jnp.where(lane >= k, pltpu.roll(x, k, 1), fill))
        k *= 2
    return x


G_B, G_R, G_W, G_BEND, G_MLOC, G_RMAX, G_ROWS = 0, 1, 2, 3, 4, 5, 6


def _mlstm_kernel(kc_ref, vc_ref, gc_ref, k_ref, v_ref, q_ref, o_ref, g_ref, wk_ref, wq_ref, hg_ref,
                  out_ref,
                  cpad, ktt, qt, kctt, vext, vcext, gs, gsc, cf, cb, hf, hb):
    L = MLSTM_CHUNK
    t_lat = k_ref.shape[1]
    t_ctx = kc_ref.shape[1]
    nc = t_lat // L
    ncc = t_ctx // L
    pad = SUBLANES
    half = QK_CONV // 2

    def conv_silu(src, w_ref, dst, t, scale, transposed):
        cpad[0:pad, :] = jnp.zeros((pad, LANES), F32)
        cpad[pad:pad + t, :] = src.astype(F32)
        cpad[pad + t:pad + t + pad, :] = jnp.zeros((pad, LANES), F32)
        for r in range(t // L):
            base = pad + r * L - half
            acc = w_ref[0:1, :] * cpad[base:base + L, :]
            for j in range(1, QK_CONV):
                acc = acc + w_ref[j:j + 1, :] * cpad[base + j:base + j + L, :]
            y = _silu(acc)
            if scale is not None:
                y = y * scale
            dst[r] = (y.T if transposed else y).astype(BF16)

    conv_silu(kc_ref[0], wk_ref, kctt, t_ctx, DQK ** -0.5, True)
    conv_silu(k_ref[0], wk_ref, ktt, t_lat, DQK ** -0.5, True)
    conv_silu(q_ref[0], wq_ref, qt, t_lat, None, False)

    def fill_vext(dst, src, t):
        dst[:, 0:DV] = src
        dst[:, DV:DV + LANES] = jnp.ones((t, LANES), BF16)

    fill_vext(vcext, vc_ref[0], t_ctx)
    fill_vext(vext, v_ref[0], t_lat)

    def gate_prep(garr, dst):
        for d in range(2):
            li = garr[2 * d]
            lf = _log_sigmoid(garr[2 * d + 1])
            b = _scan_lanes(lf, jnp.add, 0.0, reverse=(d == 1))
            b_end = b[:, LANES - 1:LANES] if d == 0 else b[:, 0:1]
            r = li - b
            a = b_end + r
            m_loc = jnp.max(a, axis=1, keepdims=True)
            dst[G_ROWS * d + G_B] = b
            dst[G_ROWS * d + G_R] = r
            dst[G_ROWS * d + G_W] = jnp.exp(a - m_loc)
            dst[G_ROWS * d + G_BEND] = jnp.broadcast_to(b_end, b.shape)
            dst[G_ROWS * d + G_MLOC] = jnp.broadcast_to(m_loc, b.shape)
            dst[G_ROWS * d + G_RMAX] = _scan_lanes(r, jnp.maximum, -jnp.inf, reverse=(d == 1))

    gate_prep(gc_ref[0, 0], gsc)
    gate_prep(g_ref[0, 0], gs)

    cf[...] = jnp.zeros(cf.shape, F32)
    cb[...] = jnp.zeros(cb.shape, F32)

    row_i = lax.broadcasted_iota(jnp.int32, (L, L), 0)
    col_i = lax.broadcasted_iota(jnp.int32, (L, L), 1)
    visible = (col_i <= row_i, col_i >= row_i)

    def per_token(row):
        return jnp.broadcast_to(row, (L, L)).T

    def step(c, d, g_scr, kt_scr, v_scr, q_scr, c_scr, m_prev):
        off = c * L if isinstance(c, int) else pl.multiple_of(c * L, L)
        g0 = G_ROWS * d
        kt_c = kt_scr[c]
        v_c = v_scr[pl.ds(off, L), :]
        b_end = g_scr[g0 + G_BEND, pl.ds(c, 1), 0:1]
        m_loc = g_scr[g0 + G_MLOC, pl.ds(c, 1), 0:1]
        c_prev = c_scr[...]
        h = None
        if q_scr is not None:
            q_c = q_scr[c]
            r_row = g_scr[g0 + G_R, pl.ds(c, 1), :]
            u = jnp.maximum(per_token(g_scr[g0 + G_RMAX, pl.ds(c, 1), :]), m_prev)
            b_t = per_token(g_scr[g0 + G_B, pl.ds(c, 1), :])
            qk = jnp.dot(q_c, kt_c, preferred_element_type=F32)
            s = (qk * jnp.exp(jnp.where(visible[d], r_row - u, -jnp.inf))).astype(BF16)
            wi = jnp.exp(m_prev - u)
            inter = jnp.dot(q_c, c_prev.astype(BF16), preferred_element_type=F32)
            res = jnp.dot(s, v_c, preferred_element_type=F32) \
                + jnp.concatenate([wi] * (c_prev.shape[1] // L), axis=1) * inter
            inv = 1.0 / jnp.maximum(jnp.abs(res[:, DV:DV + L]), jnp.exp(-(b_t + u)))
            h = res[:, 0:DV] * jnp.concatenate([inv] * (DV // L), axis=1)
        kwt = (kt_c.astype(F32) * g_scr[g0 + G_W, pl.ds(c, 1), :]).astype(BF16)
        c_loc = jnp.dot(kwt, v_c, preferred_element_type=F32)
        m_new = jnp.maximum(b_end + m_prev, m_loc)
        sp = jnp.exp(b_end + m_prev - m_new)
        sl = jnp.exp(m_loc - m_new)
        c_scr[...] = sp * c_prev + sl * c_loc
        return m_new, h

    def finish(c, h):
        off = pl.multiple_of(c * L, L)
        h = h * lax.rsqrt(jnp.mean(h * h, axis=-1, keepdims=True) + EPS)
        y = (h * hg_ref[...]) * _sigmoid(o_ref[0, pl.ds(off, L), :].astype(F32))
        out_ref[0, pl.ds(off, L), :] = y.astype(BF16)

    m_f = jnp.zeros((1, 1), F32)
    m_b = jnp.zeros((1, 1), F32)
    for i in range(ncc):
        m_f, _ = step(i, 0, gsc, kctt, vcext, None, cf, m_f)
        m_b, _ = step(ncc - 1 - i, 1, gsc, kctt, vcext, None, cb, m_b)

    assert nc % 2 == 0

    def first_half(i, carry):
        c_f, c_b = i, nc - 1 - i
        m_f, h_f = step(c_f, 0, gs, ktt, vext, qt, cf, carry[0])
        m_b, h_b = step(c_b, 1, gs, ktt, vext, qt, cb, carry[1])
        hf[pl.ds(pl.multiple_of(c_f * L, L), L), :] = h_f
        hb[pl.ds(pl.multiple_of(c_b * L, L), L), :] = h_b
        return m_f, m_b

    def second_half(i, carry):
        c_f, c_b = i, nc - 1 - i
        m_f, h_f = step(c_f, 0, gs, ktt, vext, qt, cf, carry[0])
        m_b, h_b = step(c_b, 1, gs, ktt, vext, qt, cb, carry[1])
        finish(c_f, h_f + hb[pl.ds(pl.multiple_of(c_f * L, L), L), :])
        finish(c_b, h_b + hf[pl.ds(pl.multiple_of(c_b * L, L), L), :])
        return m_f, m_b

    carry = lax.fori_loop(0, nc // 2, first_half, (m_f, m_b), unroll=2)
    lax.fori_loop(nc // 2, nc, second_half, carry, unroll=2)


def _mlstm(p, pc, g4, gc4, wk, wq, head_g):
    b_, t, _ = p.shape
    tc = pc.shape[1]
    L = MLSTM_CHUNK
    nc, ncc = t // L, tc // L
    ext = DV + LANES
    return pl.pallas_call(
        _mlstm_kernel,
        grid=(b_, N_HEADS),
        in_specs=[
            pl.BlockSpec((1, tc, DQK), lambda b, h: (b, 0, h)),
            pl.BlockSpec((1, tc, DV), lambda b, h: (b, 0, 2 + h)),
            pl.BlockSpec((1, 1, 4, ncc, L), lambda b, h: (b, h, 0, 0, 0)),
            pl.BlockSpec((1, t, DQK), lambda b, h: (b, 0, h)),
            pl.BlockSpec((1, t, DV), lambda b, h: (b, 0, 2 + h)),
            pl.BlockSpec((1, t, DQK), lambda b, h: (b, 0, 12 + h)),
            pl.BlockSpec((1, t, DV), lambda b, h: (b, 0, 8 + h)),
            pl.BlockSpec((1, 1, 4, nc, L), lambda b, h: (b, h, 0, 0, 0)),
            pl.BlockSpec((SUBLANES, DQK), lambda b, h: (0, h)),
            pl.BlockSpec((SUBLANES, DQK), lambda b, h: (0, h)),
            pl.BlockSpec((1, DV), lambda b, h: (0, h)),
        ],
        out_specs=pl.BlockSpec((1, t, DV), lambda b, h: (b, 0, h)),
        out_shape=jax.ShapeDtypeStruct((b_, t, N_HEADS * DV), BF16),
        scratch_shapes=[
            pltpu.VMEM((t + 2 * SUBLANES, LANES), F32),
            pltpu.VMEM((nc, DQK, L), BF16),
            pltpu.VMEM((nc, L, DQK), BF16),
            pltpu.VMEM((ncc, DQK, L), BF16),
            pltpu.VMEM((t, ext), BF16),
            pltpu.VMEM((tc, ext), BF16),
            pltpu.VMEM((2 * G_ROWS, nc, L), F32),
            pltpu.VMEM((2 * G_ROWS, ncc, L), F32),
            pltpu.VMEM((DQK, ext), F32),
            pltpu.VMEM((DQK, ext), F32),
            pltpu.VMEM((t, DV), F32),
            pltpu.VMEM((t, DV), F32),
        ],
        compiler_params=_cparams(("arbitrary", "arbitrary")),
    )(pc, pc, gc4, p, p, p, p, g4, wk, wq, head_g)


def _pool_kernel(u_ref, pw_ref, ps_ref, out_ref, spad):
    t = u_ref.shape[1]
    rows = t // GRID_W
    blk = 256
    halo = (max(POOL_WINDOWS) // 2) * GRID_W
    spad[0:halo, :] = jnp.zeros((halo, POOL_GC), F32)
    spad[halo + t:halo + t + halo, :] = jnp.zeros((halo, POOL_GC), F32)
    ti = lax.broadcasted_iota(jnp.int32, (blk, blk), 0)
    tj = lax.broadcasted_iota(jnp.int32, (blk, blk), 1)
    same_row = (ti >> GRID_SHIFT) == (tj >> GRID_SHIFT)
    diff = tj - ti
    tok = lax.broadcasted_iota(jnp.int32, (blk, 1), 0)
    for gi, w in enumerate(POOL_WINDOWS):
        hw = w // 2
        c0, c1 = gi * POOL_GC, (gi + 1) * POOL_GC
        band = jnp.where(same_row & (diff >= -hw) & (diff <= hw - 1), 1.0, 0.0).astype(BF16)
        for r in range(t // blk):
            spad[halo + r * blk:halo + (r + 1) * blk, :] = jnp.dot(
                band, u_ref[0, r * blk:(r + 1) * blk, c0:c1], preferred_element_type=F32)
        for r in range(t // blk):
            base = halo + r * blk
            acc = spad[base - hw * GRID_W:base - hw * GRID_W + blk, :]
            for j in range(-hw + 1, hw):
                acc = acc + spad[base + j * GRID_W:base + j * GRID_W + blk, :]
            tt = tok + r * blk
            gr = tt >> GRID_SHIFT
            gc = tt & (GRID_W - 1)
            cr = jnp.minimum(gr + hw, rows) - jnp.maximum(gr - hw, 0)
            cc = jnp.minimum(gc + hw, GRID_W) - jnp.maximum(gc - hw, 0)
            cnt = (cr * cc).astype(F32)
            d = acc / cnt - u_ref[0, r * blk:(r + 1) * blk, c0:c1].astype(F32)
            y = jnp.dot(d.astype(BF16), pw_ref[gi], preferred_element_type=F32) * ps_ref[:, c0:c1]
            out_ref[0, r * blk:(r + 1) * blk, c0:c1] = y.astype(BF16)


def _pool(p, pool_w, pool_scale):
    b_, t, _ = p.shape
    pw = len(POOL_WINDOWS) * POOL_GC
    halo = (max(POOL_WINDOWS) // 2) * GRID_W
    return pl.pallas_call(
        _pool_kernel,
        grid=(b_,),
        in_specs=[pl.BlockSpec((1, t, pw), lambda b: (b, 0, 3)),
                  pl.BlockSpec((len(POOL_WINDOWS), POOL_GC, POOL_GC), lambda b: (0, 0, 0)),
                  pl.BlockSpec((1, pw), lambda b: (0, 0))],
        out_specs=pl.BlockSpec((1, t, pw), lambda b: (b, 0, 0)),
        out_shape=jax.ShapeDtypeStruct((b_, t, pw), BF16),
        scratch_shapes=[pltpu.VMEM((t + 2 * halo, POOL_GC), F32)],
        compiler_params=_cparams(("arbitrary",)),
    )(p, pool_w, pool_scale)


def _outproj_kernel(ym_ref, yp_ref, wm_ref, wp_ref, x_ref, mod_ref, n2_ref, wr_ref,
                    x1_ref, h2_ref, lg_ref):
    tm = x_ref.shape[0]
    half = tm // 2
    scale2 = n2_ref[...] * (1.0 + mod_ref[0, 4:5, :])
    for r0 in (0, half):
        rows = slice(r0, r0 + half)
        mix = (jnp.dot(ym_ref[rows, :], wm_ref[...], preferred_element_type=F32)
               + jnp.dot(yp_ref[rows, :], wp_ref[...], preferred_element_type=F32))
        x1 = x_ref[rows, :] + mod_ref[0, 2:3, :] * mix
        x1_ref[rows, :] = x1
        y = x1 * lax.rsqrt(jnp.mean(x1 * x1, axis=-1, keepdims=True) + EPS)
        h2 = y * scale2 + mod_ref[0, 3:4, :]
        g0, g1 = r0 // SUBLANES, (r0 + half) // SUBLANES
        for j in range(h2_ref.shape[1]):
            h2_ref[g0:g1, j, :, :] = h2[:, j * LANES:(j + 1) * LANES].reshape(g1 - g0, SUBLANES, LANES)
        lg_ref[rows, :] = jnp.dot(h2.astype(BF16), wr_ref[...], preferred_element_type=F32)


def _outproj(ym, yp, wm, wp, x2, mod, rows_per_mod, n2, wr, tm):
    m, d = x2.shape
    km = ym.shape[1]
    kp = yp.shape[1]
    tiles_per_mod = rows_per_mod // tm
    return pl.pallas_call(
        _outproj_kernel,
        grid=(m // tm,),
        in_specs=[pl.BlockSpec((tm, km), lambda i: (i, 0)),
                  pl.BlockSpec((tm, kp), lambda i: (i, 0)),
                  pl.BlockSpec((km, d), lambda i: (0, 0), pipeline_mode=pl.Buffered(1)),
                  pl.BlockSpec((kp, d), lambda i: (0, 0), pipeline_mode=pl.Buffered(1)),
                  pl.BlockSpec((tm, d), lambda i: (i, 0)),
                  pl.BlockSpec((1, N_MOD, d), lambda i: (i // tiles_per_mod, 0, 0)),
                  pl.BlockSpec((1, d), lambda i: (0, 0)),
                  pl.BlockSpec((d, LANES), lambda i: (0, 0))],
        out_specs=[pl.BlockSpec((tm, d), lambda i: (i, 0)),
                   pl.BlockSpec((tm // SUBLANES, d // LANES, SUBLANES, LANES), lambda i: (i, 0, 0, 0)),
                   pl.BlockSpec((tm, LANES), lambda i: (i, 0))],
        out_shape=[jax.ShapeDtypeStruct((m, d), F32),
                   jax.ShapeDtypeStruct((m // SUBLANES, d // LANES, SUBLANES, LANES), F32),
                   jax.ShapeDtypeStruct((m, LANES), F32)],
        compiler_params=_cparams(("arbitrary",)),
    )(ym, yp, wm, wp, x2, mod, n2, wr)


def _scan_lanes_i32(x, exclusive_of=None):
    n = x.shape[1]
    lane = lax.broadcasted_iota(jnp.int32, x.shape, 1)
    k = 1
    while k < n:
        x = x + jnp.where(lane >= k, pltpu.roll(x, k, 1), 0)
        k *= 2
    return x


COMBINE_TILE = 256
GRANULE = 2 * SUBLANES


def _route_kernel(lg_ref, idx_ref, gate_ref, tokrep_ref, bounds_ref, afft):
    t = lg_ref.shape[1]
    cap = idx_ref.shape[2]
    lane = lax.broadcasted_iota(jnp.int32, (t, LANES), 1)
    lg = jnp.where(lane < N_EXPERTS, lg_ref[0], -jnp.inf)
    ex = jnp.exp(lg - jnp.max(lg, axis=1, keepdims=True))
    aff = ex / jnp.sum(ex, axis=1, keepdims=True)
    for r in range(t // LANES):
        afft[:, r * LANES:(r + 1) * LANES] = aff[r * LANES:(r + 1) * LANES, :].T
    aff_t = afft[0:N_EXPERTS, :]

    def count(mask):
        return jnp.sum(jnp.where(mask, 1.0, 0.0), axis=1, keepdims=True).astype(jnp.int32)

    def bit_step(i, thr_bits):
        cand = thr_bits | (jnp.int32(1) << (30 - i))
        cand_f = lax.bitcast_convert_type(cand, F32)
        return jnp.where(count(aff_t >= cand_f) >= cap, cand, thr_bits)

    thr_bits = lax.fori_loop(0, 31, bit_step, jnp.zeros((N_EXPERTS, 1), jnp.int32))
    thr = lax.bitcast_convert_type(thr_bits, F32)
    gt = aff_t > thr
    eq = aff_t == thr
    need = cap - count(gt)
    eq_i = jnp.where(eq, 1, 0)
    eq_rank = _scan_lanes_i32(eq_i) - eq_i
    sel = gt | (eq & (eq_rank < need))
    sel_i = jnp.where(sel, 1, 0)
    pos = _scan_lanes_i32(sel_i) - sel_i
    key = jnp.where(sel, pos, -1)

    a_hi = aff.astype(BF16).astype(F32)
    a_mid = (aff - a_hi).astype(BF16).astype(F32)
    a_lo = (aff - a_hi - a_mid).astype(BF16).astype(F32)
    tok = lax.broadcasted_iota(jnp.int32, (t, LANES), 0)
    vals = (a_hi + pltpu.roll(a_mid, N_EXPERTS, 1) + pltpu.roll(a_lo, 2 * N_EXPERTS, 1)
            + jnp.where(lane == 3 * N_EXPERTS, (tok >> GRID_SHIFT).astype(F32), 0.0)
            + jnp.where(lane == 3 * N_EXPERTS + 1, (tok & (GRID_W - 1)).astype(F32), 0.0)).astype(BF16)
    slot = lax.broadcasted_iota(jnp.int32, (cap, 1), 0)
    for e in range(N_EXPERTS):
        onehot = jnp.where(slot == key[e:e + 1, :], 1.0, 0.0).astype(BF16)
        res = jnp.dot(onehot, vals, preferred_element_type=F32)
        rt = jnp.concatenate([res[r * LANES:(r + 1) * LANES, :].T for r in range(cap // LANES)], axis=1)
        gate_ref[0, e:e + 1, :] = (rt[e:e + 1, :] + rt[N_EXPERTS + e:N_EXPERTS + e + 1, :]) \
            + rt[2 * N_EXPERTS + e:2 * N_EXPERTS + e + 1, :]
        tokf = rt[3 * N_EXPERTS:3 * N_EXPERTS + 1, :] * float(GRID_W) + rt[3 * N_EXPERTS + 1:3 * N_EXPERTS + 2, :]
        idx_ref[0, e:e + 1, :] = tokf.astype(jnp.int32)
        tok_col = res[:, 3 * N_EXPERTS:3 * N_EXPERTS + 1] * float(GRID_W) \
            + res[:, 3 * N_EXPERTS + 1:3 * N_EXPERTS + 2]
        tokrep_ref[0, e * cap:(e + 1) * cap, :] = jnp.broadcast_to(tok_col, (cap, LANES)).astype(jnp.int32)

    n_tiles = t // COMBINE_TILE
    lane_b = lax.broadcasted_iota(jnp.int32, (N_EXPERTS, LANES), 1)
    bounds = jnp.where(lane_b >= n_tiles, cap, 0)
    for i in range(n_tiles):
        bounds = jnp.where(lane_b == i, pos[:, i * COMBINE_TILE:i * COMBINE_TILE + 1], bounds)
    bounds_ref[0] = bounds


def _route(lg3, cap):
    b_, t, _ = lg3.shape
    return pl.pallas_call(
        _route_kernel,
        grid=(b_,),
        in_specs=[pl.BlockSpec((1, t, LANES), lambda b: (b, 0, 0))],
        out_specs=[pl.BlockSpec((1, N_EXPERTS, cap), lambda b: (b, 0, 0)),
                   pl.BlockSpec((1, N_EXPERTS, cap), lambda b: (b, 0, 0)),
                   pl.BlockSpec((1, N_EXPERTS * cap, LANES), lambda b: (b, 0, 0)),
                   pl.BlockSpec((1, N_EXPERTS, LANES), lambda b: (b, 0, 0))],
        out_shape=[jax.ShapeDtypeStruct((b_, N_EXPERTS, cap), jnp.int32),
                   jax.ShapeDtypeStruct((b_, N_EXPERTS, cap), F32),
                   jax.ShapeDtypeStruct((b_, N_EXPERTS * cap, LANES), jnp.int32),
                   jax.ShapeDtypeStruct((b_, N_EXPERTS, LANES), jnp.int32)],
        scratch_shapes=[pltpu.VMEM((LANES, t), F32)],
        compiler_params=_cparams(("arbitrary",)),
    )(lg3)


def _ffn_kernel(idx_ref, gate_ref, h2_hbm, wg_ref, wu_ref, wd_ref, y_ref,
                xs_buf, wg_s, wu_s, wd_s, sem):
    p = pl.program_id(0)
    b = pl.program_id(1)
    n_e = pl.num_programs(0) - 1
    n_b = pl.num_programs(1)
    k_chunks, cap = xs_buf.shape[1], xs_buf.shape[2]
    groups = h2_hbm.shape[0] // n_b
    step = p * n_b + b
    cur = step % 2
    nxt = 1 - cur
    last_b = b == n_b - 1
    b_next = jnp.where(last_b, 0, b + 1)
    p_next = jnp.where(last_b, p + 1, p)

    @pl.when((p_next >= 1) & (p_next <= n_e))
    def _():
        def issue(r, carry):
            tok = idx_ref[0, 0, r]
            pltpu.make_async_copy(h2_hbm.at[b_next * groups + (tok >> 3), :, tok & (SUBLANES - 1), :],
                                  xs_buf.at[nxt, :, r, :], sem.at[nxt]).start()
            return carry

        lax.fori_loop(0, cap, issue, 0, unroll=True)

    @pl.when(p < n_e)
    def _():
        ws = p % 2
        rows_in = wg_ref.shape[1]
        rows_mid = wd_ref.shape[1]
        wg_s[ws, pl.ds(pl.multiple_of(b * rows_in, rows_in), rows_in), :] = wg_ref[0].astype(BF16)
        wu_s[ws, pl.ds(pl.multiple_of(b * rows_in, rows_in), rows_in), :] = wu_ref[0].astype(BF16)
        wd_s[ws, pl.ds(pl.multiple_of(b * rows_mid, rows_mid), rows_mid), :] = wd_ref[0].astype(BF16)

    @pl.when(p == 0)
    def _():
        y_ref[0, 0] = jnp.zeros(y_ref.shape[2:], BF16)

    @pl.when(p >= 1)
    def _():
        pltpu.make_async_copy(xs_buf.at[cur], xs_buf.at[cur], sem.at[cur]).wait()
        ws = (p - 1) % 2
        xs = jnp.concatenate([xs_buf[cur, j].astype(BF16) for j in range(k_chunks)], axis=1)
        a = _silu(jnp.dot(xs, wg_s[ws], preferred_element_type=F32)) \
            * jnp.dot(xs, wu_s[ws], preferred_element_type=F32)
        y = jnp.dot(a.astype(BF16), wd_s[ws], preferred_element_type=F32)
        y_ref[0, 0] = (y * gate_ref[0, 0]).astype(BF16)


def _ffn(idx, gate, h2, wg, wu, wd):
    e_, d, f = wg.shape
    b_, _, cap = idx.shape
    assert h2.shape[1] * h2.shape[3] == d and h2.shape[2] == SUBLANES and h2.shape[0] % b_ == 0
    assert d % b_ == 0 and f % b_ == 0
    rows_in, rows_mid = d // b_, f // b_
    assert rows_in % (2 * SUBLANES) == 0 and rows_mid % (2 * SUBLANES) == 0

    def next_tokens(p, b):
        last_b = b == b_ - 1
        b_next = jnp.where(last_b, 0, b + 1)
        e_next = jnp.clip(jnp.where(last_b, p, p - 1), 0, e_ - 1)
        return (b_next * e_ + e_next, 0, 0)

    def this_expert(p, b):
        return (b, jnp.maximum(p - 1, 0), 0, 0)

    def weight_piece(p, b):
        return (jnp.minimum(p, e_ - 1), b, 0)

    return pl.pallas_call(
        _ffn_kernel,
        grid=(e_ + 1, b_),
        in_specs=[pl.BlockSpec((1, 1, cap), next_tokens, memory_space=pltpu.SMEM),
                  pl.BlockSpec((1, 1, cap, 1), this_expert),
                  pl.BlockSpec(memory_space=pl.ANY),
                  pl.BlockSpec((1, rows_in, f), weight_piece),
                  pl.BlockSpec((1, rows_in, f), weight_piece),
                  pl.BlockSpec((1, rows_mid, d), weight_piece)],
        out_specs=pl.BlockSpec((1, 1, cap, d), lambda p, b: (b, (p + e_) % (e_ + 1), 0, 0)),
        out_shape=jax.ShapeDtypeStruct((b_, e_ + 1, cap, d), BF16),
        scratch_shapes=[pltpu.VMEM((2, d // LANES, cap, LANES), F32),
                        pltpu.VMEM((2, d, f), BF16),
                        pltpu.VMEM((2, d, f), BF16),
                        pltpu.VMEM((2, f, d), BF16),
                        pltpu.SemaphoreType.DMA((2,))],
        compiler_params=_cparams(("arbitrary", "arbitrary")),
    )(idx.reshape(b_ * e_, 1, cap), gate.reshape(b_, e_, cap, 1), h2, wg, wu, wd)


COMBINE_CHUNK = 256


def _combine_kernel(bounds_ref, y_ref, tok_ref, x1_ref, mod_ref, nf_ref, o_ref, stage_y, stage_t):
    b = pl.program_id(0)
    i = pl.program_id(1)
    n_e, cap = y_ref.shape[1], y_ref.shape[2]
    tile = o_ref.shape[1]
    n_bounds = pl.num_programs(1) + 1
    per_chunk = COMBINE_CHUNK // GRANULE
    g_shift = GRANULE.bit_length() - 1
    assert per_chunk & (per_chunk - 1) == 0 and GRANULE == 1 << g_shift

    @pl.when((b == 0) & (i == 0))
    def _():
        stage_y[...] = jnp.zeros(stage_y.shape, BF16)
        stage_t[...] = jnp.full(stage_t.shape, -1, jnp.int32)

    o_ref[...] = jnp.zeros(o_ref.shape, F32)
    tok_lane = lax.broadcasted_iota(jnp.int32, (COMBINE_CHUNK, LANES), 1) + i * tile

    def flush():
        onehot_t = jnp.concatenate(
            [jnp.where(stage_t[...] == tok_lane + j * LANES, 1.0, 0.0).astype(BF16)
             for j in range(tile // LANES)], axis=1)
        o_ref[0] += lax.dot_general(onehot_t, stage_y[...], (((0,), (0,)), ((), ())),
                                    preferred_element_type=F32)
        stage_t[...] = jnp.full(stage_t.shape, -1, jnp.int32)

    def per_expert(e, k):
        base = (b * n_e + e) * n_bounds + i
        lo = bounds_ref[base]
        hi = bounds_ref[base + 1]
        g_lo = lax.shift_right_logical(lo, g_shift)
        g_hi = jnp.where(hi > lo, lax.shift_right_logical(hi + GRANULE - 1, g_shift), g_lo)

        def per_granule(g, k):
            src = pl.multiple_of(g * GRANULE, GRANULE)
            dst = pl.multiple_of((k & (per_chunk - 1)) * GRANULE, GRANULE)
            stage_y[pl.ds(dst, GRANULE), :] = y_ref[0, e, pl.ds(src, GRANULE), :]
            stage_t[pl.ds(dst, GRANULE), :] = tok_ref[0, pl.ds(pl.multiple_of(e * cap + src, GRANULE), GRANULE), :]
            k = k + 1

            @pl.when((k & (per_chunk - 1)) == 0)
            def _():
                flush()

            return k

        return lax.fori_loop(g_lo, g_hi, per_granule, k)

    k = lax.fori_loop(0, n_e, per_expert, jnp.int32(0))

    @pl.when((k & (per_chunk - 1)) != 0)
    def _():
        flush()

    x = x1_ref[0] + mod_ref[0, 5:6, :] * o_ref[0]
    o_ref[0] = (x * lax.rsqrt(jnp.mean(x * x, axis=-1, keepdims=True) + EPS)) * nf_ref[...]


def _combine(y, tokrep, bounds, x1, mod, nf):
    b_, e_, cap, d = y.shape
    e_ -= 1
    t = x1.shape[1]
    tile = COMBINE_TILE
    assert cap % GRANULE == 0 and tile % LANES == 0 and bounds.shape == (b_, e_, t // tile + 1)
    grid_spec = pltpu.PrefetchScalarGridSpec(
        num_scalar_prefetch=1,
        grid=(b_, t // tile),
        in_specs=[pl.BlockSpec((1, e_, cap, d), lambda b, i, bnd: (b, 0, 0, 0)),
                  pl.BlockSpec((1, e_ * cap, LANES), lambda b, i, bnd: (b, 0, 0)),
                  pl.BlockSpec((1, tile, d), lambda b, i, bnd: (b, i, 0)),
                  pl.BlockSpec((1, N_MOD, d), lambda b, i, bnd: (b, 0, 0)),
                  pl.BlockSpec((1, d), lambda b, i, bnd: (0, 0))],
        out_specs=pl.BlockSpec((1, tile, d), lambda b, i, bnd: (b, i, 0)),
        scratch_shapes=[pltpu.VMEM((COMBINE_CHUNK, d), BF16),
                        pltpu.VMEM((COMBINE_CHUNK, LANES), jnp.int32)])
    return pl.pallas_call(
        _combine_kernel,
        grid_spec=grid_spec,
        out_shape=jax.ShapeDtypeStruct((b_, t, d), F32),
        compiler_params=_cparams(("arbitrary", "arbitrary")),
    )(bounds.reshape(-1), y, tokrep, x1, mod, nf)


def _gate_layout(g, b_, t):
    g = g[:, :2 * 2 * N_HEADS].reshape(b_, t, 2, 2, N_HEADS)
    g = g.transpose(0, 4, 2, 3, 1)
    return g.reshape(b_, N_HEADS, 4, t // MLSTM_CHUNK, MLSTM_CHUNK)


def kernel(x, c, ctx, c_ctx, w_mod, b_mod, norm1, w_in, conv_q_w, conv_k_w, b_gates, head_g, pool_w,
           pool_scale, w_out, norm2, w_router, w_gate, w_up, w_down, norm_f):
    b_, t, d = x.shape
    tc = ctx.shape[1]
    depth = w_mod.shape[0]
    assert depth == 1
    l = 0
    qk_w = N_HEADS * DQK
    mw = N_HEADS * DV
    pool_wd = len(POOL_WINDOWS) * POOL_GC
    n_gate = 2 * 2 * N_HEADS
    k_off, v_off = 0, qk_w
    g_off = v_off + mw
    q_off = g_off + n_gate
    o_off = q_off + qk_w
    p_off = o_off + mw
    cap = EC_FACTOR * t // N_EXPERTS

    mod_rows = -(-(b_ + 1) // SUBLANES) * SUBLANES
    cs = jnp.zeros((mod_rows, d), F32).at[:b_].set(c).at[b_].set(c_ctx)
    mod_all = _modulation(cs, w_mod[l], b_mod[l])
    mod = mod_all[:b_].reshape(b_, N_MOD, d)
    mod_c = mod_all[b_:b_ + 1].reshape(1, N_MOD, d)

    wl = w_in[l]
    w_main = jnp.concatenate([wl[:, k_off:v_off], wl[:, v_off:g_off], wl[:, q_off:o_off],
                              wl[:, o_off:p_off], wl[:, p_off:]], axis=1).astype(BF16)
    w_g = jnp.pad(wl[:, g_off:q_off], ((0, 0), (0, LANES - n_gate))).astype(BF16)
    b_g = jnp.pad(b_gates[l], (0, LANES - n_gate)).reshape(1, LANES)
    n1 = norm1[l].reshape(1, d)

    p, g = _inproj(x.reshape(b_ * t, d), mod, t, n1, w_main, w_g, b_g, w_main.shape[1], 1024, 1024)
    pc, gc = _inproj(ctx.reshape(b_ * tc, d), mod_c, b_ * tc, n1, w_main, w_g, b_g, qk_w + mw,
                     min(1024, b_ * tc), 512)
    p = p.reshape(b_, t, -1)
    pc = pc.reshape(b_, tc, -1)

    wk = jnp.pad(conv_k_w[l], ((0, SUBLANES - QK_CONV), (0, 0)))
    wq = jnp.pad(conv_q_w[l], ((0, SUBLANES - QK_CONV), (0, 0)))
    ym = _mlstm(p, pc, _gate_layout(g, b_, t), _gate_layout(gc, b_, tc), wk, wq, head_g[l].reshape(1, mw))
    yp = _pool(p, pool_w[l].astype(BF16), pool_scale[l].reshape(1, pool_wd))

    wo = w_out[l].astype(BF16)
    wr = jnp.pad(w_router[l], ((0, 0), (0, LANES - N_EXPERTS))).astype(BF16)
    x1, h2, lg = _outproj(ym.reshape(b_ * t, mw), yp.reshape(b_ * t, pool_wd), wo[:mw], wo[mw:],
                          x.reshape(b_ * t, d), mod, t, norm2[l].reshape(1, d), wr, 512)

    idx, gate, tokrep, bounds = _route(lg.reshape(b_, t, LANES), cap)

    y = _ffn(idx, gate, h2, w_gate[l], w_up[l], w_down[l])
    return _combine(y, tokrep, bounds[:, :, :t // COMBINE_TILE + 1], x1.reshape(b_, t, d), mod,
                    norm_f.reshape(1, d))
```

```python
import functools

import jax
import jax.numpy as jnp
from jax import lax
from jax.experimental import pallas as pl
from jax.experimental.pallas import tpu as pltpu

F32 = jnp.float32
BF16 = jnp.bfloat16

N_HEADS = 4
DQK = 128
DV = 256
QK_CONV = 5
POOL_WINDOWS = (2, 4, 8, 16)
POOL_GC = 256
GRID_W = 64
GRID_SHIFT = GRID_W.bit_length() - 1
N_EXPERTS = 16
EC_FACTOR = 2
N_MOD = 6
EPS = 1e-6

LANES = 128
SUBLANES = 8
MLSTM_CHUNK = 128
VMEM_LIMIT = 56 * 1024 * 1024


def _cparams(sem):
    return pltpu.CompilerParams(dimension_semantics=sem, vmem_limit_bytes=VMEM_LIMIT)


def _sigmoid(x):
    return 1.0 / (1.0 + jnp.exp(-x))


def _silu(x):
    return x * _sigmoid(x)


def _log_sigmoid(x):
    return jnp.minimum(x, 0.0) - jnp.log(1.0 + jnp.exp(-jnp.abs(x)))


def _mod_kernel(c_ref, w_ref, b_ref, o_ref):
    s = _silu(c_ref[...]).astype(BF16)
    o_ref[...] = jnp.dot(s, w_ref[...].astype(BF16), preferred_element_type=F32) + b_ref[...]


def _modulation(cs, w_mod, b_mod):
    rows, d = cs.shape
    n = w_mod.shape[1]
    tn = 1024
    return pl.pallas_call(
        _mod_kernel,
        grid=(n // tn,),
        in_specs=[pl.BlockSpec((rows, d), lambda j: (0, 0)),
                  pl.BlockSpec((d, tn), lambda j: (0, j)),
                  pl.BlockSpec((1, tn), lambda j: (0, j))],
        out_specs=pl.BlockSpec((rows, tn), lambda j: (0, j)),
        out_shape=jax.ShapeDtypeStruct((rows, n), F32),
        compiler_params=_cparams(("arbitrary",)),
    )(cs, w_mod, b_mod.reshape(1, n))


INPROJ_COLS = 512


def _inproj_kernel(x_ref, mod_ref, n1_ref, w_ref, wg_ref, bg_ref, p_ref, gt_ref):
    tm = x_ref.shape[0]
    half = tm // 2
    n_cols = p_ref.shape[1]
    scale = n1_ref[...] * (1.0 + mod_ref[0, 1:2, :])
    for r0 in (0, half):
        rows = slice(r0, r0 + half)
        x = x_ref[rows, :]
        y = x * lax.rsqrt(jnp.mean(x * x, axis=-1, keepdims=True) + EPS)
        hb = (y * scale + mod_ref[0, 0:1, :]).astype(BF16)
        gt_ref[:, rows] = lax.dot_general(wg_ref[...], hb, (((1,), (1,)), ((), ())),
                                          preferred_element_type=F32) + bg_ref[...]
        for c0 in range(0, n_cols, INPROJ_COLS):
            p_ref[rows, c0:c0 + INPROJ_COLS] = jnp.dot(
                hb, w_ref[:, c0:c0 + INPROJ_COLS], preferred_element_type=F32).astype(BF16)


def _inproj(x2, mod, rows_per_mod, n1, w, wg, bg, tm):
    m, d = x2.shape
    n_cols = w.shape[1]
    n_g = wg.shape[0]
    tiles_per_mod = rows_per_mod // tm
    assert n_cols % INPROJ_COLS == 0 and tm % (2 * LANES) == 0
    return pl.pallas_call(
        _inproj_kernel,
        grid=(m // tm,),
        in_specs=[pl.BlockSpec((tm, d), lambda i: (i, 0)),
                  pl.BlockSpec((1, N_MOD, d), lambda i: (i // tiles_per_mod, 0, 0)),
                  pl.BlockSpec((1, d), lambda i: (0, 0)),
                  pl.BlockSpec((d, n_cols), lambda i: (0, 0), pipeline_mode=pl.Buffered(1)),
                  pl.BlockSpec((n_g, d), lambda i: (0, 0)),
                  pl.BlockSpec((n_g, 1), lambda i: (0, 0))],
        out_specs=[pl.BlockSpec((tm, n_cols), lambda i: (i, 0)),
                   pl.BlockSpec((n_g, tm), lambda i: (0, i))],
        out_shape=[jax.ShapeDtypeStruct((m, n_cols), BF16),
                   jax.ShapeDtypeStruct((n_g, m), F32)],
        compiler_params=_cparams(("arbitrary",)),
    )(x2, mod, n1, w, wg, bg)


def _scan_lanes(x, op, fill, reverse):
    lane = lax.broadcasted_iota(jnp.int32, x.shape, 1)
    k = 1
    while k < LANES:
        if reverse:
            x = op(x, jnp.where(lane < LANES - k, pltpu.roll(x, LANES - k, 1), fill))
        else:
            x = op(x, jnp.where(lane >= k, pltpu.roll(x, k, 1), fill))
        k *= 2
    return x


G_B, G_R, G_W, G_BEND, G_MLOC, G_RMAX, G_ROWS = 0, 1, 2, 3, 4, 5, 6


def _mlstm_kernel(kc_ref, vc_ref, gc_ref, k_ref, v_ref, q_ref, o_ref, g_ref, wk_ref, wq_ref, hg_ref,
                  out_ref,
                  cpad, ktt, qt, kctt, vext, vcext, gs, gsc, cf, cb, hf, hb):
    L = MLSTM_CHUNK
    t_lat = k_ref.shape[1]
    t_ctx = kc_ref.shape[1]
    nc = t_lat // L
    ncc = t_ctx // L
    pad = SUBLANES
    half = QK_CONV // 2

    def conv_silu(src, w_ref, dst, t, scale, transposed):
        cpad[0:pad, :] = jnp.zeros((pad, LANES), F32)
        cpad[pad:pad + t, :] = src.astype(F32)
        cpad[pad + t:pad + t + pad, :] = jnp.zeros((pad, LANES), F32)
        for r in range(t // L):
            base = pad + r * L - half
            acc = w_ref[0:1, :] * cpad[base:base + L, :]
            for j in range(1, QK_CONV):
                acc = acc + w_ref[j:j + 1, :] * cpad[base + j:base + j + L, :]
            y = _silu(acc)
            if scale is not None:
                y = y * scale
            dst[r] = (y.T if transposed else y).astype(BF16)

    conv_silu(kc_ref[0], wk_ref, kctt, t_ctx, DQK ** -0.5, True)
    conv_silu(k_ref[0], wk_ref, ktt, t_lat, DQK ** -0.5, True)
    conv_silu(q_ref[0], wq_ref, qt, t_lat, None, False)

    def fill_vext(dst, src, t):
        dst[:, 0:DV] = src
        dst[:, DV:DV + LANES] = jnp.ones((t, LANES), BF16)

    fill_vext(vcext, vc_ref[0], t_ctx)
    fill_vext(vext, v_ref[0], t_lat)

    def gate_prep(garr, dst):
        for d in range(2):
            li = garr[2 * d]
            lf = _log_sigmoid(garr[2 * d + 1])
            b = _scan_lanes(lf, jnp.add, 0.0, reverse=(d == 1))
            b_end = b[:, LANES - 1:LANES] if d == 0 else b[:, 0:1]
            r = li - b
            a = b_end + r
            m_loc = jnp.max(a, axis=1, keepdims=True)
            dst[G_ROWS * d + G_B] = b
            dst[G_ROWS * d + G_R] = r
            dst[G_ROWS * d + G_W] = jnp.exp(a - m_loc)
            dst[G_ROWS * d + G_BEND] = jnp.broadcast_to(b_end, b.shape)
            dst[G_ROWS * d + G_MLOC] = jnp.broadcast_to(m_loc, b.shape)
            dst[G_ROWS * d + G_RMAX] = _scan_lanes(r, jnp.maximum, -jnp.inf, reverse=(d == 1))

    gate_prep(gc_ref[0, :, 0], gsc)
    gate_prep(g_ref[0, :, 0], gs)

    cf[...] = jnp.zeros(cf.shape, F32)
    cb[...] = jnp.zeros(cb.shape, F32)

    row_i = lax.broadcasted_iota(jnp.int32, (L, L), 0)
    col_i = lax.broadcasted_iota(jnp.int32, (L, L), 1)
    visible = (col_i <= row_i, col_i >= row_i)

    def per_token(row):
        return jnp.broadcast_to(row, (L, L)).T

    def step(c, d, g_scr, kt_scr, v_scr, q_scr, c_scr, m_prev):
        off = c * L if isinstance(c, int) else pl.multiple_of(c * L, L)
        g0 = G_ROWS * d
        kt_c = kt_scr[c]
        v_c = v_scr[pl.ds(off, L), :]
        b_end = g_scr[g0 + G_BEND, pl.ds(c, 1), 0:1]
        m_loc = g_scr[g0 + G_MLOC, pl.ds(c, 1), 0:1]
        c_prev = c_scr[...]
        h = None
        if q_scr is not None:
            q_c = q_scr[c]
            r_row = g_scr[g0 + G_R, pl.ds(c, 1), :]
            u = jnp.maximum(per_token(g_scr[g0 + G_RMAX, pl.ds(c, 1), :]), m_prev)
            b_t = per_token(g_scr[g0 + G_B, pl.ds(c, 1), :])
            qk = jnp.dot(q_c, kt_c, preferred_element_type=F32)
            s = (qk * jnp.exp(jnp.where(visible[d], r_row - u, -jnp.inf))).astype(BF16)
            wi = jnp.exp(m_prev - u)
            inter = jnp.dot(q_c, c_prev.astype(BF16), preferred_element_type=F32)
            res = jnp.dot(s, v_c, preferred_element_type=F32) \
                + jnp.concatenate([wi] * (c_prev.shape[1] // L), axis=1) * inter
            inv = 1.0 / jnp.maximum(jnp.abs(res[:, DV:DV + L]), jnp.exp(-(b_t + u)))
            h = res[:, 0:DV] * jnp.concatenate([inv] * (DV // L), axis=1)
        kwt = (kt_c.astype(F32) * g_scr[g0 + G_W, pl.ds(c, 1), :]).astype(BF16)
        c_loc = jnp.dot(kwt, v_c, preferred_element_type=F32)
        m_new = jnp.maximum(b_end + m_prev, m_loc)
        sp = jnp.exp(b_end + m_prev - m_new)
        sl = jnp.exp(m_loc - m_new)
        c_scr[...] = sp * c_prev + sl * c_loc
        return m_new, h

    def finish(c, h):
        off = pl.multiple_of(c * L, L)
        h = h * lax.rsqrt(jnp.mean(h * h, axis=-1, keepdims=True) + EPS)
        y = (h * hg_ref[...]) * _sigmoid(o_ref[0, pl.ds(off, L), :].astype(F32))
        out_ref[0, pl.ds(off, L), :] = y.astype(BF16)

    m_f = jnp.zeros((1, 1), F32)
    m_b = jnp.zeros((1, 1), F32)
    for i in range(ncc):
        m_f, _ = step(i, 0, gsc, kctt, vcext, None, cf, m_f)
        m_b, _ = step(ncc - 1 - i, 1, gsc, kctt, vcext, None, cb, m_b)

    assert nc % 2 == 0

    def first_half(i, carry):
        c_f, c_b = i, nc - 1 - i
        m_f, h_f = step(c_f, 0, gs, ktt, vext, qt, cf, carry[0])
        m_b, h_b = step(c_b, 1, gs, ktt, vext, qt, cb, carry[1])
        hf[pl.ds(pl.multiple_of(c_f * L, L), L), :] = h_f
        hb[pl.ds(pl.multiple_of(c_b * L, L), L), :] = h_b
        return m_f, m_b

    def second_half(i, carry):
        c_f, c_b = i, nc - 1 - i
        m_f, h_f = step(c_f, 0, gs, ktt, vext, qt, cf, carry[0])
        m_b, h_b = step(c_b, 1, gs, ktt, vext, qt, cb, carry[1])
        finish(c_f, h_f + hb[pl.ds(pl.multiple_of(c_f * L, L), L), :])
        finish(c_b, h_b + hf[pl.ds(pl.multiple_of(c_b * L, L), L), :])
        return m_f, m_b

    carry = lax.fori_loop(0, nc // 2, first_half, (m_f, m_b), unroll=2)
    lax.fori_loop(nc // 2, nc, second_half, carry, unroll=2)


def _mlstm(p, pc, g4, gc4, wk, wq, head_g):
    b_, t, _ = p.shape
    tc = pc.shape[1]
    L = MLSTM_CHUNK
    nc, ncc = t // L, tc // L
    ext = DV + LANES
    return pl.pallas_call(
        _mlstm_kernel,
        grid=(b_, N_HEADS),
        in_specs=[
            pl.BlockSpec((1, tc, DQK), lambda b, h: (b, 0, h)),
            pl.BlockSpec((1, tc, DV), lambda b, h: (b, 0, 2 + h)),
            pl.BlockSpec((1, 4, 1, ncc, L), lambda b, h: (h, 0, b, 0, 0)),
            pl.BlockSpec((1, t, DQK), lambda b, h: (b, 0, h)),
            pl.BlockSpec((1, t, DV), lambda b, h: (b, 0, 2 + h)),
            pl.BlockSpec((1, t, DQK), lambda b, h: (b, 0, 12 + h)),
            pl.BlockSpec((1, t, DV), lambda b, h: (b, 0, 8 + h)),
            pl.BlockSpec((1, 4, 1, nc, L), lambda b, h: (h, 0, b, 0, 0)),
            pl.BlockSpec((SUBLANES, DQK), lambda b, h: (0, h)),
            pl.BlockSpec((SUBLANES, DQK), lambda b, h: (0, h)),
            pl.BlockSpec((1, DV), lambda b, h: (0, h)),
        ],
        out_specs=pl.BlockSpec((1, t, DV), lambda b, h: (b, 0, h)),
        out_shape=jax.ShapeDtypeStruct((b_, t, N_HEADS * DV), BF16),
        scratch_shapes=[
            pltpu.VMEM((t + 2 * SUBLANES, LANES), F32),
            pltpu.VMEM((nc, DQK, L), BF16),
            pltpu.VMEM((nc, L, DQK), BF16),
            pltpu.VMEM((ncc, DQK, L), BF16),
            pltpu.VMEM((t, ext), BF16),
            pltpu.VMEM((tc, ext), BF16),
            pltpu.VMEM((2 * G_ROWS, nc, L), F32),
            pltpu.VMEM((2 * G_ROWS, ncc, L), F32),
            pltpu.VMEM((DQK, ext), F32),
            pltpu.VMEM((DQK, ext), F32),
            pltpu.VMEM((t, DV), F32),
            pltpu.VMEM((t, DV), F32),
        ],
        compiler_params=_cparams(("arbitrary", "arbitrary")),
    )(pc, pc, gc4, p, p, p, p, g4, wk, wq, head_g)


def _pool_kernel(u_ref, pw_ref, ps_ref, out_ref, spad):
    t = u_ref.shape[1]
    rows = t // GRID_W
    blk = 256
    halo = (max(POOL_WINDOWS) // 2) * GRID_W
    spad[0:halo, :] = jnp.zeros((halo, POOL_GC), F32)
    spad[halo + t:halo + t + halo, :] = jnp.zeros((halo, POOL_GC), F32)
    ti = lax.broadcasted_iota(jnp.int32, (blk, blk), 0)
    tj = lax.broadcasted_iota(jnp.int32, (blk, blk), 1)
    same_row = (ti >> GRID_SHIFT) == (tj >> GRID_SHIFT)
    diff = tj - ti
    tok = lax.broadcasted_iota(jnp.int32, (blk, 1), 0)
    for gi, w in enumerate(POOL_WINDOWS):
        hw = w // 2
        c0, c1 = gi * POOL_GC, (gi + 1) * POOL_GC
        band = jnp.where(same_row & (diff >= -hw) & (diff <= hw - 1), 1.0, 0.0).astype(BF16)
        for r in range(t // blk):
            spad[halo + r * blk:halo + (r + 1) * blk, :] = jnp.dot(
                band, u_ref[0, r * blk:(r + 1) * blk, c0:c1], preferred_element_type=F32)
        for r in range(t // blk):
            base = halo + r * blk
            acc = spad[base - hw * GRID_W:base - hw * GRID_W + blk, :]
            for j in range(-hw + 1, hw):
                acc = acc + spad[base + j * GRID_W:base + j * GRID_W + blk, :]
            tt = tok + r * blk
            gr = tt >> GRID_SHIFT
            gc = tt & (GRID_W - 1)
            cr = jnp.minimum(gr + hw, rows) - jnp.maximum(gr - hw, 0)
            cc = jnp.minimum(gc + hw, GRID_W) - jnp.maximum(gc - hw, 0)
            cnt = (cr * cc).astype(F32)
            d = acc / cnt - u_ref[0, r * blk:(r + 1) * blk, c0:c1].astype(F32)
            y = jnp.dot(d.astype(BF16), pw_ref[gi], preferred_element_type=F32) * ps_ref[:, c0:c1]
            out_ref[0, r * blk:(r + 1) * blk, c0:c1] = y.astype(BF16)


def _pool(p, pool_w, pool_scale):
    b_, t, _ = p.shape
    pw = len(POOL_WINDOWS) * POOL_GC
    halo = (max(POOL_WINDOWS) // 2) * GRID_W
    return pl.pallas_call(
        _pool_kernel,
        grid=(b_,),
        in_specs=[pl.BlockSpec((1, t, pw), lambda b: (b, 0, 3)),
                  pl.BlockSpec((len(POOL_WINDOWS), POOL_GC, POOL_GC), lambda b: (0, 0, 0)),
                  pl.BlockSpec((1, pw), lambda b: (0, 0))],
        out_specs=pl.BlockSpec((1, t, pw), lambda b: (b, 0, 0)),
        out_shape=jax.ShapeDtypeStruct((b_, t, pw), BF16),
        scratch_shapes=[pltpu.VMEM((t + 2 * halo, POOL_GC), F32)],
        compiler_params=_cparams(("arbitrary",)),
    )(p, pool_w, pool_scale)


def _outproj_kernel(ym_ref, yp_ref, wm_ref, wp_ref, x_ref, mod_ref, n2_ref, wr_ref,
                    x1_ref, h2_ref, lg_ref):
    tm = x_ref.shape[0]
    half = tm // 2
    scale2 = n2_ref[...] * (1.0 + mod_ref[0, 4:5, :])
    for r0 in (0, half):
        rows = slice(r0, r0 + half)
        mix = (jnp.dot(ym_ref[rows, :], wm_ref[...], preferred_element_type=F32)
               + jnp.dot(yp_ref[rows, :], wp_ref[...], preferred_element_type=F32))
        x1 = x_ref[rows, :] + mod_ref[0, 2:3, :] * mix
        x1_ref[rows, :] = x1
        y = x1 * lax.rsqrt(jnp.mean(x1 * x1, axis=-1, keepdims=True) + EPS)
        h2 = y * scale2 + mod_ref[0, 3:4, :]
        g0, g1 = r0 // SUBLANES, (r0 + half) // SUBLANES
        for j in range(h2_ref.shape[1]):
            h2_ref[g0:g1, j, :, :] = h2[:, j * LANES:(j + 1) * LANES].reshape(g1 - g0, SUBLANES, LANES)
        lg_ref[rows, :] = jnp.dot(h2.astype(BF16), wr_ref[...], preferred_element_type=F32)


def _outproj(ym, yp, wm, wp, x2, mod, rows_per_mod, n2, wr, tm):
    m, d = x2.shape
    km = ym.shape[1]
    kp = yp.shape[1]
    tiles_per_mod = rows_per_mod // tm
    return pl.pallas_call(
        _outproj_kernel,
        grid=(m // tm,),
        in_specs=[pl.BlockSpec((tm, km), lambda i: (i, 0)),
                  pl.BlockSpec((tm, kp), lambda i: (i, 0)),
                  pl.BlockSpec((km, d), lambda i: (0, 0), pipeline_mode=pl.Buffered(1)),
                  pl.BlockSpec((kp, d), lambda i: (0, 0), pipeline_mode=pl.Buffered(1)),
                  pl.BlockSpec((tm, d), lambda i: (i, 0)),
                  pl.BlockSpec((1, N_MOD, d), lambda i: (i // tiles_per_mod, 0, 0)),
                  pl.BlockSpec((1, d), lambda i: (0, 0)),
                  pl.BlockSpec((d, LANES), lambda i: (0, 0))],
        out_specs=[pl.BlockSpec((tm, d), lambda i: (i, 0)),
                   pl.BlockSpec((tm // SUBLANES, d // LANES, SUBLANES, LANES), lambda i: (i, 0, 0, 0)),
                   pl.BlockSpec((tm, LANES), lambda i: (i, 0))],
        out_shape=[jax.ShapeDtypeStruct((m, d), F32),
                   jax.ShapeDtypeStruct((m // SUBLANES, d // LANES, SUBLANES, LANES), F32),
                   jax.ShapeDtypeStruct((m, LANES), F32)],
        compiler_params=_cparams(("arbitrary",)),
    )(ym, yp, wm, wp, x2, mod, n2, wr)


def _scan_lanes_i32(x, exclusive_of=None):
    n = x.shape[1]
    lane = lax.broadcasted_iota(jnp.int32, x.shape, 1)
    k = 1
    while k < n:
        x = x + jnp.where(lane >= k, pltpu.roll(x, k, 1), 0)
        k *= 2
    return x


COMBINE_TILE = 256
GRANULE = 2 * SUBLANES


def _route_kernel(lg_ref, idx_ref, gate_ref, tokrep_ref, bounds_ref, afft):
    t = lg_ref.shape[1]
    cap = idx_ref.shape[2]
    lane = lax.broadcasted_iota(jnp.int32, (t, LANES), 1)
    lg = jnp.where(lane < N_EXPERTS, lg_ref[0], -jnp.inf)
    ex = jnp.exp(lg - jnp.max(lg, axis=1, keepdims=True))
    aff = ex / jnp.sum(ex, axis=1, keepdims=True)
    for r in range(t // LANES):
        afft[:, r * LANES:(r + 1) * LANES] = aff[r * LANES:(r + 1) * LANES, :].T
    aff_t = afft[0:N_EXPERTS, :]

    def count(mask):
        return jnp.sum(jnp.where(mask, 1.0, 0.0), axis=1, keepdims=True).astype(jnp.int32)

    def bit_step(i, thr_bits):
        cand = thr_bits | (jnp.int32(1) << (30 - i))
        cand_f = lax.bitcast_convert_type(cand, F32)
        return jnp.where(count(aff_t >= cand_f) >= cap, cand, thr_bits)

    thr_bits = lax.fori_loop(0, 31, bit_step, jnp.zeros((N_EXPERTS, 1), jnp.int32))
    thr = lax.bitcast_convert_type(thr_bits, F32)
    gt = aff_t > thr
    eq = aff_t == thr
    need = cap - count(gt)
    eq_i = jnp.where(eq, 1, 0)
    eq_rank = _scan_lanes_i32(eq_i) - eq_i
    sel = gt | (eq & (eq_rank < need))
    sel_i = jnp.where(sel, 1, 0)
    pos = _scan_lanes_i32(sel_i) - sel_i
    key = jnp.where(sel, pos, -1)

    a_hi = aff.astype(BF16).astype(F32)
    a_mid = (aff - a_hi).astype(BF16).astype(F32)
    a_lo = (aff - a_hi - a_mid).astype(BF16).astype(F32)
    tok = lax.broadcasted_iota(jnp.int32, (t, LANES), 0)
    vals = (a_hi + pltpu.roll(a_mid, N_EXPERTS, 1) + pltpu.roll(a_lo, 2 * N_EXPERTS, 1)
            + jnp.where(lane == 3 * N_EXPERTS, (tok >> GRID_SHIFT).astype(F32), 0.0)
            + jnp.where(lane == 3 * N_EXPERTS + 1, (tok & (GRID_W - 1)).astype(F32), 0.0)).astype(BF16)
    slot = lax.broadcasted_iota(jnp.int32, (cap, 1), 0)
    for e in range(N_EXPERTS):
        onehot = jnp.where(slot == key[e:e + 1, :], 1.0, 0.0).astype(BF16)
        res = jnp.dot(onehot, vals, preferred_element_type=F32)
        rt = jnp.concatenate([res[r * LANES:(r + 1) * LANES, :].T for r in range(cap // LANES)], axis=1)
        gate_ref[0, e:e + 1, :] = (rt[e:e + 1, :] + rt[N_EXPERTS + e:N_EXPERTS + e + 1, :]) \
            + rt[2 * N_EXPERTS + e:2 * N_EXPERTS + e + 1, :]
        tokf = rt[3 * N_EXPERTS:3 * N_EXPERTS + 1, :] * float(GRID_W) + rt[3 * N_EXPERTS + 1:3 * N_EXPERTS + 2, :]
        idx_ref[0, e:e + 1, :] = tokf.astype(jnp.int32)
        tok_col = res[:, 3 * N_EXPERTS:3 * N_EXPERTS + 1] * float(GRID_W) \
            + res[:, 3 * N_EXPERTS + 1:3 * N_EXPERTS + 2]
        tokrep_ref[0, e * cap:(e + 1) * cap, :] = jnp.broadcast_to(tok_col, (cap, LANES)).astype(jnp.int32)

    n_tiles = t // COMBINE_TILE
    lane_b = lax.broadcasted_iota(jnp.int32, (N_EXPERTS, LANES), 1)
    bounds = jnp.where(lane_b >= n_tiles, cap, 0)
    for i in range(n_tiles):
        bounds = jnp.where(lane_b == i, pos[:, i * COMBINE_TILE:i * COMBINE_TILE + 1], bounds)
    bounds_ref[0] = bounds


def _route(lg3, cap):
    b_, t, _ = lg3.shape
    return pl.pallas_call(
        _route_kernel,
        grid=(b_,),
        in_specs=[pl.BlockSpec((1, t, LANES), lambda b: (b, 0, 0))],
        out_specs=[pl.BlockSpec((1, N_EXPERTS, cap), lambda b: (b, 0, 0)),
                   pl.BlockSpec((1, N_EXPERTS, cap), lambda b: (b, 0, 0)),
                   pl.BlockSpec((1, N_EXPERTS * cap, LANES), lambda b: (b, 0, 0)),
                   pl.BlockSpec((1, N_EXPERTS, LANES), lambda b: (b, 0, 0))],
        out_shape=[jax.ShapeDtypeStruct((b_, N_EXPERTS, cap), jnp.int32),
                   jax.ShapeDtypeStruct((b_, N_EXPERTS, cap), F32),
                   jax.ShapeDtypeStruct((b_, N_EXPERTS * cap, LANES), jnp.int32),
                   jax.ShapeDtypeStruct((b_, N_EXPERTS, LANES), jnp.int32)],
        scratch_shapes=[pltpu.VMEM((LANES, t), F32)],
        compiler_params=_cparams(("arbitrary",)),
    )(lg3)


def _ffn_kernel(idx_ref, gate_ref, h2_hbm, wg_ref, wu_ref, wd_ref, y_ref,
                xs_buf, wg_s, wu_s, wd_s, sem):
    p = pl.program_id(0)
    b = pl.program_id(1)
    n_e = pl.num_programs(0) - 1
    n_b = pl.num_programs(1)
    k_chunks, cap = xs_buf.shape[1], xs_buf.shape[2]
    groups = h2_hbm.shape[0] // n_b
    step = p * n_b + b
    cur = step % 2
    nxt = 1 - cur
    last_b = b == n_b - 1
    b_next = jnp.where(last_b, 0, b + 1)
    p_next = jnp.where(last_b, p + 1, p)

    @pl.when((p_next >= 1) & (p_next <= n_e))
    def _():
        def issue(r, carry):
            tok = idx_ref[0, 0, r]
            pltpu.make_async_copy(h2_hbm.at[b_next * groups + (tok >> 3), :, tok & (SUBLANES - 1), :],
                                  xs_buf.at[nxt, :, r, :], sem.at[nxt]).start()
            return carry

        lax.fori_loop(0, cap, issue, 0, unroll=True)

    @pl.when(p < n_e)
    def _():
        ws = p % 2
        rows_in = wg_ref.shape[1]
        rows_mid = wd_ref.shape[1]
        wg_s[ws, pl.ds(pl.multiple_of(b * rows_in, rows_in), rows_in), :] = wg_ref[0].astype(BF16)
        wu_s[ws, pl.ds(pl.multiple_of(b * rows_in, rows_in), rows_in), :] = wu_ref[0].astype(BF16)
        wd_s[ws, pl.ds(pl.multiple_of(b * rows_mid, rows_mid), rows_mid), :] = wd_ref[0].astype(BF16)

    @pl.when(p == 0)
    def _():
        y_ref[0, 0] = jnp.zeros(y_ref.shape[2:], BF16)

    @pl.when(p >= 1)
    def _():
        pltpu.make_async_copy(xs_buf.at[cur], xs_buf.at[cur], sem.at[cur]).wait()
        ws = (p - 1) % 2
        xs = jnp.concatenate([xs_buf[cur, j].astype(BF16) for j in range(k_chunks)], axis=1)
        a = _silu(jnp.dot(xs, wg_s[ws], preferred_element_type=F32)) \
            * jnp.dot(xs, wu_s[ws], preferred_element_type=F32)
        y = jnp.dot(a.astype(BF16), wd_s[ws], preferred_element_type=F32)
        y_ref[0, 0] = (y * gate_ref[0, 0]).astype(BF16)


def _ffn(idx, gate, h2, wg, wu, wd):
    e_, d, f = wg.shape
    b_, _, cap = idx.shape
    assert h2.shape[1] * h2.shape[3] == d and h2.shape[2] == SUBLANES and h2.shape[0] % b_ == 0
    assert d % b_ == 0 and f % b_ == 0
    rows_in, rows_mid = d // b_, f // b_
    assert rows_in % (2 * SUBLANES) == 0 and rows_mid % (2 * SUBLANES) == 0

    def next_tokens(p, b):
        last_b = b == b_ - 1
        b_next = jnp.where(last_b, 0, b + 1)
        e_next = jnp.clip(jnp.where(last_b, p, p - 1), 0, e_ - 1)
        return (b_next * e_ + e_next, 0, 0)

    def this_expert(p, b):
        return (b, jnp.maximum(p - 1, 0), 0, 0)

    def weight_piece(p, b):
        return (jnp.minimum(p, e_ - 1), b, 0)

    return pl.pallas_call(
        _ffn_kernel,
        grid=(e_ + 1, b_),
        in_specs=[pl.BlockSpec((1, 1, cap), next_tokens, memory_space=pltpu.SMEM),
                  pl.BlockSpec((1, 1, cap, 1), this_expert),
                  pl.BlockSpec(memory_space=pl.ANY),
                  pl.BlockSpec((1, rows_in, f), weight_piece),
                  pl.BlockSpec((1, rows_in, f), weight_piece),
                  pl.BlockSpec((1, rows_mid, d), weight_piece)],
        out_specs=pl.BlockSpec((1, 1, cap, d), lambda p, b: (b, (p + e_) % (e_ + 1), 0, 0)),
        out_shape=jax.ShapeDtypeStruct((b_, e_ + 1, cap, d), BF16),
        scratch_shapes=[pltpu.VMEM((2, d // LANES, cap, LANES), F32),
                        pltpu.VMEM((2, d, f), BF16),
                        pltpu.VMEM((2, d, f), BF16),
                        pltpu.VMEM((2, f, d), BF16),
                        pltpu.SemaphoreType.DMA((2,))],
        compiler_params=_cparams(("arbitrary", "arbitrary")),
    )(idx.reshape(b_ * e_, 1, cap), gate.reshape(b_, e_, cap, 1), h2, wg, wu, wd)


COMBINE_CHUNK = 256


def _combine_kernel(bounds_ref, y_ref, tok_ref, x1_ref, mod_ref, nf_ref, o_ref, stage_y, stage_t):
    b = pl.program_id(0)
    i = pl.program_id(1)
    n_e, cap = y_ref.shape[1], y_ref.shape[2]
    tile = o_ref.shape[1]
    n_bounds = pl.num_programs(1) + 1
    per_chunk = COMBINE_CHUNK // GRANULE
    g_shift = GRANULE.bit_length() - 1
    assert per_chunk & (per_chunk - 1) == 0 and GRANULE == 1 << g_shift

    @pl.when((b == 0) & (i == 0))
    def _():
        stage_y[...] = jnp.zeros(stage_y.shape, BF16)
        stage_t[...] = jnp.full(stage_t.shape, -1, jnp.int32)

    o_ref[...] = jnp.zeros(o_ref.shape, F32)
    tok_lane = lax.broadcasted_iota(jnp.int32, (COMBINE_CHUNK, LANES), 1) + i * tile

    def flush():
        onehot_t = jnp.concatenate(
            [jnp.where(stage_t[...] == tok_lane + j * LANES, 1.0, 0.0).astype(BF16)
             for j in range(tile // LANES)], axis=1)
        o_ref[0] += lax.dot_general(onehot_t, stage_y[...], (((0,), (0,)), ((), ())),
                                    preferred_element_type=F32)
        stage_t[...] = jnp.full(stage_t.shape, -1, jnp.int32)

    def per_expert(e, k):
        base = (b * n_e + e) * n_bounds + i
        lo = bounds_ref[base]
        hi = bounds_ref[base + 1]
        g_lo = lax.shift_right_logical(lo, g_shift)
        g_hi = jnp.where(hi > lo, lax.shift_right_logical(hi + GRANULE - 1, g_shift), g_lo)

        def per_granule(g, k):
            src = pl.multiple_of(g * GRANULE, GRANULE)
            dst = pl.multiple_of((k & (per_chunk - 1)) * GRANULE, GRANULE)
            stage_y[pl.ds(dst, GRANULE), :] = y_ref[0, e, pl.ds(src, GRANULE), :]
            stage_t[pl.ds(dst, GRANULE), :] = tok_ref[0, pl.ds(pl.multiple_of(e * cap + src, GRANULE), GRANULE), :]
            k = k + 1

            @pl.when((k & (per_chunk - 1)) == 0)
            def _():
                flush()

            return k

        return lax.fori_loop(g_lo, g_hi, per_granule, k)

    k = lax.fori_loop(0, n_e, per_expert, jnp.int32(0))

    @pl.when((k & (per_chunk - 1)) != 0)
    def _():
        flush()

    x = x1_ref[0] + mod_ref[0, 5:6, :] * o_ref[0]
    o_ref[0] = (x * lax.rsqrt(jnp.mean(x * x, axis=-1, keepdims=True) + EPS)) * nf_ref[...]


def _combine(y, tokrep, bounds, x1, mod, nf):
    b_, e_, cap, d = y.shape
    e_ -= 1
    t = x1.shape[1]
    tile = COMBINE_TILE
    assert cap % GRANULE == 0 and tile % LANES == 0 and bounds.shape == (b_, e_, t // tile + 1)
    grid_spec = pltpu.PrefetchScalarGridSpec(
        num_scalar_prefetch=1,
        grid=(b_, t // tile),
        in_specs=[pl.BlockSpec((1, e_, cap, d), lambda b, i, bnd: (b, 0, 0, 0)),
                  pl.BlockSpec((1, e_ * cap, LANES), lambda b, i, bnd: (b, 0, 0)),
                  pl.BlockSpec((1, tile, d), lambda b, i, bnd: (b, i, 0)),
                  pl.BlockSpec((1, N_MOD, d), lambda b, i, bnd: (b, 0, 0)),
                  pl.BlockSpec((1, d), lambda b, i, bnd: (0, 0))],
        out_specs=pl.BlockSpec((1, tile, d), lambda b, i, bnd: (b, i, 0)),
        scratch_shapes=[pltpu.VMEM((COMBINE_CHUNK, d), BF16),
                        pltpu.VMEM((COMBINE_CHUNK, LANES), jnp.int32)])
    return pl.pallas_call(
        _combine_kernel,
        grid_spec=grid_spec,
        out_shape=jax.ShapeDtypeStruct((b_, t, d), F32),
        compiler_params=_cparams(("arbitrary", "arbitrary")),
    )(bounds.reshape(-1), y, tokrep, x1, mod, nf)


def _gate_layout(gt, b_, t):
    return gt.reshape(N_HEADS, 4, b_, t // MLSTM_CHUNK, MLSTM_CHUNK)


def kernel(x, c, ctx, c_ctx, w_mod, b_mod, norm1, w_in, conv_q_w, conv_k_w, b_gates, head_g, pool_w,
           pool_scale, w_out, norm2, w_router, w_gate, w_up, w_down, norm_f):
    b_, t, d = x.shape
    tc = ctx.shape[1]
    depth = w_mod.shape[0]
    assert depth == 1
    l = 0
    qk_w = N_HEADS * DQK
    mw = N_HEADS * DV
    pool_wd = len(POOL_WINDOWS) * POOL_GC
    n_gate = 2 * 2 * N_HEADS
    k_off, v_off = 0, qk_w
    g_off = v_off + mw
    q_off = g_off + n_gate
    o_off = q_off + qk_w
    p_off = o_off + mw
    cap = EC_FACTOR * t // N_EXPERTS

    mod_rows = -(-(b_ + 1) // SUBLANES) * SUBLANES
    cs = jnp.zeros((mod_rows, d), F32).at[:b_].set(c).at[b_].set(c_ctx)
    mod_all = _modulation(cs, w_mod[l], b_mod[l])
    mod = mod_all[:b_].reshape(b_, N_MOD, d)
    mod_c = mod_all[b_:b_ + 1].reshape(1, N_MOD, d)

    wl = w_in[l]
    w_main = jnp.concatenate([wl[:, k_off:v_off], wl[:, v_off:g_off], wl[:, q_off:o_off],
                              wl[:, o_off:p_off], wl[:, p_off:]], axis=1).astype(BF16)
    perm = jnp.arange(n_gate).reshape(2, 2, N_HEADS).transpose(2, 0, 1).reshape(-1)
    w_g = wl[:, g_off:q_off][:, perm].T.astype(BF16)
    b_g = b_gates[l][perm].reshape(n_gate, 1)
    n1 = norm1[l].reshape(1, d)

    p, g = _inproj(x.reshape(b_ * t, d), mod, t, n1, w_main, w_g, b_g, 512)
    pc, gc = _inproj(ctx.reshape(b_ * tc, d), mod_c, b_ * tc, n1, w_main[:, :qk_w + mw], w_g, b_g,
                     min(512, b_ * tc))
    p = p.reshape(b_, t, -1)
    pc = pc.reshape(b_, tc, -1)

    wk = jnp.pad(conv_k_w[l], ((0, SUBLANES - QK_CONV), (0, 0)))
    wq = jnp.pad(conv_q_w[l], ((0, SUBLANES - QK_CONV), (0, 0)))
    ym = _mlstm(p, pc, _gate_layout(g, b_, t), _gate_layout(gc, b_, tc), wk, wq, head_g[l].reshape(1, mw))
    yp = _pool(p, pool_w[l].astype(BF16), pool_scale[l].reshape(1, pool_wd))

    wo = w_out[l].astype(BF16)
    wr = jnp.pad(w_router[l], ((0, 0), (0, LANES - N_EXPERTS))).astype(BF16)
    x1, h2, lg = _outproj(ym.reshape(b_ * t, mw), yp.reshape(b_ * t, pool_wd), wo[:mw], wo[mw:],
                          x.reshape(b_ * t, d), mod, t, norm2[l].reshape(1, d), wr, 512)

    idx, gate, tokrep, bounds = _route(lg.reshape(b_, t, LANES), cap)

    y = _ffn(idx, gate, h2, w_gate[l], w_up[l], w_down[l])
    return _combine(y, tokrep, bounds[:, :, :t // COMBINE_TILE + 1], x1.reshape(b_, t, d), mod,
                    norm_f.reshape(1, d))
```

```python
import functools

import jax
import jax.numpy as jnp
from jax import lax
from jax.experimental import pallas as pl
from jax.experimental.pallas import tpu as pltpu

F32 = jnp.float32
BF16 = jnp.bfloat16

N_HEADS = 4
DQK = 128
DV = 256
QK_CONV = 5
POOL_WINDOWS = (2, 4, 8, 16)
POOL_GC = 256
GRID_W = 64
GRID_SHIFT = GRID_W.bit_length() - 1
N_EXPERTS = 16
EC_FACTOR = 2
N_MOD = 6
EPS = 1e-6

LANES = 128
SUBLANES = 8
MLSTM_CHUNK = 128
VMEM_LIMIT = 56 * 1024 * 1024


def _cparams(sem):
    return pltpu.CompilerParams(dimension_semantics=sem, vmem_limit_bytes=VMEM_LIMIT)


def _sigmoid(x):
    return 1.0 / (1.0 + jnp.exp(-x))


def _silu(x):
    return x * _sigmoid(x)


def _log_sigmoid(x):
    return jnp.minimum(x, 0.0) - jnp.log(1.0 + jnp.exp(-jnp.abs(x)))


def _mod_kernel(c_ref, w_ref, b_ref, o_ref):
    s = _silu(c_ref[...]).astype(BF16)
    o_ref[...] = jnp.dot(s, w_ref[...].astype(BF16), preferred_element_type=F32) + b_ref[...]


def _modulation(cs, w_mod, b_mod):
    rows, d = cs.shape
    n = w_mod.shape[1]
    tn = 1024
    return pl.pallas_call(
        _mod_kernel,
        grid=(n // tn,),
        in_specs=[pl.BlockSpec((rows, d), lambda j: (0, 0)),
                  pl.BlockSpec((d, tn), lambda j: (0, j)),
                  pl.BlockSpec((1, tn), lambda j: (0, j))],
        out_specs=pl.BlockSpec((rows, tn), lambda j: (0, j)),
        out_shape=jax.ShapeDtypeStruct((rows, n), F32),
        compiler_params=_cparams(("arbitrary",)),
    )(cs, w_mod, b_mod.reshape(1, n))


INPROJ_COLS = 512


def _inproj_kernel(x_ref, mod_ref, n1_ref, w_ref, wg_ref, bg_ref, p_ref, gt_ref):
    tm = x_ref.shape[0]
    half = tm // 2
    n_cols = p_ref.shape[1]
    scale = n1_ref[...] * (1.0 + mod_ref[0, 1:2, :])
    for r0 in (0, half):
        rows = slice(r0, r0 + half)
        x = x_ref[rows, :]
        y = x * lax.rsqrt(jnp.mean(x * x, axis=-1, keepdims=True) + EPS)
        hb = (y * scale + mod_ref[0, 0:1, :]).astype(BF16)
        gt_ref[:, rows] = lax.dot_general(wg_ref[...], hb, (((1,), (1,)), ((), ())),
                                          preferred_element_type=F32) + bg_ref[...]
        for c0 in range(0, n_cols, INPROJ_COLS):
            p_ref[rows, c0:c0 + INPROJ_COLS] = jnp.dot(
                hb, w_ref[:, c0:c0 + INPROJ_COLS], preferred_element_type=F32).astype(BF16)


def _inproj(x2, mod, rows_per_mod, n1, w, wg, bg, tm):
    m, d = x2.shape
    n_cols = w.shape[1]
    n_g = wg.shape[0]
    tiles_per_mod = rows_per_mod // tm
    assert n_cols % INPROJ_COLS == 0 and tm % (2 * LANES) == 0
    return pl.pallas_call(
        _inproj_kernel,
        grid=(m // tm,),
        in_specs=[pl.BlockSpec((tm, d), lambda i: (i, 0)),
                  pl.BlockSpec((1, N_MOD, d), lambda i: (i // tiles_per_mod, 0, 0)),
                  pl.BlockSpec((1, d), lambda i: (0, 0)),
                  pl.BlockSpec((d, n_cols), lambda i: (0, 0), pipeline_mode=pl.Buffered(1)),
                  pl.BlockSpec((n_g, d), lambda i: (0, 0)),
                  pl.BlockSpec((n_g, 1), lambda i: (0, 0))],
        out_specs=[pl.BlockSpec((tm, n_cols), lambda i: (i, 0)),
                   pl.BlockSpec((n_g, tm), lambda i: (0, i))],
        out_shape=[jax.ShapeDtypeStruct((m, n_cols), BF16),
                   jax.ShapeDtypeStruct((n_g, m), F32)],
        compiler_params=_cparams(("arbitrary",)),
    )(x2, mod, n1, w, wg, bg)


def _scan_lanes(x, op, fill, reverse):
    lane = lax.broadcasted_iota(jnp.int32, x.shape, 1)
    k = 1
    while k < LANES:
        if reverse:
            x = op(x, jnp.where(lane < LANES - k, pltpu.roll(x, LANES - k, 1), fill))
        else:
            x = op(x, jnp.where(lane >= k, pltpu.roll(x, k, 1), fill))
        k *= 2
    return x


G_B, G_R, G_W, G_BEND, G_MLOC, G_RMAX, G_ROWS = 0, 1, 2, 3, 4, 5, 6


def _mlstm_kernel(kc_ref, vc_ref, gc_ref, k_ref, v_ref, q_ref, o_ref, g_ref, wk_ref, wq_ref, hg_ref,
                  out_ref,
                  cpad, ktt, qt, kctt, vext, vcext, gs, gsc, cf, cb, hf, hb):
    L = MLSTM_CHUNK
    t_lat = k_ref.shape[1]
    t_ctx = kc_ref.shape[1]
    nc = t_lat // L
    ncc = t_ctx // L
    hps = k_ref.shape[2] // DQK
    pad = SUBLANES
    half = QK_CONV // 2

    def conv_silu(src, w_ref, dst, t, scale, transposed):
        cpad[0:pad, :] = jnp.zeros((pad, LANES), F32)
        cpad[pad:pad + t, :] = src.astype(F32)
        cpad[pad + t:pad + t + pad, :] = jnp.zeros((pad, LANES), F32)
        for r in range(t // L):
            base = pad + r * L - half
            acc = w_ref[0:1, :] * cpad[base:base + L, :]
            for j in range(1, QK_CONV):
                acc = acc + w_ref[j:j + 1, :] * cpad[base + j:base + j + L, :]
            y = _silu(acc)
            if scale is not None:
                y = y * scale
            dst[r] = (y.T if transposed else y).astype(BF16)

    def fill_vext(dst, src, t):
        dst[:, 0:DV] = src
        dst[:, DV:DV + LANES] = jnp.ones((t, LANES), BF16)

    for hh in range(hps):
        qk_cols = slice(hh * DQK, (hh + 1) * DQK)
        v_cols = slice(hh * DV, (hh + 1) * DV)
        conv_silu(kc_ref[0, :, qk_cols], wk_ref.at[:, qk_cols], kctt.at[hh], t_ctx, DQK ** -0.5, True)
        conv_silu(k_ref[0, :, qk_cols], wk_ref.at[:, qk_cols], ktt.at[hh], t_lat, DQK ** -0.5, True)
        conv_silu(q_ref[0, :, qk_cols], wq_ref.at[:, qk_cols], qt.at[hh], t_lat, None, False)
        fill_vext(vcext.at[hh], vc_ref[0, :, v_cols], t_ctx)
        fill_vext(vext.at[hh], v_ref[0, :, v_cols], t_lat)

    def gate_prep(garr, dst):
        for d in range(2):
            li = garr[2 * d]
            lf = _log_sigmoid(garr[2 * d + 1])
            b = _scan_lanes(lf, jnp.add, 0.0, reverse=(d == 1))
            b_end = b[:, LANES - 1:LANES] if d == 0 else b[:, 0:1]
            r = li - b
            a = b_end + r
            m_loc = jnp.max(a, axis=1, keepdims=True)
            dst[G_ROWS * d + G_B] = b
            dst[G_ROWS * d + G_R] = r
            dst[G_ROWS * d + G_W] = jnp.exp(a - m_loc)
            dst[G_ROWS * d + G_BEND] = jnp.broadcast_to(b_end, b.shape)
            dst[G_ROWS * d + G_MLOC] = jnp.broadcast_to(m_loc, b.shape)
            dst[G_ROWS * d + G_RMAX] = _scan_lanes(r, jnp.maximum, -jnp.inf, reverse=(d == 1))

    for hh in range(hps):
        gate_prep(gc_ref[hh, :, 0], gsc.at[hh])
        gate_prep(g_ref[hh, :, 0], gs.at[hh])

    cf[...] = jnp.zeros(cf.shape, F32)
    cb[...] = jnp.zeros(cb.shape, F32)

    row_i = lax.broadcasted_iota(jnp.int32, (L, L), 0)
    col_i = lax.broadcasted_iota(jnp.int32, (L, L), 1)
    visible = (col_i <= row_i, col_i >= row_i)

    def per_token(row):
        return jnp.broadcast_to(row, (L, L)).T

    def step(hh, c, d, g_all, kt_all, v_all, q_all, c_all, m_prev):
        off = c * L if isinstance(c, int) else pl.multiple_of(c * L, L)
        g_scr, kt_scr, v_scr, c_scr = g_all.at[hh], kt_all.at[hh], v_all.at[hh], c_all.at[hh]
        q_scr = None if q_all is None else q_all.at[hh]
        g0 = G_ROWS * d
        kt_c = kt_scr[c]
        v_c = v_scr[pl.ds(off, L), :]
        b_end = g_scr[g0 + G_BEND, pl.ds(c, 1), 0:1]
        m_loc = g_scr[g0 + G_MLOC, pl.ds(c, 1), 0:1]
        c_prev = c_scr[...]
        h = None
        if q_scr is not None:
            q_c = q_scr[c]
            r_row = g_scr[g0 + G_R, pl.ds(c, 1), :]
            u = jnp.maximum(per_token(g_scr[g0 + G_RMAX, pl.ds(c, 1), :]), m_prev)
            b_t = per_token(g_scr[g0 + G_B, pl.ds(c, 1), :])
            qk = jnp.dot(q_c, kt_c, preferred_element_type=F32)
            s = (qk * jnp.exp(jnp.where(visible[d], r_row - u, -jnp.inf))).astype(BF16)
            wi = jnp.exp(m_prev - u)
            inter = jnp.dot(q_c, c_prev.astype(BF16), preferred_element_type=F32)
            res = jnp.dot(s, v_c, preferred_element_type=F32) \
                + jnp.concatenate([wi] * (c_prev.shape[1] // L), axis=1) * inter
            inv = 1.0 / jnp.maximum(jnp.abs(res[:, DV:DV + L]), jnp.exp(-(b_t + u)))
            h = res[:, 0:DV] * jnp.concatenate([inv] * (DV // L), axis=1)
        kwt = (kt_c.astype(F32) * g_scr[g0 + G_W, pl.ds(c, 1), :]).astype(BF16)
        c_loc = jnp.dot(kwt, v_c, preferred_element_type=F32)
        m_new = jnp.maximum(b_end + m_prev, m_loc)
        sp = jnp.exp(b_end + m_prev - m_new)
        sl = jnp.exp(m_loc - m_new)
        c_scr[...] = sp * c_prev + sl * c_loc
        return m_new, h

    def finish(hh, c, h):
        off = pl.multiple_of(c * L, L)
        cols = slice(hh * DV, (hh + 1) * DV)
        h = h * lax.rsqrt(jnp.mean(h * h, axis=-1, keepdims=True) + EPS)
        y = (h * hg_ref[:, cols]) * _sigmoid(o_ref[0, pl.ds(off, L), cols].astype(F32))
        out_ref[0, pl.ds(off, L), cols] = y.astype(BF16)

    ms = [jnp.zeros((1, 1), F32)] * (2 * hps)
    for i in range(ncc):
        for hh in range(hps):
            ms[2 * hh], _ = step(hh, i, 0, gsc, kctt, vcext, None, cf, ms[2 * hh])
            ms[2 * hh + 1], _ = step(hh, ncc - 1 - i, 1, gsc, kctt, vcext, None, cb, ms[2 * hh + 1])

    assert nc % 2 == 0

    def scan_step(i, carry, second_half):
        c_f, c_b = i, nc - 1 - i
        off_f, off_b = pl.multiple_of(c_f * L, L), pl.multiple_of(c_b * L, L)
        out = []
        for hh in range(hps):
            m_f, h_f = step(hh, c_f, 0, gs, ktt, vext, qt, cf, carry[2 * hh])
            m_b, h_b = step(hh, c_b, 1, gs, ktt, vext, qt, cb, carry[2 * hh + 1])
            if second_half:
                finish(hh, c_f, h_f + hb[hh, pl.ds(off_f, L), :])
                finish(hh, c_b, h_b + hf[hh, pl.ds(off_b, L), :])
            else:
                hf[hh, pl.ds(off_f, L), :] = h_f
                hb[hh, pl.ds(off_b, L), :] = h_b
            out += [m_f, m_b]
        return tuple(out)

    carry = lax.fori_loop(0, nc // 2, functools.partial(scan_step, second_half=False), tuple(ms), unroll=2)
    lax.fori_loop(nc // 2, nc, functools.partial(scan_step, second_half=True), carry, unroll=2)


MLSTM_HEADS_PER_STEP = 2


def _mlstm(p, pc, g4, gc4, wk, wq, head_g):
    b_, t, _ = p.shape
    tc = pc.shape[1]
    L = MLSTM_CHUNK
    nc, ncc = t // L, tc // L
    ext = DV + LANES
    hps = MLSTM_HEADS_PER_STEP
    assert N_HEADS % hps == 0
    v_blk = N_HEADS * DQK // (hps * DV)
    q_blk = (N_HEADS * DQK + N_HEADS * DV) // (hps * DQK)
    o_blk = (2 * N_HEADS * DQK + N_HEADS * DV) // (hps * DV)
    return pl.pallas_call(
        _mlstm_kernel,
        grid=(b_, N_HEADS // hps),
        in_specs=[
            pl.BlockSpec((1, tc, hps * DQK), lambda b, h: (b, 0, h)),
            pl.BlockSpec((1, tc, hps * DV), lambda b, h: (b, 0, v_blk + h)),
            pl.BlockSpec((hps, 4, 1, ncc, L), lambda b, h: (h, 0, b, 0, 0)),
            pl.BlockSpec((1, t, hps * DQK), lambda b, h: (b, 0, h)),
            pl.BlockSpec((1, t, hps * DV), lambda b, h: (b, 0, v_blk + h)),
            pl.BlockSpec((1, t, hps * DQK), lambda b, h: (b, 0, q_blk + h)),
            pl.BlockSpec((1, t, hps * DV), lambda b, h: (b, 0, o_blk + h)),
            pl.BlockSpec((hps, 4, 1, nc, L), lambda b, h: (h, 0, b, 0, 0)),
            pl.BlockSpec((SUBLANES, hps * DQK), lambda b, h: (0, h)),
            pl.BlockSpec((SUBLANES, hps * DQK), lambda b, h: (0, h)),
            pl.BlockSpec((1, hps * DV), lambda b, h: (0, h)),
        ],
        out_specs=pl.BlockSpec((1, t, hps * DV), lambda b, h: (b, 0, h)),
        out_shape=jax.ShapeDtypeStruct((b_, t, N_HEADS * DV), BF16),
        scratch_shapes=[
            pltpu.VMEM((t + 2 * SUBLANES, LANES), F32),
            pltpu.VMEM((hps, nc, DQK, L), BF16),
            pltpu.VMEM((hps, nc, L, DQK), BF16),
            pltpu.VMEM((hps, ncc, DQK, L), BF16),
            pltpu.VMEM((hps, t, ext), BF16),
            pltpu.VMEM((hps, tc, ext), BF16),
            pltpu.VMEM((hps, 2 * G_ROWS, nc, L), F32),
            pltpu.VMEM((hps, 2 * G_ROWS, ncc, L), F32),
            pltpu.VMEM((hps, DQK, ext), F32),
            pltpu.VMEM((hps, DQK, ext), F32),
            pltpu.VMEM((hps, t, DV), F32),
            pltpu.VMEM((hps, t, DV), F32),
        ],
        compiler_params=_cparams(("arbitrary", "arbitrary")),
    )(pc, pc, gc4, p, p, p, p, g4, wk, wq, head_g)


def _pool_kernel(u_ref, pw_ref, ps_ref, out_ref, spad):
    t = u_ref.shape[1]
    rows = t // GRID_W
    blk = 256
    halo = (max(POOL_WINDOWS) // 2) * GRID_W
    spad[0:halo, :] = jnp.zeros((halo, POOL_GC), F32)
    spad[halo + t:halo + t + halo, :] = jnp.zeros((halo, POOL_GC), F32)
    ti = lax.broadcasted_iota(jnp.int32, (blk, blk), 0)
    tj = lax.broadcasted_iota(jnp.int32, (blk, blk), 1)
    same_row = (ti >> GRID_SHIFT) == (tj >> GRID_SHIFT)
    diff = tj - ti
    tok = lax.broadcasted_iota(jnp.int32, (blk, 1), 0)
    for gi, w in enumerate(POOL_WINDOWS):
        hw = w // 2
        c0, c1 = gi * POOL_GC, (gi + 1) * POOL_GC
        band = jnp.where(same_row & (diff >= -hw) & (diff <= hw - 1), 1.0, 0.0).astype(BF16)
        for r in range(t // blk):
            spad[halo + r * blk:halo + (r + 1) * blk, :] = jnp.dot(
                band, u_ref[0, r * blk:(r + 1) * blk, c0:c1], preferred_element_type=F32)
        for r in range(t // blk):
            base = halo + r * blk
            acc = spad[base - hw * GRID_W:base - hw * GRID_W + blk, :]
            for j in range(-hw + 1, hw):
                acc = acc + spad[base + j * GRID_W:base + j * GRID_W + blk, :]
            tt = tok + r * blk
            gr = tt >> GRID_SHIFT
            gc = tt & (GRID_W - 1)
            cr = jnp.minimum(gr + hw, rows) - jnp.maximum(gr - hw, 0)
            cc = jnp.minimum(gc + hw, GRID_W) - jnp.maximum(gc - hw, 0)
            cnt = (cr * cc).astype(F32)
            d = acc / cnt - u_ref[0, r * blk:(r + 1) * blk, c0:c1].astype(F32)
            y = jnp.dot(d.astype(BF16), pw_ref[gi], preferred_element_type=F32) * ps_ref[:, c0:c1]
            out_ref[0, r * blk:(r + 1) * blk, c0:c1] = y.astype(BF16)


def _pool(p, pool_w, pool_scale):
    b_, t, _ = p.shape
    pw = len(POOL_WINDOWS) * POOL_GC
    halo = (max(POOL_WINDOWS) // 2) * GRID_W
    return pl.pallas_call(
        _pool_kernel,
        grid=(b_,),
        in_specs=[pl.BlockSpec((1, t, pw), lambda b: (b, 0, 3)),
                  pl.BlockSpec((len(POOL_WINDOWS), POOL_GC, POOL_GC), lambda b: (0, 0, 0)),
                  pl.BlockSpec((1, pw), lambda b: (0, 0))],
        out_specs=pl.BlockSpec((1, t, pw), lambda b: (b, 0, 0)),
        out_shape=jax.ShapeDtypeStruct((b_, t, pw), BF16),
        scratch_shapes=[pltpu.VMEM((t + 2 * halo, POOL_GC), F32)],
        compiler_params=_cparams(("arbitrary",)),
    )(p, pool_w, pool_scale)


def _outproj_kernel(ym_ref, yp_ref, wm_ref, wp_ref, x_ref, mod_ref, n2_ref, wr_ref,
                    x1_ref, h2_ref, lg_ref):
    tm = x_ref.shape[0]
    half = tm // 2
    scale2 = n2_ref[...] * (1.0 + mod_ref[0, 4:5, :])
    for r0 in (0, half):
        rows = slice(r0, r0 + half)
        mix = (jnp.dot(ym_ref[rows, :], wm_ref[...], preferred_element_type=F32)
               + jnp.dot(yp_ref[rows, :], wp_ref[...], preferred_element_type=F32))
        x1 = x_ref[rows, :] + mod_ref[0, 2:3, :] * mix
        x1_ref[rows, :] = x1
        y = x1 * lax.rsqrt(jnp.mean(x1 * x1, axis=-1, keepdims=True) + EPS)
        h2 = y * scale2 + mod_ref[0, 3:4, :]
        g0, g1 = r0 // SUBLANES, (r0 + half) // SUBLANES
        for j in range(h2_ref.shape[1]):
            h2_ref[g0:g1, j, :, :] = h2[:, j * LANES:(j + 1) * LANES].reshape(g1 - g0, SUBLANES, LANES)
        lg_ref[rows, :] = jnp.dot(h2.astype(BF16), wr_ref[...], preferred_element_type=F32)


def _outproj(ym, yp, wm, wp, x2, mod, rows_per_mod, n2, wr, tm):
    m, d = x2.shape
    km = ym.shape[1]
    kp = yp.shape[1]
    tiles_per_mod = rows_per_mod // tm
    return pl.pallas_call(
        _outproj_kernel,
        grid=(m // tm,),
        in_specs=[pl.BlockSpec((tm, km), lambda i: (i, 0)),
                  pl.BlockSpec((tm, kp), lambda i: (i, 0)),
                  pl.BlockSpec((km, d), lambda i: (0, 0), pipeline_mode=pl.Buffered(1)),
                  pl.BlockSpec((kp, d), lambda i: (0, 0), pipeline_mode=pl.Buffered(1)),
                  pl.BlockSpec((tm, d), lambda i: (i, 0)),
                  pl.BlockSpec((1, N_MOD, d), lambda i: (i // tiles_per_mod, 0, 0)),
                  pl.BlockSpec((1, d), lambda i: (0, 0)),
                  pl.BlockSpec((d, LANES), lambda i: (0, 0))],
        out_specs=[pl.BlockSpec((tm, d), lambda i: (i, 0)),
                   pl.BlockSpec((tm // SUBLANES, d // LANES, SUBLANES, LANES), lambda i: (i, 0, 0, 0)),
                   pl.BlockSpec((tm, LANES), lambda i: (i, 0))],
        out_shape=[jax.ShapeDtypeStruct((m, d), F32),
                   jax.ShapeDtypeStruct((m // SUBLANES, d // LANES, SUBLANES, LANES), F32),
                   jax.ShapeDtypeStruct((m, LANES), F32)],
        compiler_params=_cparams(("arbitrary",)),
    )(ym, yp, wm, wp, x2, mod, n2, wr)


def _scan_lanes_i32(x, exclusive_of=None):
    n = x.shape[1]
    lane = lax.broadcasted_iota(jnp.int32, x.shape, 1)
    k = 1
    while k < n:
        x = x + jnp.where(lane >= k, pltpu.roll(x, k, 1), 0)
        k *= 2
    return x


COMBINE_TILE = 256
GRANULE = 2 * SUBLANES


def _route_kernel(lg_ref, idx_ref, gate_ref, tokrep_ref, bounds_ref, afft):
    t = lg_ref.shape[1]
    cap = idx_ref.shape[2]
    lane = lax.broadcasted_iota(jnp.int32, (t, LANES), 1)
    lg = jnp.where(lane < N_EXPERTS, lg_ref[0], -jnp.inf)
    ex = jnp.exp(lg - jnp.max(lg, axis=1, keepdims=True))
    aff = ex / jnp.sum(ex, axis=1, keepdims=True)
    for r in range(t // LANES):
        afft[:, r * LANES:(r + 1) * LANES] = aff[r * LANES:(r + 1) * LANES, :].T
    aff_t = afft[0:N_EXPERTS, :]

    def count(mask):
        return jnp.sum(jnp.where(mask, 1.0, 0.0), axis=1, keepdims=True).astype(jnp.int32)

    def bit_step(i, thr_bits):
        cand = thr_bits | (jnp.int32(1) << (30 - i))
        cand_f = lax.bitcast_convert_type(cand, F32)
        return jnp.where(count(aff_t >= cand_f) >= cap, cand, thr_bits)

    thr_bits = lax.fori_loop(0, 31, bit_step, jnp.zeros((N_EXPERTS, 1), jnp.int32))
    thr = lax.bitcast_convert_type(thr_bits, F32)
    gt = aff_t > thr
    eq = aff_t == thr
    need = cap - count(gt)
    eq_i = jnp.where(eq, 1, 0)
    eq_rank = _scan_lanes_i32(eq_i) - eq_i
    sel = gt | (eq & (eq_rank < need))
    sel_i = jnp.where(sel, 1, 0)
    pos = _scan_lanes_i32(sel_i) - sel_i
    key = jnp.where(sel, pos, -1)

    a_hi = aff.astype(BF16).astype(F32)
    a_mid = (aff - a_hi).astype(BF16).astype(F32)
    a_lo = (aff - a_hi - a_mid).astype(BF16).astype(F32)
    tok = lax.broadcasted_iota(jnp.int32, (t, LANES), 0)
    vals = (a_hi + pltpu.roll(a_mid, N_EXPERTS, 1) + pltpu.roll(a_lo, 2 * N_EXPERTS, 1)
            + jnp.where(lane == 3 * N_EXPERTS, (tok >> GRID_SHIFT).astype(F32), 0.0)
            + jnp.where(lane == 3 * N_EXPERTS + 1, (tok & (GRID_W - 1)).astype(F32), 0.0)).astype(BF16)
    slot = lax.broadcasted_iota(jnp.int32, (cap, 1), 0)
    for e in range(N_EXPERTS):
        onehot = jnp.where(slot == key[e:e + 1, :], 1.0, 0.0).astype(BF16)
        res = jnp.dot(onehot, vals, preferred_element_type=F32)
        rt = jnp.concatenate([res[r * LANES:(r + 1) * LANES, :].T for r in range(cap // LANES)], axis=1)
        gate_ref[0, e:e + 1, :] = (rt[e:e + 1, :] + rt[N_EXPERTS + e:N_EXPERTS + e + 1, :]) \
            + rt[2 * N_EXPERTS + e:2 * N_EXPERTS + e + 1, :]
        tokf = rt[3 * N_EXPERTS:3 * N_EXPERTS + 1, :] * float(GRID_W) + rt[3 * N_EXPERTS + 1:3 * N_EXPERTS + 2, :]
        idx_ref[0, e:e + 1, :] = tokf.astype(jnp.int32)
        tok_col = res[:, 3 * N_EXPERTS:3 * N_EXPERTS + 1] * float(GRID_W) \
            + res[:, 3 * N_EXPERTS + 1:3 * N_EXPERTS + 2]
        tokrep_ref[0, e * cap:(e + 1) * cap, :] = jnp.broadcast_to(tok_col, (cap, LANES)).astype(jnp.int32)

    n_tiles = t // COMBINE_TILE
    lane_b = lax.broadcasted_iota(jnp.int32, (N_EXPERTS, LANES), 1)
    bounds = jnp.where(lane_b >= n_tiles, cap, 0)
    for i in range(n_tiles):
        bounds = jnp.where(lane_b == i, pos[:, i * COMBINE_TILE:i * COMBINE_TILE + 1], bounds)
    bounds_ref[0] = bounds


def _route(lg3, cap):
    b_, t, _ = lg3.shape
    return pl.pallas_call(
        _route_kernel,
        grid=(b_,),
        in_specs=[pl.BlockSpec((1, t, LANES), lambda b: (b, 0, 0))],
        out_specs=[pl.BlockSpec((1, N_EXPERTS, cap), lambda b: (b, 0, 0)),
                   pl.BlockSpec((1, N_EXPERTS, cap), lambda b: (b, 0, 0)),
                   pl.BlockSpec((1, N_EXPERTS * cap, LANES), lambda b: (b, 0, 0)),
                   pl.BlockSpec((1, N_EXPERTS, LANES), lambda b: (b, 0, 0))],
        out_shape=[jax.ShapeDtypeStruct((b_, N_EXPERTS, cap), jnp.int32),
                   jax.ShapeDtypeStruct((b_, N_EXPERTS, cap), F32),
                   jax.ShapeDtypeStruct((b_, N_EXPERTS * cap, LANES), jnp.int32),
                   jax.ShapeDtypeStruct((b_, N_EXPERTS, LANES), jnp.int32)],
        scratch_shapes=[pltpu.VMEM((LANES, t), F32)],
        compiler_params=_cparams(("arbitrary",)),
    )(lg3)


def _ffn_kernel(idx_ref, gate_ref, h2_hbm, wg_ref, wu_ref, wd_ref, y_ref,
                xs_buf, wg_s, wu_s, wd_s, sem):
    p = pl.program_id(0)
    b = pl.program_id(1)
    n_e = pl.num_programs(0) - 1
    n_b = pl.num_programs(1)
    k_chunks, cap = xs_buf.shape[1], xs_buf.shape[2]
    groups = h2_hbm.shape[0] // n_b
    step = p * n_b + b
    cur = step % 2
    nxt = 1 - cur
    last_b = b == n_b - 1
    b_next = jnp.where(last_b, 0, b + 1)
    p_next = jnp.where(last_b, p + 1, p)

    @pl.when((p_next >= 1) & (p_next <= n_e))
    def _():
        def issue(r, carry):
            tok = idx_ref[0, 0, r]
            pltpu.make_async_copy(h2_hbm.at[b_next * groups + (tok >> 3), :, tok & (SUBLANES - 1), :],
                                  xs_buf.at[nxt, :, r, :], sem.at[nxt]).start()
            return carry

        lax.fori_loop(0, cap, issue, 0, unroll=True)

    @pl.when(p < n_e)
    def _():
        ws = p % 2
        rows_in = wg_ref.shape[1]
        rows_mid = wd_ref.shape[1]
        wg_s[ws, pl.ds(pl.multiple_of(b * rows_in, rows_in), rows_in), :] = wg_ref[0].astype(BF16)
        wu_s[ws, pl.ds(pl.multiple_of(b * rows_in, rows_in), rows_in), :] = wu_ref[0].astype(BF16)
        wd_s[ws, pl.ds(pl.multiple_of(b * rows_mid, rows_mid), rows_mid), :] = wd_ref[0].astype(BF16)

    @pl.when(p == 0)
    def _():
        y_ref[0, 0] = jnp.zeros(y_ref.shape[2:], BF16)

    @pl.when(p >= 1)
    def _():
        pltpu.make_async_copy(xs_buf.at[cur], xs_buf.at[cur], sem.at[cur]).wait()
        ws = (p - 1) % 2
        xs = jnp.concatenate([xs_buf[cur, j].astype(BF16) for j in range(k_chunks)], axis=1)
        a = _silu(jnp.dot(xs, wg_s[ws], preferred_element_type=F32)) \
            * jnp.dot(xs, wu_s[ws], preferred_element_type=F32)
        y = jnp.dot(a.astype(BF16), wd_s[ws], preferred_element_type=F32)
        y_ref[0, 0] = (y * gate_ref[0, 0]).astype(BF16)


def _ffn(idx, gate, h2, wg, wu, wd):
    e_, d, f = wg.shape
    b_, _, cap = idx.shape
    assert h2.shape[1] * h2.shape[3] == d and h2.shape[2] == SUBLANES and h2.shape[0] % b_ == 0
    assert d % b_ == 0 and f % b_ == 0
    rows_in, rows_mid = d // b_, f // b_
    assert rows_in % (2 * SUBLANES) == 0 and rows_mid % (2 * SUBLANES) == 0

    def next_tokens(p, b):
        last_b = b == b_ - 1
        b_next = jnp.where(last_b, 0, b + 1)
        e_next = jnp.clip(jnp.where(last_b, p, p - 1), 0, e_ - 1)
        return (b_next * e_ + e_next, 0, 0)

    def this_expert(p, b):
        return (b, jnp.maximum(p - 1, 0), 0, 0)

    def weight_piece(p, b):
        return (jnp.minimum(p, e_ - 1), b, 0)

    return pl.pallas_call(
        _ffn_kernel,
        grid=(e_ + 1, b_),
        in_specs=[pl.BlockSpec((1, 1, cap), next_tokens, memory_space=pltpu.SMEM),
                  pl.BlockSpec((1, 1, cap, 1), this_expert),
                  pl.BlockSpec(memory_space=pl.ANY),
                  pl.BlockSpec((1, rows_in, f), weight_piece),
                  pl.BlockSpec((1, rows_in, f), weight_piece),
                  pl.BlockSpec((1, rows_mid, d), weight_piece)],
        out_specs=pl.BlockSpec((1, 1, cap, d), lambda p, b: (b, (p + e_) % (e_ + 1), 0, 0)),
        out_shape=jax.ShapeDtypeStruct((b_, e_ + 1, cap, d), BF16),
        scratch_shapes=[pltpu.VMEM((2, d // LANES, cap, LANES), F32),
                        pltpu.VMEM((2, d, f), BF16),
                        pltpu.VMEM((2, d, f), BF16),
                        pltpu.VMEM((2, f, d), BF16),
                        pltpu.SemaphoreType.DMA((2,))],
        compiler_params=_cparams(("arbitrary", "arbitrary")),
    )(idx.reshape(b_ * e_, 1, cap), gate.reshape(b_, e_, cap, 1), h2, wg, wu, wd)


COMBINE_CHUNK = 256


def _combine_kernel(bounds_ref, y_hbm, tok_ref, x1_ref, mod_ref, nf_ref, o_ref, y_buf, stage_y, stage_t, sem):
    b = pl.program_id(0)
    i = pl.program_id(1)
    n_b = pl.num_programs(0)
    n_e, cap = y_buf.shape[1], y_buf.shape[2]
    cur = b % 2

    def y_copy(bb, slot):
        return pltpu.make_async_copy(y_hbm.at[bb, pl.ds(0, n_e)], y_buf.at[slot], sem.at[slot])

    @pl.when(i == 0)
    def _():
        @pl.when(b == 0)
        def _():
            y_copy(0, 0).start()

        y_copy(b, cur).wait()

        @pl.when(b + 1 < n_b)
        def _():
            y_copy(b + 1, 1 - cur).start()

    tile = o_ref.shape[1]
    n_bounds = pl.num_programs(1) + 1
    per_chunk = COMBINE_CHUNK // GRANULE
    g_shift = GRANULE.bit_length() - 1
    assert per_chunk & (per_chunk - 1) == 0 and GRANULE == 1 << g_shift

    @pl.when((b == 0) & (i == 0))
    def _():
        stage_y[...] = jnp.zeros(stage_y.shape, BF16)
        stage_t[...] = jnp.full(stage_t.shape, -1, jnp.int32)

    o_ref[...] = jnp.zeros(o_ref.shape, F32)
    tok_lane = lax.broadcasted_iota(jnp.int32, (COMBINE_CHUNK, LANES), 1) + i * tile

    def flush():
        onehot_t = jnp.concatenate(
            [jnp.where(stage_t[...] == tok_lane + j * LANES, 1.0, 0.0).astype(BF16)
             for j in range(tile // LANES)], axis=1)
        o_ref[0] += lax.dot_general(onehot_t, stage_y[...], (((0,), (0,)), ((), ())),
                                    preferred_element_type=F32)
        stage_t[...] = jnp.full(stage_t.shape, -1, jnp.int32)

    def per_expert(e, k):
        base = (b * n_e + e) * n_bounds + i
        lo = bounds_ref[base]
        hi = bounds_ref[base + 1]
        g_lo = lax.shift_right_logical(lo, g_shift)
        g_hi = jnp.where(hi > lo, lax.shift_right_logical(hi + GRANULE - 1, g_shift), g_lo)

        def per_granule(g, k):
            src = pl.multiple_of(g * GRANULE, GRANULE)
            dst = pl.multiple_of((k & (per_chunk - 1)) * GRANULE, GRANULE)
            stage_y[pl.ds(dst, GRANULE), :] = y_buf[cur, e, pl.ds(src, GRANULE), :]
            stage_t[pl.ds(dst, GRANULE), :] = tok_ref[0, pl.ds(pl.multiple_of(e * cap + src, GRANULE), GRANULE), :]
            k = k + 1

            @pl.when((k & (per_chunk - 1)) == 0)
            def _():
                flush()

            return k

        return lax.fori_loop(g_lo, g_hi, per_granule, k)

    k = lax.fori_loop(0, n_e, per_expert, jnp.int32(0))

    @pl.when((k & (per_chunk - 1)) != 0)
    def _():
        flush()

    x = x1_ref[0] + mod_ref[0, 5:6, :] * o_ref[0]
    o_ref[0] = (x * lax.rsqrt(jnp.mean(x * x, axis=-1, keepdims=True) + EPS)) * nf_ref[...]


def _combine(y, tokrep, bounds, x1, mod, nf):
    b_, e_, cap, d = y.shape
    e_ -= 1
    t = x1.shape[1]
    tile = COMBINE_TILE
    assert cap % GRANULE == 0 and tile % LANES == 0 and bounds.shape == (b_, e_, t // tile + 1)
    grid_spec = pltpu.PrefetchScalarGridSpec(
        num_scalar_prefetch=1,
        grid=(b_, t // tile),
        in_specs=[pl.BlockSpec(memory_space=pl.ANY),
                  pl.BlockSpec((1, e_ * cap, LANES), lambda b, i, bnd: (b, 0, 0)),
                  pl.BlockSpec((1, tile, d), lambda b, i, bnd: (b, i, 0)),
                  pl.BlockSpec((1, N_MOD, d), lambda b, i, bnd: (b, 0, 0)),
                  pl.BlockSpec((1, d), lambda b, i, bnd: (0, 0))],
        out_specs=pl.BlockSpec((1, tile, d), lambda b, i, bnd: (b, i, 0)),
        scratch_shapes=[pltpu.VMEM((2, e_, cap, d), BF16),
                        pltpu.VMEM((COMBINE_CHUNK, d), BF16),
                        pltpu.VMEM((COMBINE_CHUNK, LANES), jnp.int32),
                        pltpu.SemaphoreType.DMA((2,))])
    return pl.pallas_call(
        _combine_kernel,
        grid_spec=grid_spec,
        out_shape=jax.ShapeDtypeStruct((b_, t, d), F32),
        compiler_params=_cparams(("arbitrary", "arbitrary")),
    )(bounds.reshape(-1), y, tokrep, x1, mod, nf)


def _gate_layout(gt, b_, t):
    return gt.reshape(N_HEADS, 4, b_, t // MLSTM_CHUNK, MLSTM_CHUNK)


def kernel(x, c, ctx, c_ctx, w_mod, b_mod, norm1, w_in, conv_q_w, conv_k_w, b_gates, head_g, pool_w,
           pool_scale, w_out, norm2, w_router, w_gate, w_up, w_down, norm_f):
    b_, t, d = x.shape
    tc = ctx.shape[1]
    depth = w_mod.shape[0]
    assert depth == 1
    l = 0
    qk_w = N_HEADS * DQK
    mw = N_HEADS * DV
    pool_wd = len(POOL_WINDOWS) * POOL_GC
    n_gate = 2 * 2 * N_HEADS
    k_off, v_off = 0, qk_w
    g_off = v_off + mw
    q_off = g_off + n_gate
    o_off = q_off + qk_w
    p_off = o_off + mw
    cap = EC_FACTOR * t // N_EXPERTS

    mod_rows = -(-(b_ + 1) // SUBLANES) * SUBLANES
    cs = jnp.zeros((mod_rows, d), F32).at[:b_].set(c).at[b_].set(c_ctx)
    mod_all = _modulation(cs, w_mod[l], b_mod[l])
    mod = mod_all[:b_].reshape(b_, N_MOD, d)
    mod_c = mod_all[b_:b_ + 1].reshape(1, N_MOD, d)

    wl = w_in[l]
    w_main = jnp.concatenate([wl[:, k_off:v_off], wl[:, v_off:g_off], wl[:, q_off:o_off],
                              wl[:, o_off:p_off], wl[:, p_off:]], axis=1).astype(BF16)
    perm = jnp.arange(n_gate).reshape(2, 2, N_HEADS).transpose(2, 0, 1).reshape(-1)
    w_g = wl[:, g_off:q_off][:, perm].T.astype(BF16)
    b_g = b_gates[l][perm].reshape(n_gate, 1)
    n1 = norm1[l].reshape(1, d)

    p, g = _inproj(x.reshape(b_ * t, d), mod, t, n1, w_main, w_g, b_g, 512)
    pc, gc = _inproj(ctx.reshape(b_ * tc, d), mod_c, b_ * tc, n1, w_main[:, :qk_w + mw], w_g, b_g,
                     min(512, b_ * tc))
    p = p.reshape(b_, t, -1)
    pc = pc.reshape(b_, tc, -1)

    wk = jnp.pad(conv_k_w[l], ((0, SUBLANES - QK_CONV), (0, 0)))
    wq = jnp.pad(conv_q_w[l], ((0, SUBLANES - QK_CONV), (0, 0)))
    ym = _mlstm(p, pc, _gate_layout(g, b_, t), _gate_layout(gc, b_, tc), wk, wq, head_g[l].reshape(1, mw))
    yp = _pool(p, pool_w[l].astype(BF16), pool_scale[l].reshape(1, pool_wd))

    wo = w_out[l].astype(BF16)
    wr = jnp.pad(w_router[l], ((0, 0), (0, LANES - N_EXPERTS))).astype(BF16)
    x1, h2, lg = _outproj(ym.reshape(b_ * t, mw), yp.reshape(b_ * t, pool_wd), wo[:mw], wo[mw:],
                          x.reshape(b_ * t, d), mod, t, norm2[l].reshape(1, d), wr, 512)

    idx, gate, tokrep, bounds = _route(lg.reshape(b_, t, LANES), cap)

    y = _ffn(idx, gate, h2, w_gate[l], w_up[l], w_down[l])
    return _combine(y, tokrep, bounds[:, :, :t // COMBINE_TILE + 1], x1.reshape(b_, t, d), mod,
                    norm_f.reshape(1, d))
```

```python
import functools

import jax
import jax.numpy as jnp
from jax import lax
from jax.experimental import pallas as pl
from jax.experimental.pallas import tpu as pltpu

F32 = jnp.float32
BF16 = jnp.bfloat16

N_HEADS = 4
DQK = 128
DV = 256
QK_CONV = 5
POOL_WINDOWS = (2, 4, 8, 16)
POOL_GC = 256
GRID_W = 64
GRID_SHIFT = GRID_W.bit_length() - 1
N_EXPERTS = 16
EC_FACTOR = 2
N_MOD = 6
EPS = 1e-6

LANES = 128
SUBLANES = 8
MLSTM_CHUNK = 128
VMEM_LIMIT = 56 * 1024 * 1024


def _cparams(sem):
    return pltpu.CompilerParams(dimension_semantics=sem, vmem_limit_bytes=VMEM_LIMIT)


def _sigmoid(x):
    return 1.0 / (1.0 + jnp.exp(-x))


def _silu(x):
    return x * _sigmoid(x)


def _log_sigmoid(x):
    return jnp.minimum(x, 0.0) - jnp.log(1.0 + jnp.exp(-jnp.abs(x)))


def _mod_kernel(c_ref, w_ref, b_ref, o_ref):
    s = _silu(c_ref[...]).astype(BF16)
    o_ref[...] = jnp.dot(s, w_ref[...].astype(BF16), preferred_element_type=F32) + b_ref[...]


def _modulation(cs, w_mod, b_mod):
    rows, d = cs.shape
    n = w_mod.shape[1]
    tn = 1024
    return pl.pallas_call(
        _mod_kernel,
        grid=(n // tn,),
        in_specs=[pl.BlockSpec((rows, d), lambda j: (0, 0)),
                  pl.BlockSpec((d, tn), lambda j: (0, j)),
                  pl.BlockSpec((1, tn), lambda j: (0, j))],
        out_specs=pl.BlockSpec((rows, tn), lambda j: (0, j)),
        out_shape=jax.ShapeDtypeStruct((rows, n), F32),
        compiler_params=_cparams(("arbitrary",)),
    )(cs, w_mod, b_mod.reshape(1, n))


INPROJ_COLS = 512


def _inproj_kernel(x_ref, mod_ref, n1_ref, w_ref, wg_ref, bg_ref, p_ref, gt_ref):
    tm = x_ref.shape[0]
    half = tm // 2
    n_cols = p_ref.shape[1]
    scale = n1_ref[...] * (1.0 + mod_ref[0, 1:2, :])
    for r0 in (0, half):
        rows = slice(r0, r0 + half)
        x = x_ref[rows, :]
        y = x * lax.rsqrt(jnp.mean(x * x, axis=-1, keepdims=True) + EPS)
        hb = (y * scale + mod_ref[0, 0:1, :]).astype(BF16)
        gt_ref[:, rows] = lax.dot_general(wg_ref[...], hb, (((1,), (1,)), ((), ())),
                                          preferred_element_type=F32) + bg_ref[...]
        for c0 in range(0, n_cols, INPROJ_COLS):
            p_ref[rows, c0:c0 + INPROJ_COLS] = jnp.dot(
                hb, w_ref[:, c0:c0 + INPROJ_COLS], preferred_element_type=F32).astype(BF16)


def _inproj(x2, mod, rows_per_mod, n1, w, wg, bg, tm):
    m, d = x2.shape
    n_cols = w.shape[1]
    n_g = wg.shape[0]
    tiles_per_mod = rows_per_mod // tm
    assert n_cols % INPROJ_COLS == 0 and tm % (2 * LANES) == 0
    return pl.pallas_call(
        _inproj_kernel,
        grid=(m // tm,),
        in_specs=[pl.BlockSpec((tm, d), lambda i: (i, 0)),
                  pl.BlockSpec((1, N_MOD, d), lambda i: (i // tiles_per_mod, 0, 0)),
                  pl.BlockSpec((1, d), lambda i: (0, 0)),
                  pl.BlockSpec((d, n_cols), lambda i: (0, 0), pipeline_mode=pl.Buffered(1)),
                  pl.BlockSpec((n_g, d), lambda i: (0, 0)),
                  pl.BlockSpec((n_g, 1), lambda i: (0, 0))],
        out_specs=[pl.BlockSpec((tm, n_cols), lambda i: (i, 0)),
                   pl.BlockSpec((n_g, tm), lambda i: (0, i))],
        out_shape=[jax.ShapeDtypeStruct((m, n_cols), BF16),
                   jax.ShapeDtypeStruct((n_g, m), F32)],
        compiler_params=_cparams(("arbitrary",)),
    )(x2, mod, n1, w, wg, bg)


def _scan_lanes(x, op, fill, reverse):
    lane = lax.broadcasted_iota(jnp.int32, x.shape, 1)
    k = 1
    while k < LANES:
        if reverse:
            x = op(x, jnp.where(lane < LANES - k, pltpu.roll(x, LANES - k, 1), fill))
        else:
            x = op(x, jnp.where(lane >= k, pltpu.roll(x, k, 1), fill))
        k *= 2
    return x


G_B, G_R, G_W, G_BEND, G_MLOC, G_RMAX, G_ROWS = 0, 1, 2, 3, 4, 5, 6


def _mlstm_kernel(kc_ref, vc_ref, gc_ref, k_ref, v_ref, q_ref, o_ref, g_ref, wk_ref, wq_ref, hg_ref,
                  out_ref,
                  cpad, ktt, qt, kctt, vext, vcext, gs, gsc, cf, cb, hf, hb):
    L = MLSTM_CHUNK
    t_lat = k_ref.shape[1]
    t_ctx = kc_ref.shape[1]
    nc = t_lat // L
    ncc = t_ctx // L
    hps = k_ref.shape[2] // DQK
    pad = SUBLANES
    half = QK_CONV // 2

    def conv_silu(src, w_ref, dst, t, scale, transposed):
        cpad[0:pad, :] = jnp.zeros((pad, LANES), F32)
        cpad[pad:pad + t, :] = src.astype(F32)
        cpad[pad + t:pad + t + pad, :] = jnp.zeros((pad, LANES), F32)
        for r in range(t // L):
            base = pad + r * L - half
            acc = w_ref[0:1, :] * cpad[base:base + L, :]
            for j in range(1, QK_CONV):
                acc = acc + w_ref[j:j + 1, :] * cpad[base + j:base + j + L, :]
            y = _silu(acc)
            if scale is not None:
                y = y * scale
            dst[r] = (y.T if transposed else y).astype(BF16)

    def fill_vext(dst, src, t):
        dst[:, 0:DV] = src
        dst[:, DV:DV + LANES] = jnp.ones((t, LANES), BF16)

    for hh in range(hps):
        qk_cols = slice(hh * DQK, (hh + 1) * DQK)
        v_cols = slice(hh * DV, (hh + 1) * DV)
        conv_silu(kc_ref[0, :, qk_cols], wk_ref.at[:, qk_cols], kctt.at[hh], t_ctx, DQK ** -0.5, True)
        conv_silu(k_ref[0, :, qk_cols], wk_ref.at[:, qk_cols], ktt.at[hh], t_lat, DQK ** -0.5, True)
        conv_silu(q_ref[0, :, qk_cols], wq_ref.at[:, qk_cols], qt.at[hh], t_lat, None, False)
        fill_vext(vcext.at[hh], vc_ref[0, :, v_cols], t_ctx)
        fill_vext(vext.at[hh], v_ref[0, :, v_cols], t_lat)

    def gate_prep(garr, dst):
        for d in range(2):
            li = garr[2 * d]
            lf = _log_sigmoid(garr[2 * d + 1])
            b = _scan_lanes(lf, jnp.add, 0.0, reverse=(d == 1))
            b_end = b[:, LANES - 1:LANES] if d == 0 else b[:, 0:1]
            r = li - b
            a = b_end + r
            m_loc = jnp.max(a, axis=1, keepdims=True)
            dst[G_ROWS * d + G_B] = b
            dst[G_ROWS * d + G_R] = r
            dst[G_ROWS * d + G_W] = jnp.exp(a - m_loc)
            dst[G_ROWS * d + G_BEND] = jnp.broadcast_to(b_end, b.shape)
            dst[G_ROWS * d + G_MLOC] = jnp.broadcast_to(m_loc, b.shape)
            dst[G_ROWS * d + G_RMAX] = _scan_lanes(r, jnp.maximum, -jnp.inf, reverse=(d == 1))

    for hh in range(hps):
        gate_prep(gc_ref[hh, :, 0], gsc.at[hh])
        gate_prep(g_ref[hh, :, 0], gs.at[hh])

    cf[...] = jnp.zeros(cf.shape, F32)
    cb[...] = jnp.zeros(cb.shape, F32)

    row_i = lax.broadcasted_iota(jnp.int32, (L, L), 0)
    col_i = lax.broadcasted_iota(jnp.int32, (L, L), 1)
    visible = (col_i <= row_i, col_i >= row_i)

    def per_token(row):
        return jnp.broadcast_to(row, (L, L)).T

    def step(hh, c, d, g_all, kt_all, v_all, q_all, c_all, m_prev):
        off = c * L if isinstance(c, int) else pl.multiple_of(c * L, L)
        g_scr, kt_scr, v_scr, c_scr = g_all.at[hh], kt_all.at[hh], v_all.at[hh], c_all.at[hh]
        q_scr = None if q_all is None else q_all.at[hh]
        g0 = G_ROWS * d
        kt_c = kt_scr[c]
        v_c = v_scr[pl.ds(off, L), :]
        b_end = g_scr[g0 + G_BEND, pl.ds(c, 1), 0:1]
        m_loc = g_scr[g0 + G_MLOC, pl.ds(c, 1), 0:1]
        c_prev = c_scr[...]
        h = None
        if q_scr is not None:
            q_c = q_scr[c]
            r_row = g_scr[g0 + G_R, pl.ds(c, 1), :]
            u = jnp.maximum(per_token(g_scr[g0 + G_RMAX, pl.ds(c, 1), :]), m_prev)
            b_t = per_token(g_scr[g0 + G_B, pl.ds(c, 1), :])
            qk = jnp.dot(q_c, kt_c, preferred_element_type=F32)
            s = (qk * jnp.exp(jnp.where(visible[d], r_row - u, -jnp.inf))).astype(BF16)
            wi = jnp.exp(m_prev - u)
            inter = jnp.dot(q_c, c_prev.astype(BF16), preferred_element_type=F32)
            res = jnp.dot(s, v_c, preferred_element_type=F32) \
                + jnp.concatenate([wi] * (c_prev.shape[1] // L), axis=1) * inter
            inv = 1.0 / jnp.maximum(jnp.abs(res[:, DV:DV + L]), jnp.exp(-(b_t + u)))
            h = res[:, 0:DV] * jnp.concatenate([inv] * (DV // L), axis=1)
        kwt = (kt_c.astype(F32) * g_scr[g0 + G_W, pl.ds(c, 1), :]).astype(BF16)
        c_loc = jnp.dot(kwt, v_c, preferred_element_type=F32)
        m_new = jnp.maximum(b_end + m_prev, m_loc)
        sp = jnp.exp(b_end + m_prev - m_new)
        sl = jnp.exp(m_loc - m_new)
        c_scr[...] = sp * c_prev + sl * c_loc
        return m_new, h

    def finish(hh, c, h):
        off = pl.multiple_of(c * L, L)
        cols = slice(hh * DV, (hh + 1) * DV)
        h = h * lax.rsqrt(jnp.mean(h * h, axis=-1, keepdims=True) + EPS)
        y = (h * hg_ref[:, cols]) * _sigmoid(o_ref[0, pl.ds(off, L), cols].astype(F32))
        out_ref[0, pl.ds(off, L), cols] = y.astype(BF16)

    ms = [jnp.zeros((1, 1), F32)] * (2 * hps)
    for i in range(ncc):
        for hh in range(hps):
            ms[2 * hh], _ = step(hh, i, 0, gsc, kctt, vcext, None, cf, ms[2 * hh])
            ms[2 * hh + 1], _ = step(hh, ncc - 1 - i, 1, gsc, kctt, vcext, None, cb, ms[2 * hh + 1])

    assert nc % 2 == 0

    def scan_step(i, carry, second_half):
        c_f, c_b = i, nc - 1 - i
        off_f, off_b = pl.multiple_of(c_f * L, L), pl.multiple_of(c_b * L, L)
        out = []
        for hh in range(hps):
            m_f, h_f = step(hh, c_f, 0, gs, ktt, vext, qt, cf, carry[2 * hh])
            m_b, h_b = step(hh, c_b, 1, gs, ktt, vext, qt, cb, carry[2 * hh + 1])
            if second_half:
                finish(hh, c_f, h_f + hb[hh, pl.ds(off_f, L), :])
                finish(hh, c_b, h_b + hf[hh, pl.ds(off_b, L), :])
            else:
                hf[hh, pl.ds(off_f, L), :] = h_f
                hb[hh, pl.ds(off_b, L), :] = h_b
            out += [m_f, m_b]
        return tuple(out)

    carry = lax.fori_loop(0, nc // 2, functools.partial(scan_step, second_half=False), tuple(ms), unroll=2)
    lax.fori_loop(nc // 2, nc, functools.partial(scan_step, second_half=True), carry, unroll=2)


MLSTM_HEADS_PER_STEP = 2


def _mlstm(p, pc, g4, gc4, wk, wq, head_g):
    b_, t, _ = p.shape
    tc = pc.shape[1]
    L = MLSTM_CHUNK
    nc, ncc = t // L, tc // L
    ext = DV + LANES
    hps = MLSTM_HEADS_PER_STEP
    assert N_HEADS % hps == 0
    v_blk = N_HEADS * DQK // (hps * DV)
    q_blk = (N_HEADS * DQK + N_HEADS * DV) // (hps * DQK)
    o_blk = (2 * N_HEADS * DQK + N_HEADS * DV) // (hps * DV)
    return pl.pallas_call(
        _mlstm_kernel,
        grid=(b_, N_HEADS // hps),
        in_specs=[
            pl.BlockSpec((1, tc, hps * DQK), lambda b, h: (b, 0, h)),
            pl.BlockSpec((1, tc, hps * DV), lambda b, h: (b, 0, v_blk + h)),
            pl.BlockSpec((hps, 4, 1, ncc, L), lambda b, h: (h, 0, b, 0, 0)),
            pl.BlockSpec((1, t, hps * DQK), lambda b, h: (b, 0, h)),
            pl.BlockSpec((1, t, hps * DV), lambda b, h: (b, 0, v_blk + h)),
            pl.BlockSpec((1, t, hps * DQK), lambda b, h: (b, 0, q_blk + h)),
            pl.BlockSpec((1, t, hps * DV), lambda b, h: (b, 0, o_blk + h)),
            pl.BlockSpec((hps, 4, 1, nc, L), lambda b, h: (h, 0, b, 0, 0)),
            pl.BlockSpec((SUBLANES, hps * DQK), lambda b, h: (0, h)),
            pl.BlockSpec((SUBLANES, hps * DQK), lambda b, h: (0, h)),
            pl.BlockSpec((1, hps * DV), lambda b, h: (0, h)),
        ],
        out_specs=pl.BlockSpec((1, t, hps * DV), lambda b, h: (b, 0, h)),
        out_shape=jax.ShapeDtypeStruct((b_, t, N_HEADS * DV), BF16),
        scratch_shapes=[
            pltpu.VMEM((t + 2 * SUBLANES, LANES), F32),
            pltpu.VMEM((hps, nc, DQK, L), BF16),
            pltpu.VMEM((hps, nc, L, DQK), BF16),
            pltpu.VMEM((hps, ncc, DQK, L), BF16),
            pltpu.VMEM((hps, t, ext), BF16),
            pltpu.VMEM((hps, tc, ext), BF16),
            pltpu.VMEM((hps, 2 * G_ROWS, nc, L), F32),
            pltpu.VMEM((hps, 2 * G_ROWS, ncc, L), F32),
            pltpu.VMEM((hps, DQK, ext), F32),
            pltpu.VMEM((hps, DQK, ext), F32),
            pltpu.VMEM((hps, t, DV), F32),
            pltpu.VMEM((hps, t, DV), F32),
        ],
        compiler_params=_cparams(("arbitrary", "arbitrary")),
    )(pc, pc, gc4, p, p, p, p, g4, wk, wq, head_g)


def _pool_kernel(u_ref, pw_ref, ps_ref, out_ref, spad):
    t = u_ref.shape[1]
    rows = t // GRID_W
    blk = 256
    halo = (max(POOL_WINDOWS) // 2) * GRID_W
    spad[0:halo, :] = jnp.zeros((halo, POOL_GC), F32)
    spad[halo + t:halo + t + halo, :] = jnp.zeros((halo, POOL_GC), F32)
    ti = lax.broadcasted_iota(jnp.int32, (blk, blk), 0)
    tj = lax.broadcasted_iota(jnp.int32, (blk, blk), 1)
    same_row = (ti >> GRID_SHIFT) == (tj >> GRID_SHIFT)
    diff = tj - ti
    tok = lax.broadcasted_iota(jnp.int32, (blk, 1), 0)
    for gi, w in enumerate(POOL_WINDOWS):
        hw = w // 2
        c0, c1 = gi * POOL_GC, (gi + 1) * POOL_GC
        band = jnp.where(same_row & (diff >= -hw) & (diff <= hw - 1), 1.0, 0.0).astype(BF16)
        for r in range(t // blk):
            spad[halo + r * blk:halo + (r + 1) * blk, :] = jnp.dot(
                band, u_ref[0, r * blk:(r + 1) * blk, c0:c1], preferred_element_type=F32)
        for r in range(t // blk):
            base = halo + r * blk
            acc = spad[base - hw * GRID_W:base - hw * GRID_W + blk, :]
            for j in range(-hw + 1, hw):
                acc = acc + spad[base + j * GRID_W:base + j * GRID_W + blk, :]
            tt = tok + r * blk
            gr = tt >> GRID_SHIFT
            gc = tt & (GRID_W - 1)
            cr = jnp.minimum(gr + hw, rows) - jnp.maximum(gr - hw, 0)
            cc = jnp.minimum(gc + hw, GRID_W) - jnp.maximum(gc - hw, 0)
            cnt = (cr * cc).astype(F32)
            d = acc / cnt - u_ref[0, r * blk:(r + 1) * blk, c0:c1].astype(F32)
            y = jnp.dot(d.astype(BF16), pw_ref[gi], preferred_element_type=F32) * ps_ref[:, c0:c1]
            out_ref[0, r * blk:(r + 1) * blk, c0:c1] = y.astype(BF16)


def _pool(p, pool_w, pool_scale):
    b_, t, _ = p.shape
    pw = len(POOL_WINDOWS) * POOL_GC
    halo = (max(POOL_WINDOWS) // 2) * GRID_W
    return pl.pallas_call(
        _pool_kernel,
        grid=(b_,),
        in_specs=[pl.BlockSpec((1, t, pw), lambda b: (b, 0, 3)),
                  pl.BlockSpec((len(POOL_WINDOWS), POOL_GC, POOL_GC), lambda b: (0, 0, 0)),
                  pl.BlockSpec((1, pw), lambda b: (0, 0))],
        out_specs=pl.BlockSpec((1, t, pw), lambda b: (b, 0, 0)),
        out_shape=jax.ShapeDtypeStruct((b_, t, pw), BF16),
        scratch_shapes=[pltpu.VMEM((t + 2 * halo, POOL_GC), F32)],
        compiler_params=_cparams(("arbitrary",)),
    )(p, pool_w, pool_scale)


def _outproj_kernel(ym_ref, yp_ref, wm_ref, wp_ref, x_ref, mod_ref, n2_ref, wr_ref,
                    x1_ref, h2_ref, lg_ref):
    tm = x_ref.shape[0]
    half = tm // 2
    scale2 = n2_ref[...] * (1.0 + mod_ref[0, 4:5, :])
    for r0 in (0, half):
        rows = slice(r0, r0 + half)
        mix = (jnp.dot(ym_ref[rows, :], wm_ref[...], preferred_element_type=F32)
               + jnp.dot(yp_ref[rows, :], wp_ref[...], preferred_element_type=F32))
        x1 = x_ref[rows, :] + mod_ref[0, 2:3, :] * mix
        x1_ref[rows, :] = x1
        y = x1 * lax.rsqrt(jnp.mean(x1 * x1, axis=-1, keepdims=True) + EPS)
        h2 = y * scale2 + mod_ref[0, 3:4, :]
        g0, g1 = r0 // SUBLANES, (r0 + half) // SUBLANES
        for j in range(h2_ref.shape[1]):
            h2_ref[g0:g1, j, :, :] = h2[:, j * LANES:(j + 1) * LANES].reshape(g1 - g0, SUBLANES, LANES)
        lg_ref[rows, :] = jnp.dot(h2.astype(BF16), wr_ref[...], preferred_element_type=F32)


def _outproj(ym, yp, wm, wp, x2, mod, rows_per_mod, n2, wr, tm):
    m, d = x2.shape
    km = ym.shape[1]
    kp = yp.shape[1]
    tiles_per_mod = rows_per_mod // tm
    return pl.pallas_call(
        _outproj_kernel,
        grid=(m // tm,),
        in_specs=[pl.BlockSpec((tm, km), lambda i: (i, 0)),
                  pl.BlockSpec((tm, kp), lambda i: (i, 0)),
                  pl.BlockSpec((km, d), lambda i: (0, 0), pipeline_mode=pl.Buffered(1)),
                  pl.BlockSpec((kp, d), lambda i: (0, 0), pipeline_mode=pl.Buffered(1)),
                  pl.BlockSpec((tm, d), lambda i: (i, 0)),
                  pl.BlockSpec((1, N_MOD, d), lambda i: (i // tiles_per_mod, 0, 0)),
                  pl.BlockSpec((1, d), lambda i: (0, 0)),
                  pl.BlockSpec((d, LANES), lambda i: (0, 0))],
        out_specs=[pl.BlockSpec((tm, d), lambda i: (i, 0)),
                   pl.BlockSpec((tm // SUBLANES, d // LANES, SUBLANES, LANES), lambda i: (i, 0, 0, 0)),
                   pl.BlockSpec((tm, LANES), lambda i: (i, 0))],
        out_shape=[jax.ShapeDtypeStruct((m, d), F32),
                   jax.ShapeDtypeStruct((m // SUBLANES, d // LANES, SUBLANES, LANES), F32),
                   jax.ShapeDtypeStruct((m, LANES), F32)],
        compiler_params=_cparams(("arbitrary",)),
    )(ym, yp, wm, wp, x2, mod, n2, wr)


def _scan_lanes_i32(x, exclusive_of=None):
    n = x.shape[1]
    lane = lax.broadcasted_iota(jnp.int32, x.shape, 1)
    k = 1
    while k < n:
        x = x + jnp.where(lane >= k, pltpu.roll(x, k, 1), 0)
        k *= 2
    return x


COMBINE_TILE = 128
GRANULE = 2 * SUBLANES


def _route_kernel(lg_ref, idx_ref, gate_ref, tokrep_ref, bounds_ref, afft):
    t = lg_ref.shape[1]
    cap = idx_ref.shape[2]
    lane = lax.broadcasted_iota(jnp.int32, (t, LANES), 1)
    lg = jnp.where(lane < N_EXPERTS, lg_ref[0], -jnp.inf)
    ex = jnp.exp(lg - jnp.max(lg, axis=1, keepdims=True))
    aff = ex / jnp.sum(ex, axis=1, keepdims=True)
    for r in range(t // LANES):
        afft[:, r * LANES:(r + 1) * LANES] = aff[r * LANES:(r + 1) * LANES, :].T
    aff_t = afft[0:N_EXPERTS, :]

    def count(mask):
        return jnp.sum(jnp.where(mask, 1.0, 0.0), axis=1, keepdims=True).astype(jnp.int32)

    def bit_step(i, thr_bits):
        cand = thr_bits | (jnp.int32(1) << (30 - i))
        cand_f = lax.bitcast_convert_type(cand, F32)
        return jnp.where(count(aff_t >= cand_f) >= cap, cand, thr_bits)

    thr_bits = lax.fori_loop(0, 31, bit_step, jnp.zeros((N_EXPERTS, 1), jnp.int32))
    thr = lax.bitcast_convert_type(thr_bits, F32)
    gt = aff_t > thr
    eq = aff_t == thr
    need = cap - count(gt)
    eq_i = jnp.where(eq, 1, 0)
    eq_rank = _scan_lanes_i32(eq_i) - eq_i
    sel = gt | (eq & (eq_rank < need))
    sel_i = jnp.where(sel, 1, 0)
    pos = _scan_lanes_i32(sel_i) - sel_i
    key = jnp.where(sel, pos, -1)

    a_hi = aff.astype(BF16).astype(F32)
    a_mid = (aff - a_hi).astype(BF16).astype(F32)
    a_lo = (aff - a_hi - a_mid).astype(BF16).astype(F32)
    tok = lax.broadcasted_iota(jnp.int32, (t, LANES), 0)
    vals = (a_hi + pltpu.roll(a_mid, N_EXPERTS, 1) + pltpu.roll(a_lo, 2 * N_EXPERTS, 1)
            + jnp.where(lane == 3 * N_EXPERTS, (tok >> GRID_SHIFT).astype(F32), 0.0)
            + jnp.where(lane == 3 * N_EXPERTS + 1, (tok & (GRID_W - 1)).astype(F32), 0.0)).astype(BF16)
    slot = lax.broadcasted_iota(jnp.int32, (cap, 1), 0)
    for e in range(N_EXPERTS):
        onehot = jnp.where(slot == key[e:e + 1, :], 1.0, 0.0).astype(BF16)
        res = jnp.dot(onehot, vals, preferred_element_type=F32)
        rt = jnp.concatenate([res[r * LANES:(r + 1) * LANES, :].T for r in range(cap // LANES)], axis=1)
        gate_ref[0, e:e + 1, :] = (rt[e:e + 1, :] + rt[N_EXPERTS + e:N_EXPERTS + e + 1, :]) \
            + rt[2 * N_EXPERTS + e:2 * N_EXPERTS + e + 1, :]
        tokf = rt[3 * N_EXPERTS:3 * N_EXPERTS + 1, :] * float(GRID_W) + rt[3 * N_EXPERTS + 1:3 * N_EXPERTS + 2, :]
        idx_ref[0, e:e + 1, :] = tokf.astype(jnp.int32)
        tok_col = res[:, 3 * N_EXPERTS:3 * N_EXPERTS + 1] * float(GRID_W) \
            + res[:, 3 * N_EXPERTS + 1:3 * N_EXPERTS + 2]
        tokrep_ref[0, e * cap:(e + 1) * cap, :] = jnp.broadcast_to(tok_col, (cap, LANES)).astype(jnp.int32)

    n_tiles = t // COMBINE_TILE
    lane_b = lax.broadcasted_iota(jnp.int32, (N_EXPERTS, LANES), 1)
    bounds = jnp.where(lane_b >= n_tiles, cap, 0)
    for i in range(n_tiles):
        bounds = jnp.where(lane_b == i, pos[:, i * COMBINE_TILE:i * COMBINE_TILE + 1], bounds)
    bounds_ref[0] = bounds


def _route(lg3, cap):
    b_, t, _ = lg3.shape
    return pl.pallas_call(
        _route_kernel,
        grid=(b_,),
        in_specs=[pl.BlockSpec((1, t, LANES), lambda b: (b, 0, 0))],
        out_specs=[pl.BlockSpec((1, N_EXPERTS, cap), lambda b: (b, 0, 0)),
                   pl.BlockSpec((1, N_EXPERTS, cap), lambda b: (b, 0, 0)),
                   pl.BlockSpec((1, N_EXPERTS * cap, LANES), lambda b: (b, 0, 0)),
                   pl.BlockSpec((1, N_EXPERTS, LANES), lambda b: (b, 0, 0))],
        out_shape=[jax.ShapeDtypeStruct((b_, N_EXPERTS, cap), jnp.int32),
                   jax.ShapeDtypeStruct((b_, N_EXPERTS, cap), F32),
                   jax.ShapeDtypeStruct((b_, N_EXPERTS * cap, LANES), jnp.int32),
                   jax.ShapeDtypeStruct((b_, N_EXPERTS, LANES), jnp.int32)],
        scratch_shapes=[pltpu.VMEM((LANES, t), F32)],
        compiler_params=_cparams(("arbitrary",)),
    )(lg3)


def _ffn_kernel(idx_ref, gate_ref, h2_hbm, wg_ref, wu_ref, wd_ref, y_ref,
                xs_buf, wg_s, wu_s, wd_s, sem):
    p = pl.program_id(0)
    b = pl.program_id(1)
    n_e = pl.num_programs(0) - 1
    n_b = pl.num_programs(1)
    k_chunks, cap = xs_buf.shape[1], xs_buf.shape[2]
    groups = h2_hbm.shape[0] // n_b
    step = p * n_b + b
    cur = step % 2
    nxt = 1 - cur
    last_b = b == n_b - 1
    b_next = jnp.where(last_b, 0, b + 1)
    p_next = jnp.where(last_b, p + 1, p)

    @pl.when((p_next >= 1) & (p_next <= n_e))
    def _():
        def issue(r, carry):
            tok = idx_ref[0, 0, r]
            pltpu.make_async_copy(h2_hbm.at[b_next * groups + (tok >> 3), :, tok & (SUBLANES - 1), :],
                                  xs_buf.at[nxt, :, r, :], sem.at[nxt]).start()
            return carry

        lax.fori_loop(0, cap, issue, 0, unroll=True)

    @pl.when(p < n_e)
    def _():
        ws = p % 2
        rows_in = wg_ref.shape[1]
        rows_mid = wd_ref.shape[1]
        wg_s[ws, pl.ds(pl.multiple_of(b * rows_in, rows_in), rows_in), :] = wg_ref[0].astype(BF16)
        wu_s[ws, pl.ds(pl.multiple_of(b * rows_in, rows_in), rows_in), :] = wu_ref[0].astype(BF16)
        wd_s[ws, pl.ds(pl.multiple_of(b * rows_mid, rows_mid), rows_mid), :] = wd_ref[0].astype(BF16)

    @pl.when(p == 0)
    def _():
        y_ref[0, 0] = jnp.zeros(y_ref.shape[2:], BF16)

    @pl.when(p >= 1)
    def _():
        pltpu.make_async_copy(xs_buf.at[cur], xs_buf.at[cur], sem.at[cur]).wait()
        ws = (p - 1) % 2
        xs = jnp.concatenate([xs_buf[cur, j].astype(BF16) for j in range(k_chunks)], axis=1)
        a = _silu(jnp.dot(xs, wg_s[ws], preferred_element_type=F32)) \
            * jnp.dot(xs, wu_s[ws], preferred_element_type=F32)
        y = jnp.dot(a.astype(BF16), wd_s[ws], preferred_element_type=F32)
        y_ref[0, 0] = (y * gate_ref[0, 0]).astype(BF16)


def _ffn(idx, gate, h2, wg, wu, wd):
    e_, d, f = wg.shape
    b_, _, cap = idx.shape
    assert h2.shape[1] * h2.shape[3] == d and h2.shape[2] == SUBLANES and h2.shape[0] % b_ == 0
    assert d % b_ == 0 and f % b_ == 0
    rows_in, rows_mid = d // b_, f // b_
    assert rows_in % (2 * SUBLANES) == 0 and rows_mid % (2 * SUBLANES) == 0

    def next_tokens(p, b):
        last_b = b == b_ - 1
        b_next = jnp.where(last_b, 0, b + 1)
        e_next = jnp.clip(jnp.where(last_b, p, p - 1), 0, e_ - 1)
        return (b_next * e_ + e_next, 0, 0)

    def this_expert(p, b):
        return (b, jnp.maximum(p - 1, 0), 0, 0)

    def weight_piece(p, b):
        return (jnp.minimum(p, e_ - 1), b, 0)

    return pl.pallas_call(
        _ffn_kernel,
        grid=(e_ + 1, b_),
        in_specs=[pl.BlockSpec((1, 1, cap), next_tokens, memory_space=pltpu.SMEM),
                  pl.BlockSpec((1, 1, cap, 1), this_expert),
                  pl.BlockSpec(memory_space=pl.ANY),
                  pl.BlockSpec((1, rows_in, f), weight_piece),
                  pl.BlockSpec((1, rows_in, f), weight_piece),
                  pl.BlockSpec((1, rows_mid, d), weight_piece)],
        out_specs=pl.BlockSpec((1, 1, cap, d), lambda p, b: (b, (p + e_) % (e_ + 1), 0, 0)),
        out_shape=jax.ShapeDtypeStruct((b_, e_ + 1, cap, d), BF16),
        scratch_shapes=[pltpu.VMEM((2, d // LANES, cap, LANES), F32),
                        pltpu.VMEM((2, d, f), BF16),
                        pltpu.VMEM((2, d, f), BF16),
                        pltpu.VMEM((2, f, d), BF16),
                        pltpu.SemaphoreType.DMA((2,))],
        compiler_params=_cparams(("arbitrary", "arbitrary")),
    )(idx.reshape(b_ * e_, 1, cap), gate.reshape(b_, e_, cap, 1), h2, wg, wu, wd)


COMBINE_CHUNK = 256


def _combine_kernel(bounds_ref, y_hbm, tok_ref, x1_ref, mod_ref, nf_ref, o_ref, y_buf, stage_y, stage_t, sem):
    b = pl.program_id(0)
    i = pl.program_id(1)
    n_b = pl.num_programs(0)
    n_e, cap = y_buf.shape[1], y_buf.shape[2]
    cur = b % 2

    def y_copy(bb, slot):
        return pltpu.make_async_copy(y_hbm.at[bb, pl.ds(0, n_e)], y_buf.at[slot], sem.at[slot])

    @pl.when(i == 0)
    def _():
        @pl.when(b == 0)
        def _():
            y_copy(0, 0).start()

        y_copy(b, cur).wait()

        @pl.when(b + 1 < n_b)
        def _():
            y_copy(b + 1, 1 - cur).start()

    tile = o_ref.shape[1]
    n_bounds = pl.num_programs(1) + 1
    per_chunk = COMBINE_CHUNK // GRANULE
    g_shift = GRANULE.bit_length() - 1
    assert per_chunk & (per_chunk - 1) == 0 and GRANULE == 1 << g_shift

    @pl.when((b == 0) & (i == 0))
    def _():
        stage_y[...] = jnp.zeros(stage_y.shape, BF16)
        stage_t[...] = jnp.full(stage_t.shape, -1, jnp.int32)

    o_ref[...] = jnp.zeros(o_ref.shape, F32)
    tok_lane = lax.broadcasted_iota(jnp.int32, (COMBINE_CHUNK, LANES), 1) + i * tile

    def flush():
        onehot_t = jnp.concatenate(
            [jnp.where(stage_t[...] == tok_lane + j * LANES, 1.0, 0.0).astype(BF16)
             for j in range(tile // LANES)], axis=1)
        o_ref[0] += lax.dot_general(onehot_t, stage_y[...], (((0,), (0,)), ((), ())),
                                    preferred_element_type=F32)
        stage_t[...] = jnp.full(stage_t.shape, -1, jnp.int32)

    def per_expert(e, k):
        base = (b * n_e + e) * n_bounds + i
        lo = bounds_ref[base]
        hi = bounds_ref[base + 1]
        g_lo = lax.shift_right_logical(lo, g_shift)
        g_hi = jnp.where(hi > lo, lax.shift_right_logical(hi + GRANULE - 1, g_shift), g_lo)

        def per_granule(g, k):
            src = pl.multiple_of(g * GRANULE, GRANULE)
            dst = pl.multiple_of((k & (per_chunk - 1)) * GRANULE, GRANULE)
            stage_y[pl.ds(dst, GRANULE), :] = y_buf[cur, e, pl.ds(src, GRANULE), :]
            stage_t[pl.ds(dst, GRANULE), :] = tok_ref[0, pl.ds(pl.multiple_of(e * cap + src, GRANULE), GRANULE), :]
            k = k + 1

            @pl.when((k & (per_chunk - 1)) == 0)
            def _():
                flush()

            return k

        return lax.fori_loop(g_lo, g_hi, per_granule, k)

    k = lax.fori_loop(0, n_e, per_expert, jnp.int32(0))

    @pl.when((k & (per_chunk - 1)) != 0)
    def _():
        flush()

    x = x1_ref[0] + mod_ref[0, 5:6, :] * o_ref[0]
    o_ref[0] = (x * lax.rsqrt(jnp.mean(x * x, axis=-1, keepdims=True) + EPS)) * nf_ref[...]


def _combine(y, tokrep, bounds, x1, mod, nf):
    b_, e_, cap, d = y.shape
    e_ -= 1
    t = x1.shape[1]
    tile = COMBINE_TILE
    assert cap % GRANULE == 0 and tile % LANES == 0 and bounds.shape == (b_, e_, t // tile + 1)
    grid_spec = pltpu.PrefetchScalarGridSpec(
        num_scalar_prefetch=1,
        grid=(b_, t // tile),
        in_specs=[pl.BlockSpec(memory_space=pl.ANY),
                  pl.BlockSpec((1, e_ * cap, LANES), lambda b, i, bnd: (b, 0, 0)),
                  pl.BlockSpec((1, tile, d), lambda b, i, bnd: (b, i, 0)),
                  pl.BlockSpec((1, N_MOD, d), lambda b, i, bnd: (b, 0, 0)),
                  pl.BlockSpec((1, d), lambda b, i, bnd: (0, 0))],
        out_specs=pl.BlockSpec((1, tile, d), lambda b, i, bnd: (b, i, 0)),
        scratch_shapes=[pltpu.VMEM((2, e_, cap, d), BF16),
                        pltpu.VMEM((COMBINE_CHUNK, d), BF16),
                        pltpu.VMEM((COMBINE_CHUNK, LANES), jnp.int32),
                        pltpu.SemaphoreType.DMA((2,))])
    return pl.pallas_call(
        _combine_kernel,
        grid_spec=grid_spec,
        out_shape=jax.ShapeDtypeStruct((b_, t, d), F32),
        compiler_params=_cparams(("arbitrary", "arbitrary")),
    )(bounds.reshape(-1), y, tokrep, x1, mod, nf)


def _gate_layout(gt, b_, t):
    return gt.reshape(N_HEADS, 4, b_, t // MLSTM_CHUNK, MLSTM_CHUNK)


def kernel(x, c, ctx, c_ctx, w_mod, b_mod, norm1, w_in, conv_q_w, conv_k_w, b_gates, head_g, pool_w,
           pool_scale, w_out, norm2, w_router, w_gate, w_up, w_down, norm_f):
    b_, t, d = x.shape
    tc = ctx.shape[1]
    depth = w_mod.shape[0]
    assert depth == 1
    l = 0
    qk_w = N_HEADS * DQK
    mw = N_HEADS * DV
    pool_wd = len(POOL_WINDOWS) * POOL_GC
    n_gate = 2 * 2 * N_HEADS
    k_off, v_off = 0, qk_w
    g_off = v_off + mw
    q_off = g_off + n_gate
    o_off = q_off + qk_w
    p_off = o_off + mw
    cap = EC_FACTOR * t // N_EXPERTS

    mod_rows = -(-(b_ + 1) // SUBLANES) * SUBLANES
    cs = jnp.zeros((mod_rows, d), F32).at[:b_].set(c).at[b_].set(c_ctx)
    mod_all = _modulation(cs, w_mod[l], b_mod[l])
    mod = mod_all[:b_].reshape(b_, N_MOD, d)
    mod_c = mod_all[b_:b_ + 1].reshape(1, N_MOD, d)

    wl = w_in[l]
    w_main = jnp.concatenate([wl[:, k_off:v_off], wl[:, v_off:g_off], wl[:, q_off:o_off],
                              wl[:, o_off:p_off], wl[:, p_off:]], axis=1).astype(BF16)
    perm = jnp.arange(n_gate).reshape(2, 2, N_HEADS).transpose(2, 0, 1).reshape(-1)
    w_g = wl[:, g_off:q_off][:, perm].T.astype(BF16)
    b_g = b_gates[l][perm].reshape(n_gate, 1)
    n1 = norm1[l].reshape(1, d)

    p, g = _inproj(x.reshape(b_ * t, d), mod, t, n1, w_main, w_g, b_g, 512)
    pc, gc = _inproj(ctx.reshape(b_ * tc, d), mod_c, b_ * tc, n1, w_main[:, :qk_w + mw], w_g, b_g,
                     min(512, b_ * tc))
    p = p.reshape(b_, t, -1)
    pc = pc.reshape(b_, tc, -1)

    wk = jnp.pad(conv_k_w[l], ((0, SUBLANES - QK_CONV), (0, 0)))
    wq = jnp.pad(conv_q_w[l], ((0, SUBLANES - QK_CONV), (0, 0)))
    ym = _mlstm(p, pc, _gate_layout(g, b_, t), _gate_layout(gc, b_, tc), wk, wq, head_g[l].reshape(1, mw))
    yp = _pool(p, pool_w[l].astype(BF16), pool_scale[l].reshape(1, pool_wd))

    wo = w_out[l].astype(BF16)
    wr = jnp.pad(w_router[l], ((0, 0), (0, LANES - N_EXPERTS))).astype(BF16)
    x1, h2, lg = _outproj(ym.reshape(b_ * t, mw), yp.reshape(b_ * t, pool_wd), wo[:mw], wo[mw:],
                          x.reshape(b_ * t, d), mod, t, norm2[l].reshape(1, d), wr, 512)

    idx, gate, tokrep, bounds = _route(lg.reshape(b_, t, LANES), cap)

    y = _ffn(idx, gate, h2, w_gate[l], w_up[l], w_down[l])
    return _combine(y, tokrep, bounds[:, :, :t // COMBINE_TILE + 1], x1.reshape(b_, t, d), mod,
                    norm_f.reshape(1, d))
```

```python
import functools

import jax
import jax.numpy as jnp
from jax import lax
from jax.experimental import pallas as pl
from jax.experimental.pallas import tpu as pltpu

F32 = jnp.float32
BF16 = jnp.bfloat16

N_HEADS = 4
DQK = 128
DV = 256
QK_CONV = 5
POOL_WINDOWS = (2, 4, 8, 16)
POOL_GC = 256
GRID_W = 64
GRID_SHIFT = GRID_W.bit_length() - 1
N_EXPERTS = 16
EC_FACTOR = 2
N_MOD = 6
EPS = 1e-6

LANES = 128
SUBLANES = 8
MLSTM_CHUNK = 128
VMEM_LIMIT = 56 * 1024 * 1024


def _cparams(sem):
    return pltpu.CompilerParams(dimension_semantics=sem, vmem_limit_bytes=VMEM_LIMIT)


def _sigmoid(x):
    return 1.0 / (1.0 + jnp.exp(-x))


def _silu(x):
    return x * _sigmoid(x)


def _log_sigmoid(x):
    return jnp.minimum(x, 0.0) - jnp.log(1.0 + jnp.exp(-jnp.abs(x)))


def _mod_kernel(c_ref, w_ref, b_ref, o_ref):
    s = _silu(c_ref[...]).astype(BF16)
    o_ref[...] = jnp.dot(s, w_ref[...].astype(BF16), preferred_element_type=F32) + b_ref[...]


def _modulation(cs, w_mod, b_mod):
    rows, d = cs.shape
    n = w_mod.shape[1]
    tn = 1024
    return pl.pallas_call(
        _mod_kernel,
        grid=(n // tn,),
        in_specs=[pl.BlockSpec((rows, d), lambda j: (0, 0)),
                  pl.BlockSpec((d, tn), lambda j: (0, j)),
                  pl.BlockSpec((1, tn), lambda j: (0, j))],
        out_specs=pl.BlockSpec((rows, tn), lambda j: (0, j)),
        out_shape=jax.ShapeDtypeStruct((rows, n), F32),
        compiler_params=_cparams(("arbitrary",)),
    )(cs, w_mod, b_mod.reshape(1, n))


INPROJ_COLS = 512


def _inproj_kernel(x_ref, mod_ref, n1_ref, w_ref, wg_ref, bg_ref, p_ref, gt_ref):
    tm = x_ref.shape[0]
    half = tm // 2
    n_cols = p_ref.shape[1]
    scale = n1_ref[...] * (1.0 + mod_ref[0, 1:2, :])
    for r0 in (0, half):
        rows = slice(r0, r0 + half)
        x = x_ref[rows, :]
        y = x * lax.rsqrt(jnp.mean(x * x, axis=-1, keepdims=True) + EPS)
        hb = (y * scale + mod_ref[0, 0:1, :]).astype(BF16)
        gt_ref[:, rows] = lax.dot_general(wg_ref[...], hb, (((1,), (1,)), ((), ())),
                                          preferred_element_type=F32) + bg_ref[...]
        for c0 in range(0, n_cols, INPROJ_COLS):
            p_ref[rows, c0:c0 + INPROJ_COLS] = jnp.dot(
                hb, w_ref[:, c0:c0 + INPROJ_COLS], preferred_element_type=F32).astype(BF16)


def _inproj(x2, mod, rows_per_mod, n1, w, wg, bg, tm):
    m, d = x2.shape
    n_cols = w.shape[1]
    n_g = wg.shape[0]
    tiles_per_mod = rows_per_mod // tm
    assert n_cols % INPROJ_COLS == 0 and tm % (2 * LANES) == 0
    return pl.pallas_call(
        _inproj_kernel,
        grid=(m // tm,),
        in_specs=[pl.BlockSpec((tm, d), lambda i: (i, 0)),
                  pl.BlockSpec((1, N_MOD, d), lambda i: (i // tiles_per_mod, 0, 0)),
                  pl.BlockSpec((1, d), lambda i: (0, 0)),
                  pl.BlockSpec((d, n_cols), lambda i: (0, 0), pipeline_mode=pl.Buffered(1)),
                  pl.BlockSpec((n_g, d), lambda i: (0, 0)),
                  pl.BlockSpec((n_g, 1), lambda i: (0, 0))],
        out_specs=[pl.BlockSpec((tm, n_cols), lambda i: (i, 0)),
                   pl.BlockSpec((n_g, tm), lambda i: (0, i))],
        out_shape=[jax.ShapeDtypeStruct((m, n_cols), BF16),
                   jax.ShapeDtypeStruct((n_g, m), F32)],
        compiler_params=_cparams(("arbitrary",)),
    )(x2, mod, n1, w, wg, bg)


def _scan_lanes(x, op, fill, reverse):
    lane = lax.broadcasted_iota(jnp.int32, x.shape, 1)
    k = 1
    while k < LANES:
        if reverse:
            x = op(x, jnp.where(lane < LANES - k, pltpu.roll(x, LANES - k, 1), fill))
        else:
            x = op(x, jnp.where(lane >= k, pltpu.roll(x, k, 1), fill))
        k *= 2
    return x


G_B, G_R, G_W, G_BEND, G_MLOC, G_RMAX, G_ROWS = 0, 1, 2, 3, 4, 5, 6


def _mlstm_kernel(kc_ref, vc_ref, gc_ref, k_ref, v_ref, q_ref, o_ref, g_ref, wk_ref, wq_ref, hg_ref,
                  out_ref,
                  cpad, ktt, qt, kctt, vext, vcext, gs, gsc, cf, cb, hf, hb):
    L = MLSTM_CHUNK
    t_lat = k_ref.shape[1]
    t_ctx = kc_ref.shape[1]
    nc = t_lat // L
    ncc = t_ctx // L
    hps = k_ref.shape[2] // DQK
    pad = SUBLANES
    half = QK_CONV // 2

    def conv_silu(src, w_ref, dst, t, scale, transposed):
        cpad[0:pad, :] = jnp.zeros((pad, LANES), F32)
        cpad[pad:pad + t, :] = src.astype(F32)
        cpad[pad + t:pad + t + pad, :] = jnp.zeros((pad, LANES), F32)
        for r in range(t // L):
            base = pad + r * L - half
            acc = w_ref[0:1, :] * cpad[base:base + L, :]
            for j in range(1, QK_CONV):
                acc = acc + w_ref[j:j + 1, :] * cpad[base + j:base + j + L, :]
            y = _silu(acc)
            if scale is not None:
                y = y * scale
            dst[r] = (y.T if transposed else y).astype(BF16)

    def fill_vext(dst, src, t):
        dst[:, 0:DV] = src
        dst[:, DV:DV + LANES] = jnp.ones((t, LANES), BF16)

    for hh in range(hps):
        qk_cols = slice(hh * DQK, (hh + 1) * DQK)
        v_cols = slice(hh * DV, (hh + 1) * DV)
        conv_silu(kc_ref[0, :, qk_cols], wk_ref.at[:, qk_cols], kctt.at[hh], t_ctx, DQK ** -0.5, True)
        conv_silu(k_ref[0, :, qk_cols], wk_ref.at[:, qk_cols], ktt.at[hh], t_lat, DQK ** -0.5, True)
        conv_silu(q_ref[0, :, qk_cols], wq_ref.at[:, qk_cols], qt.at[hh], t_lat, None, False)
        fill_vext(vcext.at[hh], vc_ref[0, :, v_cols], t_ctx)
        fill_vext(vext.at[hh], v_ref[0, :, v_cols], t_lat)

    def gate_prep(garr, dst):
        for d in range(2):
            li = garr[2 * d]
            lf = _log_sigmoid(garr[2 * d + 1])
            b = _scan_lanes(lf, jnp.add, 0.0, reverse=(d == 1))
            b_end = b[:, LANES - 1:LANES] if d == 0 else b[:, 0:1]
            r = li - b
            a = b_end + r
            m_loc = jnp.max(a, axis=1, keepdims=True)
            dst[G_ROWS * d + G_B] = b
            dst[G_ROWS * d + G_R] = r
            dst[G_ROWS * d + G_W] = jnp.exp(a - m_loc)
            dst[G_ROWS * d + G_BEND] = jnp.broadcast_to(b_end, b.shape)
            dst[G_ROWS * d + G_MLOC] = jnp.broadcast_to(m_loc, b.shape)
            dst[G_ROWS * d + G_RMAX] = _scan_lanes(r, jnp.maximum, -jnp.inf, reverse=(d == 1))

    for hh in range(hps):
        gate_prep(gc_ref[hh, :, 0], gsc.at[hh])
        gate_prep(g_ref[hh, :, 0], gs.at[hh])

    cf[...] = jnp.zeros(cf.shape, F32)
    cb[...] = jnp.zeros(cb.shape, F32)

    row_i = lax.broadcasted_iota(jnp.int32, (L, L), 0)
    col_i = lax.broadcasted_iota(jnp.int32, (L, L), 1)
    visible = (col_i <= row_i, col_i >= row_i)

    def per_token(row):
        return jnp.broadcast_to(row, (L, L)).T

    def step(hh, c, d, g_all, kt_all, v_all, q_all, c_all, m_prev):
        off = c * L if isinstance(c, int) else pl.multiple_of(c * L, L)
        g_scr, kt_scr, v_scr, c_scr = g_all.at[hh], kt_all.at[hh], v_all.at[hh], c_all.at[hh]
        q_scr = None if q_all is None else q_all.at[hh]
        g0 = G_ROWS * d
        kt_c = kt_scr[c]
        v_c = v_scr[pl.ds(off, L), :]
        b_end = g_scr[g0 + G_BEND, pl.ds(c, 1), 0:1]
        m_loc = g_scr[g0 + G_MLOC, pl.ds(c, 1), 0:1]
        c_prev = c_scr[...]
        h = None
        if q_scr is not None:
            q_c = q_scr[c]
            r_row = g_scr[g0 + G_R, pl.ds(c, 1), :]
            u = jnp.maximum(per_token(g_scr[g0 + G_RMAX, pl.ds(c, 1), :]), m_prev)
            b_t = per_token(g_scr[g0 + G_B, pl.ds(c, 1), :])
            qk = jnp.dot(q_c, kt_c, preferred_element_type=F32)
            s = (qk * jnp.exp(jnp.where(visible[d], r_row - u, -jnp.inf))).astype(BF16)
            wi = jnp.exp(m_prev - u)
            inter = jnp.dot(q_c, c_prev.astype(BF16), preferred_element_type=F32)
            res = jnp.dot(s, v_c, preferred_element_type=F32) \
                + jnp.concatenate([wi] * (c_prev.shape[1] // L), axis=1) * inter
            inv = 1.0 / jnp.maximum(jnp.abs(res[:, DV:DV + L]), jnp.exp(-(b_t + u)))
            h = res[:, 0:DV] * jnp.concatenate([inv] * (DV // L), axis=1)
        kwt = (kt_c.astype(F32) * g_scr[g0 + G_W, pl.ds(c, 1), :]).astype(BF16)
        c_loc = jnp.dot(kwt, v_c, preferred_element_type=F32)
        m_new = jnp.maximum(b_end + m_prev, m_loc)
        sp = jnp.exp(b_end + m_prev - m_new)
        sl = jnp.exp(m_loc - m_new)
        c_scr[...] = sp * c_prev + sl * c_loc
        return m_new, h

    def finish(hh, c, h):
        off = pl.multiple_of(c * L, L)
        cols = slice(hh * DV, (hh + 1) * DV)
        h = h * lax.rsqrt(jnp.mean(h * h, axis=-1, keepdims=True) + EPS)
        y = (h * hg_ref[:, cols]) * _sigmoid(o_ref[0, pl.ds(off, L), cols].astype(F32))
        out_ref[0, pl.ds(off, L), cols] = y.astype(BF16)

    ms = [jnp.zeros((1, 1), F32)] * (2 * hps)
    for i in range(ncc):
        for hh in range(hps):
            ms[2 * hh], _ = step(hh, i, 0, gsc, kctt, vcext, None, cf, ms[2 * hh])
            ms[2 * hh + 1], _ = step(hh, ncc - 1 - i, 1, gsc, kctt, vcext, None, cb, ms[2 * hh + 1])

    assert nc % 2 == 0

    def scan_step(i, carry, second_half):
        c_f, c_b = i, nc - 1 - i
        off_f, off_b = pl.multiple_of(c_f * L, L), pl.multiple_of(c_b * L, L)
        out = []
        for hh in range(hps):
            m_f, h_f = step(hh, c_f, 0, gs, ktt, vext, qt, cf, carry[2 * hh])
            m_b, h_b = step(hh, c_b, 1, gs, ktt, vext, qt, cb, carry[2 * hh + 1])
            if second_half:
                finish(hh, c_f, h_f + hb[hh, pl.ds(off_f, L), :])
                finish(hh, c_b, h_b + hf[hh, pl.ds(off_b, L), :])
            else:
                hf[hh, pl.ds(off_f, L), :] = h_f
                hb[hh, pl.ds(off_b, L), :] = h_b
            out += [m_f, m_b]
        return tuple(out)

    carry = lax.fori_loop(0, nc // 2, functools.partial(scan_step, second_half=False), tuple(ms), unroll=2)
    lax.fori_loop(nc // 2, nc, functools.partial(scan_step, second_half=True), carry, unroll=2)


MLSTM_HEADS_PER_STEP = 2


def _mlstm(p, pc, g4, gc4, wk, wq, head_g):
    b_, t, _ = p.shape
    tc = pc.shape[1]
    L = MLSTM_CHUNK
    nc, ncc = t // L, tc // L
    ext = DV + LANES
    hps = MLSTM_HEADS_PER_STEP
    assert N_HEADS % hps == 0
    v_blk = N_HEADS * DQK // (hps * DV)
    q_blk = (N_HEADS * DQK + N_HEADS * DV) // (hps * DQK)
    o_blk = (2 * N_HEADS * DQK + N_HEADS * DV) // (hps * DV)
    return pl.pallas_call(
        _mlstm_kernel,
        grid=(b_, N_HEADS // hps),
        in_specs=[
            pl.BlockSpec((1, tc, hps * DQK), lambda b, h: (b, 0, h)),
            pl.BlockSpec((1, tc, hps * DV), lambda b, h: (b, 0, v_blk + h)),
            pl.BlockSpec((hps, 4, 1, ncc, L), lambda b, h: (h, 0, b, 0, 0)),
            pl.BlockSpec((1, t, hps * DQK), lambda b, h: (b, 0, h)),
            pl.BlockSpec((1, t, hps * DV), lambda b, h: (b, 0, v_blk + h)),
            pl.BlockSpec((1, t, hps * DQK), lambda b, h: (b, 0, q_blk + h)),
            pl.BlockSpec((1, t, hps * DV), lambda b, h: (b, 0, o_blk + h)),
            pl.BlockSpec((hps, 4, 1, nc, L), lambda b, h: (h, 0, b, 0, 0)),
            pl.BlockSpec((SUBLANES, hps * DQK), lambda b, h: (0, h)),
            pl.BlockSpec((SUBLANES, hps * DQK), lambda b, h: (0, h)),
            pl.BlockSpec((1, hps * DV), lambda b, h: (0, h)),
        ],
        out_specs=pl.BlockSpec((1, t, hps * DV), lambda b, h: (b, 0, h)),
        out_shape=jax.ShapeDtypeStruct((b_, t, N_HEADS * DV), BF16),
        scratch_shapes=[
            pltpu.VMEM((t + 2 * SUBLANES, LANES), F32),
            pltpu.VMEM((hps, nc, DQK, L), BF16),
            pltpu.VMEM((hps, nc, L, DQK), BF16),
            pltpu.VMEM((hps, ncc, DQK, L), BF16),
            pltpu.VMEM((hps, t, ext), BF16),
            pltpu.VMEM((hps, tc, ext), BF16),
            pltpu.VMEM((hps, 2 * G_ROWS, nc, L), F32),
            pltpu.VMEM((hps, 2 * G_ROWS, ncc, L), F32),
            pltpu.VMEM((hps, DQK, ext), F32),
            pltpu.VMEM((hps, DQK, ext), F32),
            pltpu.VMEM((hps, t, DV), F32),
            pltpu.VMEM((hps, t, DV), F32),
        ],
        compiler_params=_cparams(("arbitrary", "arbitrary")),
    )(pc, pc, gc4, p, p, p, p, g4, wk, wq, head_g)


def _pool_kernel(u_ref, pw_ref, ps_ref, out_ref, spad):
    t = u_ref.shape[1]
    rows = t // GRID_W
    blk = 256
    halo = (max(POOL_WINDOWS) // 2) * GRID_W
    spad[0:halo, :] = jnp.zeros((halo, POOL_GC), F32)
    spad[halo + t:halo + t + halo, :] = jnp.zeros((halo, POOL_GC), F32)
    ti = lax.broadcasted_iota(jnp.int32, (blk, blk), 0)
    tj = lax.broadcasted_iota(jnp.int32, (blk, blk), 1)
    same_row = (ti >> GRID_SHIFT) == (tj >> GRID_SHIFT)
    diff = tj - ti
    tok = lax.broadcasted_iota(jnp.int32, (blk, 1), 0)
    for gi, w in enumerate(POOL_WINDOWS):
        hw = w // 2
        c0, c1 = gi * POOL_GC, (gi + 1) * POOL_GC
        band = jnp.where(same_row & (diff >= -hw) & (diff <= hw - 1), 1.0, 0.0).astype(BF16)
        for r in range(t // blk):
            spad[halo + r * blk:halo + (r + 1) * blk, :] = jnp.dot(
                band, u_ref[0, r * blk:(r + 1) * blk, c0:c1], preferred_element_type=F32)
        for r in range(t // blk):
            base = halo + r * blk
            acc = spad[base - hw * GRID_W:base - hw * GRID_W + blk, :]
            for j in range(-hw + 1, hw):
                acc = acc + spad[base + j * GRID_W:base + j * GRID_W + blk, :]
            tt = tok + r * blk
            gr = tt >> GRID_SHIFT
            gc = tt & (GRID_W - 1)
            cr = jnp.minimum(gr + hw, rows) - jnp.maximum(gr - hw, 0)
            cc = jnp.minimum(gc + hw, GRID_W) - jnp.maximum(gc - hw, 0)
            cnt = (cr * cc).astype(F32)
            d = acc / cnt - u_ref[0, r * blk:(r + 1) * blk, c0:c1].astype(F32)
            y = jnp.dot(d.astype(BF16), pw_ref[gi], preferred_element_type=F32) * ps_ref[:, c0:c1]
            out_ref[0, r * blk:(r + 1) * blk, c0:c1] = y.astype(BF16)


def _pool(p, pool_w, pool_scale):
    b_, t, _ = p.shape
    pw = len(POOL_WINDOWS) * POOL_GC
    halo = (max(POOL_WINDOWS) // 2) * GRID_W
    return pl.pallas_call(
        _pool_kernel,
        grid=(b_,),
        in_specs=[pl.BlockSpec((1, t, pw), lambda b: (b, 0, 3)),
                  pl.BlockSpec((len(POOL_WINDOWS), POOL_GC, POOL_GC), lambda b: (0, 0, 0)),
                  pl.BlockSpec((1, pw), lambda b: (0, 0))],
        out_specs=pl.BlockSpec((1, t, pw), lambda b: (b, 0, 0)),
        out_shape=jax.ShapeDtypeStruct((b_, t, pw), BF16),
        scratch_shapes=[pltpu.VMEM((t + 2 * halo, POOL_GC), F32)],
        compiler_params=_cparams(("arbitrary",)),
    )(p, pool_w, pool_scale)


def _outproj_kernel(ym_ref, yp_ref, wm_ref, wp_ref, x_ref, mod_ref, n2_ref, wr_ref,
                    x1_ref, h2_ref, lg_ref):
    tm = x_ref.shape[0]
    half = tm // 2
    scale2 = n2_ref[...] * (1.0 + mod_ref[0, 4:5, :])
    for r0 in (0, half):
        rows = slice(r0, r0 + half)
        mix = (jnp.dot(ym_ref[rows, :], wm_ref[...], preferred_element_type=F32)
               + jnp.dot(yp_ref[rows, :], wp_ref[...], preferred_element_type=F32))
        x1 = x_ref[rows, :] + mod_ref[0, 2:3, :] * mix
        x1_ref[rows, :] = x1
        y = x1 * lax.rsqrt(jnp.mean(x1 * x1, axis=-1, keepdims=True) + EPS)
        h2 = y * scale2 + mod_ref[0, 3:4, :]
        g0, g1 = r0 // SUBLANES, (r0 + half) // SUBLANES
        for j in range(h2_ref.shape[1]):
            h2_ref[g0:g1, j, :, :] = h2[:, j * LANES:(j + 1) * LANES].reshape(g1 - g0, SUBLANES, LANES)
        lg_ref[rows, :] = jnp.dot(h2.astype(BF16), wr_ref[...], preferred_element_type=F32)


def _outproj(ym, yp, wm, wp, x2, mod, rows_per_mod, n2, wr, tm):
    m, d = x2.shape
    km = ym.shape[1]
    kp = yp.shape[1]
    tiles_per_mod = rows_per_mod // tm
    return pl.pallas_call(
        _outproj_kernel,
        grid=(m // tm,),
        in_specs=[pl.BlockSpec((tm, km), lambda i: (i, 0)),
                  pl.BlockSpec((tm, kp), lambda i: (i, 0)),
                  pl.BlockSpec((km, d), lambda i: (0, 0), pipeline_mode=pl.Buffered(1)),
                  pl.BlockSpec((kp, d), lambda i: (0, 0), pipeline_mode=pl.Buffered(1)),
                  pl.BlockSpec((tm, d), lambda i: (i, 0)),
                  pl.BlockSpec((1, N_MOD, d), lambda i: (i // tiles_per_mod, 0, 0)),
                  pl.BlockSpec((1, d), lambda i: (0, 0)),
                  pl.BlockSpec((d, LANES), lambda i: (0, 0))],
        out_specs=[pl.BlockSpec((tm, d), lambda i: (i, 0)),
                   pl.BlockSpec((tm // SUBLANES, d // LANES, SUBLANES, LANES), lambda i: (i, 0, 0, 0)),
                   pl.BlockSpec((tm, LANES), lambda i: (i, 0))],
        out_shape=[jax.ShapeDtypeStruct((m, d), F32),
                   jax.ShapeDtypeStruct((m // SUBLANES, d // LANES, SUBLANES, LANES), F32),
                   jax.ShapeDtypeStruct((m, LANES), F32)],
        compiler_params=_cparams(("arbitrary",)),
    )(ym, yp, wm, wp, x2, mod, n2, wr)


def _scan_lanes_i32(x, exclusive_of=None):
    n = x.shape[1]
    lane = lax.broadcasted_iota(jnp.int32, x.shape, 1)
    k = 1
    while k < n:
        x = x + jnp.where(lane >= k, pltpu.roll(x, k, 1), 0)
        k *= 2
    return x


COMBINE_TILE = 256
GRANULE = 2 * SUBLANES


def _route_kernel(lg_ref, idx_ref, gate_ref, tokrep_ref, bounds_ref, afft):
    t = lg_ref.shape[1]
    cap = idx_ref.shape[2]
    lane = lax.broadcasted_iota(jnp.int32, (t, LANES), 1)
    lg = jnp.where(lane < N_EXPERTS, lg_ref[0], -jnp.inf)
    ex = jnp.exp(lg - jnp.max(lg, axis=1, keepdims=True))
    aff = ex / jnp.sum(ex, axis=1, keepdims=True)
    for r in range(t // LANES):
        afft[:, r * LANES:(r + 1) * LANES] = aff[r * LANES:(r + 1) * LANES, :].T
    aff_t = afft[0:N_EXPERTS, :]

    def count(mask):
        return jnp.sum(jnp.where(mask, 1.0, 0.0), axis=1, keepdims=True).astype(jnp.int32)

    def bit_step(i, thr_bits):
        cand = thr_bits | (jnp.int32(1) << (30 - i))
        cand_f = lax.bitcast_convert_type(cand, F32)
        return jnp.where(count(aff_t >= cand_f) >= cap, cand, thr_bits)

    thr_bits = lax.fori_loop(0, 31, bit_step, jnp.zeros((N_EXPERTS, 1), jnp.int32))
    thr = lax.bitcast_convert_type(thr_bits, F32)
    gt = aff_t > thr
    eq = aff_t == thr
    need = cap - count(gt)
    eq_i = jnp.where(eq, 1, 0)
    eq_rank = _scan_lanes_i32(eq_i) - eq_i
    sel = gt | (eq & (eq_rank < need))
    sel_i = jnp.where(sel, 1, 0)
    pos = _scan_lanes_i32(sel_i) - sel_i
    key = jnp.where(sel, pos, -1)

    a_hi = aff.astype(BF16).astype(F32)
    a_mid = (aff - a_hi).astype(BF16).astype(F32)
    a_lo = (aff - a_hi - a_mid).astype(BF16).astype(F32)
    tok = lax.broadcasted_iota(jnp.int32, (t, LANES), 0)
    vals = (a_hi + pltpu.roll(a_mid, N_EXPERTS, 1) + pltpu.roll(a_lo, 2 * N_EXPERTS, 1)
            + jnp.where(lane == 3 * N_EXPERTS, (tok >> GRID_SHIFT).astype(F32), 0.0)
            + jnp.where(lane == 3 * N_EXPERTS + 1, (tok & (GRID_W - 1)).astype(F32), 0.0)).astype(BF16)
    slot = lax.broadcasted_iota(jnp.int32, (cap, 1), 0)
    for e in range(N_EXPERTS):
        onehot = jnp.where(slot == key[e:e + 1, :], 1.0, 0.0).astype(BF16)
        res = jnp.dot(onehot, vals, preferred_element_type=F32)
        rt = jnp.concatenate([res[r * LANES:(r + 1) * LANES, :].T for r in range(cap // LANES)], axis=1)
        gate_ref[0, e:e + 1, :] = (rt[e:e + 1, :] + rt[N_EXPERTS + e:N_EXPERTS + e + 1, :]) \
            + rt[2 * N_EXPERTS + e:2 * N_EXPERTS + e + 1, :]
        tokf = rt[3 * N_EXPERTS:3 * N_EXPERTS + 1, :] * float(GRID_W) + rt[3 * N_EXPERTS + 1:3 * N_EXPERTS + 2, :]
        idx_ref[0, e:e + 1, :] = tokf.astype(jnp.int32)
        tok_col = res[:, 3 * N_EXPERTS:3 * N_EXPERTS + 1] * float(GRID_W) \
            + res[:, 3 * N_EXPERTS + 1:3 * N_EXPERTS + 2]
        tokrep_ref[0, e * cap:(e + 1) * cap, :] = jnp.broadcast_to(tok_col, (cap, LANES)).astype(jnp.int32)

    n_tiles = t // COMBINE_TILE
    lane_b = lax.broadcasted_iota(jnp.int32, (N_EXPERTS, LANES), 1)
    bounds = jnp.where(lane_b >= n_tiles, cap, 0)
    for i in range(n_tiles):
        bounds = jnp.where(lane_b == i, pos[:, i * COMBINE_TILE:i * COMBINE_TILE + 1], bounds)
    bounds_ref[0] = bounds


def _route(lg3, cap):
    b_, t, _ = lg3.shape
    return pl.pallas_call(
        _route_kernel,
        grid=(b_,),
        in_specs=[pl.BlockSpec((1, t, LANES), lambda b: (b, 0, 0))],
        out_specs=[pl.BlockSpec((1, N_EXPERTS, cap), lambda b: (b, 0, 0)),
                   pl.BlockSpec((1, N_EXPERTS, cap), lambda b: (b, 0, 0)),
                   pl.BlockSpec((1, N_EXPERTS * cap, LANES), lambda b: (b, 0, 0)),
                   pl.BlockSpec((1, N_EXPERTS, LANES), lambda b: (b, 0, 0))],
        out_shape=[jax.ShapeDtypeStruct((b_, N_EXPERTS, cap), jnp.int32),
                   jax.ShapeDtypeStruct((b_, N_EXPERTS, cap), F32),
                   jax.ShapeDtypeStruct((b_, N_EXPERTS * cap, LANES), jnp.int32),
                   jax.ShapeDtypeStruct((b_, N_EXPERTS, LANES), jnp.int32)],
        scratch_shapes=[pltpu.VMEM((LANES, t), F32)],
        compiler_params=_cparams(("arbitrary",)),
    )(lg3)


def _ffn_kernel(idx_ref, gate_ref, h2_hbm, wg_ref, wu_ref, wd_ref, y_ref,
                xs_buf, wg_s, wu_s, wd_s, sem):
    p = pl.program_id(0)
    b = pl.program_id(1)
    n_e = pl.num_programs(0) - 1
    n_b = pl.num_programs(1)
    k_chunks, cap = xs_buf.shape[1], xs_buf.shape[2]
    groups = h2_hbm.shape[0] // n_b
    step = p * n_b + b
    cur = step % 2
    nxt = 1 - cur
    last_b = b == n_b - 1
    b_next = jnp.where(last_b, 0, b + 1)
    p_next = jnp.where(last_b, p + 1, p)

    def issue_gather():
        def issue(r, carry):
            tok = idx_ref[0, 0, r]
            pltpu.make_async_copy(h2_hbm.at[b_next * groups + (tok >> 3), :, tok & (SUBLANES - 1), :],
                                  xs_buf.at[nxt, :, r, :], sem.at[nxt]).start()
            return carry

        lax.fori_loop(0, cap, issue, 0, unroll=True)

    def cast_weights():
        ws = p % 2
        rows_in = wg_ref.shape[1]
        rows_mid = wd_ref.shape[1]
        wg_s[ws, pl.ds(pl.multiple_of(b * rows_in, rows_in), rows_in), :] = wg_ref[0].astype(BF16)
        wu_s[ws, pl.ds(pl.multiple_of(b * rows_in, rows_in), rows_in), :] = wu_ref[0].astype(BF16)
        wd_s[ws, pl.ds(pl.multiple_of(b * rows_mid, rows_mid), rows_mid), :] = wd_ref[0].astype(BF16)

    def run_expert(gather_next):
        pltpu.make_async_copy(xs_buf.at[cur], xs_buf.at[cur], sem.at[cur]).wait()
        ws = (p - 1) % 2
        xs = jnp.concatenate([xs_buf[cur, j].astype(BF16) for j in range(k_chunks)], axis=1)
        if gather_next:
            issue_gather()
        a = _silu(jnp.dot(xs, wg_s[ws], preferred_element_type=F32)) \
            * jnp.dot(xs, wu_s[ws], preferred_element_type=F32)
        y = jnp.dot(a.astype(BF16), wd_s[ws], preferred_element_type=F32)
        y_ref[0, 0] = (y * gate_ref[0, 0]).astype(BF16)

    @pl.when(p == 0)
    def _():
        cast_weights()
        y_ref[0, 0] = jnp.zeros(y_ref.shape[2:], BF16)

    @pl.when((p == 0) & last_b)
    def _():
        issue_gather()

    @pl.when((p >= 1) & (p < n_e))
    def _():
        cast_weights()
        run_expert(True)

    @pl.when((p == n_e) & jnp.logical_not(last_b))
    def _():
        run_expert(True)

    @pl.when((p == n_e) & last_b)
    def _():
        run_expert(False)


def _ffn(idx, gate, h2, wg, wu, wd):
    e_, d, f = wg.shape
    b_, _, cap = idx.shape
    assert h2.shape[1] * h2.shape[3] == d and h2.shape[2] == SUBLANES and h2.shape[0] % b_ == 0
    assert d % b_ == 0 and f % b_ == 0
    rows_in, rows_mid = d // b_, f // b_
    assert rows_in % (2 * SUBLANES) == 0 and rows_mid % (2 * SUBLANES) == 0

    def next_tokens(p, b):
        last_b = b == b_ - 1
        b_next = jnp.where(last_b, 0, b + 1)
        e_next = jnp.clip(jnp.where(last_b, p, p - 1), 0, e_ - 1)
        return (b_next * e_ + e_next, 0, 0)

    def this_expert(p, b):
        return (b, jnp.maximum(p - 1, 0), 0, 0)

    def weight_piece(p, b):
        return (jnp.minimum(p, e_ - 1), b, 0)

    return pl.pallas_call(
        _ffn_kernel,
        grid=(e_ + 1, b_),
        in_specs=[pl.BlockSpec((1, 1, cap), next_tokens, memory_space=pltpu.SMEM),
                  pl.BlockSpec((1, 1, cap, 1), this_expert),
                  pl.BlockSpec(memory_space=pl.ANY),
                  pl.BlockSpec((1, rows_in, f), weight_piece),
                  pl.BlockSpec((1, rows_in, f), weight_piece),
                  pl.BlockSpec((1, rows_mid, d), weight_piece)],
        out_specs=pl.BlockSpec((1, 1, cap, d), lambda p, b: (b, (p + e_) % (e_ + 1), 0, 0)),
        out_shape=jax.ShapeDtypeStruct((b_, e_ + 1, cap, d), BF16),
        scratch_shapes=[pltpu.VMEM((2, d // LANES, cap, LANES), F32),
                        pltpu.VMEM((2, d, f), BF16),
                        pltpu.VMEM((2, d, f), BF16),
                        pltpu.VMEM((2, f, d), BF16),
                        pltpu.SemaphoreType.DMA((2,))],
        compiler_params=_cparams(("arbitrary", "arbitrary")),
    )(idx.reshape(b_ * e_, 1, cap), gate.reshape(b_, e_, cap, 1), h2, wg, wu, wd)


COMBINE_CHUNK = 256


def _combine_kernel(bounds_ref, y_hbm, tok_ref, x1_ref, mod_ref, nf_ref, o_ref, y_buf, stage_y, stage_t, sem):
    b = pl.program_id(0)
    i = pl.program_id(1)
    n_b = pl.num_programs(0)
    n_e, cap = y_buf.shape[1], y_buf.shape[2]
    cur = b % 2

    def y_copy(bb, slot):
        return pltpu.make_async_copy(y_hbm.at[bb, pl.ds(0, n_e)], y_buf.at[slot], sem.at[slot])

    @pl.when(i == 0)
    def _():
        @pl.when(b == 0)
        def _():
            y_copy(0, 0).start()

        y_copy(b, cur).wait()

        @pl.when(b + 1 < n_b)
        def _():
            y_copy(b + 1, 1 - cur).start()

    tile = o_ref.shape[1]
    n_bounds = pl.num_programs(1) + 1
    per_chunk = COMBINE_CHUNK // GRANULE
    g_shift = GRANULE.bit_length() - 1
    assert per_chunk & (per_chunk - 1) == 0 and GRANULE == 1 << g_shift

    @pl.when((b == 0) & (i == 0))
    def _():
        stage_y[...] = jnp.zeros(stage_y.shape, BF16)
        stage_t[...] = jnp.full(stage_t.shape, -1, jnp.int32)

    o_ref[...] = jnp.zeros(o_ref.shape, F32)
    tok_lane = lax.broadcasted_iota(jnp.int32, (COMBINE_CHUNK, LANES), 1) + i * tile

    def flush():
        onehot_t = jnp.concatenate(
            [jnp.where(stage_t[...] == tok_lane + j * LANES, 1.0, 0.0).astype(BF16)
             for j in range(tile // LANES)], axis=1)
        o_ref[0] += lax.dot_general(onehot_t, stage_y[...], (((0,), (0,)), ((), ())),
                                    preferred_element_type=F32)
        stage_t[...] = jnp.full(stage_t.shape, -1, jnp.int32)

    def per_expert(e, k):
        base = (b * n_e + e) * n_bounds + i
        lo = bounds_ref[base]
        hi = bounds_ref[base + 1]
        g_lo = lax.shift_right_logical(lo, g_shift)
        g_hi = jnp.where(hi > lo, lax.shift_right_logical(hi + GRANULE - 1, g_shift), g_lo)

        def per_granule(g, k):
            src = pl.multiple_of(g * GRANULE, GRANULE)
            dst = pl.multiple_of((k & (per_chunk - 1)) * GRANULE, GRANULE)
            stage_y[pl.ds(dst, GRANULE), :] = y_buf[cur, e, pl.ds(src, GRANULE), :]
            stage_t[pl.ds(dst, GRANULE), :] = tok_ref[0, pl.ds(pl.multiple_of(e * cap + src, GRANULE), GRANULE), :]
            k = k + 1

            @pl.when((k & (per_chunk - 1)) == 0)
            def _():
                flush()

            return k

        return lax.fori_loop(g_lo, g_hi, per_granule, k)

    k = lax.fori_loop(0, n_e, per_expert, jnp.int32(0))

    @pl.when((k & (per_chunk - 1)) != 0)
    def _():
        flush()

    x = x1_ref[0] + mod_ref[0, 5:6, :] * o_ref[0]
    o_ref[0] = (x * lax.rsqrt(jnp.mean(x * x, axis=-1, keepdims=True) + EPS)) * nf_ref[...]


def _combine(y, tokrep, bounds, x1, mod, nf):
    b_, e_, cap, d = y.shape
    e_ -= 1
    t = x1.shape[1]
    tile = COMBINE_TILE
    assert cap % GRANULE == 0 and tile % LANES == 0 and bounds.shape == (b_, e_, t // tile + 1)
    grid_spec = pltpu.PrefetchScalarGridSpec(
        num_scalar_prefetch=1,
        grid=(b_, t // tile),
        in_specs=[pl.BlockSpec(memory_space=pl.ANY),
                  pl.BlockSpec((1, e_ * cap, LANES), lambda b, i, bnd: (b, 0, 0)),
                  pl.BlockSpec((1, tile, d), lambda b, i, bnd: (b, i, 0)),
                  pl.BlockSpec((1, N_MOD, d), lambda b, i, bnd: (b, 0, 0)),
                  pl.BlockSpec((1, d), lambda b, i, bnd: (0, 0))],
        out_specs=pl.BlockSpec((1, tile, d), lambda b, i, bnd: (b, i, 0)),
        scratch_shapes=[pltpu.VMEM((2, e_, cap, d), BF16),
                        pltpu.VMEM((COMBINE_CHUNK, d), BF16),
                        pltpu.VMEM((COMBINE_CHUNK, LANES), jnp.int32),
                        pltpu.SemaphoreType.DMA((2,))])
    return pl.pallas_call(
        _combine_kernel,
        grid_spec=grid_spec,
        out_shape=jax.ShapeDtypeStruct((b_, t, d), F32),
        compiler_params=_cparams(("arbitrary", "arbitrary")),
    )(bounds.reshape(-1), y, tokrep, x1, mod, nf)


def _gate_layout(gt, b_, t):
    return gt.reshape(N_HEADS, 4, b_, t // MLSTM_CHUNK, MLSTM_CHUNK)


def kernel(x, c, ctx, c_ctx, w_mod, b_mod, norm1, w_in, conv_q_w, conv_k_w, b_gates, head_g, pool_w,
           pool_scale, w_out, norm2, w_router, w_gate, w_up, w_down, norm_f):
    b_, t, d = x.shape
    tc = ctx.shape[1]
    depth = w_mod.shape[0]
    assert depth == 1
    l = 0
    qk_w = N_HEADS * DQK
    mw = N_HEADS * DV
    pool_wd = len(POOL_WINDOWS) * POOL_GC
    n_gate = 2 * 2 * N_HEADS
    k_off, v_off = 0, qk_w
    g_off = v_off + mw
    q_off = g_off + n_gate
    o_off = q_off + qk_w
    p_off = o_off + mw
    cap = EC_FACTOR * t // N_EXPERTS

    mod_rows = -(-(b_ + 1) // SUBLANES) * SUBLANES
    cs = jnp.zeros((mod_rows, d), F32).at[:b_].set(c).at[b_].set(c_ctx)
    mod_all = _modulation(cs, w_mod[l], b_mod[l])
    mod = mod_all[:b_].reshape(b_, N_MOD, d)
    mod_c = mod_all[b_:b_ + 1].reshape(1, N_MOD, d)

    wl = w_in[l]
    w_main = jnp.concatenate([wl[:, k_off:v_off], wl[:, v_off:g_off], wl[:, q_off:o_off],
                              wl[:, o_off:p_off], wl[:, p_off:]], axis=1).astype(BF16)
    perm = jnp.arange(n_gate).reshape(2, 2, N_HEADS).transpose(2, 0, 1).reshape(-1)
    w_g = wl[:, g_off:q_off][:, perm].T.astype(BF16)
    b_g = b_gates[l][perm].reshape(n_gate, 1)
    n1 = norm1[l].reshape(1, d)

    p, g = _inproj(x.reshape(b_ * t, d), mod, t, n1, w_main, w_g, b_g, 512)
    pc, gc = _inproj(ctx.reshape(b_ * tc, d), mod_c, b_ * tc, n1, w_main[:, :qk_w + mw], w_g, b_g,
                     min(512, b_ * tc))
    p = p.reshape(b_, t, -1)
    pc = pc.reshape(b_, tc, -1)

    wk = jnp.pad(conv_k_w[l], ((0, SUBLANES - QK_CONV), (0, 0)))
    wq = jnp.pad(conv_q_w[l], ((0, SUBLANES - QK_CONV), (0, 0)))
    ym = _mlstm(p, pc, _gate_layout(g, b_, t), _gate_layout(gc, b_, tc), wk, wq, head_g[l].reshape(1, mw))
    yp = _pool(p, pool_w[l].astype(BF16), pool_scale[l].reshape(1, pool_wd))

    wo = w_out[l].astype(BF16)
    wr = jnp.pad(w_router[l], ((0, 0), (0, LANES - N_EXPERTS))).astype(BF16)
    x1, h2, lg = _outproj(ym.reshape(b_ * t, mw), yp.reshape(b_ * t, pool_wd), wo[:mw], wo[mw:],
                          x.reshape(b_ * t, d), mod, t, norm2[l].reshape(1, d), wr, 512)

    idx, gate, tokrep, bounds = _route(lg.reshape(b_, t, LANES), cap)

    y = _ffn(idx, gate, h2, w_gate[l], w_up[l], w_down[l])
    return _combine(y, tokrep, bounds[:, :, :t // COMBINE_TILE + 1], x1.reshape(b_, t, d), mod,
                    norm_f.reshape(1, d))
```

```python
import functools

import jax
import jax.numpy as jnp
from jax import lax
from jax.experimental import pallas as pl
from jax.experimental.pallas import tpu as pltpu

F32 = jnp.float32
BF16 = jnp.bfloat16

N_HEADS = 4
DQK = 128
DV = 256
QK_CONV = 5
POOL_WINDOWS = (2, 4, 8, 16)
POOL_GC = 256
GRID_W = 64
GRID_SHIFT = GRID_W.bit_length() - 1
N_EXPERTS = 16
EC_FACTOR = 2
N_MOD = 6
EPS = 1e-6

LANES = 128
SUBLANES = 8
MLSTM_CHUNK = 128
VMEM_LIMIT = 56 * 1024 * 1024


def _cparams(sem):
    return pltpu.CompilerParams(dimension_semantics=sem, vmem_limit_bytes=VMEM_LIMIT)


def _sigmoid(x):
    return 1.0 / (1.0 + jnp.exp(-x))


def _silu(x):
    return x * _sigmoid(x)


def _log_sigmoid(x):
    return jnp.minimum(x, 0.0) - jnp.log(1.0 + jnp.exp(-jnp.abs(x)))


def _mod_kernel(c_ref, w_ref, b_ref, o_ref):
    s = _silu(c_ref[...]).astype(BF16)
    o_ref[...] = jnp.dot(s, w_ref[...].astype(BF16), preferred_element_type=F32) + b_ref[...]


def _modulation(cs, w_mod, b_mod):
    rows, d = cs.shape
    n = w_mod.shape[1]
    tn = 1024
    return pl.pallas_call(
        _mod_kernel,
        grid=(n // tn,),
        in_specs=[pl.BlockSpec((rows, d), lambda j: (0, 0)),
                  pl.BlockSpec((d, tn), lambda j: (0, j)),
                  pl.BlockSpec((1, tn), lambda j: (0, j))],
        out_specs=pl.BlockSpec((rows, tn), lambda j: (0, j)),
        out_shape=jax.ShapeDtypeStruct((rows, n), F32),
        compiler_params=_cparams(("arbitrary",)),
    )(cs, w_mod, b_mod.reshape(1, n))


INPROJ_COLS = 512


def _inproj_kernel(x_ref, mod_ref, n1_ref, w_ref, wg_ref, bg_ref, p_ref, gt_ref):
    tm = x_ref.shape[0]
    half = tm // 2
    n_cols = p_ref.shape[1]
    scale = n1_ref[...] * (1.0 + mod_ref[0, 1:2, :])
    for r0 in (0, half):
        rows = slice(r0, r0 + half)
        x = x_ref[rows, :]
        y = x * lax.rsqrt(jnp.mean(x * x, axis=-1, keepdims=True) + EPS)
        hb = (y * scale + mod_ref[0, 0:1, :]).astype(BF16)
        gt_ref[:, rows] = lax.dot_general(wg_ref[...], hb, (((1,), (1,)), ((), ())),
                                          preferred_element_type=F32) + bg_ref[...]
        for c0 in range(0, n_cols, INPROJ_COLS):
            p_ref[rows, c0:c0 + INPROJ_COLS] = jnp.dot(
                hb, w_ref[:, c0:c0 + INPROJ_COLS], preferred_element_type=F32).astype(BF16)


def _inproj(x2, mod, rows_per_mod, n1, w, wg, bg, tm):
    m, d = x2.shape
    n_cols = w.shape[1]
    n_g = wg.shape[0]
    tiles_per_mod = rows_per_mod // tm
    assert n_cols % INPROJ_COLS == 0 and tm % (2 * LANES) == 0
    return pl.pallas_call(
        _inproj_kernel,
        grid=(m // tm,),
        in_specs=[pl.BlockSpec((tm, d), lambda i: (i, 0)),
                  pl.BlockSpec((1, N_MOD, d), lambda i: (i // tiles_per_mod, 0, 0)),
                  pl.BlockSpec((1, d), lambda i: (0, 0)),
                  pl.BlockSpec((d, n_cols), lambda i: (0, 0), pipeline_mode=pl.Buffered(1)),
                  pl.BlockSpec((n_g, d), lambda i: (0, 0)),
                  pl.BlockSpec((n_g, 1), lambda i: (0, 0))],
        out_specs=[pl.BlockSpec((tm, n_cols), lambda i: (i, 0)),
                   pl.BlockSpec((n_g, tm), lambda i: (0, i))],
        out_shape=[jax.ShapeDtypeStruct((m, n_cols), BF16),
                   jax.ShapeDtypeStruct((n_g, m), F32)],
        compiler_params=_cparams(("arbitrary",)),
    )(x2, mod, n1, w, wg, bg)


def _scan_lanes(x, op, fill, reverse):
    lane = lax.broadcasted_iota(jnp.int32, x.shape, 1)
    k = 1
    while k < LANES:
        if reverse:
            x = op(x, jnp.where(lane < LANES - k, pltpu.roll(x, LANES - k, 1), fill))
        else:
            x = op(x, jnp.where(lane >= k, pltpu.roll(x, k, 1), fill))
        k *= 2
    return x


G_B, G_R, G_W, G_BEND, G_MLOC, G_RMAX, G_ROWS = 0, 1, 2, 3, 4, 5, 6


def _mlstm_kernel(kc_ref, vc_ref, gc_ref, k_ref, v_ref, q_ref, o_ref, g_ref, wk_ref, wq_ref, hg_ref,
                  out_ref,
                  cpad, ktt, qt, kctt, vext, vcext, gs, gsc, cf, cb, hf, hb):
    L = MLSTM_CHUNK
    t_lat = k_ref.shape[1]
    t_ctx = kc_ref.shape[1]
    nc = t_lat // L
    ncc = t_ctx // L
    hps = k_ref.shape[2] // DQK
    pad = SUBLANES
    half = QK_CONV // 2

    def conv_silu(src, w_ref, dst, t, scale, transposed):
        cpad[0:pad, :] = jnp.zeros((pad, LANES), F32)
        cpad[pad:pad + t, :] = src.astype(F32)
        cpad[pad + t:pad + t + pad, :] = jnp.zeros((pad, LANES), F32)
        for r in range(t // L):
            base = pad + r * L - half
            acc = w_ref[0:1, :] * cpad[base:base + L, :]
            for j in range(1, QK_CONV):
                acc = acc + w_ref[j:j + 1, :] * cpad[base + j:base + j + L, :]
            y = _silu(acc)
            if scale is not None:
                y = y * scale
            dst[r] = (y.T if transposed else y).astype(BF16)

    def fill_vext(dst, src, t):
        dst[:, 0:DV] = src
        dst[:, DV:DV + LANES] = jnp.ones((t, LANES), BF16)

    for hh in range(hps):
        qk_cols = slice(hh * DQK, (hh + 1) * DQK)
        v_cols = slice(hh * DV, (hh + 1) * DV)
        conv_silu(kc_ref[0, :, qk_cols], wk_ref.at[:, qk_cols], kctt.at[hh], t_ctx, DQK ** -0.5, True)
        conv_silu(k_ref[0, :, qk_cols], wk_ref.at[:, qk_cols], ktt.at[hh], t_lat, DQK ** -0.5, True)
        conv_silu(q_ref[0, :, qk_cols], wq_ref.at[:, qk_cols], qt.at[hh], t_lat, None, False)
        fill_vext(vcext.at[hh], vc_ref[0, :, v_cols], t_ctx)
        fill_vext(vext.at[hh], v_ref[0, :, v_cols], t_lat)

    def gate_prep(garr, dst):
        for d in range(2):
            li = garr[2 * d]
            lf = _log_sigmoid(garr[2 * d + 1])
            b = _scan_lanes(lf, jnp.add, 0.0, reverse=(d == 1))
            b_end = b[:, LANES - 1:LANES] if d == 0 else b[:, 0:1]
            r = li - b
            a = b_end + r
            m_loc = jnp.max(a, axis=1, keepdims=True)
            dst[G_ROWS * d + G_B] = b
            dst[G_ROWS * d + G_R] = r
            dst[G_ROWS * d + G_W] = jnp.exp(a - m_loc)
            dst[G_ROWS * d + G_BEND] = jnp.broadcast_to(b_end, b.shape)
            dst[G_ROWS * d + G_MLOC] = jnp.broadcast_to(m_loc, b.shape)
            dst[G_ROWS * d + G_RMAX] = _scan_lanes(r, jnp.maximum, -jnp.inf, reverse=(d == 1))

    for hh in range(hps):
        gate_prep(gc_ref[hh, :, 0], gsc.at[hh])
        gate_prep(g_ref[hh, :, 0], gs.at[hh])

    cf[...] = jnp.zeros(cf.shape, F32)
    cb[...] = jnp.zeros(cb.shape, F32)

    row_i = lax.broadcasted_iota(jnp.int32, (L, L), 0)
    col_i = lax.broadcasted_iota(jnp.int32, (L, L), 1)
    visible = (col_i <= row_i, col_i >= row_i)

    def per_token(row):
        return jnp.broadcast_to(row, (L, L)).T

    def step(hh, c, d, g_all, kt_all, v_all, q_all, c_all, m_prev):
        off = c * L if isinstance(c, int) else pl.multiple_of(c * L, L)
        g_scr, kt_scr, v_scr, c_scr = g_all.at[hh], kt_all.at[hh], v_all.at[hh], c_all.at[hh]
        q_scr = None if q_all is None else q_all.at[hh]
        g0 = G_ROWS * d
        kt_c = kt_scr[c]
        v_c = v_scr[pl.ds(off, L), :]
        b_end = g_scr[g0 + G_BEND, pl.ds(c, 1), 0:1]
        m_loc = g_scr[g0 + G_MLOC, pl.ds(c, 1), 0:1]
        c_prev = c_scr[...]
        h = None
        if q_scr is not None:
            q_c = q_scr[c]
            r_row = g_scr[g0 + G_R, pl.ds(c, 1), :]
            u = jnp.maximum(per_token(g_scr[g0 + G_RMAX, pl.ds(c, 1), :]), m_prev)
            b_t = per_token(g_scr[g0 + G_B, pl.ds(c, 1), :])
            qk = jnp.dot(q_c, kt_c, preferred_element_type=F32)
            s = (qk * jnp.exp(jnp.where(visible[d], r_row - u, -jnp.inf))).astype(BF16)
            wi = jnp.exp(m_prev - u)
            inter = jnp.dot(q_c, c_prev.astype(BF16), preferred_element_type=F32)
            res = jnp.dot(s, v_c, preferred_element_type=F32) \
                + jnp.concatenate([wi] * (c_prev.shape[1] // L), axis=1) * inter
            inv = 1.0 / jnp.maximum(jnp.abs(res[:, DV:DV + L]), jnp.exp(-(b_t + u)))
            h = res[:, 0:DV] * jnp.concatenate([inv] * (DV // L), axis=1)
        kwt = (kt_c.astype(F32) * g_scr[g0 + G_W, pl.ds(c, 1), :]).astype(BF16)
        c_loc = jnp.dot(kwt, v_c, preferred_element_type=F32)
        m_new = jnp.maximum(b_end + m_prev, m_loc)
        sp = jnp.exp(b_end + m_prev - m_new)
        sl = jnp.exp(m_loc - m_new)
        c_scr[...] = sp * c_prev + sl * c_loc
        return m_new, h

    def finish(hh, c, h):
        off = pl.multiple_of(c * L, L)
        cols = slice(hh * DV, (hh + 1) * DV)
        h = h * lax.rsqrt(jnp.mean(h * h, axis=-1, keepdims=True) + EPS)
        y = (h * hg_ref[:, cols]) * _sigmoid(o_ref[0, pl.ds(off, L), cols].astype(F32))
        out_ref[0, pl.ds(off, L), cols] = y.astype(BF16)

    ms = [jnp.zeros((1, 1), F32)] * (2 * hps)
    for i in range(ncc):
        for hh in range(hps):
            ms[2 * hh], _ = step(hh, i, 0, gsc, kctt, vcext, None, cf, ms[2 * hh])
            ms[2 * hh + 1], _ = step(hh, ncc - 1 - i, 1, gsc, kctt, vcext, None, cb, ms[2 * hh + 1])

    assert nc % 2 == 0

    def scan_step(i, carry, second_half):
        c_f, c_b = i, nc - 1 - i
        off_f, off_b = pl.multiple_of(c_f * L, L), pl.multiple_of(c_b * L, L)
        out = []
        for hh in range(hps):
            m_f, h_f = step(hh, c_f, 0, gs, ktt, vext, qt, cf, carry[2 * hh])
            m_b, h_b = step(hh, c_b, 1, gs, ktt, vext, qt, cb, carry[2 * hh + 1])
            if second_half:
                finish(hh, c_f, h_f + hb[hh, pl.ds(off_f, L), :])
                finish(hh, c_b, h_b + hf[hh, pl.ds(off_b, L), :])
            else:
                hf[hh, pl.ds(off_f, L), :] = h_f
                hb[hh, pl.ds(off_b, L), :] = h_b
            out += [m_f, m_b]
        return tuple(out)

    carry = lax.fori_loop(0, nc // 2, functools.partial(scan_step, second_half=False), tuple(ms), unroll=2)
    lax.fori_loop(nc // 2, nc, functools.partial(scan_step, second_half=True), carry, unroll=2)


MLSTM_HEADS_PER_STEP = 2


def _mlstm(p, pc, g4, gc4, wk, wq, head_g):
    b_, t, _ = p.shape
    tc = pc.shape[1]
    L = MLSTM_CHUNK
    nc, ncc = t // L, tc // L
    ext = DV + LANES
    hps = MLSTM_HEADS_PER_STEP
    assert N_HEADS % hps == 0
    v_blk = N_HEADS * DQK // (hps * DV)
    q_blk = (N_HEADS * DQK + N_HEADS * DV) // (hps * DQK)
    o_blk = (2 * N_HEADS * DQK + N_HEADS * DV) // (hps * DV)
    return pl.pallas_call(
        _mlstm_kernel,
        grid=(b_, N_HEADS // hps),
        in_specs=[
            pl.BlockSpec((1, tc, hps * DQK), lambda b, h: (b, 0, h)),
            pl.BlockSpec((1, tc, hps * DV), lambda b, h: (b, 0, v_blk + h)),
            pl.BlockSpec((hps, 4, 1, ncc, L), lambda b, h: (h, 0, b, 0, 0)),
            pl.BlockSpec((1, t, hps * DQK), lambda b, h: (b, 0, h)),
            pl.BlockSpec((1, t, hps * DV), lambda b, h: (b, 0, v_blk + h)),
            pl.BlockSpec((1, t, hps * DQK), lambda b, h: (b, 0, q_blk + h)),
            pl.BlockSpec((1, t, hps * DV), lambda b, h: (b, 0, o_blk + h)),
            pl.BlockSpec((hps, 4, 1, nc, L), lambda b, h: (h, 0, b, 0, 0)),
            pl.BlockSpec((SUBLANES, hps * DQK), lambda b, h: (0, h)),
            pl.BlockSpec((SUBLANES, hps * DQK), lambda b, h: (0, h)),
            pl.BlockSpec((1, hps * DV), lambda b, h: (0, h)),
        ],
        out_specs=pl.BlockSpec((1, t, hps * DV), lambda b, h: (b, 0, h)),
        out_shape=jax.ShapeDtypeStruct((b_, t, N_HEADS * DV), BF16),
        scratch_shapes=[
            pltpu.VMEM((t + 2 * SUBLANES, LANES), F32),
            pltpu.VMEM((hps, nc, DQK, L), BF16),
            pltpu.VMEM((hps, nc, L, DQK), BF16),
            pltpu.VMEM((hps, ncc, DQK, L), BF16),
            pltpu.VMEM((hps, t, ext), BF16),
            pltpu.VMEM((hps, tc, ext), BF16),
            pltpu.VMEM((hps, 2 * G_ROWS, nc, L), F32),
            pltpu.VMEM((hps, 2 * G_ROWS, ncc, L), F32),
            pltpu.VMEM((hps, DQK, ext), F32),
            pltpu.VMEM((hps, DQK, ext), F32),
            pltpu.VMEM((hps, t, DV), F32),
            pltpu.VMEM((hps, t, DV), F32),
        ],
        compiler_params=_cparams(("arbitrary", "arbitrary")),
    )(pc, pc, gc4, p, p, p, p, g4, wk, wq, head_g)


def _pool_kernel(u_ref, pw_ref, ps_ref, out_ref, spad):
    t = u_ref.shape[1]
    rows = t // GRID_W
    blk = 256
    halo = (max(POOL_WINDOWS) // 2) * GRID_W
    spad[0:halo, :] = jnp.zeros((halo, POOL_GC), F32)
    spad[halo + t:halo + t + halo, :] = jnp.zeros((halo, POOL_GC), F32)
    ti = lax.broadcasted_iota(jnp.int32, (blk, blk), 0)
    tj = lax.broadcasted_iota(jnp.int32, (blk, blk), 1)
    same_row = (ti >> GRID_SHIFT) == (tj >> GRID_SHIFT)
    diff = tj - ti
    tok = lax.broadcasted_iota(jnp.int32, (blk, 1), 0)
    for gi, w in enumerate(POOL_WINDOWS):
        hw = w // 2
        c0, c1 = gi * POOL_GC, (gi + 1) * POOL_GC
        band = jnp.where(same_row & (diff >= -hw) & (diff <= hw - 1), 1.0, 0.0).astype(BF16)
        for r in range(t // blk):
            spad[halo + r * blk:halo + (r + 1) * blk, :] = jnp.dot(
                band, u_ref[0, r * blk:(r + 1) * blk, c0:c1], preferred_element_type=F32)
        for r in range(t // blk):
            base = halo + r * blk
            acc = spad[base - hw * GRID_W:base - hw * GRID_W + blk, :]
            for j in range(-hw + 1, hw):
                acc = acc + spad[base + j * GRID_W:base + j * GRID_W + blk, :]
            tt = tok + r * blk
            gr = tt >> GRID_SHIFT
            gc = tt & (GRID_W - 1)
            cr = jnp.minimum(gr + hw, rows) - jnp.maximum(gr - hw, 0)
            cc = jnp.minimum(gc + hw, GRID_W) - jnp.maximum(gc - hw, 0)
            cnt = (cr * cc).astype(F32)
            d = acc / cnt - u_ref[0, r * blk:(r + 1) * blk, c0:c1].astype(F32)
            y = jnp.dot(d.astype(BF16), pw_ref[gi], preferred_element_type=F32) * ps_ref[:, c0:c1]
            out_ref[0, r * blk:(r + 1) * blk, c0:c1] = y.astype(BF16)


def _pool(p, pool_w, pool_scale):
    b_, t, _ = p.shape
    pw = len(POOL_WINDOWS) * POOL_GC
    halo = (max(POOL_WINDOWS) // 2) * GRID_W
    return pl.pallas_call(
        _pool_kernel,
        grid=(b_,),
        in_specs=[pl.BlockSpec((1, t, pw), lambda b: (b, 0, 3)),
                  pl.BlockSpec((len(POOL_WINDOWS), POOL_GC, POOL_GC), lambda b: (0, 0, 0)),
                  pl.BlockSpec((1, pw), lambda b: (0, 0))],
        out_specs=pl.BlockSpec((1, t, pw), lambda b: (b, 0, 0)),
        out_shape=jax.ShapeDtypeStruct((b_, t, pw), BF16),
        scratch_shapes=[pltpu.VMEM((t + 2 * halo, POOL_GC), F32)],
        compiler_params=_cparams(("arbitrary",)),
    )(p, pool_w, pool_scale)


def _outproj_kernel(ym_ref, yp_ref, wm_ref, wp_ref, x_ref, mod_ref, n2_ref, wr_ref,
                    x1_ref, h2_ref, lg_ref):
    tm = x_ref.shape[0]
    half = tm // 2
    scale2 = n2_ref[...] * (1.0 + mod_ref[0, 4:5, :])
    for r0 in (0, half):
        rows = slice(r0, r0 + half)
        mix = (jnp.dot(ym_ref[rows, :], wm_ref[...], preferred_element_type=F32)
               + jnp.dot(yp_ref[rows, :], wp_ref[...], preferred_element_type=F32))
        x1 = x_ref[rows, :] + mod_ref[0, 2:3, :] * mix
        x1_ref[rows, :] = x1
        y = x1 * lax.rsqrt(jnp.mean(x1 * x1, axis=-1, keepdims=True) + EPS)
        h2 = y * scale2 + mod_ref[0, 3:4, :]
        g0, g1 = r0 // SUBLANES, (r0 + half) // SUBLANES
        for j in range(h2_ref.shape[1]):
            h2_ref[g0:g1, j, :, :] = h2[:, j * LANES:(j + 1) * LANES].reshape(g1 - g0, SUBLANES, LANES)
        lg_ref[rows, :] = jnp.dot(h2.astype(BF16), wr_ref[...], preferred_element_type=F32)


def _outproj(ym, yp, wm, wp, x2, mod, rows_per_mod, n2, wr, tm):
    m, d = x2.shape
    km = ym.shape[1]
    kp = yp.shape[1]
    tiles_per_mod = rows_per_mod // tm
    return pl.pallas_call(
        _outproj_kernel,
        grid=(m // tm,),
        in_specs=[pl.BlockSpec((tm, km), lambda i: (i, 0)),
                  pl.BlockSpec((tm, kp), lambda i: (i, 0)),
                  pl.BlockSpec((km, d), lambda i: (0, 0), pipeline_mode=pl.Buffered(1)),
                  pl.BlockSpec((kp, d), lambda i: (0, 0), pipeline_mode=pl.Buffered(1)),
                  pl.BlockSpec((tm, d), lambda i: (i, 0)),
                  pl.BlockSpec((1, N_MOD, d), lambda i: (i // tiles_per_mod, 0, 0)),
                  pl.BlockSpec((1, d), lambda i: (0, 0)),
                  pl.BlockSpec((d, LANES), lambda i: (0, 0))],
        out_specs=[pl.BlockSpec((tm, d), lambda i: (i, 0)),
                   pl.BlockSpec((tm // SUBLANES, d // LANES, SUBLANES, LANES), lambda i: (i, 0, 0, 0)),
                   pl.BlockSpec((tm, LANES), lambda i: (i, 0))],
        out_shape=[jax.ShapeDtypeStruct((m, d), F32),
                   jax.ShapeDtypeStruct((m // SUBLANES, d // LANES, SUBLANES, LANES), F32),
                   jax.ShapeDtypeStruct((m, LANES), F32)],
        compiler_params=_cparams(("arbitrary",)),
    )(ym, yp, wm, wp, x2, mod, n2, wr)


def _scan_lanes_i32(x, exclusive_of=None):
    n = x.shape[1]
    lane = lax.broadcasted_iota(jnp.int32, x.shape, 1)
    k = 1
    while k < n:
        x = x + jnp.where(lane >= k, pltpu.roll(x, k, 1), 0)
        k *= 2
    return x


COMBINE_TILE = 256
GRANULE = 2 * SUBLANES


def _route_kernel(lg_ref, idx_ref, gate_ref, tokrep_ref, bounds_ref, afft):
    t = lg_ref.shape[1]
    cap = idx_ref.shape[2]
    lane = lax.broadcasted_iota(jnp.int32, (t, LANES), 1)
    lg = jnp.where(lane < N_EXPERTS, lg_ref[0], -jnp.inf)
    ex = jnp.exp(lg - jnp.max(lg, axis=1, keepdims=True))
    aff = ex / jnp.sum(ex, axis=1, keepdims=True)
    for r in range(t // LANES):
        afft[:, r * LANES:(r + 1) * LANES] = aff[r * LANES:(r + 1) * LANES, :].T
    aff_t = afft[0:N_EXPERTS, :]

    def count(mask):
        return jnp.sum(jnp.where(mask, 1.0, 0.0), axis=1, keepdims=True).astype(jnp.int32)

    def bit_step(i, thr_bits):
        cand = thr_bits | (jnp.int32(1) << (30 - i))
        cand_f = lax.bitcast_convert_type(cand, F32)
        return jnp.where(count(aff_t >= cand_f) >= cap, cand, thr_bits)

    thr_bits = lax.fori_loop(0, 31, bit_step, jnp.zeros((N_EXPERTS, 1), jnp.int32))
    thr = lax.bitcast_convert_type(thr_bits, F32)
    gt = aff_t > thr
    eq = aff_t == thr
    need = cap - count(gt)
    eq_i = jnp.where(eq, 1, 0)
    eq_rank = _scan_lanes_i32(eq_i) - eq_i
    sel = gt | (eq & (eq_rank < need))
    sel_i = jnp.where(sel, 1, 0)
    pos = _scan_lanes_i32(sel_i) - sel_i
    key = jnp.where(sel, pos, -1)

    a_hi = aff.astype(BF16).astype(F32)
    a_mid = (aff - a_hi).astype(BF16).astype(F32)
    a_lo = (aff - a_hi - a_mid).astype(BF16).astype(F32)
    tok = lax.broadcasted_iota(jnp.int32, (t, LANES), 0)
    vals = (a_hi + pltpu.roll(a_mid, N_EXPERTS, 1) + pltpu.roll(a_lo, 2 * N_EXPERTS, 1)
            + jnp.where(lane == 3 * N_EXPERTS, (tok >> GRID_SHIFT).astype(F32), 0.0)
            + jnp.where(lane == 3 * N_EXPERTS + 1, (tok & (GRID_W - 1)).astype(F32), 0.0)).astype(BF16)
    slot = lax.broadcasted_iota(jnp.int32, (cap, 1), 0)
    for e in range(N_EXPERTS):
        onehot = jnp.where(slot == key[e:e + 1, :], 1.0, 0.0).astype(BF16)
        res = jnp.dot(onehot, vals, preferred_element_type=F32)
        rt = jnp.concatenate([res[r * LANES:(r + 1) * LANES, :].T for r in range(cap // LANES)], axis=1)
        gate_ref[0, e:e + 1, :] = (rt[e:e + 1, :] + rt[N_EXPERTS + e:N_EXPERTS + e + 1, :]) \
            + rt[2 * N_EXPERTS + e:2 * N_EXPERTS + e + 1, :]
        tokf = rt[3 * N_EXPERTS:3 * N_EXPERTS + 1, :] * float(GRID_W) + rt[3 * N_EXPERTS + 1:3 * N_EXPERTS + 2, :]
        idx_ref[0, e:e + 1, :] = tokf.astype(jnp.int32)
        tok_col = res[:, 3 * N_EXPERTS:3 * N_EXPERTS + 1] * float(GRID_W) \
            + res[:, 3 * N_EXPERTS + 1:3 * N_EXPERTS + 2]
        tokrep_ref[0, e * cap:(e + 1) * cap, :] = jnp.broadcast_to(tok_col, (cap, LANES)).astype(jnp.int32)

    n_tiles = t // COMBINE_TILE
    lane_b = lax.broadcasted_iota(jnp.int32, (N_EXPERTS, LANES), 1)
    bounds = jnp.where(lane_b >= n_tiles, cap, 0)
    for i in range(n_tiles):
        bounds = jnp.where(lane_b == i, pos[:, i * COMBINE_TILE:i * COMBINE_TILE + 1], bounds)
    bounds_ref[0] = bounds


def _route(lg3, cap):
    b_, t, _ = lg3.shape
    return pl.pallas_call(
        _route_kernel,
        grid=(b_,),
        in_specs=[pl.BlockSpec((1, t, LANES), lambda b: (b, 0, 0))],
        out_specs=[pl.BlockSpec((1, N_EXPERTS, cap), lambda b: (b, 0, 0)),
                   pl.BlockSpec((1, N_EXPERTS, cap), lambda b: (b, 0, 0)),
                   pl.BlockSpec((1, N_EXPERTS * cap, LANES), lambda b: (b, 0, 0)),
                   pl.BlockSpec((1, N_EXPERTS, LANES), lambda b: (b, 0, 0))],
        out_shape=[jax.ShapeDtypeStruct((b_, N_EXPERTS, cap), jnp.int32),
                   jax.ShapeDtypeStruct((b_, N_EXPERTS, cap), F32),
                   jax.ShapeDtypeStruct((b_, N_EXPERTS * cap, LANES), jnp.int32),
                   jax.ShapeDtypeStruct((b_, N_EXPERTS, LANES), jnp.int32)],
        scratch_shapes=[pltpu.VMEM((LANES, t), F32)],
        compiler_params=_cparams(("arbitrary",)),
    )(lg3)


def _ffn_kernel(idx_ref, gate_ref, h2_hbm, wg_ref, wu_ref, wd_ref, y_ref,
                xs_buf, wg_s, wu_s, wd_s, sem):
    p = pl.program_id(0)
    b = pl.program_id(1)
    n_e = pl.num_programs(0) - 1
    n_b = pl.num_programs(1)
    k_chunks, cap = xs_buf.shape[1], xs_buf.shape[2]
    groups = h2_hbm.shape[0] // n_b
    step = p * n_b + b
    cur = step % 2
    nxt = 1 - cur
    last_b = b == n_b - 1
    b_next = jnp.where(last_b, 0, b + 1)
    p_next = jnp.where(last_b, p + 1, p)

    def issue_gather():
        def issue(r, carry):
            tok = idx_ref[0, 0, r]
            pltpu.make_async_copy(h2_hbm.at[b_next * groups + (tok >> 3), :, tok & (SUBLANES - 1), :],
                                  xs_buf.at[nxt, :, r, :], sem.at[nxt]).start()
            return carry

        lax.fori_loop(0, cap, issue, 0, unroll=True)

    def cast_weights():
        ws = p % 2
        rows_in = wg_ref.shape[1]
        rows_mid = wd_ref.shape[1]
        wg_s[ws, pl.ds(pl.multiple_of(b * rows_in, rows_in), rows_in), :] = wg_ref[0].astype(BF16)
        wu_s[ws, pl.ds(pl.multiple_of(b * rows_in, rows_in), rows_in), :] = wu_ref[0].astype(BF16)
        wd_s[ws, pl.ds(pl.multiple_of(b * rows_mid, rows_mid), rows_mid), :] = wd_ref[0].astype(BF16)

    def run_expert():
        pltpu.make_async_copy(xs_buf.at[cur], xs_buf.at[cur], sem.at[cur]).wait()
        ws = (p - 1) % 2
        xs = jnp.concatenate([xs_buf[cur, j].astype(BF16) for j in range(k_chunks)], axis=1)
        a = _silu(jnp.dot(xs, wg_s[ws], preferred_element_type=F32)) \
            * jnp.dot(xs, wu_s[ws], preferred_element_type=F32)
        y = jnp.dot(a.astype(BF16), wd_s[ws], preferred_element_type=F32)
        g_row = gate_ref[0]
        g_col = jnp.concatenate(
            [jnp.broadcast_to(g_row[:, k * LANES:(k + 1) * LANES], (LANES, LANES)).T
             for k in range(cap // LANES)], axis=0)
        y_ref[0, 0] = (y * jnp.concatenate([g_col] * (y.shape[1] // LANES), axis=1)).astype(BF16)

    @pl.when((p_next >= 1) & (p_next <= n_e))
    def _():
        issue_gather()

    @pl.when(p == 0)
    def _():
        cast_weights()
        y_ref[0, 0] = jnp.zeros(y_ref.shape[2:], BF16)

    @pl.when((p >= 1) & (p < n_e))
    def _():
        cast_weights()
        run_expert()

    @pl.when(p == n_e)
    def _():
        run_expert()


def _ffn(idx, gate, h2, wg, wu, wd):
    e_, d, f = wg.shape
    b_, _, cap = idx.shape
    assert h2.shape[1] * h2.shape[3] == d and h2.shape[2] == SUBLANES and h2.shape[0] % b_ == 0
    assert d % b_ == 0 and f % b_ == 0
    rows_in, rows_mid = d // b_, f // b_
    assert rows_in % (2 * SUBLANES) == 0 and rows_mid % (2 * SUBLANES) == 0

    def next_tokens(p, b):
        last_b = b == b_ - 1
        b_next = jnp.where(last_b, 0, b + 1)
        e_next = jnp.clip(jnp.where(last_b, p, p - 1), 0, e_ - 1)
        return (b_next * e_ + e_next, 0, 0)

    def weight_piece(p, b):
        return (jnp.minimum(p, e_ - 1), b, 0)

    return pl.pallas_call(
        _ffn_kernel,
        grid=(e_ + 1, b_),
        in_specs=[pl.BlockSpec((1, 1, cap), next_tokens, memory_space=pltpu.SMEM),
                  pl.BlockSpec((1, 1, cap), lambda p, b: (b * e_ + jnp.maximum(p - 1, 0), 0, 0)),
                  pl.BlockSpec(memory_space=pl.ANY),
                  pl.BlockSpec((1, rows_in, f), weight_piece),
                  pl.BlockSpec((1, rows_in, f), weight_piece),
                  pl.BlockSpec((1, rows_mid, d), weight_piece)],
        out_specs=pl.BlockSpec((1, 1, cap, d), lambda p, b: (b, (p + e_) % (e_ + 1), 0, 0)),
        out_shape=jax.ShapeDtypeStruct((b_, e_ + 1, cap, d), BF16),
        scratch_shapes=[pltpu.VMEM((2, d // LANES, cap, LANES), F32),
                        pltpu.VMEM((2, d, f), BF16),
                        pltpu.VMEM((2, d, f), BF16),
                        pltpu.VMEM((2, f, d), BF16),
                        pltpu.SemaphoreType.DMA((2,))],
        compiler_params=_cparams(("arbitrary", "arbitrary")),
    )(idx.reshape(b_ * e_, 1, cap), gate.reshape(b_ * e_, 1, cap), h2, wg, wu, wd)


COMBINE_CHUNK = 256


def _combine_kernel(bounds_ref, y_hbm, tok_ref, x1_ref, mod_ref, nf_ref, o_ref, y_buf, stage_y, stage_t, sem):
    b = pl.program_id(0)
    i = pl.program_id(1)
    n_b = pl.num_programs(0)
    n_e, cap = y_buf.shape[1], y_buf.shape[2]
    cur = b % 2

    def y_copy(bb, slot):
        return pltpu.make_async_copy(y_hbm.at[bb, pl.ds(0, n_e)], y_buf.at[slot], sem.at[slot])

    @pl.when(i == 0)
    def _():
        @pl.when(b == 0)
        def _():
            y_copy(0, 0).start()

        y_copy(b, cur).wait()

        @pl.when(b + 1 < n_b)
        def _():
            y_copy(b + 1, 1 - cur).start()

    tile = o_ref.shape[1]
    n_bounds = pl.num_programs(1) + 1
    per_chunk = COMBINE_CHUNK // GRANULE
    g_shift = GRANULE.bit_length() - 1
    assert per_chunk & (per_chunk - 1) == 0 and GRANULE == 1 << g_shift

    @pl.when((b == 0) & (i == 0))
    def _():
        stage_y[...] = jnp.zeros(stage_y.shape, BF16)
        stage_t[...] = jnp.full(stage_t.shape, -1, jnp.int32)

    o_ref[...] = jnp.zeros(o_ref.shape, F32)
    tok_lane = lax.broadcasted_iota(jnp.int32, (COMBINE_CHUNK, LANES), 1) + i * tile

    def flush():
        onehot_t = jnp.concatenate(
            [jnp.where(stage_t[...] == tok_lane + j * LANES, 1.0, 0.0).astype(BF16)
             for j in range(tile // LANES)], axis=1)
        o_ref[0] += lax.dot_general(onehot_t, stage_y[...], (((0,), (0,)), ((), ())),
                                    preferred_element_type=F32)
        stage_t[...] = jnp.full(stage_t.shape, -1, jnp.int32)

    def per_expert(e, k):
        base = (b * n_e + e) * n_bounds + i
        lo = bounds_ref[base]
        hi = bounds_ref[base + 1]
        g_lo = lax.shift_right_logical(lo, g_shift)
        g_hi = jnp.where(hi > lo, lax.shift_right_logical(hi + GRANULE - 1, g_shift), g_lo)

        def per_granule(g, k):
            src = pl.multiple_of(g * GRANULE, GRANULE)
            dst = pl.multiple_of((k & (per_chunk - 1)) * GRANULE, GRANULE)
            stage_y[pl.ds(dst, GRANULE), :] = y_buf[cur, e, pl.ds(src, GRANULE), :]
            stage_t[pl.ds(dst, GRANULE), :] = tok_ref[0, pl.ds(pl.multiple_of(e * cap + src, GRANULE), GRANULE), :]
            k = k + 1

            @pl.when((k & (per_chunk - 1)) == 0)
            def _():
                flush()

            return k

        return lax.fori_loop(g_lo, g_hi, per_granule, k)

    k = lax.fori_loop(0, n_e, per_expert, jnp.int32(0))

    @pl.when((k & (per_chunk - 1)) != 0)
    def _():
        flush()

    x = x1_ref[0] + mod_ref[0, 5:6, :] * o_ref[0]
    o_ref[0] = (x * lax.rsqrt(jnp.mean(x * x, axis=-1, keepdims=True) + EPS)) * nf_ref[...]


def _combine(y, tokrep, bounds, x1, mod, nf):
    b_, e_, cap, d = y.shape
    e_ -= 1
    t = x1.shape[1]
    tile = COMBINE_TILE
    assert cap % GRANULE == 0 and tile % LANES == 0 and bounds.shape == (b_, e_, t // tile + 1)
    grid_spec = pltpu.PrefetchScalarGridSpec(
        num_scalar_prefetch=1,
        grid=(b_, t // tile),
        in_specs=[pl.BlockSpec(memory_space=pl.ANY),
                  pl.BlockSpec((1, e_ * cap, LANES), lambda b, i, bnd: (b, 0, 0)),
                  pl.BlockSpec((1, tile, d), lambda b, i, bnd: (b, i, 0)),
                  pl.BlockSpec((1, N_MOD, d), lambda b, i, bnd: (b, 0, 0)),
                  pl.BlockSpec((1, d), lambda b, i, bnd: (0, 0))],
        out_specs=pl.BlockSpec((1, tile, d), lambda b, i, bnd: (b, i, 0)),
        scratch_shapes=[pltpu.VMEM((2, e_, cap, d), BF16),
                        pltpu.VMEM((COMBINE_CHUNK, d), BF16),
                        pltpu.VMEM((COMBINE_CHUNK, LANES), jnp.int32),
                        pltpu.SemaphoreType.DMA((2,))])
    return pl.pallas_call(
        _combine_kernel,
        grid_spec=grid_spec,
        out_shape=jax.ShapeDtypeStruct((b_, t, d), F32),
        compiler_params=_cparams(("arbitrary", "arbitrary")),
    )(bounds.reshape(-1), y, tokrep, x1, mod, nf)


def _gate_layout(gt, b_, t):
    return gt.reshape(N_HEADS, 4, b_, t // MLSTM_CHUNK, MLSTM_CHUNK)


def kernel(x, c, ctx, c_ctx, w_mod, b_mod, norm1, w_in, conv_q_w, conv_k_w, b_gates, head_g, pool_w,
           pool_scale, w_out, norm2, w_router, w_gate, w_up, w_down, norm_f):
    b_, t, d = x.shape
    tc = ctx.shape[1]
    depth = w_mod.shape[0]
    assert depth == 1
    l = 0
    qk_w = N_HEADS * DQK
    mw = N_HEADS * DV
    pool_wd = len(POOL_WINDOWS) * POOL_GC
    n_gate = 2 * 2 * N_HEADS
    k_off, v_off = 0, qk_w
    g_off = v_off + mw
    q_off = g_off + n_gate
    o_off = q_off + qk_w
    p_off = o_off + mw
    cap = EC_FACTOR * t // N_EXPERTS

    mod_rows = -(-(b_ + 1) // SUBLANES) * SUBLANES
    cs = jnp.zeros((mod_rows, d), F32).at[:b_].set(c).at[b_].set(c_ctx)
    mod_all = _modulation(cs, w_mod[l], b_mod[l])
    mod = mod_all[:b_].reshape(b_, N_MOD, d)
    mod_c = mod_all[b_:b_ + 1].reshape(1, N_MOD, d)

    wl = w_in[l]
    w_main = jnp.concatenate([wl[:, k_off:v_off], wl[:, v_off:g_off], wl[:, q_off:o_off],
                              wl[:, o_off:p_off], wl[:, p_off:]], axis=1).astype(BF16)
    perm = jnp.arange(n_gate).reshape(2, 2, N_HEADS).transpose(2, 0, 1).reshape(-1)
    w_g = wl[:, g_off:q_off][:, perm].T.astype(BF16)
    b_g = b_gates[l][perm].reshape(n_gate, 1)
    n1 = norm1[l].reshape(1, d)

    p, g = _inproj(x.reshape(b_ * t, d), mod, t, n1, w_main, w_g, b_g, 512)
    pc, gc = _inproj(ctx.reshape(b_ * tc, d), mod_c, b_ * tc, n1, w_main[:, :qk_w + mw], w_g, b_g,
                     min(512, b_ * tc))
    p = p.reshape(b_, t, -1)
    pc = pc.reshape(b_, tc, -1)

    wk = jnp.pad(conv_k_w[l], ((0, SUBLANES - QK_CONV), (0, 0)))
    wq = jnp.pad(conv_q_w[l], ((0, SUBLANES - QK_CONV), (0, 0)))
    ym = _mlstm(p, pc, _gate_layout(g, b_, t), _gate_layout(gc, b_, tc), wk, wq, head_g[l].reshape(1, mw))
    yp = _pool(p, pool_w[l].astype(BF16), pool_scale[l].reshape(1, pool_wd))

    wo = w_out[l].astype(BF16)
    wr = jnp.pad(w_router[l], ((0, 0), (0, LANES - N_EXPERTS))).astype(BF16)
    x1, h2, lg = _outproj(ym.reshape(b_ * t, mw), yp.reshape(b_ * t, pool_wd), wo[:mw], wo[mw:],
                          x.reshape(b_ * t, d), mod, t, norm2[l].reshape(1, d), wr, 512)

    idx, gate, tokrep, bounds = _route(lg.reshape(b_, t, LANES), cap)

    y = _ffn(idx, gate, h2, w_gate[l], w_up[l], w_down[l])
    return _combine(y, tokrep, bounds[:, :, :t // COMBINE_TILE + 1], x1.reshape(b_, t, d), mod,
                    norm_f.reshape(1, d))
```

```python
import functools

import jax
import jax.numpy as jnp
from jax import lax
from jax.experimental import pallas as pl
from jax.experimental.pallas import tpu as pltpu

F32 = jnp.float32
BF16 = jnp.bfloat16

N_HEADS = 4
DQK = 128
DV = 256
QK_CONV = 5
POOL_WINDOWS = (2, 4, 8, 16)
POOL_GC = 256
GRID_W = 64
GRID_SHIFT = GRID_W.bit_length() - 1
N_EXPERTS = 16
EC_FACTOR = 2
N_MOD = 6
EPS = 1e-6

LANES = 128
SUBLANES = 8
MLSTM_CHUNK = 128
VMEM_LIMIT = 56 * 1024 * 1024


def _cparams(sem):
    return pltpu.CompilerParams(dimension_semantics=sem, vmem_limit_bytes=VMEM_LIMIT)


def _sigmoid(x):
    return 1.0 / (1.0 + jnp.exp(-x))


def _silu(x):
    return x * _sigmoid(x)


def _log_sigmoid(x):
    return jnp.minimum(x, 0.0) - jnp.log(1.0 + jnp.exp(-jnp.abs(x)))


def _mod_kernel(c_ref, w_ref, b_ref, o_ref):
    s = _silu(c_ref[...]).astype(BF16)
    o_ref[...] = jnp.dot(s, w_ref[...].astype(BF16), preferred_element_type=F32) + b_ref[...]


def _modulation(cs, w_mod, b_mod):
    rows, d = cs.shape
    n = w_mod.shape[1]
    tn = 1024
    return pl.pallas_call(
        _mod_kernel,
        grid=(n // tn,),
        in_specs=[pl.BlockSpec((rows, d), lambda j: (0, 0)),
                  pl.BlockSpec((d, tn), lambda j: (0, j)),
                  pl.BlockSpec((1, tn), lambda j: (0, j))],
        out_specs=pl.BlockSpec((rows, tn), lambda j: (0, j)),
        out_shape=jax.ShapeDtypeStruct((rows, n), F32),
        compiler_params=_cparams(("arbitrary",)),
    )(cs, w_mod, b_mod.reshape(1, n))


INPROJ_COLS = 512


def _inproj_kernel(x_ref, mod_ref, n1_ref, w_ref, wg_ref, bg_ref, p_ref, gt_ref):
    tm = x_ref.shape[0]
    half = tm // 2
    n_cols = p_ref.shape[1]
    scale = n1_ref[...] * (1.0 + mod_ref[0, 1:2, :])
    for r0 in (0, half):
        rows = slice(r0, r0 + half)
        x = x_ref[rows, :]
        y = x * lax.rsqrt(jnp.mean(x * x, axis=-1, keepdims=True) + EPS)
        hb = (y * scale + mod_ref[0, 0:1, :]).astype(BF16)
        gt_ref[:, rows] = lax.dot_general(wg_ref[...], hb, (((1,), (1,)), ((), ())),
                                          preferred_element_type=F32) + bg_ref[...]
        for c0 in range(0, n_cols, INPROJ_COLS):
            p_ref[rows, c0:c0 + INPROJ_COLS] = jnp.dot(
                hb, w_ref[:, c0:c0 + INPROJ_COLS], preferred_element_type=F32).astype(BF16)


def _inproj(x2, mod, rows_per_mod, n1, w, wg, bg, tm):
    m, d = x2.shape
    n_cols = w.shape[1]
    n_g = wg.shape[0]
    tiles_per_mod = rows_per_mod // tm
    assert n_cols % INPROJ_COLS == 0 and tm % (2 * LANES) == 0
    return pl.pallas_call(
        _inproj_kernel,
        grid=(m // tm,),
        in_specs=[pl.BlockSpec((tm, d), lambda i: (i, 0)),
                  pl.BlockSpec((1, N_MOD, d), lambda i: (i // tiles_per_mod, 0, 0)),
                  pl.BlockSpec((1, d), lambda i: (0, 0)),
                  pl.BlockSpec((d, n_cols), lambda i: (0, 0), pipeline_mode=pl.Buffered(1)),
                  pl.BlockSpec((n_g, d), lambda i: (0, 0)),
                  pl.BlockSpec((n_g, 1), lambda i: (0, 0))],
        out_specs=[pl.BlockSpec((tm, n_cols), lambda i: (i, 0)),
                   pl.BlockSpec((n_g, tm), lambda i: (0, i))],
        out_shape=[jax.ShapeDtypeStruct((m, n_cols), BF16),
                   jax.ShapeDtypeStruct((n_g, m), F32)],
        compiler_params=_cparams(("arbitrary",)),
    )(x2, mod, n1, w, wg, bg)


def _scan_lanes(x, op, fill, reverse):
    lane = lax.broadcasted_iota(jnp.int32, x.shape, 1)
    k = 1
    while k < LANES:
        if reverse:
            x = op(x, jnp.where(lane < LANES - k, pltpu.roll(x, LANES - k, 1), fill))
        else:
            x = op(x, jnp.where(lane >= k, pltpu.roll(x, k, 1), fill))
        k *= 2
    return x


G_B, G_R, G_W, G_BEND, G_MLOC, G_RMAX, G_ROWS = 0, 1, 2, 3, 4, 5, 6


def _mlstm_kernel(kc_ref, vc_ref, gc_ref, k_ref, v_ref, q_ref, o_ref, g_ref, wk_ref, wq_ref, hg_ref,
                  out_ref,
                  cpad, ktt, qt, kctt, vext, vcext, gs, gsc, cf, cb, hf, hb):
    L = MLSTM_CHUNK
    t_lat = k_ref.shape[1]
    t_ctx = kc_ref.shape[1]
    nc = t_lat // L
    ncc = t_ctx // L
    hps = k_ref.shape[2] // DQK
    pad = SUBLANES
    half = QK_CONV // 2

    def conv_silu(src, w_ref, dst, t, scale, transposed):
        cpad[0:pad, :] = jnp.zeros((pad, LANES), F32)
        cpad[pad:pad + t, :] = src.astype(F32)
        cpad[pad + t:pad + t + pad, :] = jnp.zeros((pad, LANES), F32)
        for r in range(t // L):
            base = pad + r * L - half
            acc = w_ref[0:1, :] * cpad[base:base + L, :]
            for j in range(1, QK_CONV):
                acc = acc + w_ref[j:j + 1, :] * cpad[base + j:base + j + L, :]
            y = _silu(acc)
            if scale is not None:
                y = y * scale
            dst[r] = (y.T if transposed else y).astype(BF16)

    def fill_vext(dst, src, t):
        dst[:, 0:DV] = src
        dst[:, DV:DV + LANES] = jnp.ones((t, LANES), BF16)

    for hh in range(hps):
        qk_cols = slice(hh * DQK, (hh + 1) * DQK)
        v_cols = slice(hh * DV, (hh + 1) * DV)
        conv_silu(kc_ref[0, :, qk_cols], wk_ref.at[:, qk_cols], kctt.at[hh], t_ctx, DQK ** -0.5, True)
        conv_silu(k_ref[0, :, qk_cols], wk_ref.at[:, qk_cols], ktt.at[hh], t_lat, DQK ** -0.5, True)
        conv_silu(q_ref[0, :, qk_cols], wq_ref.at[:, qk_cols], qt.at[hh], t_lat, None, False)
        fill_vext(vcext.at[hh], vc_ref[0, :, v_cols], t_ctx)
        fill_vext(vext.at[hh], v_ref[0, :, v_cols], t_lat)

    def gate_prep(garr, dst):
        for d in range(2):
            li = garr[2 * d]
            lf = _log_sigmoid(garr[2 * d + 1])
            b = _scan_lanes(lf, jnp.add, 0.0, reverse=(d == 1))
            b_end = b[:, LANES - 1:LANES] if d == 0 else b[:, 0:1]
            r = li - b
            a = b_end + r
            m_loc = jnp.max(a, axis=1, keepdims=True)
            dst[G_ROWS * d + G_B] = b
            dst[G_ROWS * d + G_R] = r
            dst[G_ROWS * d + G_W] = jnp.exp(a - m_loc)
            dst[G_ROWS * d + G_BEND] = jnp.broadcast_to(b_end, b.shape)
            dst[G_ROWS * d + G_MLOC] = jnp.broadcast_to(m_loc, b.shape)
            dst[G_ROWS * d + G_RMAX] = _scan_lanes(r, jnp.maximum, -jnp.inf, reverse=(d == 1))

    for hh in range(hps):
        gate_prep(gc_ref[hh, :, 0], gsc.at[hh])
        gate_prep(g_ref[hh, :, 0], gs.at[hh])

    cf[...] = jnp.zeros(cf.shape, F32)
    cb[...] = jnp.zeros(cb.shape, F32)

    row_i = lax.broadcasted_iota(jnp.int32, (L, L), 0)
    col_i = lax.broadcasted_iota(jnp.int32, (L, L), 1)
    visible = (col_i <= row_i, col_i >= row_i)

    def per_token(row):
        return jnp.broadcast_to(row, (L, L)).T

    def step(hh, c, d, g_all, kt_all, v_all, q_all, c_all, m_prev):
        off = c * L
        g_scr, kt_scr, v_scr, c_scr = g_all.at[hh], kt_all.at[hh], v_all.at[hh], c_all.at[hh]
        q_scr = None if q_all is None else q_all.at[hh]
        g0 = G_ROWS * d
        kt_c = kt_scr[c]
        v_c = v_scr[pl.ds(off, L), :]
        b_end = g_scr[g0 + G_BEND, pl.ds(c, 1), 0:1]
        m_loc = g_scr[g0 + G_MLOC, pl.ds(c, 1), 0:1]
        c_prev = c_scr[...]
        h = None
        if q_scr is not None:
            q_c = q_scr[c]
            r_row = g_scr[g0 + G_R, pl.ds(c, 1), :]
            u = jnp.maximum(per_token(g_scr[g0 + G_RMAX, pl.ds(c, 1), :]), m_prev)
            b_t = per_token(g_scr[g0 + G_B, pl.ds(c, 1), :])
            qk = jnp.dot(q_c, kt_c, preferred_element_type=F32)
            s = (qk * jnp.exp(jnp.where(visible[d], r_row - u, -jnp.inf))).astype(BF16)
            wi = jnp.exp(m_prev - u)
            inter = jnp.dot(q_c, c_prev.astype(BF16), preferred_element_type=F32)
            res = jnp.dot(s, v_c, preferred_element_type=F32) \
                + jnp.concatenate([wi] * (c_prev.shape[1] // L), axis=1) * inter
            inv = 1.0 / jnp.maximum(jnp.abs(res[:, DV:DV + L]), jnp.exp(-(b_t + u)))
            h = res[:, 0:DV] * jnp.concatenate([inv] * (DV // L), axis=1)
        kwt = (kt_c.astype(F32) * g_scr[g0 + G_W, pl.ds(c, 1), :]).astype(BF16)
        c_loc = jnp.dot(kwt, v_c, preferred_element_type=F32)
        m_new = jnp.maximum(b_end + m_prev, m_loc)
        sp = jnp.exp(b_end + m_prev - m_new)
        sl = jnp.exp(m_loc - m_new)
        c_scr[...] = sp * c_prev + sl * c_loc
        return m_new, h

    def finish(hh, c, h):
        off = c * L
        cols = slice(hh * DV, (hh + 1) * DV)
        h = h * lax.rsqrt(jnp.mean(h * h, axis=-1, keepdims=True) + EPS)
        y = (h * hg_ref[:, cols]) * _sigmoid(o_ref[0, pl.ds(off, L), cols].astype(F32))
        out_ref[0, pl.ds(off, L), cols] = y.astype(BF16)

    ms = [jnp.zeros((1, 1), F32)] * (2 * hps)
    for i in range(ncc):
        for hh in range(hps):
            ms[2 * hh], _ = step(hh, i, 0, gsc, kctt, vcext, None, cf, ms[2 * hh])
            ms[2 * hh + 1], _ = step(hh, ncc - 1 - i, 1, gsc, kctt, vcext, None, cb, ms[2 * hh + 1])

    assert nc % 2 == 0

    def scan_step(i, carry, second_half):
        c_f, c_b = i, nc - 1 - i
        off_f, off_b = c_f * L, c_b * L
        out = []
        for hh in range(hps):
            m_f, h_f = step(hh, c_f, 0, gs, ktt, vext, qt, cf, carry[2 * hh])
            m_b, h_b = step(hh, c_b, 1, gs, ktt, vext, qt, cb, carry[2 * hh + 1])
            if second_half:
                finish(hh, c_f, h_f + hb[hh, pl.ds(off_f, L), :])
                finish(hh, c_b, h_b + hf[hh, pl.ds(off_b, L), :])
            else:
                hf[hh, pl.ds(off_f, L), :] = h_f
                hb[hh, pl.ds(off_b, L), :] = h_b
            out += [m_f, m_b]
        return tuple(out)

    carry = tuple(ms)
    for i in range(nc):
        carry = scan_step(i, carry, second_half=i >= nc // 2)


MLSTM_HEADS_PER_STEP = 2


def _mlstm(p, pc, g4, gc4, wk, wq, head_g):
    b_, t, _ = p.shape
    tc = pc.shape[1]
    L = MLSTM_CHUNK
    nc, ncc = t // L, tc // L
    ext = DV + LANES
    hps = MLSTM_HEADS_PER_STEP
    assert N_HEADS % hps == 0
    v_blk = N_HEADS * DQK // (hps * DV)
    q_blk = (N_HEADS * DQK + N_HEADS * DV) // (hps * DQK)
    o_blk = (2 * N_HEADS * DQK + N_HEADS * DV) // (hps * DV)
    return pl.pallas_call(
        _mlstm_kernel,
        grid=(b_, N_HEADS // hps),
        in_specs=[
            pl.BlockSpec((1, tc, hps * DQK), lambda b, h: (b, 0, h)),
            pl.BlockSpec((1, tc, hps * DV), lambda b, h: (b, 0, v_blk + h)),
            pl.BlockSpec((hps, 4, 1, ncc, L), lambda b, h: (h, 0, b, 0, 0)),
            pl.BlockSpec((1, t, hps * DQK), lambda b, h: (b, 0, h)),
            pl.BlockSpec((1, t, hps * DV), lambda b, h: (b, 0, v_blk + h)),
            pl.BlockSpec((1, t, hps * DQK), lambda b, h: (b, 0, q_blk + h)),
            pl.BlockSpec((1, t, hps * DV), lambda b, h: (b, 0, o_blk + h)),
            pl.BlockSpec((hps, 4, 1, nc, L), lambda b, h: (h, 0, b, 0, 0)),
            pl.BlockSpec((SUBLANES, hps * DQK), lambda b, h: (0, h)),
            pl.BlockSpec((SUBLANES, hps * DQK), lambda b, h: (0, h)),
            pl.BlockSpec((1, hps * DV), lambda b, h: (0, h)),
        ],
        out_specs=pl.BlockSpec((1, t, hps * DV), lambda b, h: (b, 0, h)),
        out_shape=jax.ShapeDtypeStruct((b_, t, N_HEADS * DV), BF16),
        scratch_shapes=[
            pltpu.VMEM((t + 2 * SUBLANES, LANES), F32),
            pltpu.VMEM((hps, nc, DQK, L), BF16),
            pltpu.VMEM((hps, nc, L, DQK), BF16),
            pltpu.VMEM((hps, ncc, DQK, L), BF16),
            pltpu.VMEM((hps, t, ext), BF16),
            pltpu.VMEM((hps, tc, ext), BF16),
            pltpu.VMEM((hps, 2 * G_ROWS, nc, L), F32),
            pltpu.VMEM((hps, 2 * G_ROWS, ncc, L), F32),
            pltpu.VMEM((hps, DQK, ext), F32),
            pltpu.VMEM((hps, DQK, ext), F32),
            pltpu.VMEM((hps, t, DV), F32),
            pltpu.VMEM((hps, t, DV), F32),
        ],
        compiler_params=_cparams(("arbitrary", "arbitrary")),
    )(pc, pc, gc4, p, p, p, p, g4, wk, wq, head_g)


def _pool_kernel(u_ref, pw_ref, ps_ref, out_ref, spad):
    t = u_ref.shape[1]
    rows = t // GRID_W
    blk = 256
    halo = (max(POOL_WINDOWS) // 2) * GRID_W
    spad[0:halo, :] = jnp.zeros((halo, POOL_GC), F32)
    spad[halo + t:halo + t + halo, :] = jnp.zeros((halo, POOL_GC), F32)
    ti = lax.broadcasted_iota(jnp.int32, (blk, blk), 0)
    tj = lax.broadcasted_iota(jnp.int32, (blk, blk), 1)
    same_row = (ti >> GRID_SHIFT) == (tj >> GRID_SHIFT)
    diff = tj - ti
    tok = lax.broadcasted_iota(jnp.int32, (blk, 1), 0)
    for gi, w in enumerate(POOL_WINDOWS):
        hw = w // 2
        c0, c1 = gi * POOL_GC, (gi + 1) * POOL_GC
        band = jnp.where(same_row & (diff >= -hw) & (diff <= hw - 1), 1.0, 0.0).astype(BF16)
        for r in range(t // blk):
            spad[halo + r * blk:halo + (r + 1) * blk, :] = jnp.dot(
                band, u_ref[0, r * blk:(r + 1) * blk, c0:c1], preferred_element_type=F32)
        for r in range(t // blk):
            base = halo + r * blk
            acc = spad[base - hw * GRID_W:base - hw * GRID_W + blk, :]
            for j in range(-hw + 1, hw):
                acc = acc + spad[base + j * GRID_W:base + j * GRID_W + blk, :]
            tt = tok + r * blk
            gr = tt >> GRID_SHIFT
            gc = tt & (GRID_W - 1)
            cr = jnp.minimum(gr + hw, rows) - jnp.maximum(gr - hw, 0)
            cc = jnp.minimum(gc + hw, GRID_W) - jnp.maximum(gc - hw, 0)
            cnt = (cr * cc).astype(F32)
            d = acc / cnt - u_ref[0, r * blk:(r + 1) * blk, c0:c1].astype(F32)
            y = jnp.dot(d.astype(BF16), pw_ref[gi], preferred_element_type=F32) * ps_ref[:, c0:c1]
            out_ref[0, r * blk:(r + 1) * blk, c0:c1] = y.astype(BF16)


def _pool(p, pool_w, pool_scale):
    b_, t, _ = p.shape
    pw = len(POOL_WINDOWS) * POOL_GC
    halo = (max(POOL_WINDOWS) // 2) * GRID_W
    return pl.pallas_call(
        _pool_kernel,
        grid=(b_,),
        in_specs=[pl.BlockSpec((1, t, pw), lambda b: (b, 0, 3)),
                  pl.BlockSpec((len(POOL_WINDOWS), POOL_GC, POOL_GC), lambda b: (0, 0, 0)),
                  pl.BlockSpec((1, pw), lambda b: (0, 0))],
        out_specs=pl.BlockSpec((1, t, pw), lambda b: (b, 0, 0)),
        out_shape=jax.ShapeDtypeStruct((b_, t, pw), BF16),
        scratch_shapes=[pltpu.VMEM((t + 2 * halo, POOL_GC), F32)],
        compiler_params=_cparams(("arbitrary",)),
    )(p, pool_w, pool_scale)


def _outproj_kernel(ym_ref, yp_ref, wm_ref, wp_ref, x_ref, mod_ref, n2_ref, wr_ref,
                    x1_ref, h2_ref, lg_ref):
    tm = x_ref.shape[0]
    half = tm // 2
    scale2 = n2_ref[...] * (1.0 + mod_ref[0, 4:5, :])
    for r0 in (0, half):
        rows = slice(r0, r0 + half)
        mix = (jnp.dot(ym_ref[rows, :], wm_ref[...], preferred_element_type=F32)
               + jnp.dot(yp_ref[rows, :], wp_ref[...], preferred_element_type=F32))
        x1 = x_ref[rows, :] + mod_ref[0, 2:3, :] * mix
        x1_ref[rows, :] = x1
        y = x1 * lax.rsqrt(jnp.mean(x1 * x1, axis=-1, keepdims=True) + EPS)
        h2 = y * scale2 + mod_ref[0, 3:4, :]
        g0, g1 = r0 // SUBLANES, (r0 + half) // SUBLANES
        for j in range(h2_ref.shape[1]):
            h2_ref[g0:g1, j, :, :] = h2[:, j * LANES:(j + 1) * LANES].reshape(g1 - g0, SUBLANES, LANES)
        lg_ref[rows, :] = jnp.dot(h2.astype(BF16), wr_ref[...], preferred_element_type=F32)


def _outproj(ym, yp, wm, wp, x2, mod, rows_per_mod, n2, wr, tm):
    m, d = x2.shape
    km = ym.shape[1]
    kp = yp.shape[1]
    tiles_per_mod = rows_per_mod // tm
    return pl.pallas_call(
        _outproj_kernel,
        grid=(m // tm,),
        in_specs=[pl.BlockSpec((tm, km), lambda i: (i, 0)),
                  pl.BlockSpec((tm, kp), lambda i: (i, 0)),
                  pl.BlockSpec((km, d), lambda i: (0, 0), pipeline_mode=pl.Buffered(1)),
                  pl.BlockSpec((kp, d), lambda i: (0, 0), pipeline_mode=pl.Buffered(1)),
                  pl.BlockSpec((tm, d), lambda i: (i, 0)),
                  pl.BlockSpec((1, N_MOD, d), lambda i: (i // tiles_per_mod, 0, 0)),
                  pl.BlockSpec((1, d), lambda i: (0, 0)),
                  pl.BlockSpec((d, LANES), lambda i: (0, 0))],
        out_specs=[pl.BlockSpec((tm, d), lambda i: (i, 0)),
                   pl.BlockSpec((tm // SUBLANES, d // LANES, SUBLANES, LANES), lambda i: (i, 0, 0, 0)),
                   pl.BlockSpec((tm, LANES), lambda i: (i, 0))],
        out_shape=[jax.ShapeDtypeStruct((m, d), F32),
                   jax.ShapeDtypeStruct((m // SUBLANES, d // LANES, SUBLANES, LANES), F32),
                   jax.ShapeDtypeStruct((m, LANES), F32)],
        compiler_params=_cparams(("arbitrary",)),
    )(ym, yp, wm, wp, x2, mod, n2, wr)


def _scan_lanes_i32(x, exclusive_of=None):
    n = x.shape[1]
    lane = lax.broadcasted_iota(jnp.int32, x.shape, 1)
    k = 1
    while k < n:
        x = x + jnp.where(lane >= k, pltpu.roll(x, k, 1), 0)
        k *= 2
    return x


COMBINE_TILE = 256
GRANULE = 2 * SUBLANES


def _route_kernel(lg_ref, idx_ref, gate_ref, tokrep_ref, bounds_ref, afft):
    t = lg_ref.shape[1]
    cap = idx_ref.shape[2]
    lane = lax.broadcasted_iota(jnp.int32, (t, LANES), 1)
    lg = jnp.where(lane < N_EXPERTS, lg_ref[0], -jnp.inf)
    ex = jnp.exp(lg - jnp.max(lg, axis=1, keepdims=True))
    aff = ex / jnp.sum(ex, axis=1, keepdims=True)
    for r in range(t // LANES):
        afft[:, r * LANES:(r + 1) * LANES] = aff[r * LANES:(r + 1) * LANES, :].T
    aff_t = afft[0:N_EXPERTS, :]

    def count(mask):
        return jnp.sum(jnp.where(mask, 1.0, 0.0), axis=1, keepdims=True).astype(jnp.int32)

    def bit_step(i, thr_bits):
        cand = thr_bits | (jnp.int32(1) << (30 - i))
        cand_f = lax.bitcast_convert_type(cand, F32)
        return jnp.where(count(aff_t >= cand_f) >= cap, cand, thr_bits)

    thr_bits = lax.fori_loop(0, 31, bit_step, jnp.zeros((N_EXPERTS, 1), jnp.int32))
    thr = lax.bitcast_convert_type(thr_bits, F32)
    gt = aff_t > thr
    eq = aff_t == thr
    need = cap - count(gt)
    eq_i = jnp.where(eq, 1, 0)
    eq_rank = _scan_lanes_i32(eq_i) - eq_i
    sel = gt | (eq & (eq_rank < need))
    sel_i = jnp.where(sel, 1, 0)
    pos = _scan_lanes_i32(sel_i) - sel_i
    key = jnp.where(sel, pos, -1)

    a_hi = aff.astype(BF16).astype(F32)
    a_mid = (aff - a_hi).astype(BF16).astype(F32)
    a_lo = (aff - a_hi - a_mid).astype(BF16).astype(F32)
    tok = lax.broadcasted_iota(jnp.int32, (t, LANES), 0)
    vals = (a_hi + pltpu.roll(a_mid, N_EXPERTS, 1) + pltpu.roll(a_lo, 2 * N_EXPERTS, 1)
            + jnp.where(lane == 3 * N_EXPERTS, (tok >> GRID_SHIFT).astype(F32), 0.0)
            + jnp.where(lane == 3 * N_EXPERTS + 1, (tok & (GRID_W - 1)).astype(F32), 0.0)).astype(BF16)
    slot = lax.broadcasted_iota(jnp.int32, (cap, 1), 0)
    for e in range(N_EXPERTS):
        onehot = jnp.where(slot == key[e:e + 1, :], 1.0, 0.0).astype(BF16)
        res = jnp.dot(onehot, vals, preferred_element_type=F32)
        rt = jnp.concatenate([res[r * LANES:(r + 1) * LANES, :].T for r in range(cap // LANES)], axis=1)
        gate_ref[0, e:e + 1, :] = (rt[e:e + 1, :] + rt[N_EXPERTS + e:N_EXPERTS + e + 1, :]) \
            + rt[2 * N_EXPERTS + e:2 * N_EXPERTS + e + 1, :]
        tokf = rt[3 * N_EXPERTS:3 * N_EXPERTS + 1, :] * float(GRID_W) + rt[3 * N_EXPERTS + 1:3 * N_EXPERTS + 2, :]
        idx_ref[0, e:e + 1, :] = tokf.astype(jnp.int32)
        tok_col = res[:, 3 * N_EXPERTS:3 * N_EXPERTS + 1] * float(GRID_W) \
            + res[:, 3 * N_EXPERTS + 1:3 * N_EXPERTS + 2]
        tokrep_ref[0, e * cap:(e + 1) * cap, :] = jnp.broadcast_to(tok_col, (cap, LANES)).astype(jnp.int32)

    n_tiles = t // COMBINE_TILE
    lane_b = lax.broadcasted_iota(jnp.int32, (N_EXPERTS, LANES), 1)
    bounds = jnp.where(lane_b >= n_tiles, cap, 0)
    for i in range(n_tiles):
        bounds = jnp.where(lane_b == i, pos[:, i * COMBINE_TILE:i * COMBINE_TILE + 1], bounds)
    bounds_ref[0] = bounds


def _route(lg3, cap):
    b_, t, _ = lg3.shape
    return pl.pallas_call(
        _route_kernel,
        grid=(b_,),
        in_specs=[pl.BlockSpec((1, t, LANES), lambda b: (b, 0, 0))],
        out_specs=[pl.BlockSpec((1, N_EXPERTS, cap), lambda b: (b, 0, 0)),
                   pl.BlockSpec((1, N_EXPERTS, cap), lambda b: (b, 0, 0)),
                   pl.BlockSpec((1, N_EXPERTS * cap, LANES), lambda b: (b, 0, 0)),
                   pl.BlockSpec((1, N_EXPERTS, LANES), lambda b: (b, 0, 0))],
        out_shape=[jax.ShapeDtypeStruct((b_, N_EXPERTS, cap), jnp.int32),
                   jax.ShapeDtypeStruct((b_, N_EXPERTS, cap), F32),
                   jax.ShapeDtypeStruct((b_, N_EXPERTS * cap, LANES), jnp.int32),
                   jax.ShapeDtypeStruct((b_, N_EXPERTS, LANES), jnp.int32)],
        scratch_shapes=[pltpu.VMEM((LANES, t), F32)],
        compiler_params=_cparams(("arbitrary",)),
    )(lg3)


def _ffn_kernel(idx_ref, gate_ref, h2_hbm, wg_ref, wu_ref, wd_ref, y_ref,
                xs_buf, wg_s, wu_s, wd_s, sem):
    p = pl.program_id(0)
    b = pl.program_id(1)
    n_e = pl.num_programs(0) - 1
    n_b = pl.num_programs(1)
    k_chunks, cap = xs_buf.shape[1], xs_buf.shape[2]
    groups = h2_hbm.shape[0] // n_b
    step = p * n_b + b
    cur = step % 2
    nxt = 1 - cur
    last_b = b == n_b - 1
    b_next = jnp.where(last_b, 0, b + 1)
    p_next = jnp.where(last_b, p + 1, p)

    def issue_gather():
        def issue(r, carry):
            tok = idx_ref[0, 0, r]
            pltpu.make_async_copy(h2_hbm.at[b_next * groups + (tok >> 3), :, tok & (SUBLANES - 1), :],
                                  xs_buf.at[nxt, :, r, :], sem.at[nxt]).start()
            return carry

        lax.fori_loop(0, cap, issue, 0, unroll=True)

    def cast_weights():
        ws = p % 2
        rows_in = wg_ref.shape[1]
        rows_mid = wd_ref.shape[1]
        wg_s[ws, pl.ds(pl.multiple_of(b * rows_in, rows_in), rows_in), :] = wg_ref[0].astype(BF16)
        wu_s[ws, pl.ds(pl.multiple_of(b * rows_in, rows_in), rows_in), :] = wu_ref[0].astype(BF16)
        wd_s[ws, pl.ds(pl.multiple_of(b * rows_mid, rows_mid), rows_mid), :] = wd_ref[0].astype(BF16)

    def run_expert():
        pltpu.make_async_copy(xs_buf.at[cur], xs_buf.at[cur], sem.at[cur]).wait()
        ws = (p - 1) % 2
        xs = jnp.concatenate([xs_buf[cur, j].astype(BF16) for j in range(k_chunks)], axis=1)
        a = _silu(jnp.dot(xs, wg_s[ws], preferred_element_type=F32)) \
            * jnp.dot(xs, wu_s[ws], preferred_element_type=F32)
        y = jnp.dot(a.astype(BF16), wd_s[ws], preferred_element_type=F32)
        g_row = gate_ref[0]
        g_col = jnp.concatenate(
            [jnp.broadcast_to(g_row[:, k * LANES:(k + 1) * LANES], (LANES, LANES)).T
             for k in range(cap // LANES)], axis=0)
        y_ref[0, 0] = (y * jnp.concatenate([g_col] * (y.shape[1] // LANES), axis=1)).astype(BF16)

    @pl.when((p_next >= 1) & (p_next <= n_e))
    def _():
        issue_gather()

    @pl.when(p == 0)
    def _():
        cast_weights()
        y_ref[0, 0] = jnp.zeros(y_ref.shape[2:], BF16)

    @pl.when((p >= 1) & (p < n_e))
    def _():
        cast_weights()
        run_expert()

    @pl.when(p == n_e)
    def _():
        run_expert()


def _ffn(idx, gate, h2, wg, wu, wd):
    e_, d, f = wg.shape
    b_, _, cap = idx.shape
    assert h2.shape[1] * h2.shape[3] == d and h2.shape[2] == SUBLANES and h2.shape[0] % b_ == 0
    assert d % b_ == 0 and f % b_ == 0
    rows_in, rows_mid = d // b_, f // b_
    assert rows_in % (2 * SUBLANES) == 0 and rows_mid % (2 * SUBLANES) == 0

    def next_tokens(p, b):
        last_b = b == b_ - 1
        b_next = jnp.where(last_b, 0, b + 1)
        e_next = jnp.clip(jnp.where(last_b, p, p - 1), 0, e_ - 1)
        return (b_next * e_ + e_next, 0, 0)

    def weight_piece(p, b):
        return (jnp.minimum(p, e_ - 1), b, 0)

    return pl.pallas_call(
        _ffn_kernel,
        grid=(e_ + 1, b_),
        in_specs=[pl.BlockSpec((1, 1, cap), next_tokens, memory_space=pltpu.SMEM),
                  pl.BlockSpec((1, 1, cap), lambda p, b: (b * e_ + jnp.maximum(p - 1, 0), 0, 0)),
                  pl.BlockSpec(memory_space=pl.ANY),
                  pl.BlockSpec((1, rows_in, f), weight_piece),
                  pl.BlockSpec((1, rows_in, f), weight_piece),
                  pl.BlockSpec((1, rows_mid, d), weight_piece)],
        out_specs=pl.BlockSpec((1, 1, cap, d), lambda p, b: (b, (p + e_) % (e_ + 1), 0, 0)),
        out_shape=jax.ShapeDtypeStruct((b_, e_ + 1, cap, d), BF16),
        scratch_shapes=[pltpu.VMEM((2, d // LANES, cap, LANES), F32),
                        pltpu.VMEM((2, d, f), BF16),
                        pltpu.VMEM((2, d, f), BF16),
                        pltpu.VMEM((2, f, d), BF16),
                        pltpu.SemaphoreType.DMA((2,))],
        compiler_params=_cparams(("arbitrary", "arbitrary")),
    )(idx.reshape(b_ * e_, 1, cap), gate.reshape(b_ * e_, 1, cap), h2, wg, wu, wd)


COMBINE_CHUNK = 256


def _combine_kernel(bounds_ref, y_hbm, tok_ref, x1_ref, mod_ref, nf_ref, o_ref, y_buf, stage_y, stage_t, sem):
    b = pl.program_id(0)
    i = pl.program_id(1)
    n_b = pl.num_programs(0)
    n_e, cap = y_buf.shape[1], y_buf.shape[2]
    cur = b % 2

    def y_copy(bb, slot):
        return pltpu.make_async_copy(y_hbm.at[bb, pl.ds(0, n_e)], y_buf.at[slot], sem.at[slot])

    @pl.when(i == 0)
    def _():
        @pl.when(b == 0)
        def _():
            y_copy(0, 0).start()

        y_copy(b, cur).wait()

        @pl.when(b + 1 < n_b)
        def _():
            y_copy(b + 1, 1 - cur).start()

    tile = o_ref.shape[1]
    n_bounds = pl.num_programs(1) + 1
    per_chunk = COMBINE_CHUNK // GRANULE
    g_shift = GRANULE.bit_length() - 1
    assert per_chunk & (per_chunk - 1) == 0 and GRANULE == 1 << g_shift

    @pl.when((b == 0) & (i == 0))
    def _():
        stage_y[...] = jnp.zeros(stage_y.shape, BF16)
        stage_t[...] = jnp.full(stage_t.shape, -1, jnp.int32)

    o_ref[...] = jnp.zeros(o_ref.shape, F32)
    tok_lane = lax.broadcasted_iota(jnp.int32, (COMBINE_CHUNK, LANES), 1) + i * tile

    def flush():
        onehot_t = jnp.concatenate(
            [jnp.where(stage_t[...] == tok_lane + j * LANES, 1.0, 0.0).astype(BF16)
             for j in range(tile // LANES)], axis=1)
        o_ref[0] += lax.dot_general(onehot_t, stage_y[...], (((0,), (0,)), ((), ())),
                                    preferred_element_type=F32)
        stage_t[...] = jnp.full(stage_t.shape, -1, jnp.int32)

    def per_expert(e, k):
        base = (b * n_e + e) * n_bounds + i
        lo = bounds_ref[base]
        hi = bounds_ref[base + 1]
        g_lo = lax.shift_right_logical(lo, g_shift)
        g_hi = jnp.where(hi > lo, lax.shift_right_logical(hi + GRANULE - 1, g_shift), g_lo)

        def per_granule(g, k):
            src = pl.multiple_of(g * GRANULE, GRANULE)
            dst = pl.multiple_of((k & (per_chunk - 1)) * GRANULE, GRANULE)
            stage_y[pl.ds(dst, GRANULE), :] = y_buf[cur, e, pl.ds(src, GRANULE), :]
            stage_t[pl.ds(dst, GRANULE), :] = tok_ref[0, pl.ds(pl.multiple_of(e * cap + src, GRANULE), GRANULE), :]
            k = k + 1

            @pl.when((k & (per_chunk - 1)) == 0)
            def _():
                flush()

            return k

        return lax.fori_loop(g_lo, g_hi, per_granule, k)

    k = lax.fori_loop(0, n_e, per_expert, jnp.int32(0))

    @pl.when((k & (per_chunk - 1)) != 0)
    def _():
        flush()

    x = x1_ref[0] + mod_ref[0, 5:6, :] * o_ref[0]
    o_ref[0] = (x * lax.rsqrt(jnp.mean(x * x, axis=-1, keepdims=True) + EPS)) * nf_ref[...]


def _combine(y, tokrep, bounds, x1, mod, nf):
    b_, e_, cap, d = y.shape
    e_ -= 1
    t = x1.shape[1]
    tile = COMBINE_TILE
    assert cap % GRANULE == 0 and tile % LANES == 0 and bounds.shape == (b_, e_, t // tile + 1)
    grid_spec = pltpu.PrefetchScalarGridSpec(
        num_scalar_prefetch=1,
        grid=(b_, t // tile),
        in_specs=[pl.BlockSpec(memory_space=pl.ANY),
                  pl.BlockSpec((1, e_ * cap, LANES), lambda b, i, bnd: (b, 0, 0)),
                  pl.BlockSpec((1, tile, d), lambda b, i, bnd: (b, i, 0)),
                  pl.BlockSpec((1, N_MOD, d), lambda b, i, bnd: (b, 0, 0)),
                  pl.BlockSpec((1, d), lambda b, i, bnd: (0, 0))],
        out_specs=pl.BlockSpec((1, tile, d), lambda b, i, bnd: (b, i, 0)),
        scratch_shapes=[pltpu.VMEM((2, e_, cap, d), BF16),
                        pltpu.VMEM((COMBINE_CHUNK, d), BF16),
                        pltpu.VMEM((COMBINE_CHUNK, LANES), jnp.int32),
                        pltpu.SemaphoreType.DMA((2,))])
    return pl.pallas_call(
        _combine_kernel,
        grid_spec=grid_spec,
        out_shape=jax.ShapeDtypeStruct((b_, t, d), F32),
        compiler_params=_cparams(("arbitrary", "arbitrary")),
    )(bounds.reshape(-1), y, tokrep, x1, mod, nf)


def _gate_layout(gt, b_, t):
    return gt.reshape(N_HEADS, 4, b_, t // MLSTM_CHUNK, MLSTM_CHUNK)


def kernel(x, c, ctx, c_ctx, w_mod, b_mod, norm1, w_in, conv_q_w, conv_k_w, b_gates, head_g, pool_w,
           pool_scale, w_out, norm2, w_router, w_gate, w_up, w_down, norm_f):
    b_, t, d = x.shape
    tc = ctx.shape[1]
    depth = w_mod.shape[0]
    assert depth == 1
    l = 0
    qk_w = N_HEADS * DQK
    mw = N_HEADS * DV
    pool_wd = len(POOL_WINDOWS) * POOL_GC
    n_gate = 2 * 2 * N_HEADS
    k_off, v_off = 0, qk_w
    g_off = v_off + mw
    q_off = g_off + n_gate
    o_off = q_off + qk_w
    p_off = o_off + mw
    cap = EC_FACTOR * t // N_EXPERTS

    mod_rows = -(-(b_ + 1) // SUBLANES) * SUBLANES
    cs = jnp.zeros((mod_rows, d), F32).at[:b_].set(c).at[b_].set(c_ctx)
    mod_all = _modulation(cs, w_mod[l], b_mod[l])
    mod = mod_all[:b_].reshape(b_, N_MOD, d)
    mod_c = mod_all[b_:b_ + 1].reshape(1, N_MOD, d)

    wl = w_in[l]
    w_main = jnp.concatenate([wl[:, k_off:v_off], wl[:, v_off:g_off], wl[:, q_off:o_off],
                              wl[:, o_off:p_off], wl[:, p_off:]], axis=1).astype(BF16)
    perm = jnp.arange(n_gate).reshape(2, 2, N_HEADS).transpose(2, 0, 1).reshape(-1)
    w_g = wl[:, g_off:q_off][:, perm].T.astype(BF16)
    b_g = b_gates[l][perm].reshape(n_gate, 1)
    n1 = norm1[l].reshape(1, d)

    p, g = _inproj(x.reshape(b_ * t, d), mod, t, n1, w_main, w_g, b_g, 512)
    pc, gc = _inproj(ctx.reshape(b_ * tc, d), mod_c, b_ * tc, n1, w_main[:, :qk_w + mw], w_g, b_g,
                     min(512, b_ * tc))
    p = p.reshape(b_, t, -1)
    pc = pc.reshape(b_, tc, -1)

    wk = jnp.pad(conv_k_w[l], ((0, SUBLANES - QK_CONV), (0, 0)))
    wq = jnp.pad(conv_q_w[l], ((0, SUBLANES - QK_CONV), (0, 0)))
    ym = _mlstm(p, pc, _gate_layout(g, b_, t), _gate_layout(gc, b_, tc), wk, wq, head_g[l].reshape(1, mw))
    yp = _pool(p, pool_w[l].astype(BF16), pool_scale[l].reshape(1, pool_wd))

    wo = w_out[l].astype(BF16)
    wr = jnp.pad(w_router[l], ((0, 0), (0, LANES - N_EXPERTS))).astype(BF16)
    x1, h2, lg = _outproj(ym.reshape(b_ * t, mw), yp.reshape(b_ * t, pool_wd), wo[:mw], wo[mw:],
                          x.reshape(b_ * t, d), mod, t, norm2[l].reshape(1, d), wr, 512)

    idx, gate, tokrep, bounds = _route(lg.reshape(b_, t, LANES), cap)

    y = _ffn(idx, gate, h2, w_gate[l], w_up[l], w_down[l])
    return _combine(y, tokrep, bounds[:, :, :t // COMBINE_TILE + 1], x1.reshape(b_, t, d), mod,
                    norm_f.reshape(1, d))
```

```python
import functools

import jax
import jax.numpy as jnp
from jax import lax
from jax.experimental import pallas as pl
from jax.experimental.pallas import tpu as pltpu

F32 = jnp.float32
BF16 = jnp.bfloat16

N_HEADS = 4
DQK = 128
DV = 256
QK_CONV = 5
POOL_WINDOWS = (2, 4, 8, 16)
POOL_GC = 256
GRID_W = 64
GRID_SHIFT = GRID_W.bit_length() - 1
N_EXPERTS = 16
EC_FACTOR = 2
N_MOD = 6
EPS = 1e-6

LANES = 128
SUBLANES = 8
SUBLANE_SHIFT = SUBLANES.bit_length() - 1
PROJ_ROWS = 512
MLSTM_CHUNK = 128
VMEM_LIMIT = 56 * 1024 * 1024


def _cparams(sem):
    return pltpu.CompilerParams(dimension_semantics=sem, vmem_limit_bytes=VMEM_LIMIT)


def _sigmoid(x):
    return 1.0 / (1.0 + jnp.exp(-x))


def _silu(x):
    return x * _sigmoid(x)


def _log_sigmoid(x):
    return jnp.minimum(x, 0.0) - jnp.log(1.0 + jnp.exp(-jnp.abs(x)))


def _mod_kernel(c_ref, w_ref, b_ref, o_ref):
    s = _silu(c_ref[...]).astype(BF16)
    o_ref[...] = jnp.dot(s, w_ref[...].astype(BF16), preferred_element_type=F32) + b_ref[...]


def _modulation(cs, w_mod, b_mod):
    rows, d = cs.shape
    n = w_mod.shape[1]
    tn = 1024
    return pl.pallas_call(
        _mod_kernel,
        grid=(n // tn,),
        in_specs=[pl.BlockSpec((rows, d), lambda j: (0, 0)),
                  pl.BlockSpec((d, tn), lambda j: (0, j)),
                  pl.BlockSpec((1, tn), lambda j: (0, j))],
        out_specs=pl.BlockSpec((rows, tn), lambda j: (0, j)),
        out_shape=jax.ShapeDtypeStruct((rows, n), F32),
        compiler_params=_cparams(("arbitrary",)),
    )(cs, w_mod, b_mod.reshape(1, n))


INPROJ_COLS = 512


def _inproj_kernel(x_ref, mod_ref, n1_ref, w_ref, wg_ref, bg_ref, p_ref, gt_ref):
    tm = x_ref.shape[0]
    half = tm // 2
    n_cols = p_ref.shape[1]
    scale = n1_ref[...] * (1.0 + mod_ref[0, 1:2, :])
    for r0 in (0, half):
        rows = slice(r0, r0 + half)
        x = x_ref[rows, :]
        y = x * lax.rsqrt(jnp.mean(x * x, axis=-1, keepdims=True) + EPS)
        hb = (y * scale + mod_ref[0, 0:1, :]).astype(BF16)
        gt_ref[:, rows] = lax.dot_general(wg_ref[...], hb, (((1,), (1,)), ((), ())),
                                          preferred_element_type=F32) + bg_ref[...]
        for c0 in range(0, n_cols, INPROJ_COLS):
            p_ref[rows, c0:c0 + INPROJ_COLS] = jnp.dot(
                hb, w_ref[:, c0:c0 + INPROJ_COLS], preferred_element_type=F32).astype(BF16)


def _inproj(x2, mod, rows_per_mod, n1, w, wg, bg, tm):
    m, d = x2.shape
    n_cols = w.shape[1]
    n_g = wg.shape[0]
    tiles_per_mod = rows_per_mod // tm
    assert n_cols % INPROJ_COLS == 0 and tm % (2 * LANES) == 0
    return pl.pallas_call(
        _inproj_kernel,
        grid=(m // tm,),
        in_specs=[pl.BlockSpec((tm, d), lambda i: (i, 0)),
                  pl.BlockSpec((1, N_MOD, d), lambda i: (i // tiles_per_mod, 0, 0)),
                  pl.BlockSpec((1, d), lambda i: (0, 0)),
                  pl.BlockSpec((d, n_cols), lambda i: (0, 0), pipeline_mode=pl.Buffered(1)),
                  pl.BlockSpec((n_g, d), lambda i: (0, 0)),
                  pl.BlockSpec((n_g, 1), lambda i: (0, 0))],
        out_specs=[pl.BlockSpec((tm, n_cols), lambda i: (i, 0)),
                   pl.BlockSpec((n_g, tm), lambda i: (0, i))],
        out_shape=[jax.ShapeDtypeStruct((m, n_cols), BF16),
                   jax.ShapeDtypeStruct((n_g, m), F32)],
        compiler_params=_cparams(("arbitrary",)),
    )(x2, mod, n1, w, wg, bg)


def _scan_lanes(x, op, fill, reverse):
    lane = lax.broadcasted_iota(jnp.int32, x.shape, 1)
    k = 1
    while k < LANES:
        if reverse:
            x = op(x, jnp.where(lane < LANES - k, pltpu.roll(x, LANES - k, 1), fill))
        else:
            x = op(x, jnp.where(lane >= k, pltpu.roll(x, k, 1), fill))
        k *= 2
    return x


G_B, G_R, G_W, G_BEND, G_MLOC, G_RMAX, G_ROWS = 0, 1, 2, 3, 4, 5, 6


def _mlstm_kernel(kc_ref, vc_ref, gc_ref, k_ref, v_ref, q_ref, o_ref, g_ref, wk_ref, wq_ref, hg_ref,
                  out_ref,
                  cpad, ktt, qt, kctt, vext, vcext, gs, gsc, cf, cb, hf, hb):
    L = MLSTM_CHUNK
    t_lat = k_ref.shape[1]
    t_ctx = kc_ref.shape[1]
    nc = t_lat // L
    ncc = t_ctx // L
    hps = k_ref.shape[2] // DQK
    pad = SUBLANES
    half = QK_CONV // 2

    def conv_silu(src, w_ref, dst, t, scale, transposed):
        cpad[0:pad, :] = jnp.zeros((pad, LANES), F32)
        cpad[pad:pad + t, :] = src.astype(F32)
        cpad[pad + t:pad + t + pad, :] = jnp.zeros((pad, LANES), F32)
        for r in range(t // L):
            base = pad + r * L - half
            acc = w_ref[0:1, :] * cpad[base:base + L, :]
            for j in range(1, QK_CONV):
                acc = acc + w_ref[j:j + 1, :] * cpad[base + j:base + j + L, :]
            y = _silu(acc)
            if scale is not None:
                y = y * scale
            dst[r] = (y.T if transposed else y).astype(BF16)

    def fill_vext(dst, src, t):
        dst[:, 0:DV] = src
        dst[:, DV:DV + LANES] = jnp.ones((t, LANES), BF16)

    for hh in range(hps):
        qk_cols = slice(hh * DQK, (hh + 1) * DQK)
        v_cols = slice(hh * DV, (hh + 1) * DV)
        conv_silu(kc_ref[0, :, qk_cols], wk_ref.at[:, qk_cols], kctt.at[hh], t_ctx, DQK ** -0.5, True)
        conv_silu(k_ref[0, :, qk_cols], wk_ref.at[:, qk_cols], ktt.at[hh], t_lat, DQK ** -0.5, True)
        conv_silu(q_ref[0, :, qk_cols], wq_ref.at[:, qk_cols], qt.at[hh], t_lat, None, False)
        fill_vext(vcext.at[hh], vc_ref[0, :, v_cols], t_ctx)
        fill_vext(vext.at[hh], v_ref[0, :, v_cols], t_lat)

    def gate_prep(garr, dst):
        for d in range(2):
            li = garr[2 * d]
            lf = _log_sigmoid(garr[2 * d + 1])
            b = _scan_lanes(lf, jnp.add, 0.0, reverse=(d == 1))
            b_end = b[:, LANES - 1:LANES] if d == 0 else b[:, 0:1]
            r = li - b
            a = b_end + r
            m_loc = jnp.max(a, axis=1, keepdims=True)
            dst[G_ROWS * d + G_B] = b
            dst[G_ROWS * d + G_R] = r
            dst[G_ROWS * d + G_W] = jnp.exp(a - m_loc)
            dst[G_ROWS * d + G_BEND] = jnp.broadcast_to(b_end, b.shape)
            dst[G_ROWS * d + G_MLOC] = jnp.broadcast_to(m_loc, b.shape)
            dst[G_ROWS * d + G_RMAX] = _scan_lanes(r, jnp.maximum, -jnp.inf, reverse=(d == 1))

    for hh in range(hps):
        gate_prep(gc_ref[hh, :, 0], gsc.at[hh])
        gate_prep(g_ref[hh, :, 0], gs.at[hh])

    cf[...] = jnp.zeros(cf.shape, F32)
    cb[...] = jnp.zeros(cb.shape, F32)

    row_i = lax.broadcasted_iota(jnp.int32, (L, L), 0)
    col_i = lax.broadcasted_iota(jnp.int32, (L, L), 1)
    visible = (col_i <= row_i, col_i >= row_i)

    def per_token(row):
        return jnp.broadcast_to(row, (L, L)).T

    def step(hh, c, d, g_all, kt_all, v_all, q_all, c_all, m_prev):
        off = c * L
        g_scr, kt_scr, v_scr, c_scr = g_all.at[hh], kt_all.at[hh], v_all.at[hh], c_all.at[hh]
        q_scr = None if q_all is None else q_all.at[hh]
        g0 = G_ROWS * d
        kt_c = kt_scr[c]
        v_c = v_scr[pl.ds(off, L), :]
        b_end = g_scr[g0 + G_BEND, pl.ds(c, 1), 0:1]
        m_loc = g_scr[g0 + G_MLOC, pl.ds(c, 1), 0:1]
        c_prev = c_scr[...]
        h = None
        if q_scr is not None:
            q_c = q_scr[c]
            r_row = g_scr[g0 + G_R, pl.ds(c, 1), :]
            u = jnp.maximum(per_token(g_scr[g0 + G_RMAX, pl.ds(c, 1), :]), m_prev)
            b_t = per_token(g_scr[g0 + G_B, pl.ds(c, 1), :])
            qk = jnp.dot(q_c, kt_c, preferred_element_type=F32)
            s = (qk * jnp.exp(jnp.where(visible[d], r_row - u, -jnp.inf))).astype(BF16)
            wi = jnp.exp(m_prev - u)
            inter = jnp.dot(q_c, c_prev.astype(BF16), preferred_element_type=F32)
            res = jnp.dot(s, v_c, preferred_element_type=F32) \
                + jnp.concatenate([wi] * (c_prev.shape[1] // L), axis=1) * inter
            inv = 1.0 / jnp.maximum(jnp.abs(res[:, DV:DV + L]), jnp.exp(-(b_t + u)))
            h = res[:, 0:DV] * jnp.concatenate([inv] * (DV // L), axis=1)
        kwt = (kt_c.astype(F32) * g_scr[g0 + G_W, pl.ds(c, 1), :]).astype(BF16)
        c_loc = jnp.dot(kwt, v_c, preferred_element_type=F32)
        m_new = jnp.maximum(b_end + m_prev, m_loc)
        sp = jnp.exp(b_end + m_prev - m_new)
        sl = jnp.exp(m_loc - m_new)
        c_scr[...] = sp * c_prev + sl * c_loc
        return m_new, h

    def finish(hh, c, h):
        off = c * L
        cols = slice(hh * DV, (hh + 1) * DV)
        h = h * lax.rsqrt(jnp.mean(h * h, axis=-1, keepdims=True) + EPS)
        y = (h * hg_ref[:, cols]) * _sigmoid(o_ref[0, pl.ds(off, L), cols].astype(F32))
        out_ref[0, pl.ds(off, L), cols] = y.astype(BF16)

    ms = [jnp.zeros((1, 1), F32)] * (2 * hps)
    for i in range(ncc):
        for hh in range(hps):
            ms[2 * hh], _ = step(hh, i, 0, gsc, kctt, vcext, None, cf, ms[2 * hh])
            ms[2 * hh + 1], _ = step(hh, ncc - 1 - i, 1, gsc, kctt, vcext, None, cb, ms[2 * hh + 1])

    assert nc % 2 == 0

    def scan_step(i, carry, second_half):
        c_f, c_b = i, nc - 1 - i
        off_f, off_b = c_f * L, c_b * L
        out = []
        for hh in range(hps):
            m_f, h_f = step(hh, c_f, 0, gs, ktt, vext, qt, cf, carry[2 * hh])
            m_b, h_b = step(hh, c_b, 1, gs, ktt, vext, qt, cb, carry[2 * hh + 1])
            if second_half:
                finish(hh, c_f, h_f + hb[hh, pl.ds(off_f, L), :])
                finish(hh, c_b, h_b + hf[hh, pl.ds(off_b, L), :])
            else:
                hf[hh, pl.ds(off_f, L), :] = h_f
                hb[hh, pl.ds(off_b, L), :] = h_b
            out += [m_f, m_b]
        return tuple(out)

    carry = tuple(ms)
    for i in range(nc):
        carry = scan_step(i, carry, second_half=i >= nc // 2)


MLSTM_HEADS_PER_STEP = 2


def _mlstm(p, pc, g4, gc4, wk, wq, head_g):
    b_, t, _ = p.shape
    tc = pc.shape[1]
    L = MLSTM_CHUNK
    nc, ncc = t // L, tc // L
    ext = DV + LANES
    hps = MLSTM_HEADS_PER_STEP
    assert N_HEADS % hps == 0
    v_blk = N_HEADS * DQK // (hps * DV)
    q_blk = (N_HEADS * DQK + N_HEADS * DV) // (hps * DQK)
    o_blk = (2 * N_HEADS * DQK + N_HEADS * DV) // (hps * DV)
    return pl.pallas_call(
        _mlstm_kernel,
        grid=(b_, N_HEADS // hps),
        in_specs=[
            pl.BlockSpec((1, tc, hps * DQK), lambda b, h: (b, 0, h)),
            pl.BlockSpec((1, tc, hps * DV), lambda b, h: (b, 0, v_blk + h)),
            pl.BlockSpec((hps, 4, 1, ncc, L), lambda b, h: (h, 0, b, 0, 0)),
            pl.BlockSpec((1, t, hps * DQK), lambda b, h: (b, 0, h)),
            pl.BlockSpec((1, t, hps * DV), lambda b, h: (b, 0, v_blk + h)),
            pl.BlockSpec((1, t, hps * DQK), lambda b, h: (b, 0, q_blk + h)),
            pl.BlockSpec((1, t, hps * DV), lambda b, h: (b, 0, o_blk + h)),
            pl.BlockSpec((hps, 4, 1, nc, L), lambda b, h: (h, 0, b, 0, 0)),
            pl.BlockSpec((SUBLANES, hps * DQK), lambda b, h: (0, h)),
            pl.BlockSpec((SUBLANES, hps * DQK), lambda b, h: (0, h)),
            pl.BlockSpec((1, hps * DV), lambda b, h: (0, h)),
        ],
        out_specs=pl.BlockSpec((1, t, hps * DV), lambda b, h: (b, 0, h)),
        out_shape=jax.ShapeDtypeStruct((b_, t, N_HEADS * DV), BF16),
        scratch_shapes=[
            pltpu.VMEM((t + 2 * SUBLANES, LANES), F32),
            pltpu.VMEM((hps, nc, DQK, L), BF16),
            pltpu.VMEM((hps, nc, L, DQK), BF16),
            pltpu.VMEM((hps, ncc, DQK, L), BF16),
            pltpu.VMEM((hps, t, ext), BF16),
            pltpu.VMEM((hps, tc, ext), BF16),
            pltpu.VMEM((hps, 2 * G_ROWS, nc, L), F32),
            pltpu.VMEM((hps, 2 * G_ROWS, ncc, L), F32),
            pltpu.VMEM((hps, DQK, ext), F32),
            pltpu.VMEM((hps, DQK, ext), F32),
            pltpu.VMEM((hps, t, DV), F32),
            pltpu.VMEM((hps, t, DV), F32),
        ],
        compiler_params=_cparams(("arbitrary", "arbitrary")),
    )(pc, pc, gc4, p, p, p, p, g4, wk, wq, head_g)


def _pool_kernel(u_ref, pw_ref, ps_ref, out_ref, spad):
    t = u_ref.shape[1]
    rows = t // GRID_W
    blk = 256
    halo = (max(POOL_WINDOWS) // 2) * GRID_W
    spad[0:halo, :] = jnp.zeros((halo, POOL_GC), F32)
    spad[halo + t:halo + t + halo, :] = jnp.zeros((halo, POOL_GC), F32)
    ti = lax.broadcasted_iota(jnp.int32, (blk, blk), 0)
    tj = lax.broadcasted_iota(jnp.int32, (blk, blk), 1)
    same_row = (ti >> GRID_SHIFT) == (tj >> GRID_SHIFT)
    diff = tj - ti
    tok = lax.broadcasted_iota(jnp.int32, (blk, 1), 0)
    for gi, w in enumerate(POOL_WINDOWS):
        hw = w // 2
        c0, c1 = gi * POOL_GC, (gi + 1) * POOL_GC
        band = jnp.where(same_row & (diff >= -hw) & (diff <= hw - 1), 1.0, 0.0).astype(BF16)
        for r in range(t // blk):
            spad[halo + r * blk:halo + (r + 1) * blk, :] = jnp.dot(
                band, u_ref[0, r * blk:(r + 1) * blk, c0:c1], preferred_element_type=F32)
        for r in range(t // blk):
            base = halo + r * blk
            acc = spad[base - hw * GRID_W:base - hw * GRID_W + blk, :]
            for j in range(-hw + 1, hw):
                acc = acc + spad[base + j * GRID_W:base + j * GRID_W + blk, :]
            tt = tok + r * blk
            gr = tt >> GRID_SHIFT
            gc = tt & (GRID_W - 1)
            cr = jnp.minimum(gr + hw, rows) - jnp.maximum(gr - hw, 0)
            cc = jnp.minimum(gc + hw, GRID_W) - jnp.maximum(gc - hw, 0)
            cnt = (cr * cc).astype(F32)
            d = acc / cnt - u_ref[0, r * blk:(r + 1) * blk, c0:c1].astype(F32)
            y = jnp.dot(d.astype(BF16), pw_ref[gi], preferred_element_type=F32) * ps_ref[:, c0:c1]
            out_ref[0, r * blk:(r + 1) * blk, c0:c1] = y.astype(BF16)


def _pool(p, pool_w, pool_scale):
    b_, t, _ = p.shape
    pw = len(POOL_WINDOWS) * POOL_GC
    halo = (max(POOL_WINDOWS) // 2) * GRID_W
    return pl.pallas_call(
        _pool_kernel,
        grid=(b_,),
        in_specs=[pl.BlockSpec((1, t, pw), lambda b: (b, 0, 3)),
                  pl.BlockSpec((len(POOL_WINDOWS), POOL_GC, POOL_GC), lambda b: (0, 0, 0)),
                  pl.BlockSpec((1, pw), lambda b: (0, 0))],
        out_specs=pl.BlockSpec((1, t, pw), lambda b: (b, 0, 0)),
        out_shape=jax.ShapeDtypeStruct((b_, t, pw), BF16),
        scratch_shapes=[pltpu.VMEM((t + 2 * halo, POOL_GC), F32)],
        compiler_params=_cparams(("arbitrary",)),
    )(p, pool_w, pool_scale)


def _outproj_kernel(ym_ref, yp_ref, wm_ref, wp_ref, x_ref, mod_ref, n2_ref, wr_ref,
                    x1_ref, h2_ref, lg_ref):
    tm = x_ref.shape[0]
    half = tm // 2
    scale2 = n2_ref[...] * (1.0 + mod_ref[0, 4:5, :])
    for r0 in (0, half):
        rows = slice(r0, r0 + half)
        mix = (jnp.dot(ym_ref[rows, :], wm_ref[...], preferred_element_type=F32)
               + jnp.dot(yp_ref[rows, :], wp_ref[...], preferred_element_type=F32))
        x1 = x_ref[rows, :] + mod_ref[0, 2:3, :] * mix
        x1_ref[rows, :] = x1
        y = x1 * lax.rsqrt(jnp.mean(x1 * x1, axis=-1, keepdims=True) + EPS)
        h2 = y * scale2 + mod_ref[0, 3:4, :]
        g0, g1 = r0 // SUBLANES, (r0 + half) // SUBLANES
        for j in range(h2_ref.shape[1]):
            h2_ref[g0:g1, j, :, :] = h2[:, j * LANES:(j + 1) * LANES].reshape(g1 - g0, SUBLANES, LANES)
        lg_ref[rows, :] = jnp.dot(h2.astype(BF16), wr_ref[...], preferred_element_type=F32)


def _outproj(ym, yp, wm, wp, x2, mod, rows_per_mod, n2, wr, tm):
    m, d = x2.shape
    km = ym.shape[1]
    kp = yp.shape[1]
    tiles_per_mod = rows_per_mod // tm
    return pl.pallas_call(
        _outproj_kernel,
        grid=(m // tm,),
        in_specs=[pl.BlockSpec((tm, km), lambda i: (i, 0)),
                  pl.BlockSpec((tm, kp), lambda i: (i, 0)),
                  pl.BlockSpec((km, d), lambda i: (0, 0), pipeline_mode=pl.Buffered(1)),
                  pl.BlockSpec((kp, d), lambda i: (0, 0), pipeline_mode=pl.Buffered(1)),
                  pl.BlockSpec((tm, d), lambda i: (i, 0)),
                  pl.BlockSpec((1, N_MOD, d), lambda i: (i // tiles_per_mod, 0, 0)),
                  pl.BlockSpec((1, d), lambda i: (0, 0)),
                  pl.BlockSpec((d, LANES), lambda i: (0, 0))],
        out_specs=[pl.BlockSpec((tm, d), lambda i: (i, 0)),
                   pl.BlockSpec((tm // SUBLANES, d // LANES, SUBLANES, LANES), lambda i: (i, 0, 0, 0)),
                   pl.BlockSpec((tm, LANES), lambda i: (i, 0))],
        out_shape=[jax.ShapeDtypeStruct((m, d), F32),
                   jax.ShapeDtypeStruct((m // SUBLANES, d // LANES, SUBLANES, LANES), F32),
                   jax.ShapeDtypeStruct((m, LANES), F32)],
        compiler_params=_cparams(("arbitrary",)),
    )(ym, yp, wm, wp, x2, mod, n2, wr)


def _scan_lanes_i32(x):
    n = x.shape[1]
    lane = lax.broadcasted_iota(jnp.int32, x.shape, 1)
    k = 1
    while k < n:
        x = x + jnp.where(lane >= k, pltpu.roll(x, k, 1), 0)
        k *= 2
    return x


COMBINE_TILE = 256
GRANULE = 2 * SUBLANES


def _route_kernel(lg_ref, idx_ref, gate_ref, tokrep_ref, bounds_ref, afft):
    t = lg_ref.shape[1]
    cap = idx_ref.shape[2]
    lane = lax.broadcasted_iota(jnp.int32, (t, LANES), 1)
    lg = jnp.where(lane < N_EXPERTS, lg_ref[0], -jnp.inf)
    ex = jnp.exp(lg - jnp.max(lg, axis=1, keepdims=True))
    aff = ex / jnp.sum(ex, axis=1, keepdims=True)
    for r in range(t // LANES):
        afft[:, r * LANES:(r + 1) * LANES] = aff[r * LANES:(r + 1) * LANES, :].T
    aff_t = afft[0:N_EXPERTS, :]

    def count(mask):
        return jnp.sum(jnp.where(mask, 1.0, 0.0), axis=1, keepdims=True).astype(jnp.int32)

    def bit_step(i, thr_bits):
        cand = thr_bits | (jnp.int32(1) << (30 - i))
        cand_f = lax.bitcast_convert_type(cand, F32)
        return jnp.where(count(aff_t >= cand_f) >= cap, cand, thr_bits)

    thr_bits = lax.fori_loop(0, 31, bit_step, jnp.zeros((N_EXPERTS, 1), jnp.int32))
    thr = lax.bitcast_convert_type(thr_bits, F32)
    gt = aff_t > thr
    eq = aff_t == thr
    need = cap - count(gt)
    eq_i = jnp.where(eq, 1, 0)
    eq_rank = _scan_lanes_i32(eq_i) - eq_i
    sel = gt | (eq & (eq_rank < need))
    sel_i = jnp.where(sel, 1, 0)
    pos = _scan_lanes_i32(sel_i) - sel_i
    key = jnp.where(sel, pos, -1)

    a_hi = aff.astype(BF16).astype(F32)
    a_mid = (aff - a_hi).astype(BF16).astype(F32)
    a_lo = (aff - a_hi - a_mid).astype(BF16).astype(F32)
    tok = lax.broadcasted_iota(jnp.int32, (t, LANES), 0)
    vals = (a_hi + pltpu.roll(a_mid, N_EXPERTS, 1) + pltpu.roll(a_lo, 2 * N_EXPERTS, 1)
            + jnp.where(lane == 3 * N_EXPERTS, (tok >> GRID_SHIFT).astype(F32), 0.0)
            + jnp.where(lane == 3 * N_EXPERTS + 1, (tok & (GRID_W - 1)).astype(F32), 0.0)).astype(BF16)
    slot = lax.broadcasted_iota(jnp.int32, (cap, 1), 0)
    for e in range(N_EXPERTS):
        onehot = jnp.where(slot == key[e:e + 1, :], 1.0, 0.0).astype(BF16)
        res = jnp.dot(onehot, vals, preferred_element_type=F32)
        rt = jnp.concatenate([res[r * LANES:(r + 1) * LANES, :].T for r in range(cap // LANES)], axis=1)
        gate_ref[0, e:e + 1, :] = (rt[e:e + 1, :] + rt[N_EXPERTS + e:N_EXPERTS + e + 1, :]) \
            + rt[2 * N_EXPERTS + e:2 * N_EXPERTS + e + 1, :]
        tokf = rt[3 * N_EXPERTS:3 * N_EXPERTS + 1, :] * float(GRID_W) + rt[3 * N_EXPERTS + 1:3 * N_EXPERTS + 2, :]
        idx_ref[0, e:e + 1, :] = tokf.astype(jnp.int32)
        tok_col = res[:, 3 * N_EXPERTS:3 * N_EXPERTS + 1] * float(GRID_W) \
            + res[:, 3 * N_EXPERTS + 1:3 * N_EXPERTS + 2]
        tokrep_ref[0, e * cap:(e + 1) * cap, :] = jnp.broadcast_to(tok_col, (cap, LANES)).astype(jnp.int32)

    n_tiles = t // COMBINE_TILE
    lane_b = lax.broadcasted_iota(jnp.int32, (N_EXPERTS, LANES), 1)
    bounds = jnp.where(lane_b >= n_tiles, cap, 0)
    for i in range(n_tiles):
        bounds = jnp.where(lane_b == i, pos[:, i * COMBINE_TILE:i * COMBINE_TILE + 1], bounds)
    bounds_ref[0] = bounds


def _route(lg3, cap):
    b_, t, _ = lg3.shape
    return pl.pallas_call(
        _route_kernel,
        grid=(b_,),
        in_specs=[pl.BlockSpec((1, t, LANES), lambda b: (b, 0, 0))],
        out_specs=[pl.BlockSpec((1, N_EXPERTS, cap), lambda b: (b, 0, 0)),
                   pl.BlockSpec((1, N_EXPERTS, cap), lambda b: (b, 0, 0)),
                   pl.BlockSpec((1, N_EXPERTS * cap, LANES), lambda b: (b, 0, 0)),
                   pl.BlockSpec((1, N_EXPERTS, LANES), lambda b: (b, 0, 0))],
        out_shape=[jax.ShapeDtypeStruct((b_, N_EXPERTS, cap), jnp.int32),
                   jax.ShapeDtypeStruct((b_, N_EXPERTS, cap), F32),
                   jax.ShapeDtypeStruct((b_, N_EXPERTS * cap, LANES), jnp.int32),
                   jax.ShapeDtypeStruct((b_, N_EXPERTS, LANES), jnp.int32)],
        scratch_shapes=[pltpu.VMEM((LANES, t), F32)],
        compiler_params=_cparams(("arbitrary",)),
    )(lg3)


def _ffn_kernel(idx_ref, gate_ref, h2_hbm, wg_ref, wu_ref, wd_ref, y_ref,
                xs_buf, wg_s, wu_s, wd_s, sem):
    p = pl.program_id(0)
    b = pl.program_id(1)
    n_e = pl.num_programs(0) - 1
    n_b = pl.num_programs(1)
    k_chunks, cap = xs_buf.shape[1], xs_buf.shape[2]
    groups = h2_hbm.shape[0] // n_b
    step = p * n_b + b
    cur = step % 2
    nxt = 1 - cur
    last_b = b == n_b - 1
    b_next = jnp.where(last_b, 0, b + 1)
    p_next = jnp.where(last_b, p + 1, p)

    def issue_gather():
        def issue(r, carry):
            tok = idx_ref[0, 0, r]
            pltpu.make_async_copy(h2_hbm.at[b_next * groups + (tok >> SUBLANE_SHIFT), :, tok & (SUBLANES - 1), :],
                                  xs_buf.at[nxt, :, r, :], sem.at[nxt]).start()
            return carry

        lax.fori_loop(0, cap, issue, 0, unroll=True)

    def cast_weights():
        ws = p % 2
        rows_in = wg_ref.shape[1]
        rows_mid = wd_ref.shape[1]
        wg_s[ws, pl.ds(pl.multiple_of(b * rows_in, rows_in), rows_in), :] = wg_ref[0].astype(BF16)
        wu_s[ws, pl.ds(pl.multiple_of(b * rows_in, rows_in), rows_in), :] = wu_ref[0].astype(BF16)
        wd_s[ws, pl.ds(pl.multiple_of(b * rows_mid, rows_mid), rows_mid), :] = wd_ref[0].astype(BF16)

    def run_expert():
        pltpu.make_async_copy(xs_buf.at[cur], xs_buf.at[cur], sem.at[cur]).wait()
        ws = (p - 1) % 2
        xs = jnp.concatenate([xs_buf[cur, j].astype(BF16) for j in range(k_chunks)], axis=1)
        a = _silu(jnp.dot(xs, wg_s[ws], preferred_element_type=F32)) \
            * jnp.dot(xs, wu_s[ws], preferred_element_type=F32)
        y = jnp.dot(a.astype(BF16), wd_s[ws], preferred_element_type=F32)
        g_row = gate_ref[0]
        g_col = jnp.concatenate(
            [jnp.broadcast_to(g_row[:, k * LANES:(k + 1) * LANES], (LANES, LANES)).T
             for k in range(cap // LANES)], axis=0)
        y_ref[0, 0] = (y * jnp.concatenate([g_col] * (y.shape[1] // LANES), axis=1)).astype(BF16)

    @pl.when((p_next >= 1) & (p_next <= n_e))
    def _():
        issue_gather()

    @pl.when(p == 0)
    def _():
        cast_weights()
        y_ref[0, 0] = jnp.zeros(y_ref.shape[2:], BF16)

    @pl.when((p >= 1) & (p < n_e))
    def _():
        cast_weights()
        run_expert()

    @pl.when(p == n_e)
    def _():
        run_expert()


def _ffn(idx, gate, h2, wg, wu, wd):
    e_, d, f = wg.shape
    b_, _, cap = idx.shape
    assert h2.shape[1] * h2.shape[3] == d and h2.shape[2] == SUBLANES and h2.shape[0] % b_ == 0
    assert d % b_ == 0 and f % b_ == 0
    rows_in, rows_mid = d // b_, f // b_
    assert rows_in % (2 * SUBLANES) == 0 and rows_mid % (2 * SUBLANES) == 0

    def next_tokens(p, b):
        last_b = b == b_ - 1
        b_next = jnp.where(last_b, 0, b + 1)
        e_next = jnp.clip(jnp.where(last_b, p, p - 1), 0, e_ - 1)
        return (b_next * e_ + e_next, 0, 0)

    def weight_piece(p, b):
        return (jnp.minimum(p, e_ - 1), b, 0)

    return pl.pallas_call(
        _ffn_kernel,
        grid=(e_ + 1, b_),
        in_specs=[pl.BlockSpec((1, 1, cap), next_tokens, memory_space=pltpu.SMEM),
                  pl.BlockSpec((1, 1, cap), lambda p, b: (b * e_ + jnp.maximum(p - 1, 0), 0, 0)),
                  pl.BlockSpec(memory_space=pl.ANY),
                  pl.BlockSpec((1, rows_in, f), weight_piece),
                  pl.BlockSpec((1, rows_in, f), weight_piece),
                  pl.BlockSpec((1, rows_mid, d), weight_piece)],
        out_specs=pl.BlockSpec((1, 1, cap, d), lambda p, b: (b, (p + e_) % (e_ + 1), 0, 0)),
        out_shape=jax.ShapeDtypeStruct((b_, e_ + 1, cap, d), BF16),
        scratch_shapes=[pltpu.VMEM((2, d // LANES, cap, LANES), F32),
                        pltpu.VMEM((2, d, f), BF16),
                        pltpu.VMEM((2, d, f), BF16),
                        pltpu.VMEM((2, f, d), BF16),
                        pltpu.SemaphoreType.DMA((2,))],
        compiler_params=_cparams(("arbitrary", "arbitrary")),
    )(idx.reshape(b_ * e_, 1, cap), gate.reshape(b_ * e_, 1, cap), h2, wg, wu, wd)


COMBINE_CHUNK = 256


def _combine_kernel(bounds_ref, y_hbm, tok_ref, x1_ref, mod_ref, nf_ref, o_ref, y_buf, stage_y, stage_t, sem):
    b = pl.program_id(0)
    i = pl.program_id(1)
    n_b = pl.num_programs(0)
    n_e, cap = y_buf.shape[1], y_buf.shape[2]
    cur = b % 2

    def y_copy(bb, slot):
        return pltpu.make_async_copy(y_hbm.at[bb, pl.ds(0, n_e)], y_buf.at[slot], sem.at[slot])

    @pl.when(i == 0)
    def _():
        @pl.when(b == 0)
        def _():
            y_copy(0, 0).start()

        y_copy(b, cur).wait()

        @pl.when(b + 1 < n_b)
        def _():
            y_copy(b + 1, 1 - cur).start()

    tile = o_ref.shape[1]
    n_bounds = pl.num_programs(1) + 1
    per_chunk = COMBINE_CHUNK // GRANULE
    g_shift = GRANULE.bit_length() - 1
    assert per_chunk & (per_chunk - 1) == 0 and GRANULE == 1 << g_shift

    @pl.when((b == 0) & (i == 0))
    def _():
        stage_y[...] = jnp.zeros(stage_y.shape, BF16)
        stage_t[...] = jnp.full(stage_t.shape, -1, jnp.int32)

    o_ref[...] = jnp.zeros(o_ref.shape, F32)
    tok_lane = lax.broadcasted_iota(jnp.int32, (COMBINE_CHUNK, LANES), 1) + i * tile

    def flush():
        onehot_t = jnp.concatenate(
            [jnp.where(stage_t[...] == tok_lane + j * LANES, 1.0, 0.0).astype(BF16)
             for j in range(tile // LANES)], axis=1)
        o_ref[0] += lax.dot_general(onehot_t, stage_y[...], (((0,), (0,)), ((), ())),
                                    preferred_element_type=F32)
        stage_t[...] = jnp.full(stage_t.shape, -1, jnp.int32)

    def per_expert(e, k):
        base = (b * n_e + e) * n_bounds + i
        lo = bounds_ref[base]
        hi = bounds_ref[base + 1]
        g_lo = lax.shift_right_logical(lo, g_shift)
        g_hi = jnp.where(hi > lo, lax.shift_right_logical(hi + GRANULE - 1, g_shift), g_lo)

        def per_granule(g, k):
            src = pl.multiple_of(g * GRANULE, GRANULE)
            dst = pl.multiple_of((k & (per_chunk - 1)) * GRANULE, GRANULE)
            stage_y[pl.ds(dst, GRANULE), :] = y_buf[cur, e, pl.ds(src, GRANULE), :]
            stage_t[pl.ds(dst, GRANULE), :] = tok_ref[0, pl.ds(pl.multiple_of(e * cap + src, GRANULE), GRANULE), :]
            k = k + 1

            @pl.when((k & (per_chunk - 1)) == 0)
            def _():
                flush()

            return k

        return lax.fori_loop(g_lo, g_hi, per_granule, k)

    k = lax.fori_loop(0, n_e, per_expert, jnp.int32(0))

    @pl.when((k & (per_chunk - 1)) != 0)
    def _():
        flush()

    x = x1_ref[0] + mod_ref[0, 5:6, :] * o_ref[0]
    o_ref[0] = (x * lax.rsqrt(jnp.mean(x * x, axis=-1, keepdims=True) + EPS)) * nf_ref[...]


def _combine(y, tokrep, bounds, x1, mod, nf):
    b_, e_, cap, d = y.shape
    e_ -= 1
    t = x1.shape[1]
    tile = COMBINE_TILE
    assert cap % GRANULE == 0 and tile % LANES == 0 and bounds.shape == (b_, e_, t // tile + 1)
    grid_spec = pltpu.PrefetchScalarGridSpec(
        num_scalar_prefetch=1,
        grid=(b_, t // tile),
        in_specs=[pl.BlockSpec(memory_space=pl.ANY),
                  pl.BlockSpec((1, e_ * cap, LANES), lambda b, i, bnd: (b, 0, 0)),
                  pl.BlockSpec((1, tile, d), lambda b, i, bnd: (b, i, 0)),
                  pl.BlockSpec((1, N_MOD, d), lambda b, i, bnd: (b, 0, 0)),
                  pl.BlockSpec((1, d), lambda b, i, bnd: (0, 0))],
        out_specs=pl.BlockSpec((1, tile, d), lambda b, i, bnd: (b, i, 0)),
        scratch_shapes=[pltpu.VMEM((2, e_, cap, d), BF16),
                        pltpu.VMEM((COMBINE_CHUNK, d), BF16),
                        pltpu.VMEM((COMBINE_CHUNK, LANES), jnp.int32),
                        pltpu.SemaphoreType.DMA((2,))])
    return pl.pallas_call(
        _combine_kernel,
        grid_spec=grid_spec,
        out_shape=jax.ShapeDtypeStruct((b_, t, d), F32),
        compiler_params=_cparams(("arbitrary", "arbitrary")),
    )(bounds.reshape(-1), y, tokrep, x1, mod, nf)


def _gate_layout(gt, b_, t):
    return gt.reshape(N_HEADS, 4, b_, t // MLSTM_CHUNK, MLSTM_CHUNK)


def kernel(x, c, ctx, c_ctx, w_mod, b_mod, norm1, w_in, conv_q_w, conv_k_w, b_gates, head_g, pool_w,
           pool_scale, w_out, norm2, w_router, w_gate, w_up, w_down, norm_f):
    b_, t, d = x.shape
    tc = ctx.shape[1]
    depth = w_mod.shape[0]
    assert depth == 1
    l = 0
    qk_w = N_HEADS * DQK
    mw = N_HEADS * DV
    pool_wd = len(POOL_WINDOWS) * POOL_GC
    n_gate = 2 * 2 * N_HEADS
    k_off, v_off = 0, qk_w
    g_off = v_off + mw
    q_off = g_off + n_gate
    o_off = q_off + qk_w
    p_off = o_off + mw
    cap = EC_FACTOR * t // N_EXPERTS

    mod_rows = -(-(b_ + 1) // SUBLANES) * SUBLANES
    cs = jnp.zeros((mod_rows, d), F32).at[:b_].set(c).at[b_].set(c_ctx)
    mod_all = _modulation(cs, w_mod[l], b_mod[l])
    mod = mod_all[:b_].reshape(b_, N_MOD, d)
    mod_c = mod_all[b_:b_ + 1].reshape(1, N_MOD, d)

    wl = w_in[l]
    w_main = jnp.concatenate([wl[:, k_off:v_off], wl[:, v_off:g_off], wl[:, q_off:o_off],
                              wl[:, o_off:p_off], wl[:, p_off:]], axis=1).astype(BF16)
    perm = jnp.arange(n_gate).reshape(2, 2, N_HEADS).transpose(2, 0, 1).reshape(-1)
    w_g = wl[:, g_off:q_off][:, perm].T.astype(BF16)
    b_g = b_gates[l][perm].reshape(n_gate, 1)
    n1 = norm1[l].reshape(1, d)

    p, g = _inproj(x.reshape(b_ * t, d), mod, t, n1, w_main, w_g, b_g, PROJ_ROWS)
    pc, gc = _inproj(ctx.reshape(b_ * tc, d), mod_c, b_ * tc, n1, w_main[:, :qk_w + mw], w_g, b_g,
                     min(PROJ_ROWS, b_ * tc))
    p = p.reshape(b_, t, -1)
    pc = pc.reshape(b_, tc, -1)

    wk = jnp.pad(conv_k_w[l], ((0, SUBLANES - QK_CONV), (0, 0)))
    wq = jnp.pad(conv_q_w[l], ((0, SUBLANES - QK_CONV), (0, 0)))
    ym = _mlstm(p, pc, _gate_layout(g, b_, t), _gate_layout(gc, b_, tc), wk, wq, head_g[l].reshape(1, mw))
    yp = _pool(p, pool_w[l].astype(BF16), pool_scale[l].reshape(1, pool_wd))

    wo = w_out[l].astype(BF16)
    wr = jnp.pad(w_router[l], ((0, 0), (0, LANES - N_EXPERTS))).astype(BF16)
    x1, h2, lg = _outproj(ym.reshape(b_ * t, mw), yp.reshape(b_ * t, pool_wd), wo[:mw], wo[mw:],
                          x.reshape(b_ * t, d), mod, t, norm2[l].reshape(1, d), wr, PROJ_ROWS)

    idx, gate, tokrep, bounds = _route(lg.reshape(b_, t, LANES), cap)

    y = _ffn(idx, gate, h2, w_gate[l], w_up[l], w_down[l])
    return _combine(y, tokrep, bounds[:, :, :t // COMBINE_TILE + 1], x1.reshape(b_, t, d), mod,
                    norm_f.reshape(1, d))
```

```python
import functools

import jax
import jax.numpy as jnp
from jax import lax
from jax.experimental import pallas as pl
from jax.experimental.pallas import tpu as pltpu

F32 = jnp.float32
BF16 = jnp.bfloat16

N_HEADS = 4
DQK = 128
DV = 256
QK_CONV = 5
POOL_WINDOWS = (2, 4, 8, 16)
POOL_GC = 256
GRID_W = 64
GRID_SHIFT = GRID_W.bit_length() - 1
N_EXPERTS = 16
EC_FACTOR = 2
N_MOD = 6
EPS = 1e-6

LANES = 128
SUBLANES = 8
SUBLANE_SHIFT = SUBLANES.bit_length() - 1
PROJ_ROWS = 512
MLSTM_CHUNK = 128
VMEM_LIMIT = 56 * 1024 * 1024


def _cparams(sem):
    return pltpu.CompilerParams(dimension_semantics=sem, vmem_limit_bytes=VMEM_LIMIT)


def _sigmoid(x):
    return 1.0 / (1.0 + jnp.exp(-x))


def _silu(x):
    return x * _sigmoid(x)


def _log_sigmoid(x):
    return jnp.minimum(x, 0.0) - jnp.log(1.0 + jnp.exp(-jnp.abs(x)))


def _mod_kernel(c_ref, w_ref, b_ref, o_ref):
    s = _silu(c_ref[...]).astype(BF16)
    o_ref[...] = jnp.dot(s, w_ref[...].astype(BF16), preferred_element_type=F32) + b_ref[...]


def _modulation(cs, w_mod, b_mod):
    rows, d = cs.shape
    n = w_mod.shape[1]
    tn = 1024
    return pl.pallas_call(
        _mod_kernel,
        grid=(n // tn,),
        in_specs=[pl.BlockSpec((rows, d), lambda j: (0, 0)),
                  pl.BlockSpec((d, tn), lambda j: (0, j)),
                  pl.BlockSpec((1, tn), lambda j: (0, j))],
        out_specs=pl.BlockSpec((rows, tn), lambda j: (0, j)),
        out_shape=jax.ShapeDtypeStruct((rows, n), F32),
        compiler_params=_cparams(("arbitrary",)),
    )(cs, w_mod, b_mod.reshape(1, n))


INPROJ_COLS = 512


def _inproj_kernel(x_ref, mod_ref, n1_ref, w_ref, wg_ref, bg_ref, p_ref, gt_ref):
    tm = x_ref.shape[0]
    half = tm // 2
    n_cols = p_ref.shape[1]
    scale = n1_ref[...] * (1.0 + mod_ref[0, 1:2, :])
    for r0 in (0, half):
        rows = slice(r0, r0 + half)
        x = x_ref[rows, :]
        y = x * lax.rsqrt(jnp.mean(x * x, axis=-1, keepdims=True) + EPS)
        hb = (y * scale + mod_ref[0, 0:1, :]).astype(BF16)
        gt_ref[:, rows] = lax.dot_general(wg_ref[...], hb, (((1,), (1,)), ((), ())),
                                          preferred_element_type=F32) + bg_ref[...]
        for c0 in range(0, n_cols, INPROJ_COLS):
            p_ref[rows, c0:c0 + INPROJ_COLS] = jnp.dot(
                hb, w_ref[:, c0:c0 + INPROJ_COLS], preferred_element_type=F32).astype(BF16)


def _inproj(x2, mod, rows_per_mod, n1, w, wg, bg, tm):
    m, d = x2.shape
    n_cols = w.shape[1]
    n_g = wg.shape[0]
    tiles_per_mod = rows_per_mod // tm
    assert n_cols % INPROJ_COLS == 0 and tm % (2 * LANES) == 0
    return pl.pallas_call(
        _inproj_kernel,
        grid=(m // tm,),
        in_specs=[pl.BlockSpec((tm, d), lambda i: (i, 0)),
                  pl.BlockSpec((1, N_MOD, d), lambda i: (i // tiles_per_mod, 0, 0)),
                  pl.BlockSpec((1, d), lambda i: (0, 0)),
                  pl.BlockSpec((d, n_cols), lambda i: (0, 0), pipeline_mode=pl.Buffered(1)),
                  pl.BlockSpec((n_g, d), lambda i: (0, 0)),
                  pl.BlockSpec((n_g, 1), lambda i: (0, 0))],
        out_specs=[pl.BlockSpec((tm, n_cols), lambda i: (i, 0)),
                   pl.BlockSpec((n_g, tm), lambda i: (0, i))],
        out_shape=[jax.ShapeDtypeStruct((m, n_cols), BF16),
                   jax.ShapeDtypeStruct((n_g, m), F32)],
        compiler_params=_cparams(("arbitrary",)),
    )(x2, mod, n1, w, wg, bg)


def _scan_lanes(x, op, fill, reverse):
    lane = lax.broadcasted_iota(jnp.int32, x.shape, 1)
    k = 1
    while k < LANES:
        if reverse:
            x = op(x, jnp.where(lane < LANES - k, pltpu.roll(x, LANES - k, 1), fill))
        else:
            x = op(x, jnp.where(lane >= k, pltpu.roll(x, k, 1), fill))
        k *= 2
    return x


G_B, G_R, G_W, G_BEND, G_MLOC, G_RMAX, G_ROWS = 0, 1, 2, 3, 4, 5, 6


def _mlstm_kernel(kc_ref, vc_ref, gc_ref, k_ref, v_ref, q_ref, o_ref, g_ref, wk_ref, wq_ref, hg_ref,
                  out_ref,
                  cpad, ktt, qt, kctt, vext, vcext, gs, gsc, cf, cb, hf, hb):
    L = MLSTM_CHUNK
    t_lat = k_ref.shape[1]
    t_ctx = kc_ref.shape[1]
    nc = t_lat // L
    ncc = t_ctx // L
    hps = k_ref.shape[2] // DQK
    pad = SUBLANES
    half = QK_CONV // 2

    def conv_silu(src, w_ref, dst, t, scale, transposed):
        cpad[0:pad, :] = jnp.zeros((pad, LANES), F32)
        cpad[pad:pad + t, :] = src.astype(F32)
        cpad[pad + t:pad + t + pad, :] = jnp.zeros((pad, LANES), F32)
        for r in range(t // L):
            base = pad + r * L - half
            acc = w_ref[0:1, :] * cpad[base:base + L, :]
            for j in range(1, QK_CONV):
                acc = acc + w_ref[j:j + 1, :] * cpad[base + j:base + j + L, :]
            y = _silu(acc)
            if scale is not None:
                y = y * scale
            dst[r] = (y.T if transposed else y).astype(BF16)

    def fill_vext(dst, src, t):
        dst[:, 0:DV] = src
        dst[:, DV:DV + LANES] = jnp.ones((t, LANES), BF16)

    for hh in range(hps):
        qk_cols = slice(hh * DQK, (hh + 1) * DQK)
        v_cols = slice(hh * DV, (hh + 1) * DV)
        conv_silu(kc_ref[0, :, qk_cols], wk_ref.at[:, qk_cols], kctt.at[hh], t_ctx, DQK ** -0.5, True)
        conv_silu(k_ref[0, :, qk_cols], wk_ref.at[:, qk_cols], ktt.at[hh], t_lat, DQK ** -0.5, True)
        conv_silu(q_ref[0, :, qk_cols], wq_ref.at[:, qk_cols], qt.at[hh], t_lat, None, False)
        fill_vext(vcext.at[hh], vc_ref[0, :, v_cols], t_ctx)
        fill_vext(vext.at[hh], v_ref[0, :, v_cols], t_lat)

    def gate_prep(garr, dst):
        for d in range(2):
            li = garr[2 * d]
            lf = _log_sigmoid(garr[2 * d + 1])
            b = _scan_lanes(lf, jnp.add, 0.0, reverse=(d == 1))
            b_end = b[:, LANES - 1:LANES] if d == 0 else b[:, 0:1]
            r = li - b
            a = b_end + r
            m_loc = jnp.max(a, axis=1, keepdims=True)
            dst[G_ROWS * d + G_B] = b
            dst[G_ROWS * d + G_R] = r
            dst[G_ROWS * d + G_W] = jnp.exp(a - m_loc)
            dst[G_ROWS * d + G_BEND] = jnp.broadcast_to(b_end, b.shape)
            dst[G_ROWS * d + G_MLOC] = jnp.broadcast_to(m_loc, b.shape)
            dst[G_ROWS * d + G_RMAX] = _scan_lanes(r, jnp.maximum, -jnp.inf, reverse=(d == 1))

    for hh in range(hps):
        gate_prep(gc_ref[hh, :, 0], gsc.at[hh])
        gate_prep(g_ref[hh, :, 0], gs.at[hh])

    cf[...] = jnp.zeros(cf.shape, F32)
    cb[...] = jnp.zeros(cb.shape, F32)

    row_i = lax.broadcasted_iota(jnp.int32, (L, L), 0)
    col_i = lax.broadcasted_iota(jnp.int32, (L, L), 1)
    visible = (col_i <= row_i, col_i >= row_i)

    def per_token(row):
        return jnp.broadcast_to(row, (L, L)).T

    def step(hh, c, d, g_all, kt_all, v_all, q_all, c_all, m_prev):
        off = c * L
        g_scr, kt_scr, v_scr, c_scr = g_all.at[hh], kt_all.at[hh], v_all.at[hh], c_all.at[hh]
        q_scr = None if q_all is None else q_all.at[hh]
        g0 = G_ROWS * d
        kt_c = kt_scr[c]
        v_c = v_scr[pl.ds(off, L), :]
        b_end = g_scr[g0 + G_BEND, pl.ds(c, 1), 0:1]
        m_loc = g_scr[g0 + G_MLOC, pl.ds(c, 1), 0:1]
        c_prev = c_scr[...]
        h = None
        if q_scr is not None:
            q_c = q_scr[c]
            r_row = g_scr[g0 + G_R, pl.ds(c, 1), :]
            u = jnp.maximum(per_token(g_scr[g0 + G_RMAX, pl.ds(c, 1), :]), m_prev)
            b_t = per_token(g_scr[g0 + G_B, pl.ds(c, 1), :])
            qk = jnp.dot(q_c, kt_c, preferred_element_type=F32)
            s = (qk * jnp.exp(jnp.where(visible[d], r_row - u, -jnp.inf))).astype(BF16)
            wi = jnp.exp(m_prev - u)
            inter = jnp.dot(q_c, c_prev.astype(BF16), preferred_element_type=F32)
            res = jnp.dot(s, v_c, preferred_element_type=F32) \
                + jnp.concatenate([wi] * (c_prev.shape[1] // L), axis=1) * inter
            inv = 1.0 / jnp.maximum(jnp.abs(res[:, DV:DV + L]), jnp.exp(-(b_t + u)))
            h = res[:, 0:DV] * jnp.concatenate([inv] * (DV // L), axis=1)
        kwt = (kt_c.astype(F32) * g_scr[g0 + G_W, pl.ds(c, 1), :]).astype(BF16)
        c_loc = jnp.dot(kwt, v_c, preferred_element_type=F32)
        m_new = jnp.maximum(b_end + m_prev, m_loc)
        sp = jnp.exp(b_end + m_prev - m_new)
        sl = jnp.exp(m_loc - m_new)
        c_scr[...] = sp * c_prev + sl * c_loc
        return m_new, h

    def finish(hh, c, h):
        off = c * L
        cols = slice(hh * DV, (hh + 1) * DV)
        h = h * lax.rsqrt(jnp.mean(h * h, axis=-1, keepdims=True) + EPS)
        y = (h * hg_ref[:, cols]) * _sigmoid(o_ref[0, pl.ds(off, L), cols].astype(F32))
        out_ref[0, pl.ds(off, L), cols] = y.astype(BF16)

    ms = [jnp.zeros((1, 1), F32)] * (2 * hps)
    for i in range(ncc):
        for hh in range(hps):
            ms[2 * hh], _ = step(hh, i, 0, gsc, kctt, vcext, None, cf, ms[2 * hh])
            ms[2 * hh + 1], _ = step(hh, ncc - 1 - i, 1, gsc, kctt, vcext, None, cb, ms[2 * hh + 1])

    assert nc % 2 == 0

    def scan_step(i, carry, second_half):
        c_f, c_b = i, nc - 1 - i
        off_f, off_b = c_f * L, c_b * L
        out = []
        for hh in range(hps):
            m_f, h_f = step(hh, c_f, 0, gs, ktt, vext, qt, cf, carry[2 * hh])
            m_b, h_b = step(hh, c_b, 1, gs, ktt, vext, qt, cb, carry[2 * hh + 1])
            if second_half:
                finish(hh, c_f, h_f + hb[hh, pl.ds(off_f, L), :])
                finish(hh, c_b, h_b + hf[hh, pl.ds(off_b, L), :])
            else:
                hf[hh, pl.ds(off_f, L), :] = h_f
                hb[hh, pl.ds(off_b, L), :] = h_b
            out += [m_f, m_b]
        return tuple(out)

    carry = tuple(ms)
    for i in range(nc):
        carry = scan_step(i, carry, second_half=i >= nc // 2)


MLSTM_HEADS_PER_STEP = 2


def _mlstm(p, pc, g4, gc4, wk, wq, head_g):
    b_, t, _ = p.shape
    tc = pc.shape[1]
    L = MLSTM_CHUNK
    nc, ncc = t // L, tc // L
    ext = DV + LANES
    hps = MLSTM_HEADS_PER_STEP
    assert N_HEADS % hps == 0
    v_blk = N_HEADS * DQK // (hps * DV)
    q_blk = (N_HEADS * DQK + N_HEADS * DV) // (hps * DQK)
    o_blk = (2 * N_HEADS * DQK + N_HEADS * DV) // (hps * DV)
    return pl.pallas_call(
        _mlstm_kernel,
        grid=(b_, N_HEADS // hps),
        in_specs=[
            pl.BlockSpec((1, tc, hps * DQK), lambda b, h: (b, 0, h)),
            pl.BlockSpec((1, tc, hps * DV), lambda b, h: (b, 0, v_blk + h)),
            pl.BlockSpec((hps, 4, 1, ncc, L), lambda b, h: (h, 0, b, 0, 0)),
            pl.BlockSpec((1, t, hps * DQK), lambda b, h: (b, 0, h)),
            pl.BlockSpec((1, t, hps * DV), lambda b, h: (b, 0, v_blk + h)),
            pl.BlockSpec((1, t, hps * DQK), lambda b, h: (b, 0, q_blk + h)),
            pl.BlockSpec((1, t, hps * DV), lambda b, h: (b, 0, o_blk + h)),
            pl.BlockSpec((hps, 4, 1, nc, L), lambda b, h: (h, 0, b, 0, 0)),
            pl.BlockSpec((SUBLANES, hps * DQK), lambda b, h: (0, h)),
            pl.BlockSpec((SUBLANES, hps * DQK), lambda b, h: (0, h)),
            pl.BlockSpec((1, hps * DV), lambda b, h: (0, h)),
        ],
        out_specs=pl.BlockSpec((1, t, hps * DV), lambda b, h: (b, 0, h)),
        out_shape=jax.ShapeDtypeStruct((b_, t, N_HEADS * DV), BF16),
        scratch_shapes=[
            pltpu.VMEM((t + 2 * SUBLANES, LANES), F32),
            pltpu.VMEM((hps, nc, DQK, L), BF16),
            pltpu.VMEM((hps, nc, L, DQK), BF16),
            pltpu.VMEM((hps, ncc, DQK, L), BF16),
            pltpu.VMEM((hps, t, ext), BF16),
            pltpu.VMEM((hps, tc, ext), BF16),
            pltpu.VMEM((hps, 2 * G_ROWS, nc, L), F32),
            pltpu.VMEM((hps, 2 * G_ROWS, ncc, L), F32),
            pltpu.VMEM((hps, DQK, ext), F32),
            pltpu.VMEM((hps, DQK, ext), F32),
            pltpu.VMEM((hps, t, DV), F32),
            pltpu.VMEM((hps, t, DV), F32),
        ],
        compiler_params=_cparams(("arbitrary", "arbitrary")),
    )(pc, pc, gc4, p, p, p, p, g4, wk, wq, head_g)


def _pool_kernel(u_ref, pw_ref, ps_ref, out_ref, spad):
    t = u_ref.shape[1]
    rows = t // GRID_W
    blk = 256
    halo = (max(POOL_WINDOWS) // 2) * GRID_W
    spad[0:halo, :] = jnp.zeros((halo, POOL_GC), F32)
    spad[halo + t:halo + t + halo, :] = jnp.zeros((halo, POOL_GC), F32)
    ti = lax.broadcasted_iota(jnp.int32, (blk, blk), 0)
    tj = lax.broadcasted_iota(jnp.int32, (blk, blk), 1)
    same_row = (ti >> GRID_SHIFT) == (tj >> GRID_SHIFT)
    diff = tj - ti
    tok = lax.broadcasted_iota(jnp.int32, (blk, 1), 0)
    for gi, w in enumerate(POOL_WINDOWS):
        hw = w // 2
        c0, c1 = gi * POOL_GC, (gi + 1) * POOL_GC
        band = jnp.where(same_row & (diff >= -hw) & (diff <= hw - 1), 1.0, 0.0).astype(BF16)
        for r in range(t // blk):
            spad[halo + r * blk:halo + (r + 1) * blk, :] = jnp.dot(
                band, u_ref[0, r * blk:(r + 1) * blk, c0:c1], preferred_element_type=F32)
        for r in range(t // blk):
            base = halo + r * blk
            acc = spad[base - hw * GRID_W:base - hw * GRID_W + blk, :]
            for j in range(-hw + 1, hw):
                acc = acc + spad[base + j * GRID_W:base + j * GRID_W + blk, :]
            tt = tok + r * blk
            gr = tt >> GRID_SHIFT
            gc = tt & (GRID_W - 1)
            cr = jnp.minimum(gr + hw, rows) - jnp.maximum(gr - hw, 0)
            cc = jnp.minimum(gc + hw, GRID_W) - jnp.maximum(gc - hw, 0)
            cnt = (cr * cc).astype(F32)
            d = acc / cnt - u_ref[0, r * blk:(r + 1) * blk, c0:c1].astype(F32)
            y = jnp.dot(d.astype(BF16), pw_ref[gi], preferred_element_type=F32) * ps_ref[:, c0:c1]
            out_ref[0, r * blk:(r + 1) * blk, c0:c1] = y.astype(BF16)


def _pool(p, pool_w, pool_scale):
    b_, t, _ = p.shape
    pw = len(POOL_WINDOWS) * POOL_GC
    halo = (max(POOL_WINDOWS) // 2) * GRID_W
    return pl.pallas_call(
        _pool_kernel,
        grid=(b_,),
        in_specs=[pl.BlockSpec((1, t, pw), lambda b: (b, 0, 3)),
                  pl.BlockSpec((len(POOL_WINDOWS), POOL_GC, POOL_GC), lambda b: (0, 0, 0)),
                  pl.BlockSpec((1, pw), lambda b: (0, 0))],
        out_specs=pl.BlockSpec((1, t, pw), lambda b: (b, 0, 0)),
        out_shape=jax.ShapeDtypeStruct((b_, t, pw), BF16),
        scratch_shapes=[pltpu.VMEM((t + 2 * halo, POOL_GC), F32)],
        compiler_params=_cparams(("arbitrary",)),
    )(p, pool_w, pool_scale)


def _outproj_kernel(ym_ref, yp_ref, wm_ref, wp_ref, x_ref, mod_ref, n2_ref, wr_ref,
                    x1_ref, h2_ref, lg_ref):
    tm = x_ref.shape[0]
    half = tm // 2
    scale2 = n2_ref[...] * (1.0 + mod_ref[0, 4:5, :])
    for r0 in (0, half):
        rows = slice(r0, r0 + half)
        mix = (jnp.dot(ym_ref[rows, :], wm_ref[...], preferred_element_type=F32)
               + jnp.dot(yp_ref[rows, :], wp_ref[...], preferred_element_type=F32))
        x1 = x_ref[rows, :] + mod_ref[0, 2:3, :] * mix
        x1_ref[rows, :] = x1
        y = x1 * lax.rsqrt(jnp.mean(x1 * x1, axis=-1, keepdims=True) + EPS)
        h2 = y * scale2 + mod_ref[0, 3:4, :]
        g0, g1 = r0 // SUBLANES, (r0 + half) // SUBLANES
        for j in range(h2_ref.shape[1]):
            h2_ref[g0:g1, j, :, :] = h2[:, j * LANES:(j + 1) * LANES].reshape(g1 - g0, SUBLANES, LANES)
        lg_ref[rows, :] = jnp.dot(h2.astype(BF16), wr_ref[...], preferred_element_type=F32)


def _outproj(ym, yp, wm, wp, x2, mod, rows_per_mod, n2, wr, tm):
    m, d = x2.shape
    km = ym.shape[1]
    kp = yp.shape[1]
    tiles_per_mod = rows_per_mod // tm
    return pl.pallas_call(
        _outproj_kernel,
        grid=(m // tm,),
        in_specs=[pl.BlockSpec((tm, km), lambda i: (i, 0)),
                  pl.BlockSpec((tm, kp), lambda i: (i, 0)),
                  pl.BlockSpec((km, d), lambda i: (0, 0), pipeline_mode=pl.Buffered(1)),
                  pl.BlockSpec((kp, d), lambda i: (0, 0), pipeline_mode=pl.Buffered(1)),
                  pl.BlockSpec((tm, d), lambda i: (i, 0)),
                  pl.BlockSpec((1, N_MOD, d), lambda i: (i // tiles_per_mod, 0, 0)),
                  pl.BlockSpec((1, d), lambda i: (0, 0)),
                  pl.BlockSpec((d, LANES), lambda i: (0, 0))],
        out_specs=[pl.BlockSpec((tm, d), lambda i: (i, 0)),
                   pl.BlockSpec((tm // SUBLANES, d // LANES, SUBLANES, LANES), lambda i: (i, 0, 0, 0)),
                   pl.BlockSpec((tm, LANES), lambda i: (i, 0))],
        out_shape=[jax.ShapeDtypeStruct((m, d), F32),
                   jax.ShapeDtypeStruct((m // SUBLANES, d // LANES, SUBLANES, LANES), F32),
                   jax.ShapeDtypeStruct((m, LANES), F32)],
        compiler_params=_cparams(("arbitrary",)),
    )(ym, yp, wm, wp, x2, mod, n2, wr)


def _scan_lanes_i32(x):
    n = x.shape[1]
    lane = lax.broadcasted_iota(jnp.int32, x.shape, 1)
    k = 1
    while k < n:
        x = x + jnp.where(lane >= k, pltpu.roll(x, k, 1), 0)
        k *= 2
    return x


COMBINE_TILE = 256
GRANULE = 2 * SUBLANES


def _route_kernel(lg_ref, grp_ref, sub_ref, gate_ref, tokrep_ref, bounds_ref, afft):
    t = lg_ref.shape[1]
    cap = grp_ref.shape[2]
    lane = lax.broadcasted_iota(jnp.int32, (t, LANES), 1)
    lg = jnp.where(lane < N_EXPERTS, lg_ref[0], -jnp.inf)
    ex = jnp.exp(lg - jnp.max(lg, axis=1, keepdims=True))
    aff = ex / jnp.sum(ex, axis=1, keepdims=True)
    for r in range(t // LANES):
        afft[:, r * LANES:(r + 1) * LANES] = aff[r * LANES:(r + 1) * LANES, :].T
    aff_t = afft[0:N_EXPERTS, :]

    def count(mask):
        return jnp.sum(jnp.where(mask, 1.0, 0.0), axis=1, keepdims=True).astype(jnp.int32)

    def bit_step(i, thr_bits):
        cand = thr_bits | (jnp.int32(1) << (30 - i))
        cand_f = lax.bitcast_convert_type(cand, F32)
        return jnp.where(count(aff_t >= cand_f) >= cap, cand, thr_bits)

    thr_bits = lax.fori_loop(0, 31, bit_step, jnp.zeros((N_EXPERTS, 1), jnp.int32))
    thr = lax.bitcast_convert_type(thr_bits, F32)
    gt = aff_t > thr
    eq = aff_t == thr
    need = cap - count(gt)
    eq_i = jnp.where(eq, 1, 0)
    eq_rank = _scan_lanes_i32(eq_i) - eq_i
    sel = gt | (eq & (eq_rank < need))
    sel_i = jnp.where(sel, 1, 0)
    pos = _scan_lanes_i32(sel_i) - sel_i
    key = jnp.where(sel, pos, -1)

    a_hi = aff.astype(BF16).astype(F32)
    a_mid = (aff - a_hi).astype(BF16).astype(F32)
    a_lo = (aff - a_hi - a_mid).astype(BF16).astype(F32)
    tok = lax.broadcasted_iota(jnp.int32, (t, LANES), 0)
    vals = (a_hi + pltpu.roll(a_mid, N_EXPERTS, 1) + pltpu.roll(a_lo, 2 * N_EXPERTS, 1)
            + jnp.where(lane == 3 * N_EXPERTS, (tok >> GRID_SHIFT).astype(F32), 0.0)
            + jnp.where(lane == 3 * N_EXPERTS + 1, (tok & (GRID_W - 1)).astype(F32), 0.0)).astype(BF16)
    slot = lax.broadcasted_iota(jnp.int32, (cap, 1), 0)
    for e in range(N_EXPERTS):
        onehot = jnp.where(slot == key[e:e + 1, :], 1.0, 0.0).astype(BF16)
        res = jnp.dot(onehot, vals, preferred_element_type=F32)
        rt = jnp.concatenate([res[r * LANES:(r + 1) * LANES, :].T for r in range(cap // LANES)], axis=1)
        gate_ref[0, e:e + 1, :] = (rt[e:e + 1, :] + rt[N_EXPERTS + e:N_EXPERTS + e + 1, :]) \
            + rt[2 * N_EXPERTS + e:2 * N_EXPERTS + e + 1, :]
        tokf = rt[3 * N_EXPERTS:3 * N_EXPERTS + 1, :] * float(GRID_W) + rt[3 * N_EXPERTS + 1:3 * N_EXPERTS + 2, :]
        tok_i = tokf.astype(jnp.int32)
        grp_ref[0, e:e + 1, :] = (tok_i >> SUBLANE_SHIFT) + pl.program_id(0) * (t // SUBLANES)
        sub_ref[0, e:e + 1, :] = tok_i & (SUBLANES - 1)
        tok_col = res[:, 3 * N_EXPERTS:3 * N_EXPERTS + 1] * float(GRID_W) \
            + res[:, 3 * N_EXPERTS + 1:3 * N_EXPERTS + 2]
        tokrep_ref[0, e * cap:(e + 1) * cap, :] = jnp.broadcast_to(tok_col, (cap, LANES)).astype(jnp.int32)

    n_tiles = t // COMBINE_TILE
    lane_b = lax.broadcasted_iota(jnp.int32, (N_EXPERTS, LANES), 1)
    bounds = jnp.where(lane_b >= n_tiles, cap, 0)
    for i in range(n_tiles):
        bounds = jnp.where(lane_b == i, pos[:, i * COMBINE_TILE:i * COMBINE_TILE + 1], bounds)
    bounds_ref[0] = bounds


def _route(lg3, cap):
    b_, t, _ = lg3.shape
    return pl.pallas_call(
        _route_kernel,
        grid=(b_,),
        in_specs=[pl.BlockSpec((1, t, LANES), lambda b: (b, 0, 0))],
        out_specs=[pl.BlockSpec((1, N_EXPERTS, cap), lambda b: (b, 0, 0)),
                   pl.BlockSpec((1, N_EXPERTS, cap), lambda b: (b, 0, 0)),
                   pl.BlockSpec((1, N_EXPERTS, cap), lambda b: (b, 0, 0)),
                   pl.BlockSpec((1, N_EXPERTS * cap, LANES), lambda b: (b, 0, 0)),
                   pl.BlockSpec((1, N_EXPERTS, LANES), lambda b: (b, 0, 0))],
        out_shape=[jax.ShapeDtypeStruct((b_, N_EXPERTS, cap), jnp.int32),
                   jax.ShapeDtypeStruct((b_, N_EXPERTS, cap), jnp.int32),
                   jax.ShapeDtypeStruct((b_, N_EXPERTS, cap), F32),
                   jax.ShapeDtypeStruct((b_, N_EXPERTS * cap, LANES), jnp.int32),
                   jax.ShapeDtypeStruct((b_, N_EXPERTS, LANES), jnp.int32)],
        scratch_shapes=[pltpu.VMEM((LANES, t), F32)],
        compiler_params=_cparams(("arbitrary",)),
    )(lg3)


def _ffn_kernel(grp_ref, sub_ref, gate_ref, h2_hbm, wg_ref, wu_ref, wd_ref, y_ref,
                xs_buf, wg_s, wu_s, wd_s, sem):
    p = pl.program_id(0)
    b = pl.program_id(1)
    n_e = pl.num_programs(0) - 1
    n_b = pl.num_programs(1)
    k_chunks, cap = xs_buf.shape[1], xs_buf.shape[2]
    step = p * n_b + b
    cur = step % 2
    nxt = 1 - cur
    last_b = b == n_b - 1
    p_next = jnp.where(last_b, p + 1, p)

    def issue_gather():
        def issue(r, carry):
            pltpu.make_async_copy(h2_hbm.at[grp_ref[0, 0, r], :, sub_ref[0, 0, r], :],
                                  xs_buf.at[nxt, :, r, :], sem.at[nxt]).start()
            return carry

        lax.fori_loop(0, cap, issue, 0, unroll=True)

    def cast_weights():
        ws = p % 2
        rows_in = wg_ref.shape[1]
        rows_mid = wd_ref.shape[1]
        wg_s[ws, pl.ds(pl.multiple_of(b * rows_in, rows_in), rows_in), :] = wg_ref[0].astype(BF16)
        wu_s[ws, pl.ds(pl.multiple_of(b * rows_in, rows_in), rows_in), :] = wu_ref[0].astype(BF16)
        wd_s[ws, pl.ds(pl.multiple_of(b * rows_mid, rows_mid), rows_mid), :] = wd_ref[0].astype(BF16)

    def run_expert():
        pltpu.make_async_copy(xs_buf.at[cur], xs_buf.at[cur], sem.at[cur]).wait()
        ws = (p - 1) % 2
        xs = jnp.concatenate([xs_buf[cur, j].astype(BF16) for j in range(k_chunks)], axis=1)
        a = _silu(jnp.dot(xs, wg_s[ws], preferred_element_type=F32)) \
            * jnp.dot(xs, wu_s[ws], preferred_element_type=F32)
        y = jnp.dot(a.astype(BF16), wd_s[ws], preferred_element_type=F32)
        g_row = gate_ref[0]
        g_col = jnp.concatenate(
            [jnp.broadcast_to(g_row[:, k * LANES:(k + 1) * LANES], (LANES, LANES)).T
             for k in range(cap // LANES)], axis=0)
        y_ref[0, 0] = (y * jnp.concatenate([g_col] * (y.shape[1] // LANES), axis=1)).astype(BF16)

    @pl.when((p_next >= 1) & (p_next <= n_e))
    def _():
        issue_gather()

    @pl.when(p == 0)
    def _():
        cast_weights()
        y_ref[0, 0] = jnp.zeros(y_ref.shape[2:], BF16)

    @pl.when((p >= 1) & (p < n_e))
    def _():
        cast_weights()
        run_expert()

    @pl.when(p == n_e)
    def _():
        run_expert()


def _ffn(grp, sub, gate, h2, wg, wu, wd):
    e_, d, f = wg.shape
    b_, _, cap = grp.shape
    assert h2.shape[1] * h2.shape[3] == d and h2.shape[2] == SUBLANES and h2.shape[0] % b_ == 0
    assert d % b_ == 0 and f % b_ == 0
    rows_in, rows_mid = d // b_, f // b_
    assert rows_in % (2 * SUBLANES) == 0 and rows_mid % (2 * SUBLANES) == 0

    def next_tokens(p, b):
        last_b = b == b_ - 1
        b_next = jnp.where(last_b, 0, b + 1)
        e_next = jnp.clip(jnp.where(last_b, p, p - 1), 0, e_ - 1)
        return (b_next * e_ + e_next, 0, 0)

    def weight_piece(p, b):
        return (jnp.minimum(p, e_ - 1), b, 0)

    return pl.pallas_call(
        _ffn_kernel,
        grid=(e_ + 1, b_),
        in_specs=[pl.BlockSpec((1, 1, cap), next_tokens, memory_space=pltpu.SMEM),
                  pl.BlockSpec((1, 1, cap), next_tokens, memory_space=pltpu.SMEM),
                  pl.BlockSpec((1, 1, cap), lambda p, b: (b * e_ + jnp.maximum(p - 1, 0), 0, 0)),
                  pl.BlockSpec(memory_space=pl.ANY),
                  pl.BlockSpec((1, rows_in, f), weight_piece),
                  pl.BlockSpec((1, rows_in, f), weight_piece),
                  pl.BlockSpec((1, rows_mid, d), weight_piece)],
        out_specs=pl.BlockSpec((1, 1, cap, d), lambda p, b: (b, (p + e_) % (e_ + 1), 0, 0)),
        out_shape=jax.ShapeDtypeStruct((b_, e_ + 1, cap, d), BF16),
        scratch_shapes=[pltpu.VMEM((2, d // LANES, cap, LANES), F32),
                        pltpu.VMEM((2, d, f), BF16),
                        pltpu.VMEM((2, d, f), BF16),
                        pltpu.VMEM((2, f, d), BF16),
                        pltpu.SemaphoreType.DMA((2,))],
        compiler_params=_cparams(("arbitrary", "arbitrary")),
    )(grp.reshape(b_ * e_, 1, cap), sub.reshape(b_ * e_, 1, cap), gate.reshape(b_ * e_, 1, cap), h2, wg, wu, wd)


COMBINE_CHUNK = 256


def _combine_kernel(bounds_ref, y_hbm, tok_ref, x1_ref, mod_ref, nf_ref, o_ref, y_buf, stage_y, stage_t, sem):
    b = pl.program_id(0)
    i = pl.program_id(1)
    n_b = pl.num_programs(0)
    n_e, cap = y_buf.shape[1], y_buf.shape[2]
    cur = b % 2

    def y_copy(bb, slot):
        return pltpu.make_async_copy(y_hbm.at[bb, pl.ds(0, n_e)], y_buf.at[slot], sem.at[slot])

    @pl.when(i == 0)
    def _():
        @pl.when(b == 0)
        def _():
            y_copy(0, 0).start()

        y_copy(b, cur).wait()

        @pl.when(b + 1 < n_b)
        def _():
            y_copy(b + 1, 1 - cur).start()

    tile = o_ref.shape[1]
    n_bounds = pl.num_programs(1) + 1
    per_chunk = COMBINE_CHUNK // GRANULE
    g_shift = GRANULE.bit_length() - 1
    assert per_chunk & (per_chunk - 1) == 0 and GRANULE == 1 << g_shift

    @pl.when((b == 0) & (i == 0))
    def _():
        stage_y[...] = jnp.zeros(stage_y.shape, BF16)
        stage_t[...] = jnp.full(stage_t.shape, -1, jnp.int32)

    o_ref[...] = jnp.zeros(o_ref.shape, F32)
    tok_lane = lax.broadcasted_iota(jnp.int32, (COMBINE_CHUNK, LANES), 1) + i * tile

    def flush():
        onehot_t = jnp.concatenate(
            [jnp.where(stage_t[...] == tok_lane + j * LANES, 1.0, 0.0).astype(BF16)
             for j in range(tile // LANES)], axis=1)
        o_ref[0] += lax.dot_general(onehot_t, stage_y[...], (((0,), (0,)), ((), ())),
                                    preferred_element_type=F32)
        stage_t[...] = jnp.full(stage_t.shape, -1, jnp.int32)

    def per_expert(e, k):
        base = (b * n_e + e) * n_bounds + i
        lo = bounds_ref[base]
        hi = bounds_ref[base + 1]
        g_lo = lax.shift_right_logical(lo, g_shift)
        g_hi = jnp.where(hi > lo, lax.shift_right_logical(hi + GRANULE - 1, g_shift), g_lo)

        def per_granule(g, k):
            src = pl.multiple_of(g * GRANULE, GRANULE)
            dst = pl.multiple_of((k & (per_chunk - 1)) * GRANULE, GRANULE)
            stage_y[pl.ds(dst, GRANULE), :] = y_buf[cur, e, pl.ds(src, GRANULE), :]
            stage_t[pl.ds(dst, GRANULE), :] = tok_ref[0, pl.ds(pl.multiple_of(e * cap + src, GRANULE), GRANULE), :]
            k = k + 1

            @pl.when((k & (per_chunk - 1)) == 0)
            def _():
                flush()

            return k

        return lax.fori_loop(g_lo, g_hi, per_granule, k)

    k = lax.fori_loop(0, n_e, per_expert, jnp.int32(0))

    @pl.when((k & (per_chunk - 1)) != 0)
    def _():
        flush()

    x = x1_ref[0] + mod_ref[0, 5:6, :] * o_ref[0]
    o_ref[0] = (x * lax.rsqrt(jnp.mean(x * x, axis=-1, keepdims=True) + EPS)) * nf_ref[...]


def _combine(y, tokrep, bounds, x1, mod, nf):
    b_, e_, cap, d = y.shape
    e_ -= 1
    t = x1.shape[1]
    tile = COMBINE_TILE
    assert cap % GRANULE == 0 and tile % LANES == 0 and bounds.shape == (b_, e_, t // tile + 1)
    grid_spec = pltpu.PrefetchScalarGridSpec(
        num_scalar_prefetch=1,
        grid=(b_, t // tile),
        in_specs=[pl.BlockSpec(memory_space=pl.ANY),
                  pl.BlockSpec((1, e_ * cap, LANES), lambda b, i, bnd: (b, 0, 0)),
                  pl.BlockSpec((1, tile, d), lambda b, i, bnd: (b, i, 0)),
                  pl.BlockSpec((1, N_MOD, d), lambda b, i, bnd: (b, 0, 0)),
                  pl.BlockSpec((1, d), lambda b, i, bnd: (0, 0))],
        out_specs=pl.BlockSpec((1, tile, d), lambda b, i, bnd: (b, i, 0)),
        scratch_shapes=[pltpu.VMEM((2, e_, cap, d), BF16),
                        pltpu.VMEM((COMBINE_CHUNK, d), BF16),
                        pltpu.VMEM((COMBINE_CHUNK, LANES), jnp.int32),
                        pltpu.SemaphoreType.DMA((2,))])
    return pl.pallas_call(
        _combine_kernel,
        grid_spec=grid_spec,
        out_shape=jax.ShapeDtypeStruct((b_, t, d), F32),
        compiler_params=_cparams(("arbitrary", "arbitrary")),
    )(bounds.reshape(-1), y, tokrep, x1, mod, nf)


def _gate_layout(gt, b_, t):
    return gt.reshape(N_HEADS, 4, b_, t // MLSTM_CHUNK, MLSTM_CHUNK)


def kernel(x, c, ctx, c_ctx, w_mod, b_mod, norm1, w_in, conv_q_w, conv_k_w, b_gates, head_g, pool_w,
           pool_scale, w_out, norm2, w_router, w_gate, w_up, w_down, norm_f):
    b_, t, d = x.shape
    tc = ctx.shape[1]
    depth = w_mod.shape[0]
    assert depth == 1
    l = 0
    qk_w = N_HEADS * DQK
    mw = N_HEADS * DV
    pool_wd = len(POOL_WINDOWS) * POOL_GC
    n_gate = 2 * 2 * N_HEADS
    k_off, v_off = 0, qk_w
    g_off = v_off + mw
    q_off = g_off + n_gate
    o_off = q_off + qk_w
    p_off = o_off + mw
    cap = EC_FACTOR * t // N_EXPERTS

    mod_rows = -(-(b_ + 1) // SUBLANES) * SUBLANES
    cs = jnp.zeros((mod_rows, d), F32).at[:b_].set(c).at[b_].set(c_ctx)
    mod_all = _modulation(cs, w_mod[l], b_mod[l])
    mod = mod_all[:b_].reshape(b_, N_MOD, d)
    mod_c = mod_all[b_:b_ + 1].reshape(1, N_MOD, d)

    wl = w_in[l]
    w_main = jnp.concatenate([wl[:, k_off:v_off], wl[:, v_off:g_off], wl[:, q_off:o_off],
                              wl[:, o_off:p_off], wl[:, p_off:]], axis=1).astype(BF16)
    perm = jnp.arange(n_gate).reshape(2, 2, N_HEADS).transpose(2, 0, 1).reshape(-1)
    w_g = wl[:, g_off:q_off][:, perm].T.astype(BF16)
    b_g = b_gates[l][perm].reshape(n_gate, 1)
    n1 = norm1[l].reshape(1, d)

    p, g = _inproj(x.reshape(b_ * t, d), mod, t, n1, w_main, w_g, b_g, PROJ_ROWS)
    pc, gc = _inproj(ctx.reshape(b_ * tc, d), mod_c, b_ * tc, n1, w_main[:, :qk_w + mw], w_g, b_g,
                     min(PROJ_ROWS, b_ * tc))
    p = p.reshape(b_, t, -1)
    pc = pc.reshape(b_, tc, -1)

    wk = jnp.pad(conv_k_w[l], ((0, SUBLANES - QK_CONV), (0, 0)))
    wq = jnp.pad(conv_q_w[l], ((0, SUBLANES - QK_CONV), (0, 0)))
    ym = _mlstm(p, pc, _gate_layout(g, b_, t), _gate_layout(gc, b_, tc), wk, wq, head_g[l].reshape(1, mw))
    yp = _pool(p, pool_w[l].astype(BF16), pool_scale[l].reshape(1, pool_wd))

    wo = w_out[l].astype(BF16)
    wr = jnp.pad(w_router[l], ((0, 0), (0, LANES - N_EXPERTS))).astype(BF16)
    x1, h2, lg = _outproj(ym.reshape(b_ * t, mw), yp.reshape(b_ * t, pool_wd), wo[:mw], wo[mw:],
                          x.reshape(b_ * t, d), mod, t, norm2[l].reshape(1, d), wr, PROJ_ROWS)

    grp, sub, gate, tokrep, bounds = _route(lg.reshape(b_, t, LANES), cap)

    y = _ffn(grp, sub, gate, h2, w_gate[l], w_up[l], w_down[l])
    return _combine(y, tokrep, bounds[:, :, :t // COMBINE_TILE + 1], x1.reshape(b_, t, d), mod,
                    norm_f.reshape(1, d))
```

```python
import functools

import jax
import jax.numpy as jnp
from jax import lax
from jax.experimental import pallas as pl
from jax.experimental.pallas import tpu as pltpu

F32 = jnp.float32
BF16 = jnp.bfloat16

N_HEADS = 4
DQK = 128
DV = 256
QK_CONV = 5
POOL_WINDOWS = (2, 4, 8, 16)
POOL_GC = 256
GRID_W = 64
GRID_SHIFT = GRID_W.bit_length() - 1
N_EXPERTS = 16
EC_FACTOR = 2
N_MOD = 6
EPS = 1e-6

LANES = 128
SUBLANES = 8
SUBLANE_SHIFT = SUBLANES.bit_length() - 1
PROJ_ROWS = 512
MLSTM_CHUNK = 128
VMEM_LIMIT = 56 * 1024 * 1024


def _cparams(sem):
    return pltpu.CompilerParams(dimension_semantics=sem, vmem_limit_bytes=VMEM_LIMIT)


def _sigmoid(x):
    return 1.0 / (1.0 + jnp.exp(-x))


def _silu(x):
    return x * _sigmoid(x)


def _log_sigmoid(x):
    return jnp.minimum(x, 0.0) - jnp.log(1.0 + jnp.exp(-jnp.abs(x)))


def _mod_kernel(c_ref, w_ref, b_ref, o_ref):
    s = _silu(c_ref[...]).astype(BF16)
    o_ref[...] = jnp.dot(s, w_ref[...].astype(BF16), preferred_element_type=F32) + b_ref[...]


def _modulation(cs, w_mod, b_mod):
    rows, d = cs.shape
    n = w_mod.shape[1]
    tn = 1024
    return pl.pallas_call(
        _mod_kernel,
        grid=(n // tn,),
        in_specs=[pl.BlockSpec((rows, d), lambda j: (0, 0)),
                  pl.BlockSpec((d, tn), lambda j: (0, j)),
                  pl.BlockSpec((1, tn), lambda j: (0, j))],
        out_specs=pl.BlockSpec((rows, tn), lambda j: (0, j)),
        out_shape=jax.ShapeDtypeStruct((rows, n), F32),
        compiler_params=_cparams(("arbitrary",)),
    )(cs, w_mod, b_mod.reshape(1, n))


INPROJ_COLS = 512


def _inproj_kernel(x_ref, mod_ref, n1_ref, w_ref, wg_ref, bg_ref, p_ref, gt_ref):
    tm = x_ref.shape[0]
    half = tm // 2
    n_cols = p_ref.shape[1]
    scale = n1_ref[...] * (1.0 + mod_ref[0, 1:2, :])
    for r0 in (0, half):
        rows = slice(r0, r0 + half)
        x = x_ref[rows, :]
        y = x * lax.rsqrt(jnp.mean(x * x, axis=-1, keepdims=True) + EPS)
        hb = (y * scale + mod_ref[0, 0:1, :]).astype(BF16)
        gt_ref[:, rows] = lax.dot_general(wg_ref[...], hb, (((1,), (1,)), ((), ())),
                                          preferred_element_type=F32) + bg_ref[...]
        for c0 in range(0, n_cols, INPROJ_COLS):
            p_ref[rows, c0:c0 + INPROJ_COLS] = jnp.dot(
                hb, w_ref[:, c0:c0 + INPROJ_COLS], preferred_element_type=F32).astype(BF16)


def _inproj(x2, mod, rows_per_mod, n1, w, wg, bg, tm):
    m, d = x2.shape
    n_cols = w.shape[1]
    n_g = wg.shape[0]
    tiles_per_mod = rows_per_mod // tm
    assert n_cols % INPROJ_COLS == 0 and tm % (2 * LANES) == 0
    return pl.pallas_call(
        _inproj_kernel,
        grid=(m // tm,),
        in_specs=[pl.BlockSpec((tm, d), lambda i: (i, 0)),
                  pl.BlockSpec((1, N_MOD, d), lambda i: (i // tiles_per_mod, 0, 0)),
                  pl.BlockSpec((1, d), lambda i: (0, 0)),
                  pl.BlockSpec((d, n_cols), lambda i: (0, 0), pipeline_mode=pl.Buffered(1)),
                  pl.BlockSpec((n_g, d), lambda i: (0, 0)),
                  pl.BlockSpec((n_g, 1), lambda i: (0, 0))],
        out_specs=[pl.BlockSpec((tm, n_cols), lambda i: (i, 0)),
                   pl.BlockSpec((n_g, tm), lambda i: (0, i))],
        out_shape=[jax.ShapeDtypeStruct((m, n_cols), BF16),
                   jax.ShapeDtypeStruct((n_g, m), F32)],
        compiler_params=_cparams(("arbitrary",)),
    )(x2, mod, n1, w, wg, bg)


def _scan_lanes(x, op, fill, reverse):
    lane = lax.broadcasted_iota(jnp.int32, x.shape, 1)
    k = 1
    while k < LANES:
        if reverse:
            x = op(x, jnp.where(lane < LANES - k, pltpu.roll(x, LANES - k, 1), fill))
        else:
            x = op(x, jnp.where(lane >= k, pltpu.roll(x, k, 1), fill))
        k *= 2
    return x


G_B, G_R, G_W, G_BEND, G_MLOC, G_RMAX, G_ROWS = 0, 1, 2, 3, 4, 5, 6


def _mlstm_kernel(kc_ref, vc_ref, gc_ref, k_ref, v_ref, q_ref, o_ref, g_ref, wk_ref, wq_ref, hg_ref,
                  out_ref,
                  cpad, ktt, qt, kctt, vext, vcext, gs, gsc, cf, cb, hf, hb):
    L = MLSTM_CHUNK
    t_lat = k_ref.shape[1]
    t_ctx = kc_ref.shape[1]
    nc = t_lat // L
    ncc = t_ctx // L
    hps = k_ref.shape[2] // DQK
    pad = SUBLANES
    half = QK_CONV // 2

    def conv_silu(src, w_ref, dst, t, scale, transposed):
        cpad[0:pad, :] = jnp.zeros((pad, LANES), F32)
        cpad[pad:pad + t, :] = src.astype(F32)
        cpad[pad + t:pad + t + pad, :] = jnp.zeros((pad, LANES), F32)
        for r in range(t // L):
            base = pad + r * L - half
            acc = w_ref[0:1, :] * cpad[base:base + L, :]
            for j in range(1, QK_CONV):
                acc = acc + w_ref[j:j + 1, :] * cpad[base + j:base + j + L, :]
            y = _silu(acc)
            if scale is not None:
                y = y * scale
            dst[r] = (y.T if transposed else y).astype(BF16)

    def fill_vext(dst, src, t):
        dst[:, 0:DV] = src
        dst[:, DV:DV + LANES] = jnp.ones((t, LANES), BF16)

    for hh in range(hps):
        qk_cols = slice(hh * DQK, (hh + 1) * DQK)
        v_cols = slice(hh * DV, (hh + 1) * DV)
        conv_silu(kc_ref[0, :, qk_cols], wk_ref.at[:, qk_cols], kctt.at[hh], t_ctx, DQK ** -0.5, True)
        conv_silu(k_ref[0, :, qk_cols], wk_ref.at[:, qk_cols], ktt.at[hh], t_lat, DQK ** -0.5, True)
        conv_silu(q_ref[0, :, qk_cols], wq_ref.at[:, qk_cols], qt.at[hh], t_lat, None, False)
        fill_vext(vcext.at[hh], vc_ref[0, :, v_cols], t_ctx)
        fill_vext(vext.at[hh], v_ref[0, :, v_cols], t_lat)

    def gate_prep(garr, dst):
        for d in range(2):
            li = garr[2 * d]
            lf = _log_sigmoid(garr[2 * d + 1])
            b = _scan_lanes(lf, jnp.add, 0.0, reverse=(d == 1))
            b_end = b[:, LANES - 1:LANES] if d == 0 else b[:, 0:1]
            r = li - b
            a = b_end + r
            m_loc = jnp.max(a, axis=1, keepdims=True)
            dst[G_ROWS * d + G_B] = b
            dst[G_ROWS * d + G_R] = r
            dst[G_ROWS * d + G_W] = jnp.exp(a - m_loc)
            dst[G_ROWS * d + G_BEND] = jnp.broadcast_to(b_end, b.shape)
            dst[G_ROWS * d + G_MLOC] = jnp.broadcast_to(m_loc, b.shape)
            dst[G_ROWS * d + G_RMAX] = _scan_lanes(r, jnp.maximum, -jnp.inf, reverse=(d == 1))

    for hh in range(hps):
        gate_prep(gc_ref[hh, :, 0], gsc.at[hh])
        gate_prep(g_ref[hh, :, 0], gs.at[hh])

    cf[...] = jnp.zeros(cf.shape, F32)
    cb[...] = jnp.zeros(cb.shape, F32)

    row_i = lax.broadcasted_iota(jnp.int32, (L, L), 0)
    col_i = lax.broadcasted_iota(jnp.int32, (L, L), 1)
    visible = (col_i <= row_i, col_i >= row_i)

    def per_token(row):
        return jnp.broadcast_to(row, (L, L)).T

    def step(hh, c, d, g_all, kt_all, v_all, q_all, c_all, m_prev):
        off = c * L
        g_scr, kt_scr, v_scr, c_scr = g_all.at[hh], kt_all.at[hh], v_all.at[hh], c_all.at[hh]
        q_scr = None if q_all is None else q_all.at[hh]
        g0 = G_ROWS * d
        kt_c = kt_scr[c]
        v_c = v_scr[pl.ds(off, L), :]
        b_end = g_scr[g0 + G_BEND, pl.ds(c, 1), 0:1]
        m_loc = g_scr[g0 + G_MLOC, pl.ds(c, 1), 0:1]
        c_prev = c_scr[...]
        h = None
        if q_scr is not None:
            q_c = q_scr[c]
            r_row = g_scr[g0 + G_R, pl.ds(c, 1), :]
            u = jnp.maximum(per_token(g_scr[g0 + G_RMAX, pl.ds(c, 1), :]), m_prev)
            b_t = per_token(g_scr[g0 + G_B, pl.ds(c, 1), :])
            qk = jnp.dot(q_c, kt_c, preferred_element_type=F32)
            s = (qk * jnp.exp(jnp.where(visible[d], r_row - u, -jnp.inf))).astype(BF16)
            wi = jnp.exp(m_prev - u)
            inter = jnp.dot(q_c, c_prev.astype(BF16), preferred_element_type=F32)
            res = jnp.dot(s, v_c, preferred_element_type=F32) \
                + jnp.concatenate([wi] * (c_prev.shape[1] // L), axis=1) * inter
            inv = 1.0 / jnp.maximum(jnp.abs(res[:, DV:DV + L]), jnp.exp(-(b_t + u)))
            h = res[:, 0:DV] * jnp.concatenate([inv] * (DV // L), axis=1)
        kwt = (kt_c.astype(F32) * g_scr[g0 + G_W, pl.ds(c, 1), :]).astype(BF16)
        c_loc = jnp.dot(kwt, v_c, preferred_element_type=F32)
        m_new = jnp.maximum(b_end + m_prev, m_loc)
        sp = jnp.exp(b_end + m_prev - m_new)
        sl = jnp.exp(m_loc - m_new)
        c_scr[...] = sp * c_prev + sl * c_loc
        return m_new, h

    def finish(hh, c, h):
        off = c * L
        cols = slice(hh * DV, (hh + 1) * DV)
        h = h * lax.rsqrt(jnp.mean(h * h, axis=-1, keepdims=True) + EPS)
        y = (h * hg_ref[:, cols]) * _sigmoid(o_ref[0, pl.ds(off, L), cols].astype(F32))
        out_ref[0, pl.ds(off, L), cols] = y.astype(BF16)

    ms = [jnp.zeros((1, 1), F32)] * (2 * hps)
    for i in range(ncc):
        for hh in range(hps):
            ms[2 * hh], _ = step(hh, i, 0, gsc, kctt, vcext, None, cf, ms[2 * hh])
            ms[2 * hh + 1], _ = step(hh, ncc - 1 - i, 1, gsc, kctt, vcext, None, cb, ms[2 * hh + 1])

    assert nc % 2 == 0

    def scan_step(i, carry, second_half):
        c_f, c_b = i, nc - 1 - i
        off_f, off_b = c_f * L, c_b * L
        out = []
        for hh in range(hps):
            m_f, h_f = step(hh, c_f, 0, gs, ktt, vext, qt, cf, carry[2 * hh])
            m_b, h_b = step(hh, c_b, 1, gs, ktt, vext, qt, cb, carry[2 * hh + 1])
            if second_half:
                finish(hh, c_f, h_f + hb[hh, pl.ds(off_f, L), :])
                finish(hh, c_b, h_b + hf[hh, pl.ds(off_b, L), :])
            else:
                hf[hh, pl.ds(off_f, L), :] = h_f
                hb[hh, pl.ds(off_b, L), :] = h_b
            out += [m_f, m_b]
        return tuple(out)

    carry = tuple(ms)
    for i in range(nc):
        carry = scan_step(i, carry, second_half=i >= nc // 2)


MLSTM_HEADS_PER_STEP = 2


def _mlstm(p, pc, g4, gc4, wk, wq, head_g):
    b_, t, _ = p.shape
    tc = pc.shape[1]
    L = MLSTM_CHUNK
    nc, ncc = t // L, tc // L
    ext = DV + LANES
    hps = MLSTM_HEADS_PER_STEP
    assert N_HEADS % hps == 0
    v_blk = N_HEADS * DQK // (hps * DV)
    q_blk = (N_HEADS * DQK + N_HEADS * DV) // (hps * DQK)
    o_blk = (2 * N_HEADS * DQK + N_HEADS * DV) // (hps * DV)
    return pl.pallas_call(
        _mlstm_kernel,
        grid=(b_, N_HEADS // hps),
        in_specs=[
            pl.BlockSpec((1, tc, hps * DQK), lambda b, h: (b, 0, h)),
            pl.BlockSpec((1, tc, hps * DV), lambda b, h: (b, 0, v_blk + h)),
            pl.BlockSpec((hps, 4, 1, ncc, L), lambda b, h: (h, 0, b, 0, 0)),
            pl.BlockSpec((1, t, hps * DQK), lambda b, h: (b, 0, h)),
            pl.BlockSpec((1, t, hps * DV), lambda b, h: (b, 0, v_blk + h)),
            pl.BlockSpec((1, t, hps * DQK), lambda b, h: (b, 0, q_blk + h)),
            pl.BlockSpec((1, t, hps * DV), lambda b, h: (b, 0, o_blk + h)),
            pl.BlockSpec((hps, 4, 1, nc, L), lambda b, h: (h, 0, b, 0, 0)),
            pl.BlockSpec((SUBLANES, hps * DQK), lambda b, h: (0, h)),
            pl.BlockSpec((SUBLANES, hps * DQK), lambda b, h: (0, h)),
            pl.BlockSpec((1, hps * DV), lambda b, h: (0, h)),
        ],
        out_specs=pl.BlockSpec((1, t, hps * DV), lambda b, h: (b, 0, h)),
        out_shape=jax.ShapeDtypeStruct((b_, t, N_HEADS * DV), BF16),
        scratch_shapes=[
            pltpu.VMEM((t + 2 * SUBLANES, LANES), F32),
            pltpu.VMEM((hps, nc, DQK, L), BF16),
            pltpu.VMEM((hps, nc, L, DQK), BF16),
            pltpu.VMEM((hps, ncc, DQK, L), BF16),
            pltpu.VMEM((hps, t, ext), BF16),
            pltpu.VMEM((hps, tc, ext), BF16),
            pltpu.VMEM((hps, 2 * G_ROWS, nc, L), F32),
            pltpu.VMEM((hps, 2 * G_ROWS, ncc, L), F32),
            pltpu.VMEM((hps, DQK, ext), F32),
            pltpu.VMEM((hps, DQK, ext), F32),
            pltpu.VMEM((hps, t, DV), F32),
            pltpu.VMEM((hps, t, DV), F32),
        ],
        compiler_params=_cparams(("arbitrary", "arbitrary")),
    )(pc, pc, gc4, p, p, p, p, g4, wk, wq, head_g)


def _pool_kernel(u_ref, pw_ref, ps_ref, out_ref, spad):
    t = u_ref.shape[1]
    rows = t // GRID_W
    blk = 256
    halo = (max(POOL_WINDOWS) // 2) * GRID_W
    spad[0:halo, :] = jnp.zeros((halo, POOL_GC), F32)
    spad[halo + t:halo + t + halo, :] = jnp.zeros((halo, POOL_GC), F32)
    ti = lax.broadcasted_iota(jnp.int32, (blk, blk), 0)
    tj = lax.broadcasted_iota(jnp.int32, (blk, blk), 1)
    same_row = (ti >> GRID_SHIFT) == (tj >> GRID_SHIFT)
    diff = tj - ti
    tok = lax.broadcasted_iota(jnp.int32, (blk, 1), 0)
    for gi, w in enumerate(POOL_WINDOWS):
        hw = w // 2
        c0, c1 = gi * POOL_GC, (gi + 1) * POOL_GC
        band = jnp.where(same_row & (diff >= -hw) & (diff <= hw - 1), 1.0, 0.0).astype(BF16)
        for r in range(t // blk):
            spad[halo + r * blk:halo + (r + 1) * blk, :] = jnp.dot(
                band, u_ref[0, r * blk:(r + 1) * blk, c0:c1], preferred_element_type=F32)
        for r in range(t // blk):
            base = halo + r * blk
            acc = spad[base - hw * GRID_W:base - hw * GRID_W + blk, :]
            for j in range(-hw + 1, hw):
                acc = acc + spad[base + j * GRID_W:base + j * GRID_W + blk, :]
            tt = tok + r * blk
            gr = tt >> GRID_SHIFT
            gc = tt & (GRID_W - 1)
            cr = jnp.minimum(gr + hw, rows) - jnp.maximum(gr - hw, 0)
            cc = jnp.minimum(gc + hw, GRID_W) - jnp.maximum(gc - hw, 0)
            cnt = (cr * cc).astype(F32)
            d = acc / cnt - u_ref[0, r * blk:(r + 1) * blk, c0:c1].astype(F32)
            y = jnp.dot(d.astype(BF16), pw_ref[gi], preferred_element_type=F32) * ps_ref[:, c0:c1]
            out_ref[0, r * blk:(r + 1) * blk, c0:c1] = y.astype(BF16)


def _pool(p, pool_w, pool_scale):
    b_, t, _ = p.shape
    pw = len(POOL_WINDOWS) * POOL_GC
    halo = (max(POOL_WINDOWS) // 2) * GRID_W
    return pl.pallas_call(
        _pool_kernel,
        grid=(b_,),
        in_specs=[pl.BlockSpec((1, t, pw), lambda b: (b, 0, 3)),
                  pl.BlockSpec((len(POOL_WINDOWS), POOL_GC, POOL_GC), lambda b: (0, 0, 0)),
                  pl.BlockSpec((1, pw), lambda b: (0, 0))],
        out_specs=pl.BlockSpec((1, t, pw), lambda b: (b, 0, 0)),
        out_shape=jax.ShapeDtypeStruct((b_, t, pw), BF16),
        scratch_shapes=[pltpu.VMEM((t + 2 * halo, POOL_GC), F32)],
        compiler_params=_cparams(("arbitrary",)),
    )(p, pool_w, pool_scale)


def _outproj_kernel(ym_ref, yp_ref, wm_ref, wp_ref, x_ref, mod_ref, n2_ref, wr_ref,
                    x1_ref, h2_ref, lg_ref):
    tm = x_ref.shape[0]
    half = tm // 2
    scale2 = n2_ref[...] * (1.0 + mod_ref[0, 4:5, :])
    for r0 in (0, half):
        rows = slice(r0, r0 + half)
        mix = (jnp.dot(ym_ref[rows, :], wm_ref[...], preferred_element_type=F32)
               + jnp.dot(yp_ref[rows, :], wp_ref[...], preferred_element_type=F32))
        x1 = x_ref[rows, :] + mod_ref[0, 2:3, :] * mix
        x1_ref[rows, :] = x1
        y = x1 * lax.rsqrt(jnp.mean(x1 * x1, axis=-1, keepdims=True) + EPS)
        h2 = y * scale2 + mod_ref[0, 3:4, :]
        g0, g1 = r0 // SUBLANES, (r0 + half) // SUBLANES
        for j in range(h2_ref.shape[1]):
            h2_ref[g0:g1, j, :, :] = h2[:, j * LANES:(j + 1) * LANES].reshape(g1 - g0, SUBLANES, LANES)
        lg_ref[:, rows] = lax.dot_general(wr_ref[...], h2.astype(BF16), (((1,), (1,)), ((), ())),
                                          preferred_element_type=F32)


def _outproj(ym, yp, wm, wp, x2, mod, rows_per_mod, n2, wr, tm):
    m, d = x2.shape
    km = ym.shape[1]
    kp = yp.shape[1]
    tiles_per_mod = rows_per_mod // tm
    return pl.pallas_call(
        _outproj_kernel,
        grid=(m // tm,),
        in_specs=[pl.BlockSpec((tm, km), lambda i: (i, 0)),
                  pl.BlockSpec((tm, kp), lambda i: (i, 0)),
                  pl.BlockSpec((km, d), lambda i: (0, 0), pipeline_mode=pl.Buffered(1)),
                  pl.BlockSpec((kp, d), lambda i: (0, 0), pipeline_mode=pl.Buffered(1)),
                  pl.BlockSpec((tm, d), lambda i: (i, 0)),
                  pl.BlockSpec((1, N_MOD, d), lambda i: (i // tiles_per_mod, 0, 0)),
                  pl.BlockSpec((1, d), lambda i: (0, 0)),
                  pl.BlockSpec((N_EXPERTS, d), lambda i: (0, 0))],
        out_specs=[pl.BlockSpec((tm, d), lambda i: (i, 0)),
                   pl.BlockSpec((tm // SUBLANES, d // LANES, SUBLANES, LANES), lambda i: (i, 0, 0, 0)),
                   pl.BlockSpec((N_EXPERTS, tm), lambda i: (0, i))],
        out_shape=[jax.ShapeDtypeStruct((m, d), F32),
                   jax.ShapeDtypeStruct((m // SUBLANES, d // LANES, SUBLANES, LANES), F32),
                   jax.ShapeDtypeStruct((N_EXPERTS, m), F32)],
        compiler_params=_cparams(("arbitrary",)),
    )(ym, yp, wm, wp, x2, mod, n2, wr)


def _scan_lanes_i32(x):
    n = x.shape[1]
    lane = lax.broadcasted_iota(jnp.int32, x.shape, 1)
    k = 1
    while k < n:
        x = x + jnp.where(lane >= k, pltpu.roll(x, k, 1), 0)
        k *= 2
    return x


COMBINE_TILE = 256
GRANULE = 2 * SUBLANES


def _route_kernel(lg_ref, grp_ref, sub_ref, gate_ref, tokrep_ref, bounds_ref, afft):
    t = lg_ref.shape[1]
    cap = grp_ref.shape[2]
    lane = lax.broadcasted_iota(jnp.int32, (t, LANES), 1)
    lg = lg_ref[...]
    ex = jnp.exp(lg - jnp.max(lg, axis=0, keepdims=True))
    aff_t = ex / jnp.sum(ex, axis=0, keepdims=True)
    afft[0:N_EXPERTS, :] = aff_t
    afft[N_EXPERTS:LANES, :] = jnp.zeros((LANES - N_EXPERTS, t), F32)
    aff = jnp.concatenate([afft[:, r * LANES:(r + 1) * LANES].T for r in range(t // LANES)], axis=0)

    def count(mask):
        return jnp.sum(jnp.where(mask, 1.0, 0.0), axis=1, keepdims=True).astype(jnp.int32)

    def bit_step(i, thr_bits):
        cand = thr_bits | (jnp.int32(1) << (30 - i))
        cand_f = lax.bitcast_convert_type(cand, F32)
        return jnp.where(count(aff_t >= cand_f) >= cap, cand, thr_bits)

    thr_bits = lax.fori_loop(0, 31, bit_step, jnp.zeros((N_EXPERTS, 1), jnp.int32))
    thr = lax.bitcast_convert_type(thr_bits, F32)
    gt = aff_t > thr
    eq = aff_t == thr
    need = cap - count(gt)
    eq_i = jnp.where(eq, 1, 0)
    eq_rank = _scan_lanes_i32(eq_i) - eq_i
    sel = gt | (eq & (eq_rank < need))
    sel_i = jnp.where(sel, 1, 0)
    pos = _scan_lanes_i32(sel_i) - sel_i
    key = jnp.where(sel, pos, -1)

    a_hi = aff.astype(BF16).astype(F32)
    a_mid = (aff - a_hi).astype(BF16).astype(F32)
    a_lo = (aff - a_hi - a_mid).astype(BF16).astype(F32)
    tok = lax.broadcasted_iota(jnp.int32, (t, LANES), 0)
    vals = (a_hi + pltpu.roll(a_mid, N_EXPERTS, 1) + pltpu.roll(a_lo, 2 * N_EXPERTS, 1)
            + jnp.where(lane == 3 * N_EXPERTS, (tok >> GRID_SHIFT).astype(F32), 0.0)
            + jnp.where(lane == 3 * N_EXPERTS + 1, (tok & (GRID_W - 1)).astype(F32), 0.0)).astype(BF16)
    slot = lax.broadcasted_iota(jnp.int32, (cap, 1), 0)
    for e in range(N_EXPERTS):
        onehot = jnp.where(slot == key[e:e + 1, :], 1.0, 0.0).astype(BF16)
        res = jnp.dot(onehot, vals, preferred_element_type=F32)
        rt = jnp.concatenate([res[r * LANES:(r + 1) * LANES, :].T for r in range(cap // LANES)], axis=1)
        gate_ref[0, e:e + 1, :] = (rt[e:e + 1, :] + rt[N_EXPERTS + e:N_EXPERTS + e + 1, :]) \
            + rt[2 * N_EXPERTS + e:2 * N_EXPERTS + e + 1, :]
        tokf = rt[3 * N_EXPERTS:3 * N_EXPERTS + 1, :] * float(GRID_W) + rt[3 * N_EXPERTS + 1:3 * N_EXPERTS + 2, :]
        tok_i = tokf.astype(jnp.int32)
        grp_ref[0, e:e + 1, :] = (tok_i >> SUBLANE_SHIFT) + pl.program_id(0) * (t // SUBLANES)
        sub_ref[0, e:e + 1, :] = tok_i & (SUBLANES - 1)
        tok_col = res[:, 3 * N_EXPERTS:3 * N_EXPERTS + 1] * float(GRID_W) \
            + res[:, 3 * N_EXPERTS + 1:3 * N_EXPERTS + 2]
        tokrep_ref[0, e * cap:(e + 1) * cap, :] = jnp.broadcast_to(tok_col, (cap, LANES)).astype(jnp.int32)

    n_tiles = t // COMBINE_TILE
    lane_b = lax.broadcasted_iota(jnp.int32, (N_EXPERTS, LANES), 1)
    bounds = jnp.where(lane_b >= n_tiles, cap, 0)
    for i in range(n_tiles):
        bounds = jnp.where(lane_b == i, pos[:, i * COMBINE_TILE:i * COMBINE_TILE + 1], bounds)
    bounds_ref[0] = bounds


def _route(lg3, b_, cap):
    t = lg3.shape[1] // b_
    return pl.pallas_call(
        _route_kernel,
        grid=(b_,),
        in_specs=[pl.BlockSpec((N_EXPERTS, t), lambda b: (0, b))],
        out_specs=[pl.BlockSpec((1, N_EXPERTS, cap), lambda b: (b, 0, 0)),
                   pl.BlockSpec((1, N_EXPERTS, cap), lambda b: (b, 0, 0)),
                   pl.BlockSpec((1, N_EXPERTS, cap), lambda b: (b, 0, 0)),
                   pl.BlockSpec((1, N_EXPERTS * cap, LANES), lambda b: (b, 0, 0)),
                   pl.BlockSpec((1, N_EXPERTS, LANES), lambda b: (b, 0, 0))],
        out_shape=[jax.ShapeDtypeStruct((b_, N_EXPERTS, cap), jnp.int32),
                   jax.ShapeDtypeStruct((b_, N_EXPERTS, cap), jnp.int32),
                   jax.ShapeDtypeStruct((b_, N_EXPERTS, cap), F32),
                   jax.ShapeDtypeStruct((b_, N_EXPERTS * cap, LANES), jnp.int32),
                   jax.ShapeDtypeStruct((b_, N_EXPERTS, LANES), jnp.int32)],
        scratch_shapes=[pltpu.VMEM((LANES, t), F32)],
        compiler_params=_cparams(("arbitrary",)),
    )(lg3)


def _ffn_kernel(grp_ref, sub_ref, gate_ref, h2_hbm, wg_ref, wu_ref, wd_ref, y_ref,
                xs_buf, wg_s, wu_s, wd_s, sem):
    p = pl.program_id(0)
    b = pl.program_id(1)
    n_e = pl.num_programs(0) - 1
    n_b = pl.num_programs(1)
    k_chunks, cap = xs_buf.shape[1], xs_buf.shape[2]
    step = p * n_b + b
    cur = step % 2
    nxt = 1 - cur
    last_b = b == n_b - 1
    p_next = jnp.where(last_b, p + 1, p)

    def issue_gather():
        def issue(r, carry):
            pltpu.make_async_copy(h2_hbm.at[grp_ref[0, 0, r], :, sub_ref[0, 0, r], :],
                                  xs_buf.at[nxt, :, r, :], sem.at[nxt]).start()
            return carry

        lax.fori_loop(0, cap, issue, 0, unroll=True)

    def cast_weights():
        ws = p % 2
        rows_in = wg_ref.shape[1]
        rows_mid = wd_ref.shape[1]
        wg_s[ws, pl.ds(pl.multiple_of(b * rows_in, rows_in), rows_in), :] = wg_ref[0].astype(BF16)
        wu_s[ws, pl.ds(pl.multiple_of(b * rows_in, rows_in), rows_in), :] = wu_ref[0].astype(BF16)
        wd_s[ws, pl.ds(pl.multiple_of(b * rows_mid, rows_mid), rows_mid), :] = wd_ref[0].astype(BF16)

    def run_expert():
        pltpu.make_async_copy(xs_buf.at[cur], xs_buf.at[cur], sem.at[cur]).wait()
        ws = (p - 1) % 2
        xs = jnp.concatenate([xs_buf[cur, j].astype(BF16) for j in range(k_chunks)], axis=1)
        a = _silu(jnp.dot(xs, wg_s[ws], preferred_element_type=F32)) \
            * jnp.dot(xs, wu_s[ws], preferred_element_type=F32)
        y = jnp.dot(a.astype(BF16), wd_s[ws], preferred_element_type=F32)
        g_row = gate_ref[0]
        g_col = jnp.concatenate(
            [jnp.broadcast_to(g_row[:, k * LANES:(k + 1) * LANES], (LANES, LANES)).T
             for k in range(cap // LANES)], axis=0)
        y_ref[0, 0] = (y * jnp.concatenate([g_col] * (y.shape[1] // LANES), axis=1)).astype(BF16)

    @pl.when((p_next >= 1) & (p_next <= n_e))
    def _():
        issue_gather()

    @pl.when(p == 0)
    def _():
        cast_weights()
        y_ref[0, 0] = jnp.zeros(y_ref.shape[2:], BF16)

    @pl.when((p >= 1) & (p < n_e))
    def _():
        cast_weights()
        run_expert()

    @pl.when(p == n_e)
    def _():
        run_expert()


def _ffn(grp, sub, gate, h2, wg, wu, wd):
    e_, d, f = wg.shape
    b_, _, cap = grp.shape
    assert h2.shape[1] * h2.shape[3] == d and h2.shape[2] == SUBLANES and h2.shape[0] % b_ == 0
    assert d % b_ == 0 and f % b_ == 0
    rows_in, rows_mid = d // b_, f // b_
    assert rows_in % (2 * SUBLANES) == 0 and rows_mid % (2 * SUBLANES) == 0

    def next_tokens(p, b):
        last_b = b == b_ - 1
        b_next = jnp.where(last_b, 0, b + 1)
        e_next = jnp.clip(jnp.where(last_b, p, p - 1), 0, e_ - 1)
        return (b_next * e_ + e_next, 0, 0)

    def weight_piece(p, b):
        return (jnp.minimum(p, e_ - 1), b, 0)

    return pl.pallas_call(
        _ffn_kernel,
        grid=(e_ + 1, b_),
        in_specs=[pl.BlockSpec((1, 1, cap), next_tokens, memory_space=pltpu.SMEM),
                  pl.BlockSpec((1, 1, cap), next_tokens, memory_space=pltpu.SMEM),
                  pl.BlockSpec((1, 1, cap), lambda p, b: (b * e_ + jnp.maximum(p - 1, 0), 0, 0)),
                  pl.BlockSpec(memory_space=pl.ANY),
                  pl.BlockSpec((1, rows_in, f), weight_piece),
                  pl.BlockSpec((1, rows_in, f), weight_piece),
                  pl.BlockSpec((1, rows_mid, d), weight_piece)],
        out_specs=pl.BlockSpec((1, 1, cap, d), lambda p, b: (b, (p + e_) % (e_ + 1), 0, 0)),
        out_shape=jax.ShapeDtypeStruct((b_, e_ + 1, cap, d), BF16),
        scratch_shapes=[pltpu.VMEM((2, d // LANES, cap, LANES), F32),
                        pltpu.VMEM((2, d, f), BF16),
                        pltpu.VMEM((2, d, f), BF16),
                        pltpu.VMEM((2, f, d), BF16),
                        pltpu.SemaphoreType.DMA((2,))],
        compiler_params=_cparams(("arbitrary", "arbitrary")),
    )(grp.reshape(b_ * e_, 1, cap), sub.reshape(b_ * e_, 1, cap), gate.reshape(b_ * e_, 1, cap), h2, wg, wu, wd)


COMBINE_CHUNK = 256


def _combine_kernel(bounds_ref, y_hbm, tok_ref, x1_ref, mod_ref, nf_ref, o_ref, y_buf, stage_y, stage_t, sem):
    b = pl.program_id(0)
    i = pl.program_id(1)
    n_b = pl.num_programs(0)
    n_e, cap = y_buf.shape[1], y_buf.shape[2]
    cur = b % 2

    def y_copy(bb, slot):
        return pltpu.make_async_copy(y_hbm.at[bb, pl.ds(0, n_e)], y_buf.at[slot], sem.at[slot])

    @pl.when(i == 0)
    def _():
        @pl.when(b == 0)
        def _():
            y_copy(0, 0).start()

        y_copy(b, cur).wait()

        @pl.when(b + 1 < n_b)
        def _():
            y_copy(b + 1, 1 - cur).start()

    tile = o_ref.shape[1]
    n_bounds = pl.num_programs(1) + 1
    per_chunk = COMBINE_CHUNK // GRANULE
    g_shift = GRANULE.bit_length() - 1
    assert per_chunk & (per_chunk - 1) == 0 and GRANULE == 1 << g_shift

    @pl.when((b == 0) & (i == 0))
    def _():
        stage_y[...] = jnp.zeros(stage_y.shape, BF16)
        stage_t[...] = jnp.full(stage_t.shape, -1, jnp.int32)

    o_ref[...] = jnp.zeros(o_ref.shape, F32)
    tok_lane = lax.broadcasted_iota(jnp.int32, (COMBINE_CHUNK, LANES), 1) + i * tile

    def flush():
        onehot_t = jnp.concatenate(
            [jnp.where(stage_t[...] == tok_lane + j * LANES, 1.0, 0.0).astype(BF16)
             for j in range(tile // LANES)], axis=1)
        o_ref[0] += lax.dot_general(onehot_t, stage_y[...], (((0,), (0,)), ((), ())),
                                    preferred_element_type=F32)
        stage_t[...] = jnp.full(stage_t.shape, -1, jnp.int32)

    def per_expert(e, k):
        base = (b * n_e + e) * n_bounds + i
        lo = bounds_ref[base]
        hi = bounds_ref[base + 1]
        g_lo = lax.shift_right_logical(lo, g_shift)
        g_hi = jnp.where(hi > lo, lax.shift_right_logical(hi + GRANULE - 1, g_shift), g_lo)

        def per_granule(g, k):
            src = pl.multiple_of(g * GRANULE, GRANULE)
            dst = pl.multiple_of((k & (per_chunk - 1)) * GRANULE, GRANULE)
            stage_y[pl.ds(dst, GRANULE), :] = y_buf[cur, e, pl.ds(src, GRANULE), :]
            stage_t[pl.ds(dst, GRANULE), :] = tok_ref[0, pl.ds(pl.multiple_of(e * cap + src, GRANULE), GRANULE), :]
            k = k + 1

            @pl.when((k & (per_chunk - 1)) == 0)
            def _():
                flush()

            return k

        return lax.fori_loop(g_lo, g_hi, per_granule, k)

    k = lax.fori_loop(0, n_e, per_expert, jnp.int32(0))

    @pl.when((k & (per_chunk - 1)) != 0)
    def _():
        flush()

    x = x1_ref[0] + mod_ref[0, 5:6, :] * o_ref[0]
    o_ref[0] = (x * lax.rsqrt(jnp.mean(x * x, axis=-1, keepdims=True) + EPS)) * nf_ref[...]


def _combine(y, tokrep, bounds, x1, mod, nf):
    b_, e_, cap, d = y.shape
    e_ -= 1
    t = x1.shape[1]
    tile = COMBINE_TILE
    assert cap % GRANULE == 0 and tile % LANES == 0 and bounds.shape == (b_, e_, t // tile + 1)
    grid_spec = pltpu.PrefetchScalarGridSpec(
        num_scalar_prefetch=1,
        grid=(b_, t // tile),
        in_specs=[pl.BlockSpec(memory_space=pl.ANY),
                  pl.BlockSpec((1, e_ * cap, LANES), lambda b, i, bnd: (b, 0, 0)),
                  pl.BlockSpec((1, tile, d), lambda b, i, bnd: (b, i, 0)),
                  pl.BlockSpec((1, N_MOD, d), lambda b, i, bnd: (b, 0, 0)),
                  pl.BlockSpec((1, d), lambda b, i, bnd: (0, 0))],
        out_specs=pl.BlockSpec((1, tile, d), lambda b, i, bnd: (b, i, 0)),
        scratch_shapes=[pltpu.VMEM((2, e_, cap, d), BF16),
                        pltpu.VMEM((COMBINE_CHUNK, d), BF16),
                        pltpu.VMEM((COMBINE_CHUNK, LANES), jnp.int32),
                        pltpu.SemaphoreType.DMA((2,))])
    return pl.pallas_call(
        _combine_kernel,
        grid_spec=grid_spec,
        out_shape=jax.ShapeDtypeStruct((b_, t, d), F32),
        compiler_params=_cparams(("arbitrary", "arbitrary")),
    )(bounds.reshape(-1), y, tokrep, x1, mod, nf)


def _gate_layout(gt, b_, t):
    return gt.reshape(N_HEADS, 4, b_, t // MLSTM_CHUNK, MLSTM_CHUNK)


def kernel(x, c, ctx, c_ctx, w_mod, b_mod, norm1, w_in, conv_q_w, conv_k_w, b_gates, head_g, pool_w,
           pool_scale, w_out, norm2, w_router, w_gate, w_up, w_down, norm_f):
    b_, t, d = x.shape
    tc = ctx.shape[1]
    depth = w_mod.shape[0]
    assert depth == 1
    l = 0
    qk_w = N_HEADS * DQK
    mw = N_HEADS * DV
    pool_wd = len(POOL_WINDOWS) * POOL_GC
    n_gate = 2 * 2 * N_HEADS
    k_off, v_off = 0, qk_w
    g_off = v_off + mw
    q_off = g_off + n_gate
    o_off = q_off + qk_w
    p_off = o_off + mw
    cap = EC_FACTOR * t // N_EXPERTS

    mod_rows = -(-(b_ + 1) // SUBLANES) * SUBLANES
    cs = jnp.zeros((mod_rows, d), F32).at[:b_].set(c).at[b_].set(c_ctx)
    mod_all = _modulation(cs, w_mod[l], b_mod[l])
    mod = mod_all[:b_].reshape(b_, N_MOD, d)
    mod_c = mod_all[b_:b_ + 1].reshape(1, N_MOD, d)

    wl = w_in[l]
    w_main = jnp.concatenate([wl[:, k_off:v_off], wl[:, v_off:g_off], wl[:, q_off:o_off],
                              wl[:, o_off:p_off], wl[:, p_off:]], axis=1).astype(BF16)
    perm = jnp.arange(n_gate).reshape(2, 2, N_HEADS).transpose(2, 0, 1).reshape(-1)
    w_g = wl[:, g_off:q_off][:, perm].T.astype(BF16)
    b_g = b_gates[l][perm].reshape(n_gate, 1)
    n1 = norm1[l].reshape(1, d)

    p, g = _inproj(x.reshape(b_ * t, d), mod, t, n1, w_main, w_g, b_g, PROJ_ROWS)
    pc, gc = _inproj(ctx.reshape(b_ * tc, d), mod_c, b_ * tc, n1, w_main[:, :qk_w + mw], w_g, b_g,
                     min(PROJ_ROWS, b_ * tc))
    p = p.reshape(b_, t, -1)
    pc = pc.reshape(b_, tc, -1)

    wk = jnp.pad(conv_k_w[l], ((0, SUBLANES - QK_CONV), (0, 0)))
    wq = jnp.pad(conv_q_w[l], ((0, SUBLANES - QK_CONV), (0, 0)))
    ym = _mlstm(p, pc, _gate_layout(g, b_, t), _gate_layout(gc, b_, tc), wk, wq, head_g[l].reshape(1, mw))
    yp = _pool(p, pool_w[l].astype(BF16), pool_scale[l].reshape(1, pool_wd))

    wo = w_out[l].astype(BF16)
    wr = w_router[l].T.astype(BF16)
    x1, h2, lg = _outproj(ym.reshape(b_ * t, mw), yp.reshape(b_ * t, pool_wd), wo[:mw], wo[mw:],
                          x.reshape(b_ * t, d), mod, t, norm2[l].reshape(1, d), wr, PROJ_ROWS)

    grp, sub, gate, tokrep, bounds = _route(lg, b_, cap)

    y = _ffn(grp, sub, gate, h2, w_gate[l], w_up[l], w_down[l])
    return _combine(y, tokrep, bounds[:, :, :t // COMBINE_TILE + 1], x1.reshape(b_, t, d), mod,
                    norm_f.reshape(1, d))
```
